```python
import math
import jax, jax.numpy as jnp
from jax import lax
import numpy as np

D_MODEL = 2048
BATCH = 4
SEQ = 2048
DEPTH = 1

N_META = 16
S5_WIDTH = 1024
S5_GROUP = 16
S5_GROUPS = S5_WIDTH // S5_GROUP
S5_STATE = 64
N_DIR = 2
CONV_WIDTH = 1024
CONV_K = 3
FFN_HIDDEN = ((math.ceil(8 * D_MODEL / 3) + 255) // 256) * 256
IN_COLS = S5_WIDTH + 3 * CONV_WIDTH + 2 * D_MODEL
RMS_EPS = 1e-6
DT_MIN = 1e-3
DT_MAX = 1e-1
LAM_RE_MAX = -1e-4

kernel_name = 'hybrid_s5_shortconv_gated_encoder_block'


def rms_norm(x, g):
    xf = x.astype(jnp.float32)
    r = lax.rsqrt(jnp.mean(xf * xf, axis=-1, keepdims=True) + RMS_EPS)
    return (xf * r * g.astype(jnp.float32)).astype(x.dtype)


def _complex_linear_combine(e1, e2):
    a1r, a1i, b1r, b1i = e1
    a2r, a2i, b2r, b2i = e2
    ar = a2r * a1r - a2i * a1i
    ai = a2r * a1i + a2i * a1r
    br = a2r * b1r - a2i * b1i + b2r
    bi = a2r * b1i + a2i * b1r + b2i
    return (ar, ai, br, bi)


def s5_bidirectional(u, lam_re, lam_im, log_dt, b_re, b_im, c_re, c_im, d_skip):
    bsz, seq_len, _ = u.shape
    f32 = jnp.float32
    uf = u.astype(f32)
    ug = uf.reshape(bsz, seq_len, S5_GROUPS, S5_GROUP)
    y = uf * d_skip.astype(f32)
    for direction in range(N_DIR):
        lr = jnp.minimum(lam_re[direction].astype(f32), LAM_RE_MAX)
        li = lam_im[direction].astype(f32)
        dt = jnp.exp(log_dt[direction].astype(f32))[:, None]
        mag = jnp.exp(lr * dt)
        ar = mag * jnp.cos(li * dt)
        ai = mag * jnp.sin(li * dt)
        den = lr * lr + li * li
        nr = ar - 1.0
        zr = (nr * lr + ai * li) / den
        zi = (ai * lr - nr * li) / den
        bu_r = jnp.einsum('blgc,gpc->blgp', ug, b_re[direction].astype(f32))
        bu_i = jnp.einsum('blgc,gpc->blgp', ug, b_im[direction].astype(f32))
        xr = zr * bu_r - zi * bu_i
        xi = zr * bu_i + zi * bu_r
        a_r = jnp.broadcast_to(ar, xr.shape)
        a_i = jnp.broadcast_to(ai, xr.shape)
        _, _, sr, si = lax.associative_scan(_complex_linear_combine, (a_r, a_i, xr, xi),
                                            axis=1, reverse=(direction == 1))
        yd = (jnp.einsum('blgp,gcp->blgc', sr, c_re[direction].astype(f32))
              - jnp.einsum('blgp,gcp->blgc', si, c_im[direction].astype(f32)))
        y = y + yd.reshape(bsz, seq_len, S5_WIDTH)
    return y.astype(u.dtype)


def centred_dwconv3(v, w, b):
    seq_len = v.shape[1]
    vp = jnp.pad(v, ((0, 0), (1, 1), (0, 0)))
    return w[0] * vp[:, :seq_len] + w[1] * vp[:, 1:seq_len + 1] + w[2] * vp[:, 2:] + b


def setup_inputs(seed: int = 0) -> dict:
    key = jax.random.key(seed)
    ks = jax.random.split(key, 32)
    f32 = jnp.float32
    G, P, Q = S5_GROUPS, S5_STATE, S5_GROUP

    def nrm(k, shape, scale):
        return jax.random.normal(k, shape, f32) * scale

    x = nrm(ks[0], (BATCH, SEQ, D_MODEL), 1.0)
    meta = nrm(ks[1], (N_META, D_MODEL), 1.0)
    g_mix_pre = 1.0 + nrm(ks[2], (DEPTH, D_MODEL), 0.02)
    g_mix_post = 1.0 + nrm(ks[3], (DEPTH, D_MODEL), 0.02)
    g_ffn_pre = 1.0 + nrm(ks[4], (DEPTH, D_MODEL), 0.02)
    g_ffn_post = 1.0 + nrm(ks[5], (DEPTH, D_MODEL), 0.02)
    w_in = nrm(ks[6], (DEPTH, D_MODEL, IN_COLS), D_MODEL ** -0.5)
    gate_b = nrm(ks[7], (DEPTH, 2 * D_MODEL), 0.01)
    n_idx = jnp.arange(P, dtype=f32)
    lam_re = -0.5 + nrm(ks[8], (DEPTH, N_DIR, G, P), 0.01)
    lam_im = math.pi * n_idx + nrm(ks[9], (DEPTH, N_DIR, G, P), 0.01)
    log_dt = jax.random.uniform(ks[10], (DEPTH, N_DIR, G), f32,
                                math.log(DT_MIN), math.log(DT_MAX))
    b_re = nrm(ks[11], (DEPTH, N_DIR, G, P, Q), (2.0 * Q) ** -0.5)
    b_im = nrm(ks[12], (DEPTH, N_DIR, G, P, Q), (2.0 * Q) ** -0.5)
    c_re = nrm(ks[13], (DEPTH, N_DIR, G, Q, P), (2.0 * P) ** -0.5)
    c_im = nrm(ks[14], (DEPTH, N_DIR, G, Q, P), (2.0 * P) ** -0.5)
    d_skip = nrm(ks[15], (DEPTH, S5_WIDTH), 1.0)
    w_glu = nrm(ks[16], (DEPTH, S5_WIDTH, S5_WIDTH), S5_WIDTH ** -0.5)
    b_glu = nrm(ks[17], (DEPTH, S5_WIDTH), 0.01)
    w_s_up = nrm(ks[18], (DEPTH, S5_WIDTH, D_MODEL), S5_WIDTH ** -0.5)
    conv_w = nrm(ks[19], (DEPTH, CONV_K, CONV_WIDTH), CONV_K ** -0.5)
    conv_b = nrm(ks[20], (DEPTH, CONV_WIDTH), 0.01)
    w_c_up = nrm(ks[21], (DEPTH, CONV_WIDTH, D_MODEL), CONV_WIDTH ** -0.5)
    w_o = nrm(ks[22], (DEPTH, D_MODEL, D_MODEL), D_MODEL ** -0.5)
    w_ffn_in = nrm(ks[23], (DEPTH, D_MODEL, 2 * FFN_HIDDEN), D_MODEL ** -0.5)
    w_ffn_out = nrm(ks[24], (DEPTH, FFN_HIDDEN, D_MODEL), FFN_HIDDEN ** -0.5)
    return {'x': x, 'meta': meta, 'g_mix_pre': g_mix_pre, 'g_mix_post': g_mix_post,
            'g_ffn_pre': g_ffn_pre, 'g_ffn_post': g_ffn_post, 'w_in': w_in, 'gate_b': gate_b,
            'lam_re': lam_re, 'lam_im': lam_im, 'log_dt': log_dt, 'b_re': b_re, 'b_im': b_im,
            'c_re': c_re, 'c_im': c_im, 'd_skip': d_skip, 'w_glu': w_glu, 'b_glu': b_glu,
            'w_s_up': w_s_up, 'conv_w': conv_w, 'conv_b': conv_b, 'w_c_up': w_c_up,
            'w_o': w_o, 'w_ffn_in': w_ffn_in, 'w_ffn_out': w_ffn_out}


def reference(x, meta, g_mix_pre, g_mix_post, g_ffn_pre, g_ffn_post, w_in, gate_b,
              lam_re, lam_im, log_dt, b_re, b_im, c_re, c_im, d_skip, w_glu, b_glu,
              w_s_up, conv_w, conv_b, w_c_up, w_o, w_ffn_in, w_ffn_out):
    bsz = x.shape[0]
    meta_b = jnp.broadcast_to(meta[None].astype(x.dtype), (bsz, N_META, D_MODEL))
    h_stream = jnp.concatenate([meta_b, x], axis=1)
    splits = [S5_WIDTH, S5_WIDTH + CONV_WIDTH, S5_WIDTH + 2 * CONV_WIDTH,
              S5_WIDTH + 3 * CONV_WIDTH, S5_WIDTH + 3 * CONV_WIDTH + D_MODEL]
    for l in range(DEPTH):
        h = rms_norm(h_stream, g_mix_pre[l])
        proj = h @ w_in[l]
        u_s, x_c, b_c, c_c, gl_s, gl_c = jnp.split(proj, splits, axis=-1)
        gb_s, gb_c = jnp.split(gate_b[l], 2)
        y_s = s5_bidirectional(u_s, lam_re[l], lam_im[l], log_dt[l], b_re[l], b_im[l],
                               c_re[l], c_im[l], d_skip[l])
        y_s = jax.nn.gelu(y_s)
        y_s = y_s * jax.nn.sigmoid(y_s @ w_glu[l] + b_glu[l])
        y_s = y_s @ w_s_up[l]
        v = centred_dwconv3(c_c * x_c, conv_w[l], conv_b[l])
        y_c = (b_c * v) @ w_c_up[l]
        merged = jax.nn.sigmoid(gl_s + gb_s) * y_s + jax.nn.sigmoid(gl_c + gb_c) * y_c
        h_stream = h_stream + rms_norm(merged @ w_o[l], g_mix_post[l])
        h = rms_norm(h_stream, g_ffn_pre[l])
        gate, up = jnp.split(h @ w_ffn_in[l], 2, axis=-1)
        f = (jax.nn.silu(gate) * up) @ w_ffn_out[l]
        h_stream = h_stream + rms_norm(f, g_ffn_post[l])
    return h_stream[:, N_META:]
```

```python
import functools
import math

import jax
import jax.numpy as jnp
from jax import lax
from jax.experimental import pallas as pl
from jax.experimental.pallas import tpu as pltpu

D_MODEL = 2048
BATCH = 4
SEQ = 2048
N_META = 16
S5_WIDTH = 1024
S5_GROUP = 16
S5_GROUPS = S5_WIDTH // S5_GROUP
S5_STATE = 64
N_DIR = 2
CONV_WIDTH = 1024
FFN_HIDDEN = ((math.ceil(8 * D_MODEL / 3) + 255) // 256) * 256
IN_COLS = S5_WIDTH + 3 * CONV_WIDTH + 2 * D_MODEL
RMS_EPS = 1e-6
LAM_RE_MAX = -1e-4

ROWS = SEQ * BATCH
META_ROWS = N_META * BATCH

V7X_SUBLANES = 8
V7X_LANES = 128
V7X_MXU_DIM = 256
V7X_VMEM_LIMIT_BYTES = 56 * 1024 * 1024

S5_BLOCK_CH = V7X_MXU_DIM
S5_BLOCKS = S5_WIDTH // S5_BLOCK_CH
S5_BLOCK_STATES = (S5_BLOCK_CH // S5_GROUP) * S5_STATE
S5_CHAINS = S5_BLOCK_STATES // V7X_LANES

IN_TILE_ROWS = 512
IN_TILE_COLS = 2048
SCAN_TILE_ROWS = 512
MIX_TILE_ROWS = 256
FFN_TILE_ROWS = 512
FFN_TILE_HIDDEN = 512
HALO_ROWS = 16

_f32 = jnp.float32
_bf16 = jnp.bfloat16


def _rms_norm(xf, g):
    r = lax.rsqrt(jnp.mean(xf * xf, axis=-1, keepdims=True) + RMS_EPS)
    return xf * r * g


def _params(sem):
    return pltpu.CompilerParams(dimension_semantics=sem,
                                vmem_limit_bytes=V7X_VMEM_LIMIT_BYTES)


def _discretise_kernel(lam_re_ref, lam_im_ref, log_dt_ref, b_re_ref, b_im_ref,
                       a_ref, bz_re_ref, bz_im_ref):
    lr = jnp.minimum(lam_re_ref[...], LAM_RE_MAX)
    li = lam_im_ref[...]
    dt = jnp.exp(log_dt_ref[...])
    mag = jnp.exp(lr * dt)
    ar = mag * jnp.cos(li * dt)
    ai = mag * jnp.sin(li * dt)
    den = lr * lr + li * li
    nr = ar - 1.0
    zr = (nr * lr + ai * li) / den
    zi = (ai * lr - nr * li) / den
    a_ref[0] = ar
    a_ref[1] = ai
    a_ref[2] = ar * ar - ai * ai
    a_ref[3] = 2.0 * ar * ai
    b_re = b_re_ref[...]
    b_im = b_im_ref[...]
    bz_re_ref[...] = zr[:, None, :] * b_re - zi[:, None, :] * b_im
    bz_im_ref[...] = zr[:, None, :] * b_im + zi[:, None, :] * b_re


def _discretise(lam_re, lam_im, log_dt, b_re, b_im):
    dg = N_DIR * S5_GROUPS
    b_re_t = jnp.swapaxes(b_re, -1, -2).reshape(dg, S5_GROUP, S5_STATE)
    b_im_t = jnp.swapaxes(b_im, -1, -2).reshape(dg, S5_GROUP, S5_STATE)
    return pl.pallas_call(
        _discretise_kernel,
        out_shape=(jax.ShapeDtypeStruct((4, dg, S5_STATE), _f32),
                   jax.ShapeDtypeStruct((dg, S5_GROUP, S5_STATE), _f32),
                   jax.ShapeDtypeStruct((dg, S5_GROUP, S5_STATE), _f32)),
        name="s5_discretise",
    )(lam_re.reshape(dg, S5_STATE), lam_im.reshape(dg, S5_STATE),
      log_dt.reshape(dg, 1), b_re_t, b_im_t)


def _block_diag_weights(bz_re, bz_im, c_re, c_im):
    gl = S5_BLOCK_CH // S5_GROUP
    eye = jnp.eye(gl, dtype=_f32)

    def wb_part(bz):
        bz = bz.reshape(N_DIR, S5_BLOCKS, gl, S5_GROUP, S5_STATE)
        w = jnp.einsum('dkgcp,gh->dkgchp', bz, eye)
        return w.reshape(N_DIR, S5_BLOCKS, S5_BLOCK_CH, S5_BLOCK_STATES)

    def wc_part(c):
        c = c.reshape(N_DIR, S5_BLOCKS, gl, S5_GROUP, S5_STATE)
        w = jnp.einsum('dkgcp,gh->dkgphc', c, eye)
        return w.reshape(N_DIR, S5_BLOCKS, S5_BLOCK_STATES, S5_BLOCK_CH)

    wb = jnp.concatenate([wb_part(bz_re), wb_part(bz_im)], axis=-1).astype(_bf16)
    wc = jnp.concatenate([wc_part(c_re), wc_part(-c_im)], axis=-2).astype(_bf16)
    return wb, wc


def _in_proj_kernel(x_ref, g_ref, w_ref, gb_ref, u_ref, bc_ref, q_ref, gate_ref, h_scr):
    j = pl.program_id(1)

    @pl.when(j == 0)
    def _():
        h_scr[...] = _rms_norm(x_ref[...], g_ref[...]).astype(_bf16)

    proj = jnp.dot(h_scr[...], w_ref[...], preferred_element_type=_f32)

    @pl.when(j == 0)
    def _():
        u_ref[...] = proj[:, :S5_WIDTH].astype(_bf16)
        bc_ref[...] = proj[:, S5_WIDTH:].astype(_bf16)

    @pl.when(j == 1)
    def _():
        q_ref[...] = (proj[:, CONV_WIDTH:] * proj[:, :CONV_WIDTH]).astype(_bf16)

    @pl.when(j >= 2)
    def _():
        gate_ref[...] = jax.nn.sigmoid(proj + gb_ref[0]).astype(_bf16)


def _in_proj(rows, g, w_perm, gate_b, tile_rows):
    n = rows.shape[0]
    grid = (n // tile_rows, IN_COLS // IN_TILE_COLS)
    seg = lambda i, j: (i, 0)
    return pl.pallas_call(
        _in_proj_kernel,
        grid=grid,
        in_specs=[
            pl.BlockSpec((tile_rows, D_MODEL), lambda i, j: (i, 0)),
            pl.BlockSpec((1, D_MODEL), lambda i, j: (0, 0)),
            pl.BlockSpec((D_MODEL, IN_TILE_COLS), lambda i, j: (0, j)),
            pl.BlockSpec((1, 1, D_MODEL), lambda i, j: (jnp.maximum(j - 2, 0), 0, 0)),
        ],
        out_specs=[
            pl.BlockSpec((tile_rows, S5_WIDTH), seg),
            pl.BlockSpec((tile_rows, CONV_WIDTH), seg),
            pl.BlockSpec((tile_rows, CONV_WIDTH), seg),
            pl.BlockSpec((tile_rows, D_MODEL), lambda i, j: (i, jnp.maximum(j - 2, 0))),
        ],
        out_shape=(jax.ShapeDtypeStruct((n, S5_WIDTH), _bf16),
                   jax.ShapeDtypeStruct((n, CONV_WIDTH), _bf16),
                   jax.ShapeDtypeStruct((n, CONV_WIDTH), _bf16),
                   jax.ShapeDtypeStruct((n, 2 * D_MODEL), _bf16)),
        scratch_shapes=[pltpu.VMEM((tile_rows, D_MODEL), _bf16)],
        compiler_params=_params(("arbitrary", "arbitrary")),
        name="in_proj",
    )(rows, g, w_perm, gate_b)


def _scan_tile(x_scr, pat_scr, carry, n_rows, reverse):
    n_tiles = n_rows // V7X_SUBLANES
    half = BATCH
    assert 2 * half == V7X_SUBLANES
    row = lax.broadcasted_iota(jnp.int32, (V7X_SUBLANES, V7X_LANES), 0)
    keep = (row < half) if reverse else (row >= half)

    def body(j, carry):
        jj = (n_tiles - 1 - j) if reverse else j
        r0 = pl.multiple_of(jj * V7X_SUBLANES, V7X_SUBLANES)
        new = []
        for c in range(S5_CHAINS):
            cr, ci = carry[2 * c], carry[2 * c + 1]
            re = slice(c * V7X_LANES, (c + 1) * V7X_LANES)
            im = slice(S5_BLOCK_STATES + c * V7X_LANES, S5_BLOCK_STATES + (c + 1) * V7X_LANES)
            xr = x_scr[pl.ds(r0, V7X_SUBLANES), re]
            xi = x_scr[pl.ds(r0, V7X_SUBLANES), im]
            pr, pi = pat_scr[0, :, re], pat_scr[1, :, re]
            qr, qi = pat_scr[2, :, re], pat_scr[3, :, re]
            xsr = pltpu.roll(xr, half, 0)
            xsi = pltpu.roll(xi, half, 0)
            sr = xr + (pr * cr - pi * ci) + (qr * xsr - qi * xsi)
            si = xi + (pr * ci + pi * cr) + (qr * xsi + qi * xsr)
            x_scr[pl.ds(r0, V7X_SUBLANES), re] = sr
            x_scr[pl.ds(r0, V7X_SUBLANES), im] = si
            new.append(jnp.where(keep, sr, pltpu.roll(sr, half, 0)))
            new.append(jnp.where(keep, si, pltpu.roll(si, half, 0)))
        return tuple(new)

    return lax.fori_loop(0, n_tiles, body, carry)


def _load_carry(carry_scr, k):
    out = []
    for c in range(S5_CHAINS):
        out.append(carry_scr[k, :, c * V7X_LANES:(c + 1) * V7X_LANES])
        out.append(carry_scr[k, :, S5_BLOCK_STATES + c * V7X_LANES:
                             S5_BLOCK_STATES + (c + 1) * V7X_LANES])
    return tuple(out)


def _store_carry(carry_scr, k, carry):
    for c in range(S5_CHAINS):
        carry_scr[k, :, c * V7X_LANES:(c + 1) * V7X_LANES] = carry[2 * c]
        carry_scr[k, :, S5_BLOCK_STATES + c * V7X_LANES:
                  S5_BLOCK_STATES + (c + 1) * V7X_LANES] = carry[2 * c + 1]


def _write_patterns(a_ref, pat_scr, reverse):
    row = lax.broadcasted_iota(jnp.int32, (V7X_SUBLANES, S5_BLOCK_STATES), 0)
    first = (row >= V7X_SUBLANES // 2) if reverse else (row < V7X_SUBLANES // 2)
    shape = (V7X_SUBLANES, S5_BLOCK_STATES)
    ar = jnp.broadcast_to(a_ref[0, 0, 0:1, :], shape)
    ai = jnp.broadcast_to(a_ref[0, 0, 1:2, :], shape)
    a2r = jnp.broadcast_to(a_ref[0, 0, 2:3, :], shape)
    a2i = jnp.broadcast_to(a_ref[0, 0, 3:4, :], shape)
    pat_scr[0] = jnp.where(first, ar, a2r)
    pat_scr[1] = jnp.where(first, ai, a2i)
    pat_scr[2] = jnp.where(first, 0.0, ar)
    pat_scr[3] = jnp.where(first, 0.0, ai)


def _s5_kernel(u_ref, um_ref, wb_ref, wc_ref, a_ref, y_ref, x_scr, pat_scr, carry_scr):
    d = pl.program_id(0)
    i = pl.program_id(1)
    k = pl.program_id(2)
    zeros = tuple(jnp.zeros((V7X_SUBLANES, V7X_LANES), _f32) for _ in range(2 * S5_CHAINS))

    def run(reverse):
        _write_patterns(a_ref, pat_scr, reverse)
        x_scr[...] = jnp.dot(u_ref[...], wb_ref[0, 0], preferred_element_type=_f32)
        carry = _scan_tile(x_scr, pat_scr, _load_carry(carry_scr, k), SCAN_TILE_ROWS, reverse)
        _store_carry(carry_scr, k, carry)
        y_ref[0] = jnp.dot(x_scr[...].astype(_bf16), wc_ref[0, 0],
                           preferred_element_type=_f32)

    @pl.when(jnp.logical_and(d == 0, i == 0))
    def _():
        _write_patterns(a_ref, pat_scr, False)
        x_scr[0:META_ROWS, :] = jnp.dot(um_ref[...], wb_ref[0, 0],
                                        preferred_element_type=_f32)
        _store_carry(carry_scr, k, _scan_tile(x_scr, pat_scr, zeros, META_ROWS, False))

    @pl.when(jnp.logical_and(d == 1, i == 0))
    def _():
        _store_carry(carry_scr, k, zeros)

    @pl.when(d == 0)
    def _():
        run(False)

    @pl.when(d == 1)
    def _():
        run(True)


def _s5_scan(u, u_meta, wb, wc, a_tab):
    n_tiles = ROWS // SCAN_TILE_ROWS

    def row_block(d, i):
        return i + d * (n_tiles - 1 - 2 * i)

    return pl.pallas_call(
        _s5_kernel,
        grid=(N_DIR, n_tiles, S5_BLOCKS),
        in_specs=[
            pl.BlockSpec((SCAN_TILE_ROWS, S5_BLOCK_CH), lambda d, i, k: (row_block(d, i), k)),
            pl.BlockSpec((META_ROWS, S5_BLOCK_CH), lambda d, i, k: (0, k)),
            pl.BlockSpec((1, 1, S5_BLOCK_CH, 2 * S5_BLOCK_STATES), lambda d, i, k: (d, k, 0, 0)),
            pl.BlockSpec((1, 1, 2 * S5_BLOCK_STATES, S5_BLOCK_CH), lambda d, i, k: (d, k, 0, 0)),
            pl.BlockSpec((1, 1, 4, S5_BLOCK_STATES), lambda d, i, k: (d, k, 0, 0)),
        ],
        out_specs=pl.BlockSpec((1, SCAN_TILE_ROWS, S5_BLOCK_CH),
                               lambda d, i, k: (d, row_block(d, i), k)),
        out_shape=jax.ShapeDtypeStruct((N_DIR, ROWS, S5_WIDTH), _f32),
        scratch_shapes=[
            pltpu.VMEM((SCAN_TILE_ROWS, 2 * S5_BLOCK_STATES), _f32),
            pltpu.VMEM((4, V7X_SUBLANES, S5_BLOCK_STATES), _f32),
            pltpu.VMEM((S5_BLOCKS, V7X_SUBLANES, 2 * S5_BLOCK_STATES), _f32),
        ],
        compiler_params=_params(("arbitrary", "arbitrary", "arbitrary")),
        name="s5_scan",
    )(u, u_meta, wb, wc, a_tab)


def _mixer_kernel(h_ref, y_ref, u_ref, q_ref, qp_ref, qn_ref, qm_ref, bc_ref, gate_ref,
                  d_ref, wglu_ref, bglu_ref, wsup_ref, cw_ref, cb_ref, wcup_ref, wo_ref,
                  g_ref, o_ref, q_scr):
    i = pl.program_id(0)
    last = pl.num_programs(0) - 1
    tm = MIX_TILE_ROWS

    ys = y_ref[0] + y_ref[1] + u_ref[...].astype(_f32) * d_ref[...]
    ys = jax.nn.gelu(ys)
    glu = jnp.dot(ys.astype(_bf16), wglu_ref[...], preferred_element_type=_f32) + bglu_ref[...]
    ys = ys * jax.nn.sigmoid(glu)
    y_s = jnp.dot(ys.astype(_bf16), wsup_ref[...], preferred_element_type=_f32)

    prev = jnp.where(i == 0, qm_ref[...], qp_ref[...]).astype(_f32)
    nxt = jnp.where(i == last, 0.0, qn_ref[...].astype(_f32))
    q_scr[0:HALO_ROWS, :] = prev
    q_scr[HALO_ROWS:HALO_ROWS + tm, :] = q_ref[...].astype(_f32)
    q_scr[HALO_ROWS + tm:, :] = nxt
    v = (cw_ref[0:1, :] * q_scr[HALO_ROWS - BATCH:HALO_ROWS - BATCH + tm, :]
         + cw_ref[1:2, :] * q_scr[HALO_ROWS:HALO_ROWS + tm, :]
         + cw_ref[2:3, :] * q_scr[HALO_ROWS + BATCH:HALO_ROWS + BATCH + tm, :]
         + cb_ref[...])
    y_c = jnp.dot((bc_ref[...].astype(_f32) * v).astype(_bf16), wcup_ref[...],
                  preferred_element_type=_f32)

    merged = (gate_ref[:, :D_MODEL].astype(_f32) * y_s
              + gate_ref[:, D_MODEL:].astype(_f32) * y_c)
    mixed = jnp.dot(merged.astype(_bf16), wo_ref[...], preferred_element_type=_f32)
    o_ref[...] = h_ref[...] + _rms_norm(mixed, g_ref[...])


def _mixer(h0, y, u, q, q_meta, bc, gate, d_skip, w_glu, b_glu, w_s_up, conv_w, conv_b,
           w_c_up, w_o, g_post):
    tm = MIX_TILE_ROWS
    n_tiles = ROWS // tm
    halo_per_tile = tm // HALO_ROWS
    n_halo = ROWS // HALO_ROWS
    row = lambda i: (i, 0)
    const = lambda i: (0, 0)
    resident = functools.partial(pl.BlockSpec, index_map=const, pipeline_mode=pl.Buffered(1))
    return pl.pallas_call(
        _mixer_kernel,
        grid=(n_tiles,),
        in_specs=[
            pl.BlockSpec((tm, D_MODEL), row),
            pl.BlockSpec((N_DIR, tm, S5_WIDTH), lambda i: (0, i, 0)),
            pl.BlockSpec((tm, S5_WIDTH), row),
            pl.BlockSpec((tm, CONV_WIDTH), row),
            pl.BlockSpec((HALO_ROWS, CONV_WIDTH),
                         lambda i: (jnp.maximum(i * halo_per_tile - 1, 0), 0)),
            pl.BlockSpec((HALO_ROWS, CONV_WIDTH),
                         lambda i: (jnp.minimum((i + 1) * halo_per_tile, n_halo - 1), 0)),
            resident((HALO_ROWS, CONV_WIDTH)),
            pl.BlockSpec((tm, CONV_WIDTH), row),
            pl.BlockSpec((tm, 2 * D_MODEL), row),
            resident((1, S5_WIDTH)),
            resident((S5_WIDTH, S5_WIDTH)),
            resident((1, S5_WIDTH)),
            resident((S5_WIDTH, D_MODEL)),
            resident((3, CONV_WIDTH)),
            resident((1, CONV_WIDTH)),
            resident((CONV_WIDTH, D_MODEL)),
            resident((D_MODEL, D_MODEL)),
            resident((1, D_MODEL)),
        ],
        out_specs=pl.BlockSpec((tm, D_MODEL), row),
        out_shape=jax.ShapeDtypeStruct((ROWS, D_MODEL), _f32),
        scratch_shapes=[pltpu.VMEM((tm + 2 * HALO_ROWS, CONV_WIDTH), _f32)],
        compiler_params=_params(("arbitrary",)),
        name="mixer_tail",
    )(h0, y, u, q, q, q, q_meta, bc, gate, d_skip, w_glu, b_glu, w_s_up, conv_w, conv_b,
      w_c_up, w_o, g_post)


def _ffn_kernel(h_ref, gpre_ref, wg_ref, wu_ref, wout_ref, gpost_ref, o_ref, hb_scr, acc_scr):
    j = pl.program_id(1)

    @pl.when(j == 0)
    def _():
        hb_scr[...] = _rms_norm(h_ref[...], gpre_ref[...]).astype(_bf16)
        acc_scr[...] = jnp.zeros_like(acc_scr)

    hb = hb_scr[...]
    gate = jnp.dot(hb, wg_ref[...], preferred_element_type=_f32)
    up = jnp.dot(hb, wu_ref[...], preferred_element_type=_f32)
    act = (jax.nn.silu(gate) * up).astype(_bf16)
    acc_scr[...] += jnp.dot(act, wout_ref[...], preferred_element_type=_f32)

    @pl.when(j == pl.num_programs(1) - 1)
    def _():
        o_ref[...] = h_ref[...] + _rms_norm(acc_scr[...], gpost_ref[...])


def _ffn(h1, g_pre, w_in, w_out, g_post):
    tm, th = FFN_TILE_ROWS, FFN_TILE_HIDDEN
    n_hidden = FFN_HIDDEN // th
    return pl.pallas_call(
        _ffn_kernel,
        grid=(ROWS // tm, n_hidden),
        in_specs=[
            pl.BlockSpec((tm, D_MODEL), lambda i, j: (i, 0)),
            pl.BlockSpec((1, D_MODEL), lambda i, j: (0, 0)),
            pl.BlockSpec((D_MODEL, th), lambda i, j: (0, j)),
            pl.BlockSpec((D_MODEL, th), lambda i, j: (0, j + n_hidden)),
            pl.BlockSpec((th, D_MODEL), lambda i, j: (j, 0)),
            pl.BlockSpec((1, D_MODEL), lambda i, j: (0, 0)),
        ],
        out_specs=pl.BlockSpec((tm, D_MODEL), lambda i, j: (i, 0)),
        out_shape=jax.ShapeDtypeStruct((ROWS, D_MODEL), _f32),
        scratch_shapes=[pltpu.VMEM((tm, D_MODEL), _bf16), pltpu.VMEM((tm, D_MODEL), _f32)],
        compiler_params=_params(("arbitrary", "arbitrary")),
        name="ffn",
    )(h1, g_pre, w_in, w_in, w_out, g_post)


def _permute_in_cols(w):
    s, c = S5_WIDTH, CONV_WIDTH
    return jnp.concatenate([w[:, :s], w[:, s + c:s + 2 * c], w[:, s:s + c],
                            w[:, s + 2 * c:]], axis=1)


def kernel(x, meta, g_mix_pre, g_mix_post, g_ffn_pre, g_ffn_post, w_in, gate_b, lam_re, lam_im,
           log_dt, b_re, b_im, c_re, c_im, d_skip, w_glu, b_glu, w_s_up, conv_w, conv_b, w_c_up,
           w_o, w_ffn_in, w_ffn_out):
    l = 0
    rows = jnp.swapaxes(x, 0, 1).reshape(ROWS, D_MODEL)
    meta_rows = jnp.repeat(meta.astype(x.dtype), BATCH, axis=0)

    w_in_p = _permute_in_cols(w_in[l]).astype(_bf16)
    gate_b2 = gate_b[l].reshape(2, 1, D_MODEL)
    g_pre = g_mix_pre[l].reshape(1, D_MODEL)

    a_tab, bz_re, bz_im = _discretise(lam_re[l], lam_im[l], log_dt[l], b_re[l], b_im[l])
    wb, wc = _block_diag_weights(bz_re, bz_im, c_re[l], c_im[l])
    a_tab = a_tab.reshape(4, N_DIR, S5_BLOCKS, S5_BLOCK_STATES).transpose(1, 2, 0, 3)

    u, bc, q, gate = _in_proj(rows, g_pre, w_in_p, gate_b2, IN_TILE_ROWS)
    u_meta, _, q_meta, _ = _in_proj(meta_rows, g_pre, w_in_p, gate_b2, META_ROWS)

    y = _s5_scan(u, u_meta, wb, wc, a_tab)

    h1 = _mixer(rows, y, u, q, q_meta[META_ROWS - HALO_ROWS:], bc, gate,
                d_skip[l].reshape(1, S5_WIDTH), w_glu[l].astype(_bf16),
                b_glu[l].reshape(1, S5_WIDTH), w_s_up[l].astype(_bf16), conv_w[l],
                conv_b[l].reshape(1, CONV_WIDTH), w_c_up[l].astype(_bf16),
                w_o[l].astype(_bf16), g_mix_post[l].reshape(1, D_MODEL))

    h2 = _ffn(h1, g_ffn_pre[l].reshape(1, D_MODEL), w_ffn_in[l].astype(_bf16),
              w_ffn_out[l].astype(_bf16), g_ffn_post[l].reshape(1, D_MODEL))
    return jnp.swapaxes(h2.reshape(SEQ, BATCH, D_MODEL), 0, 1)
```

```python
import functools
import math

import jax
import jax.numpy as jnp
from jax import lax
from jax.experimental import pallas as pl
from jax.experimental.pallas import tpu as pltpu

D_MODEL = 2048
BATCH = 4
SEQ = 2048
N_META = 16
S5_WIDTH = 1024
S5_GROUP = 16
S5_GROUPS = S5_WIDTH // S5_GROUP
S5_STATE = 64
N_DIR = 2
CONV_WIDTH = 1024
FFN_HIDDEN = ((math.ceil(8 * D_MODEL / 3) + 255) // 256) * 256
IN_COLS = S5_WIDTH + 3 * CONV_WIDTH + 2 * D_MODEL
RMS_EPS = 1e-6
LAM_RE_MAX = -1e-4

ROWS = SEQ * BATCH

V7X_SUBLANES = 8
V7X_LANES = 128
V7X_MXU_DIM = 256
V7X_VMEM_LIMIT_BYTES = 56 * 1024 * 1024

S5_BLOCK_CH = V7X_MXU_DIM
S5_BLOCKS = S5_WIDTH // S5_BLOCK_CH
S5_BLOCK_STATES = (S5_BLOCK_CH // S5_GROUP) * S5_STATE
S5_LANE_TILES = S5_BLOCK_STATES // V7X_LANES
S5_CHUNK = V7X_SUBLANES
S5_CHUNKS = SEQ // S5_CHUNK
S5_CHUNK_ROWS = S5_CHUNKS * BATCH
S5_CHUNK_COLS = S5_CHUNK * S5_BLOCK_CH
S5_TAPS = 2 * S5_CHUNK - 1
META_CHUNKS = N_META // S5_CHUNK

IN_TILE_ROWS = 512
IN_TILE_COLS = 2048
MIX_TILE_ROWS = 256
FFN_TILE_ROWS = 512
FFN_TILE_HIDDEN = 512
HALO_ROWS = 16

_f32 = jnp.float32
_bf16 = jnp.bfloat16


def _rms_norm(xf, g):
    r = lax.rsqrt(jnp.mean(xf * xf, axis=-1, keepdims=True) + RMS_EPS)
    return xf * r * g


def _params(sem):
    return pltpu.CompilerParams(dimension_semantics=sem,
                                vmem_limit_bytes=V7X_VMEM_LIMIT_BYTES)


def _dot(a, b):
    return jnp.dot(a, b, preferred_element_type=_f32)


def _dot_nt(a, b):
    return lax.dot_general(a, b, (((1,), (1,)), ((), ())), preferred_element_type=_f32)


def _discretise_kernel(lam_re_ref, lam_im_ref, log_dt_ref, b_re_ref, b_im_ref,
                       pow_ref, bz_re_ref, bz_im_ref):
    lr = jnp.minimum(lam_re_ref[...], LAM_RE_MAX)
    li = lam_im_ref[...]
    dt = jnp.exp(log_dt_ref[...])
    mag = jnp.exp(lr * dt)
    ar = mag * jnp.cos(li * dt)
    ai = mag * jnp.sin(li * dt)
    den = lr * lr + li * li
    nr = ar - 1.0
    zr = (nr * lr + ai * li) / den
    zi = (ai * lr - nr * li) / den
    pr = jnp.ones_like(ar)
    pi = jnp.zeros_like(ar)
    for m in range(S5_CHUNK + 1):
        pow_ref[m, 0] = pr
        pow_ref[m, 1] = pi
        pr, pi = pr * ar - pi * ai, pr * ai + pi * ar
    b_re = b_re_ref[...]
    b_im = b_im_ref[...]
    bz_re_ref[...] = zr[:, None, :] * b_re - zi[:, None, :] * b_im
    bz_im_ref[...] = zr[:, None, :] * b_im + zi[:, None, :] * b_re


def _discretise(lam_re, lam_im, log_dt, b_re, b_im):
    dg = N_DIR * S5_GROUPS
    b_re_t = jnp.swapaxes(b_re, -1, -2).reshape(dg, S5_GROUP, S5_STATE)
    b_im_t = jnp.swapaxes(b_im, -1, -2).reshape(dg, S5_GROUP, S5_STATE)
    return pl.pallas_call(
        _discretise_kernel,
        out_shape=(jax.ShapeDtypeStruct((S5_CHUNK + 1, 2, dg, S5_STATE), _f32),
                   jax.ShapeDtypeStruct((dg, S5_GROUP, S5_STATE), _f32),
                   jax.ShapeDtypeStruct((dg, S5_GROUP, S5_STATE), _f32)),
        name="s5_discretise",
    )(lam_re.reshape(dg, S5_STATE), lam_im.reshape(dg, S5_STATE),
      log_dt.reshape(dg, 1), b_re_t, b_im_t)


def _block_diag(w_re, w_im):
    gl = S5_BLOCK_CH // S5_GROUP
    eye = jnp.eye(gl, dtype=_f32)

    def part(w):
        w = w.reshape(N_DIR, S5_BLOCKS, gl, S5_GROUP, S5_STATE)
        w = jnp.einsum('dkgcp,gh->dkgchp', w, eye)
        return w.reshape(N_DIR, S5_BLOCKS, S5_BLOCK_CH, S5_BLOCK_STATES)

    return jnp.concatenate([part(w_re), part(w_im)], axis=-1)


def _in_proj_kernel(x_ref, g_ref, w_ref, gb_ref, u_ref, bc_ref, q_ref, gate_ref, h_scr):
    j = pl.program_id(2)

    @pl.when(j == 0)
    def _():
        h_scr[...] = _rms_norm(x_ref[...], g_ref[...]).astype(_bf16)

    proj = _dot(h_scr[...], w_ref[...])

    @pl.when(j == 0)
    def _():
        for k in range(S5_BLOCKS):
            blk = proj[:, k * S5_BLOCK_CH:(k + 1) * S5_BLOCK_CH]
            u_ref[k] = blk.reshape(-1, S5_CHUNK, S5_BLOCK_CH)
        bc_ref[...] = proj[:, S5_WIDTH:].astype(_bf16)

    @pl.when(j == 1)
    def _():
        q_ref[...] = (proj[:, CONV_WIDTH:] * proj[:, :CONV_WIDTH]).astype(_bf16)

    @pl.when(j >= 2)
    def _():
        gate_ref[...] = jax.nn.sigmoid(proj + gb_ref[0]).astype(_bf16)


def _in_proj(x3, g, w_perm, gate_b, tile_rows):
    nb, t_len, _ = x3.shape
    n_t = t_len // tile_rows
    n = nb * t_len
    row = lambda b, t, j: (b * n_t + t, 0)
    return pl.pallas_call(
        _in_proj_kernel,
        grid=(nb, n_t, IN_COLS // IN_TILE_COLS),
        in_specs=[
            pl.BlockSpec((None, tile_rows, D_MODEL), lambda b, t, j: (b, t, 0)),
            pl.BlockSpec((1, D_MODEL), lambda b, t, j: (0, 0)),
            pl.BlockSpec((D_MODEL, IN_TILE_COLS), lambda b, t, j: (0, j)),
            pl.BlockSpec((1, 1, D_MODEL), lambda b, t, j: (jnp.maximum(j - 2, 0), 0, 0)),
        ],
        out_specs=[
            pl.BlockSpec((S5_BLOCKS, tile_rows // S5_CHUNK, None, S5_CHUNK, S5_BLOCK_CH),
                         lambda b, t, j: (0, t, b, 0, 0)),
            pl.BlockSpec((tile_rows, CONV_WIDTH), row),
            pl.BlockSpec((tile_rows, CONV_WIDTH), row),
            pl.BlockSpec((tile_rows, D_MODEL),
                         lambda b, t, j: (b * n_t + t, jnp.maximum(j - 2, 0))),
        ],
        out_shape=(jax.ShapeDtypeStruct(
                       (S5_BLOCKS, t_len // S5_CHUNK, nb, S5_CHUNK, S5_BLOCK_CH), _f32),
                   jax.ShapeDtypeStruct((n, CONV_WIDTH), _bf16),
                   jax.ShapeDtypeStruct((n, CONV_WIDTH), _bf16),
                   jax.ShapeDtypeStruct((n, 2 * D_MODEL), _bf16)),
        scratch_shapes=[pltpu.VMEM((tile_rows, D_MODEL), _bf16)],
        compiler_params=_params(("arbitrary", "arbitrary", "arbitrary")),
        name="in_proj",
    )(x3, g, w_perm, gate_b)


def _boundary_scan(x_scr, a_re, a_im, init, reverse):
    n_tiles = S5_CHUNK_ROWS // V7X_SUBLANES
    row = lax.broadcasted_iota(jnp.int32, (V7X_SUBLANES, V7X_LANES), 0)
    first = (row >= BATCH) if reverse else (row < BATCH)

    def step(ar, ai, sr, si, xr, xi):
        return ar * sr - ai * si + xr, ar * si + ai * sr + xi

    def body(i, carry):
        ii = (n_tiles - 1 - i) if reverse else i
        r0 = pl.multiple_of(ii * V7X_SUBLANES, V7X_SUBLANES)
        new = []
        for t in range(S5_LANE_TILES):
            re = slice(t * V7X_LANES, (t + 1) * V7X_LANES)
            im = slice(S5_BLOCK_STATES + t * V7X_LANES, S5_BLOCK_STATES + (t + 1) * V7X_LANES)
            cr, ci = carry[2 * t], carry[2 * t + 1]
            xr = x_scr[pl.ds(r0, V7X_SUBLANES), re]
            xi = x_scr[pl.ds(r0, V7X_SUBLANES), im]
            tr, ti = step(a_re[t], a_im[t], cr, ci, xr, xi)
            outr = jnp.where(first, cr, pltpu.roll(tr, BATCH, 0))
            outi = jnp.where(first, ci, pltpu.roll(ti, BATCH, 0))
            x_scr[pl.ds(r0, V7X_SUBLANES), re] = outr
            x_scr[pl.ds(r0, V7X_SUBLANES), im] = outi
            tr, ti = step(a_re[t], a_im[t], outr, outi, xr, xi)
            new.append(jnp.where(first, pltpu.roll(tr, BATCH, 0), tr))
            new.append(jnp.where(first, pltpu.roll(ti, BATCH, 0), ti))
        return tuple(new)

    lax.fori_loop(0, n_tiles, body, init)


def _s5_kernel(u_ref, um_ref, wb_ref, wct_ref, p_ref, y_ref,
               ub_scr, w_scr, x_scr, sp_scr, tap_scr, tapf_scr, mid_scr, xm_scr):
    o = pl.program_id(1)
    ns = S5_BLOCK_STATES

    @pl.when(o == 0)
    def _():
        ub_scr[...] = u_ref[0].astype(_bf16)
        um = um_ref[0].astype(_bf16)
        for d in range(N_DIR):
            for part in range(2):
                cols = slice(part * ns, (part + 1) * ns)
                w_re = wb_ref[d, 0, :, 0:ns]
                w_im = wb_ref[d, 0, :, ns:2 * ns]
                for j in range(S5_CHUNK):
                    m = (S5_CHUNK - 1 - j) if d == 0 else j
                    pr = p_ref[0, d, m, 0:1, :]
                    pi = p_ref[0, d, m, 1:2, :]
                    g = (w_re * pr - w_im * pi) if part == 0 else (w_re * pi + w_im * pr)
                    w_scr[j * S5_BLOCK_CH:(j + 1) * S5_BLOCK_CH, :] = g.astype(_bf16)
                x_scr[:, cols] = _dot(ub_scr[...], w_scr[...])
                taps = _dot_nt(w_scr[...], wct_ref[d, 0, :, cols].astype(_bf16))
                if part == 0:
                    tapf_scr[...] = taps
                else:
                    tapf_scr[...] += taps
                if d == 0:
                    xm_scr[:, cols] = _dot(um, w_scr[...])

            edge = (S5_CHUNK - 1) * S5_BLOCK_CH
            if d == 0:
                tap_scr[0:edge, :] = tapf_scr[0:edge, :].astype(_bf16)
                mid_scr[...] = tapf_scr[edge:, :]
            else:
                tap_scr[edge:edge + S5_BLOCK_CH, :] = (
                    mid_scr[...] + tapf_scr[0:S5_BLOCK_CH, :]).astype(_bf16)
                tap_scr[edge + S5_BLOCK_CH:, :] = tapf_scr[S5_BLOCK_CH:, :].astype(_bf16)

            a_re, a_im, init = [], [], []
            for t in range(S5_LANE_TILES):
                lanes = slice(t * V7X_LANES, (t + 1) * V7X_LANES)
                lanes_im = slice(ns + t * V7X_LANES, ns + (t + 1) * V7X_LANES)
                ar = p_ref[0, d, S5_CHUNK, 0:1, lanes]
                ai = p_ref[0, d, S5_CHUNK, 1:2, lanes]
                a_re.append(jnp.broadcast_to(ar, (V7X_SUBLANES, V7X_LANES)))
                a_im.append(jnp.broadcast_to(ai, (V7X_SUBLANES, V7X_LANES)))
                if d == 0:
                    sr = jnp.zeros((1, V7X_LANES), _f32)
                    si = jnp.zeros((1, V7X_LANES), _f32)
                    for c in range(META_CHUNKS):
                        xr = xm_scr[c:c + 1, lanes]
                        xi = xm_scr[c:c + 1, lanes_im]
                        sr, si = ar * sr - ai * si + xr, ar * si + ai * sr + xi
                    init.append(jnp.broadcast_to(sr, (V7X_SUBLANES, V7X_LANES)))
                    init.append(jnp.broadcast_to(si, (V7X_SUBLANES, V7X_LANES)))
                else:
                    init.append(jnp.zeros((V7X_SUBLANES, V7X_LANES), _f32))
                    init.append(jnp.zeros((V7X_SUBLANES, V7X_LANES), _f32))
            _boundary_scan(x_scr, a_re, a_im, tuple(init), reverse=(d == 1))
            sp_scr[d] = x_scr[...].astype(_bf16)

    def out_weights(d, m):
        pw = p_ref[0, d, m]
        pr, pi = pw[0:1, :], pw[1:2, :]
        c_re = wct_ref[d, 0, :, 0:ns]
        c_imn = wct_ref[d, 0, :, ns:2 * ns]
        return jnp.concatenate([c_re * pr + c_imn * pi, c_imn * pr - c_re * pi],
                               axis=1).astype(_bf16)

    tap0 = pl.multiple_of((S5_CHUNK - 1 - o) * S5_BLOCK_CH, S5_BLOCK_CH)
    y = _dot_nt(sp_scr[0], out_weights(0, o + 1))
    y += _dot_nt(sp_scr[1], out_weights(1, S5_CHUNK - o))
    y += _dot(ub_scr[...], tap_scr[pl.ds(tap0, S5_CHUNK_COLS), :])
    y_ref[0] = y


def _s5(u, u_meta, wb, wct, ptab):
    const = functools.partial(pl.BlockSpec, pipeline_mode=pl.Buffered(1))
    return pl.pallas_call(
        _s5_kernel,
        grid=(S5_BLOCKS, S5_CHUNK),
        in_specs=[
            const((1, S5_CHUNK_ROWS, S5_CHUNK_COLS), lambda k, o: (k, 0, 0)),
            pl.BlockSpec((1, V7X_SUBLANES, S5_CHUNK_COLS), lambda k, o: (k, 0, 0)),
            const((N_DIR, 1, S5_BLOCK_CH, 2 * S5_BLOCK_STATES), lambda k, o: (0, k, 0, 0)),
            const((N_DIR, 1, S5_BLOCK_CH, 2 * S5_BLOCK_STATES), lambda k, o: (0, k, 0, 0)),
            pl.BlockSpec((1, N_DIR, S5_CHUNK + 1, 2, S5_BLOCK_STATES),
                         lambda k, o: (k, 0, 0, 0, 0)),
        ],
        out_specs=pl.BlockSpec((1, S5_CHUNK_ROWS, S5_BLOCK_CH), lambda k, o: (k, 0, o)),
        out_shape=jax.ShapeDtypeStruct((S5_BLOCKS, S5_CHUNK_ROWS, S5_CHUNK_COLS), _f32),
        scratch_shapes=[
            pltpu.VMEM((S5_CHUNK_ROWS, S5_CHUNK_COLS), _bf16),
            pltpu.VMEM((S5_CHUNK_COLS, S5_BLOCK_STATES), _bf16),
            pltpu.VMEM((S5_CHUNK_ROWS, 2 * S5_BLOCK_STATES), _f32),
            pltpu.VMEM((N_DIR, S5_CHUNK_ROWS, 2 * S5_BLOCK_STATES), _bf16),
            pltpu.VMEM((S5_TAPS * S5_BLOCK_CH, S5_BLOCK_CH), _bf16),
            pltpu.VMEM((S5_CHUNK_COLS, S5_BLOCK_CH), _f32),
            pltpu.VMEM((S5_BLOCK_CH, S5_BLOCK_CH), _f32),
            pltpu.VMEM((V7X_SUBLANES, 2 * S5_BLOCK_STATES), _f32),
        ],
        compiler_params=_params(("arbitrary", "arbitrary")),
        name="s5",
    )(u, u_meta, wb, wct, ptab)


def _mixer_kernel(h_ref, y_ref, u_ref, q_ref, qp_ref, qn_ref, qm_ref, bc_ref, gate_ref,
                  d_ref, wglu_ref, bglu_ref, wsup_ref, cw_ref, cb_ref, wcup_ref, wo_ref,
                  g_ref, o_ref, q_scr):
    t = pl.program_id(1)
    last = pl.num_programs(1) - 1
    tm = MIX_TILE_ROWS

    ys = jnp.concatenate(
        [(y_ref[k] + u_ref[k] * d_ref[:, k * S5_BLOCK_CH:(k + 1) * S5_BLOCK_CH]
          ).reshape(tm, S5_BLOCK_CH) for k in range(S5_BLOCKS)], axis=1)
    ys = jax.nn.gelu(ys)
    glu = _dot(ys.astype(_bf16), wglu_ref[...]) + bglu_ref[...]
    ys = ys * jax.nn.sigmoid(glu)
    y_s = _dot(ys.astype(_bf16), wsup_ref[...])

    prev = jnp.where(t == 0, qm_ref[...], qp_ref[...]).astype(_f32)
    nxt = jnp.where(t == last, 0.0, qn_ref[...].astype(_f32))
    q_scr[0:HALO_ROWS, :] = prev
    q_scr[HALO_ROWS:HALO_ROWS + tm, :] = q_ref[...].astype(_f32)
    q_scr[HALO_ROWS + tm:, :] = nxt
    v = (cw_ref[0:1, :] * q_scr[HALO_ROWS - 1:HALO_ROWS - 1 + tm, :]
         + cw_ref[1:2, :] * q_scr[HALO_ROWS:HALO_ROWS + tm, :]
         + cw_ref[2:3, :] * q_scr[HALO_ROWS + 1:HALO_ROWS + 1 + tm, :]
         + cb_ref[...])
    y_c = _dot((bc_ref[...].astype(_f32) * v).astype(_bf16), wcup_ref[...])

    merged = (gate_ref[:, :D_MODEL].astype(_f32) * y_s
              + gate_ref[:, D_MODEL:].astype(_f32) * y_c)
    mixed = _dot(merged.astype(_bf16), wo_ref[...])
    o_ref[...] = h_ref[...] + _rms_norm(mixed, g_ref[...])


def _mixer(x, y5, u5, q, q_meta, bc, gate, d_skip, w_glu, b_glu, w_s_up, conv_w, conv_b,
           w_c_up, w_o, g_post):
    tm = MIX_TILE_ROWS
    n_t = SEQ // tm
    halo_per_tile = tm // HALO_ROWS
    n_halo = ROWS // HALO_ROWS
    row = lambda b, t: (b * n_t + t, 0)
    chunked = pl.BlockSpec((S5_BLOCKS, tm // S5_CHUNK, None, S5_CHUNK, S5_BLOCK_CH),
                           lambda b, t: (0, t, b, 0, 0))
    resident = functools.partial(pl.BlockSpec, index_map=lambda b, t: (0, 0),
                                 pipeline_mode=pl.Buffered(1))
    return pl.pallas_call(
        _mixer_kernel,
        grid=(BATCH, n_t),
        in_specs=[
            pl.BlockSpec((None, tm, D_MODEL), lambda b, t: (b, t, 0)),
            chunked,
            chunked,
            pl.BlockSpec((tm, CONV_WIDTH), row),
            pl.BlockSpec((HALO_ROWS, CONV_WIDTH),
                         lambda b, t: (jnp.maximum((b * n_t + t) * halo_per_tile - 1, 0), 0)),
            pl.BlockSpec((HALO_ROWS, CONV_WIDTH),
                         lambda b, t: (jnp.minimum((b * n_t + t + 1) * halo_per_tile,
                                                   n_halo - 1), 0)),
            resident((HALO_ROWS, CONV_WIDTH)),
            pl.BlockSpec((tm, CONV_WIDTH), row),
            pl.BlockSpec((tm, 2 * D_MODEL), row),
            resident((1, S5_WIDTH)),
            resident((S5_WIDTH, S5_WIDTH)),
            resident((1, S5_WIDTH)),
            resident((S5_WIDTH, D_MODEL)),
            resident((3, CONV_WIDTH)),
            resident((1, CONV_WIDTH)),
            resident((CONV_WIDTH, D_MODEL)),
            resident((D_MODEL, D_MODEL)),
            resident((1, D_MODEL)),
        ],
        out_specs=pl.BlockSpec((tm, D_MODEL), row),
        out_shape=jax.ShapeDtypeStruct((ROWS, D_MODEL), _f32),
        scratch_shapes=[pltpu.VMEM((tm + 2 * HALO_ROWS, CONV_WIDTH), _f32)],
        compiler_params=_params(("arbitrary", "arbitrary")),
        name="mixer_tail",
    )(x, y5, u5, q, q, q, q_meta, bc, gate, d_skip, w_glu, b_glu, w_s_up, conv_w, conv_b,
      w_c_up, w_o, g_post)


def _ffn_kernel(h_ref, gpre_ref, wg_ref, wu_ref, wout_ref, gpost_ref, o_ref, hb_scr, acc_scr):
    j = pl.program_id(1)

    @pl.when(j == 0)
    def _():
        hb_scr[...] = _rms_norm(h_ref[...], gpre_ref[...]).astype(_bf16)
        acc_scr[...] = jnp.zeros_like(acc_scr)

    hb = hb_scr[...]
    gate = _dot(hb, wg_ref[...])
    up = _dot(hb, wu_ref[...])
    act = (jax.nn.silu(gate) * up).astype(_bf16)
    acc_scr[...] += _dot(act, wout_ref[...])

    @pl.when(j == pl.num_programs(1) - 1)
    def _():
        o_ref[...] = h_ref[...] + _rms_norm(acc_scr[...], gpost_ref[...])


def _ffn(h1, g_pre, w_in, w_out, g_post):
    tm, th = FFN_TILE_ROWS, FFN_TILE_HIDDEN
    n_hidden = FFN_HIDDEN // th
    return pl.pallas_call(
        _ffn_kernel,
        grid=(ROWS // tm, n_hidden),
        in_specs=[
            pl.BlockSpec((tm, D_MODEL), lambda i, j: (i, 0)),
            pl.BlockSpec((1, D_MODEL), lambda i, j: (0, 0)),
            pl.BlockSpec((D_MODEL, th), lambda i, j: (0, j)),
            pl.BlockSpec((D_MODEL, th), lambda i, j: (0, j + n_hidden)),
            pl.BlockSpec((th, D_MODEL), lambda i, j: (j, 0)),
            pl.BlockSpec((1, D_MODEL), lambda i, j: (0, 0)),
        ],
        out_specs=pl.BlockSpec((tm, D_MODEL), lambda i, j: (i, 0)),
        out_shape=jax.ShapeDtypeStruct((ROWS, D_MODEL), _f32),
        scratch_shapes=[pltpu.VMEM((tm, D_MODEL), _bf16), pltpu.VMEM((tm, D_MODEL), _f32)],
        compiler_params=_params(("arbitrary", "arbitrary")),
        name="ffn",
    )(h1, g_pre, w_in, w_in, w_out, g_post)


def _permute_in_cols(w):
    s, c = S5_WIDTH, CONV_WIDTH
    return jnp.concatenate([w[:, :s], w[:, s + c:s + 2 * c], w[:, s:s + c],
                            w[:, s + 2 * c:]], axis=1)


def kernel(x, meta, g_mix_pre, g_mix_post, g_ffn_pre, g_ffn_post, w_in, gate_b, lam_re, lam_im,
           log_dt, b_re, b_im, c_re, c_im, d_skip, w_glu, b_glu, w_s_up, conv_w, conv_b, w_c_up,
           w_o, w_ffn_in, w_ffn_out):
    l = 0
    w_in_p = _permute_in_cols(w_in[l]).astype(_bf16)
    gate_b2 = gate_b[l].reshape(2, 1, D_MODEL)
    g_pre = g_mix_pre[l].reshape(1, D_MODEL)

    ptab, bz_re, bz_im = _discretise(lam_re[l], lam_im[l], log_dt[l], b_re[l], b_im[l])
    wb = _block_diag(bz_re, bz_im)
    dg = N_DIR * S5_GROUPS
    wct = _block_diag(c_re[l].reshape(dg, S5_GROUP, S5_STATE),
                      -c_im[l].reshape(dg, S5_GROUP, S5_STATE))
    ptab = ptab.reshape(S5_CHUNK + 1, 2, N_DIR, S5_BLOCKS, S5_BLOCK_STATES)
    ptab = ptab.transpose(3, 2, 0, 1, 4)

    u5, bc, q, gate = _in_proj(x, g_pre, w_in_p, gate_b2, IN_TILE_ROWS)
    um5, _, q_meta, _ = _in_proj(meta.astype(x.dtype)[None], g_pre, w_in_p, gate_b2, N_META)

    u = u5.reshape(S5_BLOCKS, S5_CHUNK_ROWS, S5_CHUNK_COLS)
    u_meta = jnp.pad(um5.reshape(S5_BLOCKS, META_CHUNKS, S5_CHUNK_COLS),
                     ((0, 0), (0, V7X_SUBLANES - META_CHUNKS), (0, 0)))
    y = _s5(u, u_meta, wb, wct, ptab)
    y5 = y.reshape(S5_BLOCKS, S5_CHUNKS, BATCH, S5_CHUNK, S5_BLOCK_CH)

    h1 = _mixer(x, y5, u5, q, q_meta, bc, gate,
                d_skip[l].reshape(1, S5_WIDTH), w_glu[l].astype(_bf16),
                b_glu[l].reshape(1, S5_WIDTH), w_s_up[l].astype(_bf16), conv_w[l],
                conv_b[l].reshape(1, CONV_WIDTH), w_c_up[l].astype(_bf16),
                w_o[l].astype(_bf16), g_mix_post[l].reshape(1, D_MODEL))

    h2 = _ffn(h1, g_ffn_pre[l].reshape(1, D_MODEL), w_ffn_in[l].astype(_bf16),
              w_ffn_out[l].astype(_bf16), g_ffn_post[l].reshape(1, D_MODEL))
    return h2.reshape(BATCH, SEQ, D_MODEL)
```

```python
import functools
import math

import jax
import jax.numpy as jnp
from jax import lax
from jax.experimental import pallas as pl
from jax.experimental.pallas import tpu as pltpu

D_MODEL = 2048
BATCH = 4
SEQ = 2048
N_META = 16
S5_WIDTH = 1024
S5_GROUP = 16
S5_GROUPS = S5_WIDTH // S5_GROUP
S5_STATE = 64
N_DIR = 2
CONV_WIDTH = 1024
FFN_HIDDEN = ((math.ceil(8 * D_MODEL / 3) + 255) // 256) * 256
IN_COLS = S5_WIDTH + 3 * CONV_WIDTH + 2 * D_MODEL
RMS_EPS = 1e-6
LAM_RE_MAX = -1e-4

ROWS = SEQ * BATCH

V7X_SUBLANES = 8
V7X_LANES = 128
V7X_MXU_DIM = 256
V7X_VMEM_LIMIT_BYTES = 56 * 1024 * 1024

S5_BLOCK_CH = V7X_MXU_DIM
S5_BLOCKS = S5_WIDTH // S5_BLOCK_CH
S5_BLOCK_STATES = (S5_BLOCK_CH // S5_GROUP) * S5_STATE
S5_LANE_TILES = S5_BLOCK_STATES // V7X_LANES
S5_CHUNK = V7X_SUBLANES
S5_CHUNKS = SEQ // S5_CHUNK
S5_CHUNK_ROWS = S5_CHUNKS * BATCH
S5_CHUNK_COLS = S5_CHUNK * S5_BLOCK_CH
S5_TAPS = 2 * S5_CHUNK - 1
META_CHUNKS = N_META // S5_CHUNK
S5_SLABS = S5_WIDTH // V7X_LANES
SLABS_PER_BLOCK = S5_BLOCK_CH // V7X_LANES
GROUPS_PER_SLAB = V7X_LANES // S5_STATE

IN_TILE_ROWS = 512
IN_TILE_COLS = 1024
MIX_TILE_T = 64
FFN_TILE_ROWS = 512
FFN_TILE_HIDDEN = 512
HALO_ROWS = 16

_f32 = jnp.float32
_bf16 = jnp.bfloat16


def _rms_norm(xf, g):
    r = lax.rsqrt(jnp.mean(xf * xf, axis=-1, keepdims=True) + RMS_EPS)
    return xf * r * g


def _params(sem):
    return pltpu.CompilerParams(dimension_semantics=sem,
                                vmem_limit_bytes=V7X_VMEM_LIMIT_BYTES)


def _dot(a, b):
    return jnp.dot(a, b, preferred_element_type=_f32)


def _dot_nt(a, b):
    return lax.dot_general(a, b, (((1,), (1,)), ((), ())), preferred_element_type=_f32)


def _discretise_kernel(lam_re_ref, lam_im_ref, log_dt_ref, b_re_ref, b_im_ref,
                       pow_ref, bz_re_ref, bz_im_ref):
    lr = jnp.minimum(lam_re_ref[...], LAM_RE_MAX)
    li = lam_im_ref[...]
    dt = jnp.exp(log_dt_ref[...])
    mag = jnp.exp(lr * dt)
    ar = mag * jnp.cos(li * dt)
    ai = mag * jnp.sin(li * dt)
    den = lr * lr + li * li
    nr = ar - 1.0
    zr = (nr * lr + ai * li) / den
    zi = (ai * lr - nr * li) / den
    pr = jnp.ones_like(ar)
    pi = jnp.zeros_like(ar)
    for m in range(S5_CHUNK + 1):
        pow_ref[m, 0] = pr
        pow_ref[m, 1] = pi
        pr, pi = pr * ar - pi * ai, pr * ai + pi * ar
    b_re = b_re_ref[...]
    b_im = b_im_ref[...]
    bz_re_ref[...] = zr[:, None, :] * b_re - zi[:, None, :] * b_im
    bz_im_ref[...] = zr[:, None, :] * b_im + zi[:, None, :] * b_re


def _discretise(lam_re, lam_im, log_dt, b_re, b_im):
    dg = N_DIR * S5_GROUPS
    b_re_t = jnp.swapaxes(b_re, -1, -2).reshape(dg, S5_GROUP, S5_STATE)
    b_im_t = jnp.swapaxes(b_im, -1, -2).reshape(dg, S5_GROUP, S5_STATE)
    return pl.pallas_call(
        _discretise_kernel,
        out_shape=(jax.ShapeDtypeStruct((S5_CHUNK + 1, 2, dg, S5_STATE), _f32),
                   jax.ShapeDtypeStruct((dg, S5_GROUP, S5_STATE), _f32),
                   jax.ShapeDtypeStruct((dg, S5_GROUP, S5_STATE), _f32)),
        name="s5_discretise",
    )(lam_re.reshape(dg, S5_STATE), lam_im.reshape(dg, S5_STATE),
      log_dt.reshape(dg, 1), b_re_t, b_im_t)


def _compact_maps(w_re, w_im):
    def part(w):
        w = w.reshape(N_DIR, S5_BLOCKS, S5_BLOCK_CH, S5_STATE)
        return jnp.concatenate([w] * GROUPS_PER_SLAB, axis=-1)
    return jnp.stack([part(w_re), part(w_im)], axis=2)


def _dense_map(ref, d, part):
    rows = lax.broadcasted_iota(jnp.int32, (S5_BLOCK_CH, S5_BLOCK_STATES), 0)
    cols = lax.broadcasted_iota(jnp.int32, (S5_BLOCK_CH, S5_BLOCK_STATES), 1)
    same_group = (rows // S5_GROUP) == (cols // S5_STATE)
    tiled = jnp.concatenate([ref[d, 0, part]] * S5_LANE_TILES, axis=1)
    return jnp.where(same_group, tiled, 0.0)


def _in_proj_kernel(x_ref, g_ref, wa_ref, wb_ref, gb_ref, u_ref, bc_ref, q_ref, gate_ref,
                    h_scr):
    j = pl.program_id(2)

    @pl.when(j == 0)
    def _():
        h_scr[...] = _rms_norm(x_ref[...], g_ref[...]).astype(_bf16)

    pa = _dot(h_scr[...], wa_ref[...])
    pb = _dot(h_scr[...], wb_ref[...])

    @pl.when(j == 0)
    def _():
        for s in range(S5_SLABS):
            u_ref[s] = pa[:, s * V7X_LANES:(s + 1) * V7X_LANES].reshape(
                -1, S5_CHUNK, V7X_LANES)
        bc_ref[...] = pb.astype(_bf16)

    @pl.when(j == 1)
    def _():
        q_ref[...] = (pb * pa).astype(_bf16)

    @pl.when(j >= 2)
    def _():
        gate_ref[:, :IN_TILE_COLS] = jax.nn.sigmoid(pa + gb_ref[0, :, :IN_TILE_COLS]).astype(_bf16)
        gate_ref[:, IN_TILE_COLS:] = jax.nn.sigmoid(pb + gb_ref[0, :, IN_TILE_COLS:]).astype(_bf16)


def _in_proj(x3, g, w_in, gate_b, tile_rows):
    nb, t_len, _ = x3.shape
    n_t = t_len // tile_rows
    row = lambda b, t, j: (b, t, 0)
    col_a = lambda b, t, j: (0, jnp.where(j < 2, j, 2 * j))
    col_b = lambda b, t, j: (0, jnp.where(j < 2, j + 2, 2 * j + 1))
    gate_idx = lambda j: jnp.maximum(j - 2, 0)
    return pl.pallas_call(
        _in_proj_kernel,
        grid=(nb, n_t, 4),
        in_specs=[
            pl.BlockSpec((None, tile_rows, D_MODEL), row),
            pl.BlockSpec((1, D_MODEL), lambda b, t, j: (0, 0)),
            pl.BlockSpec((D_MODEL, IN_TILE_COLS), col_a),
            pl.BlockSpec((D_MODEL, IN_TILE_COLS), col_b),
            pl.BlockSpec((1, 1, D_MODEL), lambda b, t, j: (gate_idx(j), 0, 0)),
        ],
        out_specs=[
            pl.BlockSpec((S5_SLABS, tile_rows // S5_CHUNK, None, S5_CHUNK, V7X_LANES),
                         lambda b, t, j: (0, t, b, 0, 0)),
            pl.BlockSpec((None, tile_rows, CONV_WIDTH), row),
            pl.BlockSpec((None, tile_rows, CONV_WIDTH), row),
            pl.BlockSpec((None, tile_rows, D_MODEL), lambda b, t, j: (b, t, gate_idx(j))),
        ],
        out_shape=(jax.ShapeDtypeStruct(
                       (S5_SLABS, t_len // S5_CHUNK, nb, S5_CHUNK, V7X_LANES), _f32),
                   jax.ShapeDtypeStruct((nb, t_len, CONV_WIDTH), _bf16),
                   jax.ShapeDtypeStruct((nb, t_len, CONV_WIDTH), _bf16),
                   jax.ShapeDtypeStruct((nb, t_len, 2 * D_MODEL), _bf16)),
        scratch_shapes=[pltpu.VMEM((tile_rows, D_MODEL), _bf16)],
        compiler_params=_params(("arbitrary", "arbitrary", "arbitrary")),
        name="in_proj",
    )(x3, g, w_in, w_in, gate_b)


def _boundary_scan(x_scr, a_re, a_im, init, reverse):
    n_tiles = S5_CHUNK_ROWS // V7X_SUBLANES
    row = lax.broadcasted_iota(jnp.int32, (V7X_SUBLANES, V7X_LANES), 0)
    first = (row >= BATCH) if reverse else (row < BATCH)

    def step(ar, ai, sr, si, xr, xi):
        return ar * sr - ai * si + xr, ar * si + ai * sr + xi

    def body(i, carry):
        ii = (n_tiles - 1 - i) if reverse else i
        r0 = pl.multiple_of(ii * V7X_SUBLANES, V7X_SUBLANES)
        new = []
        for t in range(S5_LANE_TILES):
            re = slice(t * V7X_LANES, (t + 1) * V7X_LANES)
            im = slice(S5_BLOCK_STATES + t * V7X_LANES, S5_BLOCK_STATES + (t + 1) * V7X_LANES)
            cr, ci = carry[2 * t], carry[2 * t + 1]
            xr = x_scr[pl.ds(r0, V7X_SUBLANES), re]
            xi = x_scr[pl.ds(r0, V7X_SUBLANES), im]
            tr, ti = step(a_re[t], a_im[t], cr, ci, xr, xi)
            outr = jnp.where(first, cr, pltpu.roll(tr, BATCH, 0))
            outi = jnp.where(first, ci, pltpu.roll(ti, BATCH, 0))
            x_scr[pl.ds(r0, V7X_SUBLANES), re] = outr
            x_scr[pl.ds(r0, V7X_SUBLANES), im] = outi
            tr, ti = step(a_re[t], a_im[t], outr, outi, xr, xi)
            new.append(jnp.where(first, pltpu.roll(tr, BATCH, 0), tr))
            new.append(jnp.where(first, pltpu.roll(ti, BATCH, 0), ti))
        return tuple(new)

    lax.fori_loop(0, n_tiles, body, init)


def _s5_kernel(u_ref, um_ref, b_ref, c_ref, p_ref, d_ref, y_ref,
               ub_scr, w_scr, x_scr, sp_scr, tap_scr, tapf_scr, mid_scr, xm_scr):
    o = pl.program_id(1)
    ns = S5_BLOCK_STATES

    @pl.when(o == 0)
    def _():
        for j in range(S5_CHUNK):
            for h in range(SLABS_PER_BLOCK):
                c0 = j * S5_BLOCK_CH + h * V7X_LANES
                ub_scr[:, c0:c0 + V7X_LANES] = u_ref[
                    h, pl.ds(j, S5_CHUNK_ROWS, stride=S5_CHUNK), :].astype(_bf16)
        um = um_ref[0].astype(_bf16)
        for d in range(N_DIR):
            w_re = _dense_map(b_ref, d, 0)
            w_im = _dense_map(b_ref, d, 1)
            for part in range(2):
                cols = slice(part * ns, (part + 1) * ns)
                for j in range(S5_CHUNK):
                    m = (S5_CHUNK - 1 - j) if d == 0 else j
                    pr = p_ref[0, d, m, 0:1, :]
                    pi = p_ref[0, d, m, 1:2, :]
                    g = (w_re * pr - w_im * pi) if part == 0 else (w_re * pi + w_im * pr)
                    w_scr[j * S5_BLOCK_CH:(j + 1) * S5_BLOCK_CH, :] = g.astype(_bf16)
                x_scr[:, cols] = _dot(ub_scr[...], w_scr[...])
                taps = _dot_nt(w_scr[...], _dense_map(c_ref, d, part).astype(_bf16))
                if part == 0:
                    tapf_scr[...] = taps
                else:
                    tapf_scr[...] += taps
                if d == 0:
                    xm_scr[:, cols] = _dot(um, w_scr[...])

            edge = (S5_CHUNK - 1) * S5_BLOCK_CH
            if d == 0:
                tap_scr[0:edge, :] = tapf_scr[0:edge, :].astype(_bf16)
                mid_scr[...] = tapf_scr[edge:, :]
            else:
                tap_scr[edge:edge + S5_BLOCK_CH, :] = (
                    mid_scr[...] + tapf_scr[0:S5_BLOCK_CH, :]).astype(_bf16)
                tap_scr[edge + S5_BLOCK_CH:, :] = tapf_scr[S5_BLOCK_CH:, :].astype(_bf16)

            a_re, a_im, init = [], [], []
            for t in range(S5_LANE_TILES):
                lanes = slice(t * V7X_LANES, (t + 1) * V7X_LANES)
                lanes_im = slice(ns + t * V7X_LANES, ns + (t + 1) * V7X_LANES)
                ar = p_ref[0, d, S5_CHUNK, 0:1, lanes]
                ai = p_ref[0, d, S5_CHUNK, 1:2, lanes]
                a_re.append(jnp.broadcast_to(ar, (V7X_SUBLANES, V7X_LANES)))
                a_im.append(jnp.broadcast_to(ai, (V7X_SUBLANES, V7X_LANES)))
                if d == 0:
                    sr = jnp.zeros((1, V7X_LANES), _f32)
                    si = jnp.zeros((1, V7X_LANES), _f32)
                    for c in range(META_CHUNKS):
                        xr = xm_scr[c:c + 1, lanes]
                        xi = xm_scr[c:c + 1, lanes_im]
                        sr, si = ar * sr - ai * si + xr, ar * si + ai * sr + xi
                    init.append(jnp.broadcast_to(sr, (V7X_SUBLANES, V7X_LANES)))
                    init.append(jnp.broadcast_to(si, (V7X_SUBLANES, V7X_LANES)))
                else:
                    init.append(jnp.zeros((V7X_SUBLANES, V7X_LANES), _f32))
                    init.append(jnp.zeros((V7X_SUBLANES, V7X_LANES), _f32))
            _boundary_scan(x_scr, a_re, a_im, tuple(init), reverse=(d == 1))
            sp_scr[d] = x_scr[...].astype(_bf16)

    def out_weights(d, m):
        pw = p_ref[0, d, m]
        pr, pi = pw[0:1, :], pw[1:2, :]
        c_re = _dense_map(c_ref, d, 0)
        c_imn = _dense_map(c_ref, d, 1)
        return jnp.concatenate([c_re * pr + c_imn * pi, c_imn * pr - c_re * pi],
                               axis=1).astype(_bf16)

    tap0 = pl.multiple_of((S5_CHUNK - 1 - o) * S5_BLOCK_CH, S5_BLOCK_CH)
    y = _dot_nt(sp_scr[0], out_weights(0, o + 1))
    y += _dot_nt(sp_scr[1], out_weights(1, S5_CHUNK - o))
    y += _dot(ub_scr[...], tap_scr[pl.ds(tap0, S5_CHUNK_COLS), :])
    for h in range(SLABS_PER_BLOCK):
        lanes = slice(h * V7X_LANES, (h + 1) * V7X_LANES)
        skip = u_ref[h, pl.ds(o, S5_CHUNK_ROWS, stride=S5_CHUNK), :] * d_ref[0, :, lanes]
        y_ref[h, 0] = y[:, lanes] + skip


def _s5(u, u_meta, b_maps, c_maps, ptab, d_skip):
    const = functools.partial(pl.BlockSpec, pipeline_mode=pl.Buffered(1))
    maps = const((N_DIR, 1, 2, S5_BLOCK_CH, V7X_LANES), lambda k, o: (0, k, 0, 0, 0))
    return pl.pallas_call(
        _s5_kernel,
        grid=(S5_BLOCKS, S5_CHUNK),
        in_specs=[
            const((SLABS_PER_BLOCK, S5_CHUNK_ROWS * S5_CHUNK, V7X_LANES),
                  lambda k, o: (k, 0, 0)),
            pl.BlockSpec((1, V7X_SUBLANES, S5_CHUNK_COLS), lambda k, o: (k, 0, 0)),
            maps,
            maps,
            pl.BlockSpec((1, N_DIR, S5_CHUNK + 1, 2, S5_BLOCK_STATES),
                         lambda k, o: (k, 0, 0, 0, 0)),
            pl.BlockSpec((1, 1, S5_BLOCK_CH), lambda k, o: (k, 0, 0)),
        ],
        out_specs=pl.BlockSpec((SLABS_PER_BLOCK, 1, S5_CHUNK_ROWS, V7X_LANES),
                               lambda k, o: (k, o, 0, 0)),
        out_shape=jax.ShapeDtypeStruct((S5_SLABS, S5_CHUNK, S5_CHUNK_ROWS, V7X_LANES), _f32),
        scratch_shapes=[
            pltpu.VMEM((S5_CHUNK_ROWS, S5_CHUNK_COLS), _bf16),
            pltpu.VMEM((S5_CHUNK_COLS, S5_BLOCK_STATES), _bf16),
            pltpu.VMEM((S5_CHUNK_ROWS, 2 * S5_BLOCK_STATES), _f32),
            pltpu.VMEM((N_DIR, S5_CHUNK_ROWS, 2 * S5_BLOCK_STATES), _bf16),
            pltpu.VMEM((S5_TAPS * S5_BLOCK_CH, S5_BLOCK_CH), _bf16),
            pltpu.VMEM((S5_CHUNK_COLS, S5_BLOCK_CH), _f32),
            pltpu.VMEM((S5_BLOCK_CH, S5_BLOCK_CH), _f32),
            pltpu.VMEM((V7X_SUBLANES, 2 * S5_BLOCK_STATES), _f32),
        ],
        compiler_params=_params(("arbitrary", "arbitrary")),
        name="s5",
    )(u, u_meta, b_maps, c_maps, ptab, d_skip)


def _mixer_kernel(h_ref, y_ref, q_ref, qp_ref, qn_ref, qm_ref, bc_ref, gate_ref,
                  wglu_ref, bglu_ref, wsup_ref, cw_ref, cb_ref, wcup_ref, wo_ref,
                  g_ref, o_ref, ys_scr, q_scr):
    t = pl.program_id(0)
    last = pl.num_programs(0) - 1
    tt = MIX_TILE_T
    tm = BATCH * tt
    n_c = tt // S5_CHUNK

    for s in range(S5_SLABS):
        for o in range(S5_CHUNK):
            for b in range(BATCH):
                ys_scr[s, pl.ds(b * tt + o, n_c, stride=S5_CHUNK), :] = (
                    y_ref[s, o, pl.ds(b, n_c, stride=BATCH), :])
    ys = jnp.concatenate([ys_scr[s] for s in range(S5_SLABS)], axis=1)
    ys = jax.nn.gelu(ys)
    glu = _dot(ys.astype(_bf16), wglu_ref[...]) + bglu_ref[...]
    ys = ys * jax.nn.sigmoid(glu)
    y_s = _dot(ys.astype(_bf16), wsup_ref[...])

    vs = []
    for b in range(BATCH):
        prev = jnp.where(t == 0, qm_ref[...], qp_ref[b]).astype(_f32)
        nxt = jnp.where(t == last, 0.0, qn_ref[b].astype(_f32))
        q_scr[b, 0:HALO_ROWS, :] = prev
        q_scr[b, HALO_ROWS:HALO_ROWS + tt, :] = q_ref[b].astype(_f32)
        q_scr[b, HALO_ROWS + tt:, :] = nxt
        vs.append(cw_ref[0:1, :] * q_scr[b, HALO_ROWS - 1:HALO_ROWS - 1 + tt, :]
                  + cw_ref[1:2, :] * q_scr[b, HALO_ROWS:HALO_ROWS + tt, :]
                  + cw_ref[2:3, :] * q_scr[b, HALO_ROWS + 1:HALO_ROWS + 1 + tt, :]
                  + cb_ref[...])
    v = jnp.concatenate(vs, axis=0)
    bc = bc_ref[...].reshape(tm, CONV_WIDTH).astype(_f32)
    y_c = _dot((bc * v).astype(_bf16), wcup_ref[...])

    gate = gate_ref[...].reshape(tm, 2 * D_MODEL)
    merged = gate[:, :D_MODEL].astype(_f32) * y_s + gate[:, D_MODEL:].astype(_f32) * y_c
    mixed = _dot(merged.astype(_bf16), wo_ref[...])
    out = h_ref[...].reshape(tm, D_MODEL) + _rms_norm(mixed, g_ref[...])
    o_ref[...] = out.reshape(BATCH, tt, D_MODEL)


def _mixer(x, y, q, q_meta, bc, gate, w_glu, b_glu, w_s_up, conv_w, conv_b, w_c_up, w_o,
           g_post):
    tt = MIX_TILE_T
    n_t = SEQ // tt
    halo_per_tile = tt // HALO_ROWS
    n_halo = SEQ // HALO_ROWS
    tile = lambda t: (0, t, 0)
    resident = functools.partial(pl.BlockSpec, index_map=lambda t: (0, 0),
                                 pipeline_mode=pl.Buffered(1))
    return pl.pallas_call(
        _mixer_kernel,
        grid=(n_t,),
        in_specs=[
            pl.BlockSpec((BATCH, tt, D_MODEL), tile),
            pl.BlockSpec((S5_SLABS, S5_CHUNK, tt // S5_CHUNK * BATCH, V7X_LANES),
                         lambda t: (0, 0, t, 0)),
            pl.BlockSpec((BATCH, tt, CONV_WIDTH), tile),
            pl.BlockSpec((BATCH, HALO_ROWS, CONV_WIDTH),
                         lambda t: (0, jnp.maximum(t * halo_per_tile - 1, 0), 0)),
            pl.BlockSpec((BATCH, HALO_ROWS, CONV_WIDTH),
                         lambda t: (0, jnp.minimum((t + 1) * halo_per_tile, n_halo - 1), 0)),
            resident((HALO_ROWS, CONV_WIDTH)),
            pl.BlockSpec((BATCH, tt, CONV_WIDTH), tile),
            pl.BlockSpec((BATCH, tt, 2 * D_MODEL), tile),
            resident((S5_WIDTH, S5_WIDTH)),
            resident((1, S5_WIDTH)),
            resident((S5_WIDTH, D_MODEL)),
            resident((3, CONV_WIDTH)),
            resident((1, CONV_WIDTH)),
            resident((CONV_WIDTH, D_MODEL)),
            resident((D_MODEL, D_MODEL)),
            resident((1, D_MODEL)),
        ],
        out_specs=pl.BlockSpec((BATCH, tt, D_MODEL), tile),
        out_shape=jax.ShapeDtypeStruct((BATCH, SEQ, D_MODEL), _f32),
        scratch_shapes=[pltpu.VMEM((S5_SLABS, BATCH * tt, V7X_LANES), _f32),
                        pltpu.VMEM((BATCH, tt + 2 * HALO_ROWS, CONV_WIDTH), _f32)],
        compiler_params=_params(("arbitrary",)),
        name="mixer_tail",
    )(x, y, q, q, q, q_meta, bc, gate, w_glu, b_glu, w_s_up, conv_w, conv_b, w_c_up, w_o,
      g_post)


def _ffn_kernel(h_ref, gpre_ref, wg_ref, wu_ref, wout_ref, gpost_ref, o_ref, hb_scr, acc_scr):
    j = pl.program_id(1)

    @pl.when(j == 0)
    def _():
        hb_scr[...] = _rms_norm(h_ref[...], gpre_ref[...]).astype(_bf16)
        acc_scr[...] = jnp.zeros_like(acc_scr)

    hb = hb_scr[...]
    gate = _dot(hb, wg_ref[...])
    up = _dot(hb, wu_ref[...])
    act = (jax.nn.silu(gate) * up).astype(_bf16)
    acc_scr[...] += _dot(act, wout_ref[...])

    @pl.when(j == pl.num_programs(1) - 1)
    def _():
        o_ref[...] = h_ref[...] + _rms_norm(acc_scr[...], gpost_ref[...])


def _ffn(h1, g_pre, w_in, w_out, g_post):
    tm, th = FFN_TILE_ROWS, FFN_TILE_HIDDEN
    n_hidden = FFN_HIDDEN // th
    return pl.pallas_call(
        _ffn_kernel,
        grid=(ROWS // tm, n_hidden),
        in_specs=[
            pl.BlockSpec((tm, D_MODEL), lambda i, j: (i, 0)),
            pl.BlockSpec((1, D_MODEL), lambda i, j: (0, 0)),
            pl.BlockSpec((D_MODEL, th), lambda i, j: (0, j)),
            pl.BlockSpec((D_MODEL, th), lambda i, j: (0, j + n_hidden)),
            pl.BlockSpec((th, D_MODEL), lambda i, j: (j, 0)),
            pl.BlockSpec((1, D_MODEL), lambda i, j: (0, 0)),
        ],
        out_specs=pl.BlockSpec((tm, D_MODEL), lambda i, j: (i, 0)),
        out_shape=jax.ShapeDtypeStruct((ROWS, D_MODEL), _f32),
        scratch_shapes=[pltpu.VMEM((tm, D_MODEL), _bf16), pltpu.VMEM((tm, D_MODEL), _f32)],
        compiler_params=_params(("arbitrary", "arbitrary")),
        name="ffn",
    )(h1, g_pre, w_in, w_in, w_out, g_post)


def kernel(x, meta, g_mix_pre, g_mix_post, g_ffn_pre, g_ffn_post, w_in, gate_b, lam_re, lam_im,
           log_dt, b_re, b_im, c_re, c_im, d_skip, w_glu, b_glu, w_s_up, conv_w, conv_b, w_c_up,
           w_o, w_ffn_in, w_ffn_out):
    l = 0
    w_in_b = w_in[l].astype(_bf16)
    gate_b2 = gate_b[l].reshape(2, 1, D_MODEL)
    g_pre = g_mix_pre[l].reshape(1, D_MODEL)

    ptab, bz_re, bz_im = _discretise(lam_re[l], lam_im[l], log_dt[l], b_re[l], b_im[l])
    dg = N_DIR * S5_GROUPS
    b_maps = _compact_maps(bz_re, bz_im)
    c_maps = _compact_maps(c_re[l].reshape(dg, S5_GROUP, S5_STATE),
                           -c_im[l].reshape(dg, S5_GROUP, S5_STATE))
    ptab = ptab.reshape(S5_CHUNK + 1, 2, N_DIR, S5_BLOCKS, S5_BLOCK_STATES)
    ptab = ptab.transpose(3, 2, 0, 1, 4)

    u5, bc, q, gate = _in_proj(x, g_pre, w_in_b, gate_b2, IN_TILE_ROWS)
    um5, _, q_meta, _ = _in_proj(meta.astype(x.dtype)[None], g_pre, w_in_b, gate_b2, N_META)

    u = u5.reshape(S5_SLABS, S5_CHUNK_ROWS * S5_CHUNK, V7X_LANES)
    u_meta = um5.reshape(S5_BLOCKS, SLABS_PER_BLOCK, META_CHUNKS, S5_CHUNK, V7X_LANES)
    u_meta = u_meta.transpose(0, 2, 3, 1, 4).reshape(S5_BLOCKS, META_CHUNKS, S5_CHUNK_COLS)
    u_meta = jnp.pad(u_meta, ((0, 0), (0, V7X_SUBLANES - META_CHUNKS), (0, 0)))
    y = _s5(u, u_meta, b_maps, c_maps, ptab, d_skip[l].reshape(S5_BLOCKS, 1, S5_BLOCK_CH))

    h1 = _mixer(x, y, q, q_meta[0], bc, gate, w_glu[l].astype(_bf16),
                b_glu[l].reshape(1, S5_WIDTH), w_s_up[l].astype(_bf16), conv_w[l],
                conv_b[l].reshape(1, CONV_WIDTH), w_c_up[l].astype(_bf16),
                w_o[l].astype(_bf16), g_mix_post[l].reshape(1, D_MODEL))

    h2 = _ffn(h1.reshape(ROWS, D_MODEL), g_ffn_pre[l].reshape(1, D_MODEL),
              w_ffn_in[l].astype(_bf16), w_ffn_out[l].astype(_bf16),
              g_ffn_post[l].reshape(1, D_MODEL))
    return h2.reshape(BATCH, SEQ, D_MODEL)
```

```python
import functools
import math

import jax
import jax.numpy as jnp
from jax import lax
from jax.experimental import pallas as pl
from jax.experimental.pallas import tpu as pltpu

D_MODEL = 2048
BATCH = 4
SEQ = 2048
N_META = 16
S5_WIDTH = 1024
S5_GROUP = 16
S5_GROUPS = S5_WIDTH // S5_GROUP
S5_STATE = 64
N_DIR = 2
CONV_WIDTH = 1024
FFN_HIDDEN = ((math.ceil(8 * D_MODEL / 3) + 255) // 256) * 256
IN_COLS = S5_WIDTH + 3 * CONV_WIDTH + 2 * D_MODEL
RMS_EPS = 1e-6
LAM_RE_MAX = -1e-4

ROWS = SEQ * BATCH

V7X_SUBLANES = 8
V7X_LANES = 128
V7X_MXU_DIM = 256
V7X_VMEM_LIMIT_BYTES = 56 * 1024 * 1024

S5_SLABS = S5_WIDTH // V7X_LANES
SLAB_GROUPS = V7X_LANES // S5_GROUP
SLAB_STATES = SLAB_GROUPS * S5_STATE
SLAB_LANE_TILES = SLAB_STATES // V7X_LANES
S5_CHUNK = V7X_SUBLANES
S5_CHUNKS = SEQ // S5_CHUNK
S5_CHUNK_ROWS = S5_CHUNKS * BATCH
S5_CHUNK_COLS = S5_CHUNK * V7X_LANES
S5_TAPS = 2 * S5_CHUNK - 1
S5_OFFSETS_PER_STEP = V7X_MXU_DIM // V7X_LANES
META_CHUNKS = N_META // S5_CHUNK
STATE_REPEATS = V7X_LANES // S5_STATE

IN_TILE_ROWS = 512
IN_STEPS = 4
IN_SEG_COLS = S5_WIDTH // IN_STEPS
MIX_TILE_T = 64
FFN_TILE_ROWS = 512
FFN_TILE_HIDDEN = 512
HALO_ROWS = 16

_f32 = jnp.float32
_bf16 = jnp.bfloat16


def _rms_norm(xf, g):
    r = lax.rsqrt(jnp.mean(xf * xf, axis=-1, keepdims=True) + RMS_EPS)
    return xf * r * g


def _params(sem):
    return pltpu.CompilerParams(dimension_semantics=sem,
                                vmem_limit_bytes=V7X_VMEM_LIMIT_BYTES)


def _dot(a, b):
    return jnp.dot(a, b, preferred_element_type=_f32)


def _dot_nt(a, b):
    return lax.dot_general(a, b, (((1,), (1,)), ((), ())), preferred_element_type=_f32)


def _discretise_kernel(lam_re_ref, lam_im_ref, log_dt_ref, b_re_ref, b_im_ref,
                       pow_ref, bz_re_ref, bz_im_ref):
    lr = jnp.minimum(lam_re_ref[...], LAM_RE_MAX)
    li = lam_im_ref[...]
    dt = jnp.exp(log_dt_ref[...])
    mag = jnp.exp(lr * dt)
    ar = mag * jnp.cos(li * dt)
    ai = mag * jnp.sin(li * dt)
    den = lr * lr + li * li
    nr = ar - 1.0
    zr = (nr * lr + ai * li) / den
    zi = (ai * lr - nr * li) / den
    pr = jnp.ones_like(ar)
    pi = jnp.zeros_like(ar)
    for m in range(S5_CHUNK + 1):
        pow_ref[m, 0] = pr
        pow_ref[m, 1] = pi
        pr, pi = pr * ar - pi * ai, pr * ai + pi * ar
    b_re = b_re_ref[...]
    b_im = b_im_ref[...]
    bz_re_ref[...] = zr[:, None, :] * b_re - zi[:, None, :] * b_im
    bz_im_ref[...] = zr[:, None, :] * b_im + zi[:, None, :] * b_re


def _discretise(lam_re, lam_im, log_dt, b_re, b_im):
    dg = N_DIR * S5_GROUPS
    b_re_t = jnp.swapaxes(b_re, -1, -2).reshape(dg, S5_GROUP, S5_STATE)
    b_im_t = jnp.swapaxes(b_im, -1, -2).reshape(dg, S5_GROUP, S5_STATE)
    return pl.pallas_call(
        _discretise_kernel,
        out_shape=(jax.ShapeDtypeStruct((S5_CHUNK + 1, 2, dg, S5_STATE), _f32),
                   jax.ShapeDtypeStruct((dg, S5_GROUP, S5_STATE), _f32),
                   jax.ShapeDtypeStruct((dg, S5_GROUP, S5_STATE), _f32)),
        name="s5_discretise",
    )(lam_re.reshape(dg, S5_STATE), lam_im.reshape(dg, S5_STATE),
      log_dt.reshape(dg, 1), b_re_t, b_im_t)


def _compact_maps(w_re, w_im):
    def part(w):
        w = w.reshape(N_DIR, S5_SLABS, V7X_LANES, S5_STATE)
        return jnp.concatenate([w] * STATE_REPEATS, axis=-1)
    return jnp.stack([part(w_re), part(w_im)], axis=2)


def _dense_map(ref, d, part):
    rows = lax.broadcasted_iota(jnp.int32, (V7X_LANES, SLAB_STATES), 0)
    cols = lax.broadcasted_iota(jnp.int32, (V7X_LANES, SLAB_STATES), 1)
    same_group = (rows // S5_GROUP) == (cols // S5_STATE)
    tiled = jnp.concatenate([ref[d, 0, part]] * SLAB_LANE_TILES, axis=1)
    return jnp.where(same_group, tiled, 0.0)


def _in_proj_kernel(x_ref, g_ref, wu_ref, wx_ref, wb_ref, wc_ref, wgs_ref, wgc_ref,
                    gbs_ref, gbc_ref, u_ref, bc_ref, q_ref, gs_ref, gc_ref, h_scr):
    @pl.when(pl.program_id(2) == 0)
    def _():
        h_scr[...] = _rms_norm(x_ref[...], g_ref[...]).astype(_bf16)

    h = h_scr[...]
    u = _dot(h, wu_ref[...])
    for s in range(IN_SEG_COLS // V7X_LANES):
        u_ref[s] = u[:, s * V7X_LANES:(s + 1) * V7X_LANES].reshape(-1, S5_CHUNK, V7X_LANES)
    bc_ref[...] = _dot(h, wb_ref[...]).astype(_bf16)
    q_ref[...] = (_dot(h, wc_ref[...]) * _dot(h, wx_ref[...])).astype(_bf16)
    gs_ref[...] = jax.nn.sigmoid(_dot(h, wgs_ref[...]) + gbs_ref[0]).astype(_bf16)
    gc_ref[...] = jax.nn.sigmoid(_dot(h, wgc_ref[...]) + gbc_ref[0]).astype(_bf16)


def _in_proj(x3, g, w_in, gate_b, tile_rows):
    nb, t_len, _ = x3.shape
    n_t = t_len // tile_rows
    seg, gseg = IN_SEG_COLS, 2 * IN_SEG_COLS
    tile = lambda b, t, j: (b, t, j)
    w_seg = lambda n: pl.BlockSpec((D_MODEL, seg), lambda b, t, j: (0, n * IN_STEPS + j))
    w_gate = lambda n: pl.BlockSpec((D_MODEL, gseg), lambda b, t, j: (0, (2 + n) * IN_STEPS + j))
    gb = lambda n: pl.BlockSpec((1, 1, gseg), lambda b, t, j: (n, 0, j))
    return pl.pallas_call(
        _in_proj_kernel,
        grid=(nb, n_t, IN_STEPS),
        in_specs=[
            pl.BlockSpec((None, tile_rows, D_MODEL), lambda b, t, j: (b, t, 0)),
            pl.BlockSpec((1, D_MODEL), lambda b, t, j: (0, 0)),
            w_seg(0), w_seg(1), w_seg(2), w_seg(3), w_gate(0), w_gate(1), gb(0), gb(1),
        ],
        out_specs=[
            pl.BlockSpec((seg // V7X_LANES, tile_rows // S5_CHUNK, None, S5_CHUNK, V7X_LANES),
                         lambda b, t, j: (j, t, b, 0, 0)),
            pl.BlockSpec((None, tile_rows, seg), tile),
            pl.BlockSpec((None, tile_rows, seg), tile),
            pl.BlockSpec((None, tile_rows, gseg), tile),
            pl.BlockSpec((None, tile_rows, gseg), tile),
        ],
        out_shape=(jax.ShapeDtypeStruct(
                       (S5_SLABS, t_len // S5_CHUNK, nb, S5_CHUNK, V7X_LANES), _f32),
                   jax.ShapeDtypeStruct((nb, t_len, CONV_WIDTH), _bf16),
                   jax.ShapeDtypeStruct((nb, t_len, CONV_WIDTH), _bf16),
                   jax.ShapeDtypeStruct((nb, t_len, D_MODEL), _bf16),
                   jax.ShapeDtypeStruct((nb, t_len, D_MODEL), _bf16)),
        scratch_shapes=[pltpu.VMEM((tile_rows, D_MODEL), _bf16)],
        compiler_params=_params(("arbitrary", "arbitrary", "arbitrary")),
        name="in_proj",
    )(x3, g, w_in, w_in, w_in, w_in, w_in, w_in, gate_b, gate_b)


def _boundary_scan(x_scr, a_re, a_im, init):
    n_tiles = S5_CHUNK_ROWS // V7X_SUBLANES
    row = lax.broadcasted_iota(jnp.int32, (V7X_SUBLANES, V7X_LANES), 0)

    def step(ar, ai, sr, si, xr, xi):
        return ar * sr - ai * si + xr, ar * si + ai * sr + xi

    def body(i, carry):
        new = []
        for d in range(N_DIR):
            first = (row < BATCH) if d == 0 else (row >= BATCH)
            ii = i if d == 0 else (n_tiles - 1 - i)
            r0 = pl.multiple_of(ii * V7X_SUBLANES, V7X_SUBLANES)
            for t in range(SLAB_LANE_TILES):
                n = d * SLAB_LANE_TILES + t
                re = slice(t * V7X_LANES, (t + 1) * V7X_LANES)
                im = slice(SLAB_STATES + t * V7X_LANES, SLAB_STATES + (t + 1) * V7X_LANES)
                cr, ci = carry[2 * n], carry[2 * n + 1]
                xr = x_scr[d, pl.ds(r0, V7X_SUBLANES), re]
                xi = x_scr[d, pl.ds(r0, V7X_SUBLANES), im]
                tr, ti = step(a_re[n], a_im[n], cr, ci, xr, xi)
                outr = jnp.where(first, cr, pltpu.roll(tr, BATCH, 0))
                outi = jnp.where(first, ci, pltpu.roll(ti, BATCH, 0))
                x_scr[d, pl.ds(r0, V7X_SUBLANES), re] = outr
                x_scr[d, pl.ds(r0, V7X_SUBLANES), im] = outi
                tr, ti = step(a_re[n], a_im[n], outr, outi, xr, xi)
                new.append(jnp.where(first, pltpu.roll(tr, BATCH, 0), tr))
                new.append(jnp.where(first, pltpu.roll(ti, BATCH, 0), ti))
        return tuple(new)

    lax.fori_loop(0, n_tiles, body, init)


def _s5_kernel(u_ref, um_ref, b_ref, c_ref, p_ref, d_ref, y_ref,
               ub_scr, w_scr, x_scr, sp_scr, tap_scr, tapf_scr, mid_scr, xm_scr):
    step = pl.program_id(1)
    ns = SLAB_STATES
    blk = V7X_LANES

    @pl.when(step == 0)
    def _():
        for j in range(S5_CHUNK):
            ub_scr[:, j * blk:(j + 1) * blk] = u_ref[
                0, pl.ds(j, S5_CHUNK_ROWS, stride=S5_CHUNK), :].astype(_bf16)
        um = um_ref[0].astype(_bf16)
        for d in range(N_DIR):
            w_re = _dense_map(b_ref, d, 0)
            w_im = _dense_map(b_ref, d, 1)
            for part in range(2):
                cols = slice(part * ns, (part + 1) * ns)
                for j in range(S5_CHUNK):
                    m = (S5_CHUNK - 1 - j) if d == 0 else j
                    pr = p_ref[0, d, m, 0:1, :]
                    pi = p_ref[0, d, m, 1:2, :]
                    g = (w_re * pr - w_im * pi) if part == 0 else (w_re * pi + w_im * pr)
                    w_scr[j * blk:(j + 1) * blk, :] = g.astype(_bf16)
                x_scr[d, :, cols] = _dot(ub_scr[...], w_scr[...])
                taps = _dot_nt(w_scr[...], _dense_map(c_ref, d, part).astype(_bf16))
                if part == 0:
                    tapf_scr[...] = taps
                else:
                    tapf_scr[...] += taps
                if d == 0:
                    xm_scr[:, cols] = _dot(um, w_scr[...])

            edge = (S5_CHUNK - 1) * blk
            if d == 0:
                tap_scr[0:edge, :] = tapf_scr[0:edge, :].astype(_bf16)
                mid_scr[...] = tapf_scr[edge:, :]
            else:
                tap_scr[edge:edge + blk, :] = (mid_scr[...] + tapf_scr[0:blk, :]).astype(_bf16)
                tap_scr[edge + blk:, :] = tapf_scr[blk:, :].astype(_bf16)

        a_re, a_im, init = [], [], []
        for d in range(N_DIR):
            for t in range(SLAB_LANE_TILES):
                lanes = slice(t * V7X_LANES, (t + 1) * V7X_LANES)
                lanes_im = slice(ns + t * V7X_LANES, ns + (t + 1) * V7X_LANES)
                ar = p_ref[0, d, S5_CHUNK, 0:1, lanes]
                ai = p_ref[0, d, S5_CHUNK, 1:2, lanes]
                a_re.append(jnp.broadcast_to(ar, (V7X_SUBLANES, V7X_LANES)))
                a_im.append(jnp.broadcast_to(ai, (V7X_SUBLANES, V7X_LANES)))
                sr = jnp.zeros((1, V7X_LANES), _f32)
                si = jnp.zeros((1, V7X_LANES), _f32)
                if d == 0:
                    for c in range(META_CHUNKS):
                        xr = xm_scr[c:c + 1, lanes]
                        xi = xm_scr[c:c + 1, lanes_im]
                        sr, si = ar * sr - ai * si + xr, ar * si + ai * sr + xi
                init.append(jnp.broadcast_to(sr, (V7X_SUBLANES, V7X_LANES)))
                init.append(jnp.broadcast_to(si, (V7X_SUBLANES, V7X_LANES)))
        _boundary_scan(x_scr, a_re, a_im, tuple(init))
        for d in range(N_DIR):
            sp_scr[d] = x_scr[d].astype(_bf16)

    def out_weights(d, m):
        pw = p_ref[0, d, m]
        pr, pi = pw[0:1, :], pw[1:2, :]
        c_re = _dense_map(c_ref, d, 0)
        c_imn = _dense_map(c_ref, d, 1)
        return jnp.concatenate([c_re * pr + c_imn * pi, c_imn * pr - c_re * pi],
                               axis=1).astype(_bf16)

    o0 = step * S5_OFFSETS_PER_STEP
    offsets = range(S5_OFFSETS_PER_STEP)
    wy_f = jnp.concatenate([out_weights(0, o0 + e + 1) for e in offsets], axis=0)
    wy_b = jnp.concatenate([out_weights(1, S5_CHUNK - o0 - e) for e in offsets], axis=0)
    toeplitz = jnp.concatenate(
        [tap_scr[pl.ds(pl.multiple_of((S5_CHUNK - 1 - o0 - e) * blk, blk), S5_CHUNK_COLS), :]
         for e in offsets], axis=1)
    y = _dot_nt(sp_scr[0], wy_f)
    y += _dot_nt(sp_scr[1], wy_b)
    y += _dot(ub_scr[...], toeplitz)
    for e in offsets:
        skip = u_ref[0, pl.ds(o0 + e, S5_CHUNK_ROWS, stride=S5_CHUNK), :] * d_ref[0]
        y_ref[0, e] = y[:, e * blk:(e + 1) * blk] + skip


def _s5(u, u_meta, b_maps, c_maps, ptab, d_skip):
    maps = pl.BlockSpec((N_DIR, 1, 2, V7X_LANES, V7X_LANES), lambda s, o: (0, s, 0, 0, 0))
    return pl.pallas_call(
        _s5_kernel,
        grid=(S5_SLABS, S5_CHUNK // S5_OFFSETS_PER_STEP),
        in_specs=[
            pl.BlockSpec((1, S5_CHUNK_ROWS * S5_CHUNK, V7X_LANES), lambda s, o: (s, 0, 0)),
            pl.BlockSpec((1, V7X_SUBLANES, S5_CHUNK_COLS), lambda s, o: (s, 0, 0)),
            maps,
            maps,
            pl.BlockSpec((1, N_DIR, S5_CHUNK + 1, 2, SLAB_STATES),
                         lambda s, o: (s, 0, 0, 0, 0)),
            pl.BlockSpec((1, 1, V7X_LANES), lambda s, o: (s, 0, 0)),
        ],
        out_specs=pl.BlockSpec((1, S5_OFFSETS_PER_STEP, S5_CHUNK_ROWS, V7X_LANES),
                               lambda s, o: (s, o, 0, 0)),
        out_shape=jax.ShapeDtypeStruct((S5_SLABS, S5_CHUNK, S5_CHUNK_ROWS, V7X_LANES), _f32),
        scratch_shapes=[
            pltpu.VMEM((S5_CHUNK_ROWS, S5_CHUNK_COLS), _bf16),
            pltpu.VMEM((S5_CHUNK_COLS, SLAB_STATES), _bf16),
            pltpu.VMEM((N_DIR, S5_CHUNK_ROWS, 2 * SLAB_STATES), _f32),
            pltpu.VMEM((N_DIR, S5_CHUNK_ROWS, 2 * SLAB_STATES), _bf16),
            pltpu.VMEM((S5_TAPS * V7X_LANES, V7X_LANES), _bf16),
            pltpu.VMEM((S5_CHUNK_COLS, V7X_LANES), _f32),
            pltpu.VMEM((V7X_LANES, V7X_LANES), _f32),
            pltpu.VMEM((V7X_SUBLANES, 2 * SLAB_STATES), _f32),
        ],
        compiler_params=_params(("arbitrary", "arbitrary")),
        name="s5",
    )(u, u_meta, b_maps, c_maps, ptab, d_skip)


def _mixer_kernel(h_ref, y_ref, q_ref, qp_ref, qn_ref, qm_ref, bc_ref, gs_ref, gc_ref,
                  wglu_ref, bglu_ref, wsup_ref, cw_ref, cb_ref, wcup_ref, wo_ref,
                  g_ref, o_ref, ys_scr, q_scr):
    t = pl.program_id(0)
    last = pl.num_programs(0) - 1
    tt = MIX_TILE_T
    tm = BATCH * tt
    n_c = tt // S5_CHUNK

    for s in range(S5_SLABS):
        for o in range(S5_CHUNK):
            for b in range(BATCH):
                ys_scr[s, pl.ds(b * tt + o, n_c, stride=S5_CHUNK), :] = (
                    y_ref[s, o, pl.ds(b, n_c, stride=BATCH), :])
    ys = jnp.concatenate([ys_scr[s] for s in range(S5_SLABS)], axis=1)
    ys = jax.nn.gelu(ys)
    glu = _dot(ys.astype(_bf16), wglu_ref[...]) + bglu_ref[...]
    ys = ys * jax.nn.sigmoid(glu)
    y_s = _dot(ys.astype(_bf16), wsup_ref[...])

    vs = []
    for b in range(BATCH):
        prev = jnp.where(t == 0, qm_ref[...], qp_ref[b]).astype(_f32)
        nxt = jnp.where(t == last, 0.0, qn_ref[b].astype(_f32))
        q_scr[b, 0:HALO_ROWS, :] = prev
        q_scr[b, HALO_ROWS:HALO_ROWS + tt, :] = q_ref[b].astype(_f32)
        q_scr[b, HALO_ROWS + tt:, :] = nxt
        vs.append(cw_ref[0:1, :] * q_scr[b, HALO_ROWS - 1:HALO_ROWS - 1 + tt, :]
                  + cw_ref[1:2, :] * q_scr[b, HALO_ROWS:HALO_ROWS + tt, :]
                  + cw_ref[2:3, :] * q_scr[b, HALO_ROWS + 1:HALO_ROWS + 1 + tt, :]
                  + cb_ref[...])
    v = jnp.concatenate(vs, axis=0)
    bc = bc_ref[...].reshape(tm, CONV_WIDTH).astype(_f32)
    y_c = _dot((bc * v).astype(_bf16), wcup_ref[...])

    merged = (gs_ref[...].reshape(tm, D_MODEL).astype(_f32) * y_s
              + gc_ref[...].reshape(tm, D_MODEL).astype(_f32) * y_c)
    mixed = _dot(merged.astype(_bf16), wo_ref[...])
    out = h_ref[...].reshape(tm, D_MODEL) + _rms_norm(mixed, g_ref[...])
    o_ref[...] = out.reshape(BATCH, tt, D_MODEL)


def _mixer(x, y, q, q_meta, bc, gate_s, gate_c, w_glu, b_glu, w_s_up, conv_w, conv_b, w_c_up,
           w_o, g_post):
    tt = MIX_TILE_T
    n_t = SEQ // tt
    halo_per_tile = tt // HALO_ROWS
    n_halo = SEQ // HALO_ROWS
    tile = lambda t: (0, t, 0)
    resident = functools.partial(pl.BlockSpec, index_map=lambda t: (0, 0),
                                 pipeline_mode=pl.Buffered(1))
    return pl.pallas_call(
        _mixer_kernel,
        grid=(n_t,),
        in_specs=[
            pl.BlockSpec((BATCH, tt, D_MODEL), tile),
            pl.BlockSpec((S5_SLABS, S5_CHUNK, tt // S5_CHUNK * BATCH, V7X_LANES),
                         lambda t: (0, 0, t, 0)),
            pl.BlockSpec((BATCH, tt, CONV_WIDTH), tile),
            pl.BlockSpec((BATCH, HALO_ROWS, CONV_WIDTH),
                         lambda t: (0, jnp.maximum(t * halo_per_tile - 1, 0), 0)),
            pl.BlockSpec((BATCH, HALO_ROWS, CONV_WIDTH),
                         lambda t: (0, jnp.minimum((t + 1) * halo_per_tile, n_halo - 1), 0)),
            resident((HALO_ROWS, CONV_WIDTH)),
            pl.BlockSpec((BATCH, tt, CONV_WIDTH), tile),
            pl.BlockSpec((BATCH, tt, D_MODEL), tile),
            pl.BlockSpec((BATCH, tt, D_MODEL), tile),
            resident((S5_WIDTH, S5_WIDTH)),
            resident((1, S5_WIDTH)),
            resident((S5_WIDTH, D_MODEL)),
            resident((3, CONV_WIDTH)),
            resident((1, CONV_WIDTH)),
            resident((CONV_WIDTH, D_MODEL)),
            resident((D_MODEL, D_MODEL)),
            resident((1, D_MODEL)),
        ],
        out_specs=pl.BlockSpec((BATCH, tt, D_MODEL), tile),
        out_shape=jax.ShapeDtypeStruct((BATCH, SEQ, D_MODEL), _f32),
        scratch_shapes=[pltpu.VMEM((S5_SLABS, BATCH * tt, V7X_LANES), _f32),
                        pltpu.VMEM((BATCH, tt + 2 * HALO_ROWS, CONV_WIDTH), _f32)],
        compiler_params=_params(("arbitrary",)),
        name="mixer_tail",
    )(x, y, q, q, q, q_meta, bc, gate_s, gate_c, w_glu, b_glu, w_s_up, conv_w, conv_b, w_c_up,
      w_o, g_post)


def _ffn_kernel(h_ref, gpre_ref, wg_ref, wu_ref, wout_ref, gpost_ref, o_ref, hb_scr, acc_scr):
    j = pl.program_id(1)

    @pl.when(j == 0)
    def _():
        hb_scr[...] = _rms_norm(h_ref[...], gpre_ref[...]).astype(_bf16)
        acc_scr[...] = jnp.zeros_like(acc_scr)

    hb = hb_scr[...]
    gate = _dot(hb, wg_ref[...])
    up = _dot(hb, wu_ref[...])
    act = (jax.nn.silu(gate) * up).astype(_bf16)
    acc_scr[...] += _dot(act, wout_ref[...])

    @pl.when(j == pl.num_programs(1) - 1)
    def _():
        o_ref[...] = h_ref[...] + _rms_norm(acc_scr[...], gpost_ref[...])


def _ffn(h1, g_pre, w_in, w_out, g_post):
    tm, th = FFN_TILE_ROWS, FFN_TILE_HIDDEN
    n_hidden = FFN_HIDDEN // th
    return pl.pallas_call(
        _ffn_kernel,
        grid=(ROWS // tm, n_hidden),
        in_specs=[
            pl.BlockSpec((tm, D_MODEL), lambda i, j: (i, 0)),
            pl.BlockSpec((1, D_MODEL), lambda i, j: (0, 0)),
            pl.BlockSpec((D_MODEL, th), lambda i, j: (0, j)),
            pl.BlockSpec((D_MODEL, th), lambda i, j: (0, j + n_hidden)),
            pl.BlockSpec((th, D_MODEL), lambda i, j: (j, 0)),
            pl.BlockSpec((1, D_MODEL), lambda i, j: (0, 0)),
        ],
        out_specs=pl.BlockSpec((tm, D_MODEL), lambda i, j: (i, 0)),
        out_shape=jax.ShapeDtypeStruct((ROWS, D_MODEL), _f32),
        scratch_shapes=[pltpu.VMEM((tm, D_MODEL), _bf16), pltpu.VMEM((tm, D_MODEL), _f32)],
        compiler_params=_params(("arbitrary", "arbitrary")),
        name="ffn",
    )(h1, g_pre, w_in, w_in, w_out, g_post)


def kernel(x, meta, g_mix_pre, g_mix_post, g_ffn_pre, g_ffn_post, w_in, gate_b, lam_re, lam_im,
           log_dt, b_re, b_im, c_re, c_im, d_skip, w_glu, b_glu, w_s_up, conv_w, conv_b, w_c_up,
           w_o, w_ffn_in, w_ffn_out):
    l = 0
    w_in_b = w_in[l].astype(_bf16)
    gate_b2 = gate_b[l].reshape(2, 1, D_MODEL)
    g_pre = g_mix_pre[l].reshape(1, D_MODEL)

    ptab, bz_re, bz_im = _discretise(lam_re[l], lam_im[l], log_dt[l], b_re[l], b_im[l])
    dg = N_DIR * S5_GROUPS
    b_maps = _compact_maps(bz_re, bz_im)
    c_maps = _compact_maps(c_re[l].reshape(dg, S5_GROUP, S5_STATE),
                           -c_im[l].reshape(dg, S5_GROUP, S5_STATE))
    ptab = ptab.reshape(S5_CHUNK + 1, 2, N_DIR, S5_SLABS, SLAB_STATES)
    ptab = ptab.transpose(3, 2, 0, 1, 4)

    u5, bc, q, gate_s, gate_c = _in_proj(x, g_pre, w_in_b, gate_b2, IN_TILE_ROWS)
    um5, _, q_meta, _, _ = _in_proj(meta.astype(x.dtype)[None], g_pre, w_in_b, gate_b2, N_META)

    u = u5.reshape(S5_SLABS, S5_CHUNK_ROWS * S5_CHUNK, V7X_LANES)
    u_meta = jnp.pad(um5.reshape(S5_SLABS, META_CHUNKS, S5_CHUNK_COLS),
                     ((0, 0), (0, V7X_SUBLANES - META_CHUNKS), (0, 0)))
    y = _s5(u, u_meta, b_maps, c_maps, ptab, d_skip[l].reshape(S5_SLABS, 1, V7X_LANES))

    h1 = _mixer(x, y, q, q_meta[0], bc, gate_s, gate_c, w_glu[l].astype(_bf16),
                b_glu[l].reshape(1, S5_WIDTH), w_s_up[l].astype(_bf16), conv_w[l],
                conv_b[l].reshape(1, CONV_WIDTH), w_c_up[l].astype(_bf16),
                w_o[l].astype(_bf16), g_mix_post[l].reshape(1, D_MODEL))

    h2 = _ffn(h1.reshape(ROWS, D_MODEL), g_ffn_pre[l].reshape(1, D_MODEL),
              w_ffn_in[l].astype(_bf16), w_ffn_out[l].astype(_bf16),
              g_ffn_post[l].reshape(1, D_MODEL))
    return h2.reshape(BATCH, SEQ, D_MODEL)
```

```python
import functools
import math

import jax
import jax.numpy as jnp
from jax import lax
from jax.experimental import pallas as pl
from jax.experimental.pallas import tpu as pltpu

D_MODEL = 2048
BATCH = 4
SEQ = 2048
N_META = 16
S5_WIDTH = 1024
S5_GROUP = 16
S5_GROUPS = S5_WIDTH // S5_GROUP
S5_STATE = 64
N_DIR = 2
CONV_WIDTH = 1024
FFN_HIDDEN = ((math.ceil(8 * D_MODEL / 3) + 255) // 256) * 256
IN_COLS = S5_WIDTH + 3 * CONV_WIDTH + 2 * D_MODEL
RMS_EPS = 1e-6
LAM_RE_MAX = -1e-4

ROWS = SEQ * BATCH

V7X_SUBLANES = 8
V7X_LANES = 128
V7X_MXU_DIM = 256
V7X_VMEM_LIMIT_BYTES = 56 * 1024 * 1024

S5_SLABS = S5_WIDTH // V7X_LANES
SLAB_GROUPS = V7X_LANES // S5_GROUP
SLAB_STATES = SLAB_GROUPS * S5_STATE
SLAB_LANE_TILES = SLAB_STATES // V7X_LANES
S5_CHUNK = V7X_SUBLANES
S5_CHUNKS = SEQ // S5_CHUNK
S5_CHUNK_ROWS = S5_CHUNKS * BATCH
S5_CHUNK_COLS = S5_CHUNK * V7X_LANES
S5_TAPS = 2 * S5_CHUNK - 1
S5_OFFSETS_PER_STEP = V7X_MXU_DIM // V7X_LANES
META_CHUNKS = N_META // S5_CHUNK
STATE_REPEATS = V7X_LANES // S5_STATE

IN_TILE_ROWS = 1024
IN_STEPS = 8
IN_SEG_COLS = S5_WIDTH // IN_STEPS
CAST_TILE_ROWS = 128
MIX_TILE_T = 64
FFN_TILE_ROWS = 1024
FFN_TILE_HIDDEN = 256
HALO_ROWS = 16

_f32 = jnp.float32
_bf16 = jnp.bfloat16


def _rms_norm(xf, g):
    r = lax.rsqrt(jnp.mean(xf * xf, axis=-1, keepdims=True) + RMS_EPS)
    return xf * r * g


def _params(sem):
    return pltpu.CompilerParams(dimension_semantics=sem,
                                vmem_limit_bytes=V7X_VMEM_LIMIT_BYTES)


def _dot(a, b):
    return jnp.dot(a, b, preferred_element_type=_f32)


def _dot_nt(a, b):
    return lax.dot_general(a, b, (((1,), (1,)), ((), ())), preferred_element_type=_f32)


def _discretise_kernel(lam_re_ref, lam_im_ref, log_dt_ref, b_re_ref, b_im_ref,
                       pow_ref, bz_re_ref, bz_im_ref):
    lr = jnp.minimum(lam_re_ref[...], LAM_RE_MAX)
    li = lam_im_ref[...]
    dt = jnp.exp(log_dt_ref[...])
    mag = jnp.exp(lr * dt)
    ar = mag * jnp.cos(li * dt)
    ai = mag * jnp.sin(li * dt)
    den = lr * lr + li * li
    nr = ar - 1.0
    zr = (nr * lr + ai * li) / den
    zi = (ai * lr - nr * li) / den
    pr = jnp.ones_like(ar)
    pi = jnp.zeros_like(ar)
    for m in range(S5_CHUNK + 1):
        pow_ref[m, 0] = pr
        pow_ref[m, 1] = pi
        pr, pi = pr * ar - pi * ai, pr * ai + pi * ar
    b_re = b_re_ref[...]
    b_im = b_im_ref[...]
    bz_re_ref[...] = zr[:, None, :] * b_re - zi[:, None, :] * b_im
    bz_im_ref[...] = zr[:, None, :] * b_im + zi[:, None, :] * b_re


def _discretise(lam_re, lam_im, log_dt, b_re, b_im):
    dg = N_DIR * S5_GROUPS
    b_re_t = jnp.swapaxes(b_re, -1, -2).reshape(dg, S5_GROUP, S5_STATE)
    b_im_t = jnp.swapaxes(b_im, -1, -2).reshape(dg, S5_GROUP, S5_STATE)
    return pl.pallas_call(
        _discretise_kernel,
        out_shape=(jax.ShapeDtypeStruct((S5_CHUNK + 1, 2, dg, S5_STATE), _f32),
                   jax.ShapeDtypeStruct((dg, S5_GROUP, S5_STATE), _f32),
                   jax.ShapeDtypeStruct((dg, S5_GROUP, S5_STATE), _f32)),
        name="s5_discretise",
    )(lam_re.reshape(dg, S5_STATE), lam_im.reshape(dg, S5_STATE),
      log_dt.reshape(dg, 1), b_re_t, b_im_t)


def _compact_maps(w_re, w_im):
    def part(w):
        w = w.reshape(N_DIR, S5_SLABS, V7X_LANES, S5_STATE)
        return jnp.concatenate([w] * STATE_REPEATS, axis=-1)
    return jnp.stack([part(w_re), part(w_im)], axis=2)


def _dense_map(ref, d, part):
    rows = lax.broadcasted_iota(jnp.int32, (V7X_LANES, SLAB_STATES), 0)
    cols = lax.broadcasted_iota(jnp.int32, (V7X_LANES, SLAB_STATES), 1)
    same_group = (rows // S5_GROUP) == (cols // S5_STATE)
    tiled = jnp.concatenate([ref[d, 0, part]] * SLAB_LANE_TILES, axis=1)
    return jnp.where(same_group, tiled, 0.0)


def _w_in_pieces(j):
    seg, gseg = IN_SEG_COLS, 2 * IN_SEG_COLS
    pieces = [(n * S5_WIDTH + j * seg, seg) for n in range(4)]
    pieces += [(4 * S5_WIDTH + n * D_MODEL + j * gseg, gseg) for n in range(2)]
    return pieces


def _cast_w_in_kernel(w_ref, o_ref):
    dst = 0
    for j in range(IN_STEPS):
        for src, width in _w_in_pieces(j):
            o_ref[:, dst:dst + width] = w_ref[:, src:src + width].astype(_bf16)
            dst += width


def _cast_w_in(w):
    spec = pl.BlockSpec((CAST_TILE_ROWS, IN_COLS), lambda i: (i, 0))
    return pl.pallas_call(
        _cast_w_in_kernel,
        grid=(D_MODEL // CAST_TILE_ROWS,),
        in_specs=[spec],
        out_specs=spec,
        out_shape=jax.ShapeDtypeStruct(w.shape, _bf16),
        compiler_params=_params(("arbitrary",)),
        name="cast_w_in",
    )(w)


def _ffn_in_plan():
    th = FFN_TILE_HIDDEN
    plan = []
    for j in range(FFN_HIDDEN // th):
        plan.append((j * th, 2 * j * th, th))
        plan.append((FFN_HIDDEN + j * th, (2 * j + 1) * th, th))
    return plan


def _in_proj_kernel(x_ref, g_ref, w_ref, gbs_ref, gbc_ref, *rest, cast_plans):
    n_cast = len(cast_plans)
    cast_in = rest[:n_cast]
    u_ref, bc_ref, q_ref, gs_ref, gc_ref = rest[n_cast:n_cast + 5]
    cast_out = rest[n_cast + 5:2 * n_cast + 5]
    h_scr = rest[2 * n_cast + 5]
    seg = IN_SEG_COLS

    @pl.when(pl.program_id(2) == 0)
    def _():
        h_scr[...] = _rms_norm(x_ref[...], g_ref[...]).astype(_bf16)

    proj = _dot(h_scr[...], w_ref[...])
    for s in range(seg // V7X_LANES):
        u_ref[s] = proj[:, s * V7X_LANES:(s + 1) * V7X_LANES].reshape(-1, S5_CHUNK, V7X_LANES)
    bc_ref[...] = proj[:, 2 * seg:3 * seg].astype(_bf16)
    q_ref[...] = (proj[:, 3 * seg:4 * seg] * proj[:, seg:2 * seg]).astype(_bf16)
    gs_ref[...] = jax.nn.sigmoid(proj[:, 4 * seg:6 * seg] + gbs_ref[0]).astype(_bf16)
    gc_ref[...] = jax.nn.sigmoid(proj[:, 6 * seg:8 * seg] + gbc_ref[0]).astype(_bf16)
    for src, dst, plan in zip(cast_in, cast_out, cast_plans):
        for s0, d0, width in plan:
            dst[:, d0:d0 + width] = src[:, s0:s0 + width].astype(_bf16)


def _in_proj(x3, g, w_in_b, gate_b, tile_rows, cast=()):
    nb, t_len, _ = x3.shape
    n_t = t_len // tile_rows
    n_steps = nb * n_t * IN_STEPS
    seg, gseg = IN_SEG_COLS, 2 * IN_SEG_COLS
    tile = lambda b, t, j: (b, t, j)
    step = lambda b, t, j: ((b * n_t + t) * IN_STEPS + j, 0)
    cast_w = [w for w, _ in cast]
    cast_plans = tuple(tuple(plan) if plan else ((0, 0, w.shape[1]),) for w, plan in cast)
    cast_specs = [pl.BlockSpec((w.shape[0] // n_steps, w.shape[1]), step) for w in cast_w]
    gb = lambda n: pl.BlockSpec((1, 1, gseg), lambda b, t, j: (n, 0, j))
    return pl.pallas_call(
        functools.partial(_in_proj_kernel, cast_plans=cast_plans),
        grid=(nb, n_t, IN_STEPS),
        in_specs=[
            pl.BlockSpec((None, tile_rows, D_MODEL), lambda b, t, j: (b, t, 0)),
            pl.BlockSpec((1, D_MODEL), lambda b, t, j: (0, 0)),
            pl.BlockSpec((D_MODEL, IN_COLS // IN_STEPS), lambda b, t, j: (0, j)),
            gb(0), gb(1),
        ] + cast_specs,
        out_specs=[
            pl.BlockSpec((seg // V7X_LANES, tile_rows // S5_CHUNK, None, S5_CHUNK, V7X_LANES),
                         lambda b, t, j: (j, t, b, 0, 0)),
            pl.BlockSpec((None, tile_rows, seg), tile),
            pl.BlockSpec((None, tile_rows, seg), tile),
            pl.BlockSpec((None, tile_rows, gseg), tile),
            pl.BlockSpec((None, tile_rows, gseg), tile),
        ] + cast_specs,
        out_shape=[jax.ShapeDtypeStruct(
                       (S5_SLABS, t_len // S5_CHUNK, nb, S5_CHUNK, V7X_LANES), _f32),
                   jax.ShapeDtypeStruct((nb, t_len, CONV_WIDTH), _bf16),
                   jax.ShapeDtypeStruct((nb, t_len, CONV_WIDTH), _bf16),
                   jax.ShapeDtypeStruct((nb, t_len, D_MODEL), _bf16),
                   jax.ShapeDtypeStruct((nb, t_len, D_MODEL), _bf16)]
                  + [jax.ShapeDtypeStruct(w.shape, _bf16) for w in cast_w],
        scratch_shapes=[pltpu.VMEM((tile_rows, D_MODEL), _bf16)],
        compiler_params=_params(("arbitrary", "arbitrary", "arbitrary")),
        name="in_proj",
    )(x3, g, w_in_b, gate_b, gate_b, *cast_w)


def _meta_proj_kernel(x_ref, g_ref, w_ref, u_ref, q_ref, h_scr):
    seg = IN_SEG_COLS

    @pl.when(pl.program_id(0) == 0)
    def _():
        h_scr[...] = _rms_norm(x_ref[...], g_ref[...]).astype(_bf16)

    proj = _dot(h_scr[...], w_ref[...])
    for s in range(seg // V7X_LANES):
        u_ref[s] = proj[:, s * V7X_LANES:(s + 1) * V7X_LANES].reshape(-1, S5_CHUNK, V7X_LANES)
    q_ref[...] = (proj[:, 3 * seg:4 * seg] * proj[:, seg:2 * seg]).astype(_bf16)


def _meta_proj(meta, g, w_in_b):
    seg = IN_SEG_COLS
    return pl.pallas_call(
        _meta_proj_kernel,
        grid=(IN_STEPS,),
        in_specs=[
            pl.BlockSpec((N_META, D_MODEL), lambda j: (0, 0)),
            pl.BlockSpec((1, D_MODEL), lambda j: (0, 0)),
            pl.BlockSpec((D_MODEL, 4 * seg), lambda j: (0, 2 * j)),
        ],
        out_specs=[
            pl.BlockSpec((seg // V7X_LANES, META_CHUNKS, S5_CHUNK, V7X_LANES),
                         lambda j: (j, 0, 0, 0)),
            pl.BlockSpec((N_META, seg), lambda j: (0, j)),
        ],
        out_shape=[jax.ShapeDtypeStruct((S5_SLABS, META_CHUNKS, S5_CHUNK, V7X_LANES), _f32),
                   jax.ShapeDtypeStruct((N_META, CONV_WIDTH), _bf16)],
        scratch_shapes=[pltpu.VMEM((N_META, D_MODEL), _bf16)],
        compiler_params=_params(("arbitrary",)),
        name="meta_proj",
    )(meta, g, w_in_b)


def _boundary_scan(x_scr, a_re, a_im, init):
    n_tiles = S5_CHUNK_ROWS // V7X_SUBLANES
    row = lax.broadcasted_iota(jnp.int32, (V7X_SUBLANES, V7X_LANES), 0)

    def step(ar, ai, sr, si, xr, xi):
        return ar * sr - ai * si + xr, ar * si + ai * sr + xi

    def body(i, carry):
        new = []
        for d in range(N_DIR):
            first = (row < BATCH) if d == 0 else (row >= BATCH)
            ii = i if d == 0 else (n_tiles - 1 - i)
            r0 = pl.multiple_of(ii * V7X_SUBLANES, V7X_SUBLANES)
            for t in range(SLAB_LANE_TILES):
                n = d * SLAB_LANE_TILES + t
                re = slice(t * V7X_LANES, (t + 1) * V7X_LANES)
                im = slice(SLAB_STATES + t * V7X_LANES, SLAB_STATES + (t + 1) * V7X_LANES)
                cr, ci = carry[2 * n], carry[2 * n + 1]
                xr = x_scr[d, pl.ds(r0, V7X_SUBLANES), re]
                xi = x_scr[d, pl.ds(r0, V7X_SUBLANES), im]
                tr, ti = step(a_re[n], a_im[n], cr, ci, xr, xi)
                outr = jnp.where(first, cr, pltpu.roll(tr, BATCH, 0))
                outi = jnp.where(first, ci, pltpu.roll(ti, BATCH, 0))
                x_scr[d, pl.ds(r0, V7X_SUBLANES), re] = outr
                x_scr[d, pl.ds(r0, V7X_SUBLANES), im] = outi
                tr, ti = step(a_re[n], a_im[n], outr, outi, xr, xi)
                new.append(jnp.where(first, pltpu.roll(tr, BATCH, 0), tr))
                new.append(jnp.where(first, pltpu.roll(ti, BATCH, 0), ti))
        return tuple(new)

    lax.fori_loop(0, n_tiles, body, init)


def _s5_kernel(u_ref, um_ref, b_ref, c_ref, p_ref, d_ref, y_ref,
               ub_scr, w_scr, x_scr, sp_scr, tap_scr, tapf_scr, mid_scr, xm_scr):
    step = pl.program_id(1)
    ns = SLAB_STATES
    blk = V7X_LANES

    @pl.when(step == 0)
    def _():
        for j in range(S5_CHUNK):
            ub_scr[:, j * blk:(j + 1) * blk] = u_ref[
                0, pl.ds(j, S5_CHUNK_ROWS, stride=S5_CHUNK), :].astype(_bf16)
        um = um_ref[0].astype(_bf16)
        for d in range(N_DIR):
            w_re = _dense_map(b_ref, d, 0)
            w_im = _dense_map(b_ref, d, 1)
            for part in range(2):
                cols = slice(part * ns, (part + 1) * ns)
                for j in range(S5_CHUNK):
                    m = (S5_CHUNK - 1 - j) if d == 0 else j
                    pr = p_ref[0, d, m, 0:1, :]
                    pi = p_ref[0, d, m, 1:2, :]
                    g = (w_re * pr - w_im * pi) if part == 0 else (w_re * pi + w_im * pr)
                    w_scr[j * blk:(j + 1) * blk, :] = g.astype(_bf16)
                x_scr[d, :, cols] = _dot(ub_scr[...], w_scr[...])
                taps = _dot_nt(w_scr[...], _dense_map(c_ref, d, part).astype(_bf16))
                if part == 0:
                    tapf_scr[...] = taps
                else:
                    tapf_scr[...] += taps
                if d == 0:
                    xm_scr[:, cols] = _dot(um, w_scr[...])

            edge = (S5_CHUNK - 1) * blk
            if d == 0:
                tap_scr[0:edge, :] = tapf_scr[0:edge, :].astype(_bf16)
                mid_scr[...] = tapf_scr[edge:, :]
            else:
                tap_scr[edge:edge + blk, :] = (mid_scr[...] + tapf_scr[0:blk, :]).astype(_bf16)
                tap_scr[edge + blk:, :] = tapf_scr[blk:, :].astype(_bf16)

        a_re, a_im, init = [], [], []
        for d in range(N_DIR):
            for t in range(SLAB_LANE_TILES):
                lanes = slice(t * V7X_LANES, (t + 1) * V7X_LANES)
                lanes_im = slice(ns + t * V7X_LANES, ns + (t + 1) * V7X_LANES)
                ar = p_ref[0, d, S5_CHUNK, 0:1, lanes]
                ai = p_ref[0, d, S5_CHUNK, 1:2, lanes]
                a_re.append(jnp.broadcast_to(ar, (V7X_SUBLANES, V7X_LANES)))
                a_im.append(jnp.broadcast_to(ai, (V7X_SUBLANES, V7X_LANES)))
                sr = jnp.zeros((1, V7X_LANES), _f32)
                si = jnp.zeros((1, V7X_LANES), _f32)
                if d == 0:
                    for c in range(META_CHUNKS):
                        xr = xm_scr[c:c + 1, lanes]
                        xi = xm_scr[c:c + 1, lanes_im]
                        sr, si = ar * sr - ai * si + xr, ar * si + ai * sr + xi
                init.append(jnp.broadcast_to(sr, (V7X_SUBLANES, V7X_LANES)))
                init.append(jnp.broadcast_to(si, (V7X_SUBLANES, V7X_LANES)))
        _boundary_scan(x_scr, a_re, a_im, tuple(init))
        for d in range(N_DIR):
            sp_scr[d] = x_scr[d].astype(_bf16)

    def out_weights(d, m):
        pw = p_ref[0, d, m]
        pr, pi = pw[0:1, :], pw[1:2, :]
        c_re = _dense_map(c_ref, d, 0)
        c_imn = _dense_map(c_ref, d, 1)
        return jnp.concatenate([c_re * pr + c_imn * pi, c_imn * pr - c_re * pi],
                               axis=1).astype(_bf16)

    o0 = step * S5_OFFSETS_PER_STEP
    offsets = range(S5_OFFSETS_PER_STEP)
    wy_f = jnp.concatenate([out_weights(0, o0 + e + 1) for e in offsets], axis=0)
    wy_b = jnp.concatenate([out_weights(1, S5_CHUNK - o0 - e) for e in offsets], axis=0)
    toeplitz = jnp.concatenate(
        [tap_scr[pl.ds(pl.multiple_of((S5_CHUNK - 1 - o0 - e) * blk, blk), S5_CHUNK_COLS), :]
         for e in offsets], axis=1)
    y = _dot_nt(sp_scr[0], wy_f)
    y += _dot_nt(sp_scr[1], wy_b)
    y += _dot(ub_scr[...], toeplitz)
    for e in offsets:
        skip = u_ref[0, pl.ds(o0 + e, S5_CHUNK_ROWS, stride=S5_CHUNK), :] * d_ref[0]
        y_ref[0, e] = y[:, e * blk:(e + 1) * blk] + skip


def _s5(u, u_meta, b_maps, c_maps, ptab, d_skip):
    maps = pl.BlockSpec((N_DIR, 1, 2, V7X_LANES, V7X_LANES), lambda s, o: (0, s, 0, 0, 0))
    return pl.pallas_call(
        _s5_kernel,
        grid=(S5_SLABS, S5_CHUNK // S5_OFFSETS_PER_STEP),
        in_specs=[
            pl.BlockSpec((1, S5_CHUNK_ROWS * S5_CHUNK, V7X_LANES), lambda s, o: (s, 0, 0)),
            pl.BlockSpec((1, V7X_SUBLANES, S5_CHUNK_COLS), lambda s, o: (s, 0, 0)),
            maps,
            maps,
            pl.BlockSpec((1, N_DIR, S5_CHUNK + 1, 2, SLAB_STATES),
                         lambda s, o: (s, 0, 0, 0, 0)),
            pl.BlockSpec((1, 1, V7X_LANES), lambda s, o: (s, 0, 0)),
        ],
        out_specs=pl.BlockSpec((1, S5_OFFSETS_PER_STEP, S5_CHUNK_ROWS, V7X_LANES),
                               lambda s, o: (s, o, 0, 0)),
        out_shape=jax.ShapeDtypeStruct((S5_SLABS, S5_CHUNK, S5_CHUNK_ROWS, V7X_LANES), _f32),
        scratch_shapes=[
            pltpu.VMEM((S5_CHUNK_ROWS, S5_CHUNK_COLS), _bf16),
            pltpu.VMEM((S5_CHUNK_COLS, SLAB_STATES), _bf16),
            pltpu.VMEM((N_DIR, S5_CHUNK_ROWS, 2 * SLAB_STATES), _f32),
            pltpu.VMEM((N_DIR, S5_CHUNK_ROWS, 2 * SLAB_STATES), _bf16),
            pltpu.VMEM((S5_TAPS * V7X_LANES, V7X_LANES), _bf16),
            pltpu.VMEM((S5_CHUNK_COLS, V7X_LANES), _f32),
            pltpu.VMEM((V7X_LANES, V7X_LANES), _f32),
            pltpu.VMEM((V7X_SUBLANES, 2 * SLAB_STATES), _f32),
        ],
        compiler_params=_params(("arbitrary", "arbitrary")),
        name="s5",
    )(u, u_meta, b_maps, c_maps, ptab, d_skip)


def _mixer_kernel(h_ref, y_ref, q_ref, qp_ref, qn_ref, qm_ref, bc_ref, gs_ref, gc_ref,
                  wglu_ref, bglu_ref, wsup_ref, cw_ref, cb_ref, wcup_ref, wo_ref,
                  g_ref, wcast_ref, o_ref, wcast_out_ref, ys_scr, q_scr):
    t = pl.program_id(0)
    last = pl.num_programs(0) - 1
    tt = MIX_TILE_T
    tm = BATCH * tt
    n_c = tt // S5_CHUNK

    for s in range(S5_SLABS):
        for o in range(S5_CHUNK):
            for b in range(BATCH):
                ys_scr[s, pl.ds(b * tt + o, n_c, stride=S5_CHUNK), :] = (
                    y_ref[s, o, pl.ds(b, n_c, stride=BATCH), :])
    ys = jnp.concatenate([ys_scr[s] for s in range(S5_SLABS)], axis=1)
    ys = jax.nn.gelu(ys)
    glu = _dot(ys.astype(_bf16), wglu_ref[...]) + bglu_ref[...]
    ys = ys * jax.nn.sigmoid(glu)
    y_s = _dot(ys.astype(_bf16), wsup_ref[...])

    vs = []
    for b in range(BATCH):
        prev = jnp.where(t == 0, qm_ref[...], qp_ref[b]).astype(_f32)
        nxt = jnp.where(t == last, 0.0, qn_ref[b].astype(_f32))
        q_scr[b, 0:HALO_ROWS, :] = prev
        q_scr[b, HALO_ROWS:HALO_ROWS + tt, :] = q_ref[b].astype(_f32)
        q_scr[b, HALO_ROWS + tt:, :] = nxt
        vs.append(cw_ref[0:1, :] * q_scr[b, HALO_ROWS - 1:HALO_ROWS - 1 + tt, :]
                  + cw_ref[1:2, :] * q_scr[b, HALO_ROWS:HALO_ROWS + tt, :]
                  + cw_ref[2:3, :] * q_scr[b, HALO_ROWS + 1:HALO_ROWS + 1 + tt, :]
                  + cb_ref[...])
    v = jnp.concatenate(vs, axis=0)
    bc = bc_ref[...].reshape(tm, CONV_WIDTH).astype(_f32)
    y_c = _dot((bc * v).astype(_bf16), wcup_ref[...])

    merged = (gs_ref[...].reshape(tm, D_MODEL).astype(_f32) * y_s
              + gc_ref[...].reshape(tm, D_MODEL).astype(_f32) * y_c)
    mixed = _dot(merged.astype(_bf16), wo_ref[...])
    out = h_ref[...].reshape(tm, D_MODEL) + _rms_norm(mixed, g_ref[...])
    o_ref[...] = out.reshape(BATCH, tt, D_MODEL)
    wcast_out_ref[...] = wcast_ref[...].astype(_bf16)


def _mixer(x, y, q, q_meta, bc, gate_s, gate_c, w_glu, b_glu, w_s_up, conv_w, conv_b, w_c_up,
           w_o, g_post, w_cast):
    tt = MIX_TILE_T
    n_t = SEQ // tt
    cast_spec = pl.BlockSpec((w_cast.shape[0] // n_t, w_cast.shape[1]), lambda t: (t, 0))
    halo_per_tile = tt // HALO_ROWS
    n_halo = SEQ // HALO_ROWS
    tile = lambda t: (0, t, 0)
    resident = functools.partial(pl.BlockSpec, index_map=lambda t: (0, 0),
                                 pipeline_mode=pl.Buffered(1))
    return pl.pallas_call(
        _mixer_kernel,
        grid=(n_t,),
        in_specs=[
            pl.BlockSpec((BATCH, tt, D_MODEL), tile),
            pl.BlockSpec((S5_SLABS, S5_CHUNK, tt // S5_CHUNK * BATCH, V7X_LANES),
                         lambda t: (0, 0, t, 0)),
            pl.BlockSpec((BATCH, tt, CONV_WIDTH), tile),
            pl.BlockSpec((BATCH, HALO_ROWS, CONV_WIDTH),
                         lambda t: (0, jnp.maximum(t * halo_per_tile - 1, 0), 0)),
            pl.BlockSpec((BATCH, HALO_ROWS, CONV_WIDTH),
                         lambda t: (0, jnp.minimum((t + 1) * halo_per_tile, n_halo - 1), 0)),
            resident((HALO_ROWS, CONV_WIDTH)),
            pl.BlockSpec((BATCH, tt, CONV_WIDTH), tile),
            pl.BlockSpec((BATCH, tt, D_MODEL), tile),
            pl.BlockSpec((BATCH, tt, D_MODEL), tile),
            resident((S5_WIDTH, S5_WIDTH)),
            resident((1, S5_WIDTH)),
            resident((S5_WIDTH, D_MODEL)),
            resident((3, CONV_WIDTH)),
            resident((1, CONV_WIDTH)),
            resident((CONV_WIDTH, D_MODEL)),
            resident((D_MODEL, D_MODEL)),
            resident((1, D_MODEL)),
            cast_spec,
        ],
        out_specs=[pl.BlockSpec((BATCH, tt, D_MODEL), tile), cast_spec],
        out_shape=[jax.ShapeDtypeStruct((BATCH, SEQ, D_MODEL), _f32),
                   jax.ShapeDtypeStruct(w_cast.shape, _bf16)],
        scratch_shapes=[pltpu.VMEM((S5_SLABS, BATCH * tt, V7X_LANES), _f32),
                        pltpu.VMEM((BATCH, tt + 2 * HALO_ROWS, CONV_WIDTH), _f32)],
        compiler_params=_params(("arbitrary",)),
        name="mixer_tail",
    )(x, y, q, q, q, q_meta, bc, gate_s, gate_c, w_glu, b_glu, w_s_up, conv_w, conv_b, w_c_up,
      w_o, g_post, w_cast)


def _ffn_kernel(h_ref, gpre_ref, wgu_ref, wout_ref, gpost_ref, o_ref, hb_scr):
    j = pl.program_id(1)

    @pl.when(j == 0)
    def _():
        hb_scr[...] = _rms_norm(h_ref[...], gpre_ref[...]).astype(_bf16)
        o_ref[...] = jnp.zeros_like(o_ref)

    gu = _dot(hb_scr[...], wgu_ref[...])
    gate, up = gu[:, :FFN_TILE_HIDDEN], gu[:, FFN_TILE_HIDDEN:]
    act = (jax.nn.silu(gate) * up).astype(_bf16)
    o_ref[...] += _dot(act, wout_ref[...])

    @pl.when(j == pl.num_programs(1) - 1)
    def _():
        o_ref[...] = h_ref[...] + _rms_norm(o_ref[...], gpost_ref[...])


def _ffn(h1, g_pre, w_in, w_out, g_post):
    tm, th = FFN_TILE_ROWS, FFN_TILE_HIDDEN
    n_hidden = FFN_HIDDEN // th
    return pl.pallas_call(
        _ffn_kernel,
        grid=(ROWS // tm, n_hidden),
        in_specs=[
            pl.BlockSpec((tm, D_MODEL), lambda i, j: (i, 0)),
            pl.BlockSpec((1, D_MODEL), lambda i, j: (0, 0)),
            pl.BlockSpec((D_MODEL, 2 * th), lambda i, j: (0, j)),
            pl.BlockSpec((th, D_MODEL), lambda i, j: (j, 0)),
            pl.BlockSpec((1, D_MODEL), lambda i, j: (0, 0)),
        ],
        out_specs=pl.BlockSpec((tm, D_MODEL), lambda i, j: (i, 0)),
        out_shape=jax.ShapeDtypeStruct((ROWS, D_MODEL), _f32),
        scratch_shapes=[pltpu.VMEM((tm, D_MODEL), _bf16)],
        compiler_params=_params(("arbitrary", "arbitrary")),
        name="ffn",
    )(h1, g_pre, w_in, w_out, g_post)


def kernel(x, meta, g_mix_pre, g_mix_post, g_ffn_pre, g_ffn_post, w_in, gate_b, lam_re, lam_im,
           log_dt, b_re, b_im, c_re, c_im, d_skip, w_glu, b_glu, w_s_up, conv_w, conv_b, w_c_up,
           w_o, w_ffn_in, w_ffn_out):
    l = 0
    w_in_b = _cast_w_in(w_in[l])
    gate_b2 = gate_b[l].reshape(2, 1, D_MODEL)
    g_pre = g_mix_pre[l].reshape(1, D_MODEL)

    ptab, bz_re, bz_im = _discretise(lam_re[l], lam_im[l], log_dt[l], b_re[l], b_im[l])
    dg = N_DIR * S5_GROUPS
    b_maps = _compact_maps(bz_re, bz_im)
    c_maps = _compact_maps(c_re[l].reshape(dg, S5_GROUP, S5_STATE),
                           -c_im[l].reshape(dg, S5_GROUP, S5_STATE))
    ptab = ptab.reshape(S5_CHUNK + 1, 2, N_DIR, S5_SLABS, SLAB_STATES)
    ptab = ptab.transpose(3, 2, 0, 1, 4)

    (u5, bc, q, gate_s, gate_c, w_ffn_in_b, w_glu_b, w_s_up_b, w_c_up_b, w_o_b) = _in_proj(
        x, g_pre, w_in_b, gate_b2, IN_TILE_ROWS,
        cast=((w_ffn_in[l], _ffn_in_plan()), (w_glu[l], None), (w_s_up[l], None),
              (w_c_up[l], None), (w_o[l], None)))
    um5, q_meta = _meta_proj(meta.astype(x.dtype), g_pre, w_in_b)

    u = u5.reshape(S5_SLABS, S5_CHUNK_ROWS * S5_CHUNK, V7X_LANES)
    u_meta = jnp.pad(um5.reshape(S5_SLABS, META_CHUNKS, S5_CHUNK_COLS),
                     ((0, 0), (0, V7X_SUBLANES - META_CHUNKS), (0, 0)))
    y = _s5(u, u_meta, b_maps, c_maps, ptab, d_skip[l].reshape(S5_SLABS, 1, V7X_LANES))

    h1, w_ffn_out_b = _mixer(x, y, q, q_meta, bc, gate_s, gate_c, w_glu_b,
                             b_glu[l].reshape(1, S5_WIDTH), w_s_up_b, conv_w[l],
                             conv_b[l].reshape(1, CONV_WIDTH), w_c_up_b, w_o_b,
                             g_mix_post[l].reshape(1, D_MODEL), w_ffn_out[l])

    h2 = _ffn(h1.reshape(ROWS, D_MODEL), g_ffn_pre[l].reshape(1, D_MODEL),
              w_ffn_in_b, w_ffn_out_b, g_ffn_post[l].reshape(1, D_MODEL))
    return h2.reshape(BATCH, SEQ, D_MODEL)
```

```python
import functools
import math

import jax
import jax.numpy as jnp
from jax import lax
from jax.experimental import pallas as pl
from jax.experimental.pallas import tpu as pltpu

D_MODEL = 2048
BATCH = 4
SEQ = 2048
N_META = 16
S5_WIDTH = 1024
S5_GROUP = 16
S5_GROUPS = S5_WIDTH // S5_GROUP
S5_STATE = 64
N_DIR = 2
CONV_WIDTH = 1024
FFN_HIDDEN = ((math.ceil(8 * D_MODEL / 3) + 255) // 256) * 256
IN_COLS = S5_WIDTH + 3 * CONV_WIDTH + 2 * D_MODEL
RMS_EPS = 1e-6
LAM_RE_MAX = -1e-4

ROWS = SEQ * BATCH

V7X_SUBLANES = 8
V7X_LANES = 128
V7X_MXU_DIM = 256
V7X_VMEM_LIMIT_BYTES = 56 * 1024 * 1024

S5_SLABS = S5_WIDTH // V7X_LANES
SLAB_GROUPS = V7X_LANES // S5_GROUP
SLAB_STATES = SLAB_GROUPS * S5_STATE
PAIR_LANES = 2 * S5_GROUP
SLAB_PAIRS = V7X_LANES // PAIR_LANES
PAIR_STATES = 2 * S5_STATE
S5_CHUNK = V7X_SUBLANES
PAIR_COLS = S5_CHUNK * PAIR_LANES
S5_CHUNKS = SEQ // S5_CHUNK
S5_CHUNK_ROWS = S5_CHUNKS * BATCH
S5_CHUNK_COLS = S5_CHUNK * V7X_LANES
META_CHUNKS = N_META // S5_CHUNK
STATE_REPEATS = V7X_LANES // S5_STATE

IN_TILE_ROWS = 1024
IN_STEPS = 8
IN_SEG_COLS = S5_WIDTH // IN_STEPS
CAST_TILE_ROWS = 128
MIX_TILE_T = 64
FFN_TILE_ROWS = 1024
FFN_TILE_HIDDEN = 256
HALO_ROWS = 16

_f32 = jnp.float32
_bf16 = jnp.bfloat16


def _rms_norm(xf, g):
    r = lax.rsqrt(jnp.mean(xf * xf, axis=-1, keepdims=True) + RMS_EPS)
    return xf * r * g


def _params(sem):
    return pltpu.CompilerParams(dimension_semantics=sem,
                                vmem_limit_bytes=V7X_VMEM_LIMIT_BYTES)


def _dot(a, b):
    return jnp.dot(a, b, preferred_element_type=_f32)


def _dot_nt(a, b):
    return lax.dot_general(a, b, (((1,), (1,)), ((), ())), preferred_element_type=_f32)


def _discretise_kernel(lam_re_ref, lam_im_ref, log_dt_ref, b_re_ref, b_im_ref,
                       pow_ref, bz_re_ref, bz_im_ref):
    lr = jnp.minimum(lam_re_ref[...], LAM_RE_MAX)
    li = lam_im_ref[...]
    dt = jnp.exp(log_dt_ref[...])
    mag = jnp.exp(lr * dt)
    ar = mag * jnp.cos(li * dt)
    ai = mag * jnp.sin(li * dt)
    den = lr * lr + li * li
    nr = ar - 1.0
    zr = (nr * lr + ai * li) / den
    zi = (ai * lr - nr * li) / den
    pr = jnp.ones_like(ar)
    pi = jnp.zeros_like(ar)
    for m in range(S5_CHUNK + 1):
        pow_ref[m, 0] = pr
        pow_ref[m, 1] = pi
        pr, pi = pr * ar - pi * ai, pr * ai + pi * ar
    b_re = b_re_ref[...]
    b_im = b_im_ref[...]
    bz_re_ref[...] = zr[:, None, :] * b_re - zi[:, None, :] * b_im
    bz_im_ref[...] = zr[:, None, :] * b_im + zi[:, None, :] * b_re


def _discretise(lam_re, lam_im, log_dt, b_re, b_im):
    dg = N_DIR * S5_GROUPS
    b_re_t = jnp.swapaxes(b_re, -1, -2).reshape(dg, S5_GROUP, S5_STATE)
    b_im_t = jnp.swapaxes(b_im, -1, -2).reshape(dg, S5_GROUP, S5_STATE)
    return pl.pallas_call(
        _discretise_kernel,
        out_shape=(jax.ShapeDtypeStruct((S5_CHUNK + 1, 2, dg, S5_STATE), _f32),
                   jax.ShapeDtypeStruct((dg, S5_GROUP, S5_STATE), _f32),
                   jax.ShapeDtypeStruct((dg, S5_GROUP, S5_STATE), _f32)),
        name="s5_discretise",
    )(lam_re.reshape(dg, S5_STATE), lam_im.reshape(dg, S5_STATE),
      log_dt.reshape(dg, 1), b_re_t, b_im_t)


def _compact_maps(w_re, w_im):
    def part(w):
        w = w.reshape(N_DIR, S5_SLABS, V7X_LANES, S5_STATE)
        return jnp.concatenate([w] * STATE_REPEATS, axis=-1)
    return jnp.stack([part(w_re), part(w_im)], axis=2)


def _w_in_pieces(j):
    seg, gseg = IN_SEG_COLS, 2 * IN_SEG_COLS
    pieces = [(n * S5_WIDTH + j * seg, seg) for n in range(4)]
    pieces += [(4 * S5_WIDTH + n * D_MODEL + j * gseg, gseg) for n in range(2)]
    return pieces


def _cast_w_in_kernel(w_ref, o_ref):
    dst = 0
    for j in range(IN_STEPS):
        for src, width in _w_in_pieces(j):
            o_ref[:, dst:dst + width] = w_ref[:, src:src + width].astype(_bf16)
            dst += width


def _cast_w_in(w):
    spec = pl.BlockSpec((CAST_TILE_ROWS, IN_COLS), lambda i: (i, 0))
    return pl.pallas_call(
        _cast_w_in_kernel,
        grid=(D_MODEL // CAST_TILE_ROWS,),
        in_specs=[spec],
        out_specs=spec,
        out_shape=jax.ShapeDtypeStruct(w.shape, _bf16),
        compiler_params=_params(("arbitrary",)),
        name="cast_w_in",
    )(w)


def _ffn_in_plan():
    th = FFN_TILE_HIDDEN
    plan = []
    for j in range(FFN_HIDDEN // th):
        plan.append((j * th, 2 * j * th, th))
        plan.append((FFN_HIDDEN + j * th, (2 * j + 1) * th, th))
    return plan


def _in_proj_kernel(x_ref, g_ref, w_ref, gbs_ref, gbc_ref, *rest, cast_plans):
    n_cast = len(cast_plans)
    cast_in = rest[:n_cast]
    u_ref, bc_ref, q_ref, gs_ref, gc_ref = rest[n_cast:n_cast + 5]
    cast_out = rest[n_cast + 5:2 * n_cast + 5]
    h_scr = rest[2 * n_cast + 5]
    seg = IN_SEG_COLS

    @pl.when(pl.program_id(2) == 0)
    def _():
        h_scr[...] = _rms_norm(x_ref[...], g_ref[...]).astype(_bf16)

    proj = _dot(h_scr[...], w_ref[...])
    for s in range(seg // V7X_LANES):
        u_ref[s] = proj[:, s * V7X_LANES:(s + 1) * V7X_LANES].reshape(-1, S5_CHUNK, V7X_LANES)
    bc_ref[...] = proj[:, 2 * seg:3 * seg].astype(_bf16)
    q_ref[...] = (proj[:, 3 * seg:4 * seg] * proj[:, seg:2 * seg]).astype(_bf16)
    gs_ref[...] = jax.nn.sigmoid(proj[:, 4 * seg:6 * seg] + gbs_ref[0]).astype(_bf16)
    gc_ref[...] = jax.nn.sigmoid(proj[:, 6 * seg:8 * seg] + gbc_ref[0]).astype(_bf16)
    for src, dst, plan in zip(cast_in, cast_out, cast_plans):
        for s0, d0, width in plan:
            dst[:, d0:d0 + width] = src[:, s0:s0 + width].astype(_bf16)


def _in_proj(x3, g, w_in_b, gate_b, tile_rows, cast=()):
    nb, t_len, _ = x3.shape
    n_t = t_len // tile_rows
    n_steps = nb * n_t * IN_STEPS
    seg, gseg = IN_SEG_COLS, 2 * IN_SEG_COLS
    tile = lambda b, t, j: (b, t, j)
    step = lambda b, t, j: ((b * n_t + t) * IN_STEPS + j, 0)
    cast_w = [w for w, _ in cast]
    cast_plans = tuple(tuple(plan) if plan else ((0, 0, w.shape[1]),) for w, plan in cast)
    cast_specs = [pl.BlockSpec((w.shape[0] // n_steps, w.shape[1]), step) for w in cast_w]
    gb = lambda n: pl.BlockSpec((1, 1, gseg), lambda b, t, j: (n, 0, j))
    return pl.pallas_call(
        functools.partial(_in_proj_kernel, cast_plans=cast_plans),
        grid=(nb, n_t, IN_STEPS),
        in_specs=[
            pl.BlockSpec((None, tile_rows, D_MODEL), lambda b, t, j: (b, t, 0)),
            pl.BlockSpec((1, D_MODEL), lambda b, t, j: (0, 0)),
            pl.BlockSpec((D_MODEL, IN_COLS // IN_STEPS), lambda b, t, j: (0, j)),
            gb(0), gb(1),
        ] + cast_specs,
        out_specs=[
            pl.BlockSpec((seg // V7X_LANES, tile_rows // S5_CHUNK, None, S5_CHUNK, V7X_LANES),
                         lambda b, t, j: (j, t, b, 0, 0)),
            pl.BlockSpec((None, tile_rows, seg), tile),
            pl.BlockSpec((None, tile_rows, seg), tile),
            pl.BlockSpec((None, tile_rows, gseg), tile),
            pl.BlockSpec((None, tile_rows, gseg), tile),
        ] + cast_specs,
        out_shape=[jax.ShapeDtypeStruct(
                       (S5_SLABS, t_len // S5_CHUNK, nb, S5_CHUNK, V7X_LANES), _f32),
                   jax.ShapeDtypeStruct((nb, t_len, CONV_WIDTH), _bf16),
                   jax.ShapeDtypeStruct((nb, t_len, CONV_WIDTH), _bf16),
                   jax.ShapeDtypeStruct((nb, t_len, D_MODEL), _bf16),
                   jax.ShapeDtypeStruct((nb, t_len, D_MODEL), _bf16)]
                  + [jax.ShapeDtypeStruct(w.shape, _bf16) for w in cast_w],
        scratch_shapes=[pltpu.VMEM((tile_rows, D_MODEL), _bf16)],
        compiler_params=_params(("arbitrary", "arbitrary", "arbitrary")),
        name="in_proj",
    )(x3, g, w_in_b, gate_b, gate_b, *cast_w)


def _meta_proj_kernel(x_ref, g_ref, w_ref, u_ref, q_ref, h_scr):
    seg = IN_SEG_COLS

    @pl.when(pl.program_id(0) == 0)
    def _():
        h_scr[...] = _rms_norm(x_ref[...], g_ref[...]).astype(_bf16)

    proj = _dot(h_scr[...], w_ref[...])
    for s in range(seg // V7X_LANES):
        u_ref[s] = proj[:, s * V7X_LANES:(s + 1) * V7X_LANES].reshape(-1, S5_CHUNK, V7X_LANES)
    q_ref[...] = (proj[:, 3 * seg:4 * seg] * proj[:, seg:2 * seg]).astype(_bf16)


def _meta_proj(meta, g, w_in_b):
    seg = IN_SEG_COLS
    return pl.pallas_call(
        _meta_proj_kernel,
        grid=(IN_STEPS,),
        in_specs=[
            pl.BlockSpec((N_META, D_MODEL), lambda j: (0, 0)),
            pl.BlockSpec((1, D_MODEL), lambda j: (0, 0)),
            pl.BlockSpec((D_MODEL, 4 * seg), lambda j: (0, 2 * j)),
        ],
        out_specs=[
            pl.BlockSpec((seg // V7X_LANES, META_CHUNKS, S5_CHUNK, V7X_LANES),
                         lambda j: (j, 0, 0, 0)),
            pl.BlockSpec((N_META, seg), lambda j: (0, j)),
        ],
        out_shape=[jax.ShapeDtypeStruct((S5_SLABS, META_CHUNKS, S5_CHUNK, V7X_LANES), _f32),
                   jax.ShapeDtypeStruct((N_META, CONV_WIDTH), _bf16)],
        scratch_shapes=[pltpu.VMEM((N_META, D_MODEL), _bf16)],
        compiler_params=_params(("arbitrary",)),
        name="meta_proj",
    )(meta, g, w_in_b)


def _pair_map(ref, d, part, q):
    rows = lax.broadcasted_iota(jnp.int32, (PAIR_LANES, V7X_LANES), 0)
    cols = lax.broadcasted_iota(jnp.int32, (PAIR_LANES, V7X_LANES), 1)
    same_group = (rows // S5_GROUP) == (cols // S5_STATE)
    return jnp.where(same_group, ref[d, 0, part, q * PAIR_LANES:(q + 1) * PAIR_LANES, :], 0.0)


def _transpose_pieces(src):
    piece = lax.broadcasted_iota(jnp.int32, src[0].shape, 1) // PAIR_LANES
    out = []
    for i in range(SLAB_PAIRS):
        acc = None
        for j in reversed(range(SLAB_PAIRS)):
            shift = ((j - i) * PAIR_LANES) % V7X_LANES
            moved = pltpu.roll(src[j], shift, 1) if shift else src[j]
            acc = moved if acc is None else jnp.where(piece == j, moved, acc)
        out.append(acc)
    return out


def _boundary_scan(x_scr, a_re, a_im, init):
    n_tiles = S5_CHUNK_ROWS // V7X_SUBLANES
    row = lax.broadcasted_iota(jnp.int32, (V7X_SUBLANES, V7X_LANES), 0)

    def step(ar, ai, sr, si, xr, xi):
        return ar * sr - ai * si + xr, ar * si + ai * sr + xi

    def body(i, carry):
        new = []
        for d in range(N_DIR):
            first = (row < BATCH) if d == 0 else (row >= BATCH)
            ii = i if d == 0 else (n_tiles - 1 - i)
            r0 = pl.multiple_of(ii * V7X_SUBLANES, V7X_SUBLANES)
            for q in range(SLAB_PAIRS):
                n = d * SLAB_PAIRS + q
                re = slice(q * PAIR_COLS, q * PAIR_COLS + PAIR_STATES)
                im = slice(q * PAIR_COLS + PAIR_STATES, (q + 1) * PAIR_COLS)
                cr, ci = carry[2 * n], carry[2 * n + 1]
                xr = x_scr[d, pl.ds(r0, V7X_SUBLANES), re]
                xi = x_scr[d, pl.ds(r0, V7X_SUBLANES), im]
                tr, ti = step(a_re[n], a_im[n], cr, ci, xr, xi)
                outr = jnp.where(first, cr, pltpu.roll(tr, BATCH, 0))
                outi = jnp.where(first, ci, pltpu.roll(ti, BATCH, 0))
                x_scr[d, pl.ds(r0, V7X_SUBLANES), re] = outr
                x_scr[d, pl.ds(r0, V7X_SUBLANES), im] = outi
                tr, ti = step(a_re[n], a_im[n], outr, outi, xr, xi)
                new.append(jnp.where(first, pltpu.roll(tr, BATCH, 0), tr))
                new.append(jnp.where(first, pltpu.roll(ti, BATCH, 0), ti))
        return tuple(new)

    lax.fori_loop(0, n_tiles, body, init)


def _s5_kernel(u_ref, um_ref, b_ref, c_ref, p_ref, d_ref, y_ref,
               ub_scr, w_scr, x_scr, sp_scr, tap_scr, toep_scr, yq_scr, xm_scr):
    half_steps = S5_CHUNK // 2
    blk = PAIR_LANES

    def pair_cols(q):
        return slice(q * PAIR_COLS, (q + 1) * PAIR_COLS)

    def pair_states(q):
        return slice(q * PAIR_STATES, (q + 1) * PAIR_STATES)

    def strided_u(j):
        return u_ref[0, pl.ds(j, S5_CHUNK_ROWS, stride=S5_CHUNK), :]

    um = um_ref[0]
    um_q = [[] for _ in range(SLAB_PAIRS)]
    for h in range(2):
        steps = range(h * half_steps, (h + 1) * half_steps)
        for q, t in enumerate(_transpose_pieces([strided_u(j) for j in steps])):
            ub_scr[q, :, h * V7X_LANES:(h + 1) * V7X_LANES] = t.astype(_bf16)
        meta = [um[:, j * V7X_LANES:(j + 1) * V7X_LANES] for j in steps]
        for q, t in enumerate(_transpose_pieces(meta)):
            um_q[q].append(t)

    for d in range(N_DIR):
        for q in range(SLAB_PAIRS):
            w_re = _pair_map(b_ref, d, 0, q)
            w_im = _pair_map(b_ref, d, 1, q)
            for j in range(S5_CHUNK):
                m = (S5_CHUNK - 1 - j) if d == 0 else j
                pr = p_ref[0, d, m, 0:1, pair_states(q)]
                pi = p_ref[0, d, m, 1:2, pair_states(q)]
                rows = slice(j * blk, (j + 1) * blk)
                w_scr[d, q, rows, 0:PAIR_STATES] = (w_re * pr - w_im * pi).astype(_bf16)
                w_scr[d, q, rows, PAIR_STATES:] = (w_re * pi + w_im * pr).astype(_bf16)
            x_scr[d, :, pair_cols(q)] = _dot(ub_scr[q], w_scr[d, q])
            if d == 0:
                xm_scr[:, pair_cols(q)] = _dot(
                    jnp.concatenate(um_q[q], axis=1).astype(_bf16), w_scr[0, q])

    a_re, a_im, init = [], [], []
    for d in range(N_DIR):
        for q in range(SLAB_PAIRS):
            re = slice(q * PAIR_COLS, q * PAIR_COLS + PAIR_STATES)
            im = slice(q * PAIR_COLS + PAIR_STATES, (q + 1) * PAIR_COLS)
            ar = p_ref[0, d, S5_CHUNK, 0:1, pair_states(q)]
            ai = p_ref[0, d, S5_CHUNK, 1:2, pair_states(q)]
            a_re.append(jnp.broadcast_to(ar, (V7X_SUBLANES, V7X_LANES)))
            a_im.append(jnp.broadcast_to(ai, (V7X_SUBLANES, V7X_LANES)))
            sr = jnp.zeros((1, V7X_LANES), _f32)
            si = jnp.zeros((1, V7X_LANES), _f32)
            if d == 0:
                for c in range(META_CHUNKS):
                    xr = xm_scr[c:c + 1, re]
                    xi = xm_scr[c:c + 1, im]
                    sr, si = ar * sr - ai * si + xr, ar * si + ai * sr + xi
            init.append(jnp.broadcast_to(sr, (V7X_SUBLANES, V7X_LANES)))
            init.append(jnp.broadcast_to(si, (V7X_SUBLANES, V7X_LANES)))
    _boundary_scan(x_scr, a_re, a_im, tuple(init))
    for d in range(N_DIR):
        sp_scr[d] = x_scr[d].astype(_bf16)

    def out_weights(c_re, c_imn, d, m, q):
        pr = p_ref[0, d, m, 0:1, pair_states(q)]
        pi = p_ref[0, d, m, 1:2, pair_states(q)]
        return jnp.concatenate([c_re * pr + c_imn * pi, c_imn * pr - c_re * pi],
                               axis=1).astype(_bf16)

    for q in range(SLAB_PAIRS):
        wy = []
        for d in range(N_DIR):
            c_re = _pair_map(c_ref, d, 0, q)
            c_imn = _pair_map(c_ref, d, 1, q)
            c_map = jnp.concatenate([c_re, c_imn], axis=1).astype(_bf16)
            tap_scr[d] = _dot_nt(w_scr[d, q], jnp.concatenate([c_map] * S5_CHUNK, axis=0))
            powers = [o + 1 if d == 0 else S5_CHUNK - o for o in range(S5_CHUNK)]
            wy.append(jnp.concatenate([out_weights(c_re, c_imn, d, m, q) for m in powers],
                                      axis=0))
        for o in range(S5_CHUNK):
            lanes = slice(o * blk, (o + 1) * blk)
            if o:
                toep_scr[0:o * blk, lanes] = tap_scr[0, (S5_CHUNK - 1 - o) * blk:
                                                     (S5_CHUNK - 1) * blk, lanes]
            toep_scr[o * blk:(o + 1) * blk, lanes] = (
                tap_scr[0, (S5_CHUNK - 1) * blk:, lanes] + tap_scr[1, 0:blk, lanes])
            if o < S5_CHUNK - 1:
                toep_scr[(o + 1) * blk:, lanes] = tap_scr[1, blk:(S5_CHUNK - o) * blk, lanes]
        y = _dot_nt(sp_scr[0, :, pair_cols(q)], wy[0])
        y += _dot_nt(sp_scr[1, :, pair_cols(q)], wy[1])
        y += _dot(ub_scr[q], toep_scr[...].astype(_bf16))
        yq_scr[q] = y

    for h in range(2):
        tiles = [yq_scr[q, :, h * V7X_LANES:(h + 1) * V7X_LANES] for q in range(SLAB_PAIRS)]
        for k, t in enumerate(_transpose_pieces(tiles)):
            o = h * half_steps + k
            y_ref[0, o] = t + strided_u(o) * d_ref[0]


def _s5(u, u_meta, b_maps, c_maps, ptab, d_skip):
    maps = pl.BlockSpec((N_DIR, 1, 2, V7X_LANES, V7X_LANES), lambda s: (0, s, 0, 0, 0))
    return pl.pallas_call(
        _s5_kernel,
        grid=(S5_SLABS,),
        in_specs=[
            pl.BlockSpec((1, S5_CHUNK_ROWS * S5_CHUNK, V7X_LANES), lambda s: (s, 0, 0)),
            pl.BlockSpec((1, V7X_SUBLANES, S5_CHUNK_COLS), lambda s: (s, 0, 0)),
            maps,
            maps,
            pl.BlockSpec((1, N_DIR, S5_CHUNK + 1, 2, SLAB_STATES), lambda s: (s, 0, 0, 0, 0)),
            pl.BlockSpec((1, 1, V7X_LANES), lambda s: (s, 0, 0)),
        ],
        out_specs=pl.BlockSpec((1, S5_CHUNK, S5_CHUNK_ROWS, V7X_LANES), lambda s: (s, 0, 0, 0)),
        out_shape=jax.ShapeDtypeStruct((S5_SLABS, S5_CHUNK, S5_CHUNK_ROWS, V7X_LANES), _f32),
        scratch_shapes=[
            pltpu.VMEM((SLAB_PAIRS, S5_CHUNK_ROWS, PAIR_COLS), _bf16),
            pltpu.VMEM((N_DIR, SLAB_PAIRS, PAIR_COLS, PAIR_COLS), _bf16),
            pltpu.VMEM((N_DIR, S5_CHUNK_ROWS, 2 * SLAB_STATES), _f32),
            pltpu.VMEM((N_DIR, S5_CHUNK_ROWS, 2 * SLAB_STATES), _bf16),
            pltpu.VMEM((N_DIR, PAIR_COLS, PAIR_COLS), _f32),
            pltpu.VMEM((PAIR_COLS, PAIR_COLS), _f32),
            pltpu.VMEM((SLAB_PAIRS, S5_CHUNK_ROWS, PAIR_COLS), _f32),
            pltpu.VMEM((V7X_SUBLANES, 2 * SLAB_STATES), _f32),
        ],
        compiler_params=_params(("arbitrary",)),
        name="s5",
    )(u, u_meta, b_maps, c_maps, ptab, d_skip)


def _mixer_kernel(h_ref, y_ref, q_ref, qp_ref, qn_ref, qm_ref, bc_ref, gs_ref, gc_ref,
                  wglu_ref, bglu_ref, wsup_ref, cw_ref, cb_ref, wcup_ref, wo_ref,
                  g_ref, wcast_ref, o_ref, wcast_out_ref, ys_scr, q_scr):
    t = pl.program_id(0)
    last = pl.num_programs(0) - 1
    tt = MIX_TILE_T
    tm = BATCH * tt
    n_c = tt // S5_CHUNK

    for s in range(S5_SLABS):
        for o in range(S5_CHUNK):
            for b in range(BATCH):
                ys_scr[s, pl.ds(b * tt + o, n_c, stride=S5_CHUNK), :] = (
                    y_ref[s, o, pl.ds(b, n_c, stride=BATCH), :])
    ys = jnp.concatenate([ys_scr[s] for s in range(S5_SLABS)], axis=1)
    ys = jax.nn.gelu(ys)
    glu = _dot(ys.astype(_bf16), wglu_ref[...]) + bglu_ref[...]
    ys = ys * jax.nn.sigmoid(glu)
    y_s = _dot(ys.astype(_bf16), wsup_ref[...])

    vs = []
    for b in range(BATCH):
        prev = jnp.where(t == 0, qm_ref[...], qp_ref[b]).astype(_f32)
        nxt = jnp.where(t == last, 0.0, qn_ref[b].astype(_f32))
        q_scr[b, 0:HALO_ROWS, :] = prev
        q_scr[b, HALO_ROWS:HALO_ROWS + tt, :] = q_ref[b].astype(_f32)
        q_scr[b, HALO_ROWS + tt:, :] = nxt
        vs.append(cw_ref[0:1, :] * q_scr[b, HALO_ROWS - 1:HALO_ROWS - 1 + tt, :]
                  + cw_ref[1:2, :] * q_scr[b, HALO_ROWS:HALO_ROWS + tt, :]
                  + cw_ref[2:3, :] * q_scr[b, HALO_ROWS + 1:HALO_ROWS + 1 + tt, :]
                  + cb_ref[...])
    v = jnp.concatenate(vs, axis=0)
    bc = bc_ref[...].reshape(tm, CONV_WIDTH).astype(_f32)
    y_c = _dot((bc * v).astype(_bf16), wcup_ref[...])

    merged = (gs_ref[...].reshape(tm, D_MODEL).astype(_f32) * y_s
              + gc_ref[...].reshape(tm, D_MODEL).astype(_f32) * y_c)
    mixed = _dot(merged.astype(_bf16), wo_ref[...])
    out = h_ref[...].reshape(tm, D_MODEL) + _rms_norm(mixed, g_ref[...])
    o_ref[...] = out.reshape(BATCH, tt, D_MODEL)
    wcast_out_ref[...] = wcast_ref[...].astype(_bf16)


def _mixer(x, y, q, q_meta, bc, gate_s, gate_c, w_glu, b_glu, w_s_up, conv_w, conv_b, w_c_up,
           w_o, g_post, w_cast):
    tt = MIX_TILE_T
    n_t = SEQ // tt
    cast_spec = pl.BlockSpec((w_cast.shape[0] // n_t, w_cast.shape[1]), lambda t: (t, 0))
    halo_per_tile = tt // HALO_ROWS
    n_halo = SEQ // HALO_ROWS
    tile = lambda t: (0, t, 0)
    resident = functools.partial(pl.BlockSpec, index_map=lambda t: (0, 0),
                                 pipeline_mode=pl.Buffered(1))
    return pl.pallas_call(
        _mixer_kernel,
        grid=(n_t,),
        in_specs=[
            pl.BlockSpec((BATCH, tt, D_MODEL), tile),
            pl.BlockSpec((S5_SLABS, S5_CHUNK, tt // S5_CHUNK * BATCH, V7X_LANES),
                         lambda t: (0, 0, t, 0)),
            pl.BlockSpec((BATCH, tt, CONV_WIDTH), tile),
            pl.BlockSpec((BATCH, HALO_ROWS, CONV_WIDTH),
                         lambda t: (0, jnp.maximum(t * halo_per_tile - 1, 0), 0)),
            pl.BlockSpec((BATCH, HALO_ROWS, CONV_WIDTH),
                         lambda t: (0, jnp.minimum((t + 1) * halo_per_tile, n_halo - 1), 0)),
            resident((HALO_ROWS, CONV_WIDTH)),
            pl.BlockSpec((BATCH, tt, CONV_WIDTH), tile),
            pl.BlockSpec((BATCH, tt, D_MODEL), tile),
            pl.BlockSpec((BATCH, tt, D_MODEL), tile),
            resident((S5_WIDTH, S5_WIDTH)),
            resident((1, S5_WIDTH)),
            resident((S5_WIDTH, D_MODEL)),
            resident((3, CONV_WIDTH)),
            resident((1, CONV_WIDTH)),
            resident((CONV_WIDTH, D_MODEL)),
            resident((D_MODEL, D_MODEL)),
            resident((1, D_MODEL)),
            cast_spec,
        ],
        out_specs=[pl.BlockSpec((BATCH, tt, D_MODEL), tile), cast_spec],
        out_shape=[jax.ShapeDtypeStruct((BATCH, SEQ, D_MODEL), _f32),
                   jax.ShapeDtypeStruct(w_cast.shape, _bf16)],
        scratch_shapes=[pltpu.VMEM((S5_SLABS, BATCH * tt, V7X_LANES), _f32),
                        pltpu.VMEM((BATCH, tt + 2 * HALO_ROWS, CONV_WIDTH), _f32)],
        compiler_params=_params(("arbitrary",)),
        name="mixer_tail",
    )(x, y, q, q, q, q_meta, bc, gate_s, gate_c, w_glu, b_glu, w_s_up, conv_w, conv_b, w_c_up,
      w_o, g_post, w_cast)


def _ffn_kernel(h_ref, gpre_ref, wgu_ref, wout_ref, gpost_ref, o_ref, hb_scr):
    j = pl.program_id(1)

    @pl.when(j == 0)
    def _():
        hb_scr[...] = _rms_norm(h_ref[...], gpre_ref[...]).astype(_bf16)
        o_ref[...] = jnp.zeros_like(o_ref)

    gu = _dot(hb_scr[...], wgu_ref[...])
    gate, up = gu[:, :FFN_TILE_HIDDEN], gu[:, FFN_TILE_HIDDEN:]
    act = (jax.nn.silu(gate) * up).astype(_bf16)
    o_ref[...] += _dot(act, wout_ref[...])

    @pl.when(j == pl.num_programs(1) - 1)
    def _():
        o_ref[...] = h_ref[...] + _rms_norm(o_ref[...], gpost_ref[...])


def _ffn(h1, g_pre, w_in, w_out, g_post):
    tm, th = FFN_TILE_ROWS, FFN_TILE_HIDDEN
    n_hidden = FFN_HIDDEN // th
    return pl.pallas_call(
        _ffn_kernel,
        grid=(ROWS // tm, n_hidden),
        in_specs=[
            pl.BlockSpec((tm, D_MODEL), lambda i, j: (i, 0)),
            pl.BlockSpec((1, D_MODEL), lambda i, j: (0, 0)),
            pl.BlockSpec((D_MODEL, 2 * th), lambda i, j: (0, j)),
            pl.BlockSpec((th, D_MODEL), lambda i, j: (j, 0)),
            pl.BlockSpec((1, D_MODEL), lambda i, j: (0, 0)),
        ],
        out_specs=pl.BlockSpec((tm, D_MODEL), lambda i, j: (i, 0)),
        out_shape=jax.ShapeDtypeStruct((ROWS, D_MODEL), _f32),
        scratch_shapes=[pltpu.VMEM((tm, D_MODEL), _bf16)],
        compiler_params=_params(("arbitrary", "arbitrary")),
        name="ffn",
    )(h1, g_pre, w_in, w_out, g_post)


def kernel(x, meta, g_mix_pre, g_mix_post, g_ffn_pre, g_ffn_post, w_in, gate_b, lam_re, lam_im,
           log_dt, b_re, b_im, c_re, c_im, d_skip, w_glu, b_glu, w_s_up, conv_w, conv_b, w_c_up,
           w_o, w_ffn_in, w_ffn_out):
    l = 0
    w_in_b = _cast_w_in(w_in[l])
    gate_b2 = gate_b[l].reshape(2, 1, D_MODEL)
    g_pre = g_mix_pre[l].reshape(1, D_MODEL)

    ptab, bz_re, bz_im = _discretise(lam_re[l], lam_im[l], log_dt[l], b_re[l], b_im[l])
    dg = N_DIR * S5_GROUPS
    b_maps = _compact_maps(bz_re, bz_im)
    c_maps = _compact_maps(c_re[l].reshape(dg, S5_GROUP, S5_STATE),
                           -c_im[l].reshape(dg, S5_GROUP, S5_STATE))
    ptab = ptab.reshape(S5_CHUNK + 1, 2, N_DIR, S5_SLABS, SLAB_STATES)
    ptab = ptab.transpose(3, 2, 0, 1, 4)

    (u5, bc, q, gate_s, gate_c, w_ffn_in_b, w_glu_b, w_s_up_b, w_c_up_b, w_o_b) = _in_proj(
        x, g_pre, w_in_b, gate_b2, IN_TILE_ROWS,
        cast=((w_ffn_in[l], _ffn_in_plan()), (w_glu[l], None), (w_s_up[l], None),
              (w_c_up[l], None), (w_o[l], None)))
    um5, q_meta = _meta_proj(meta.astype(x.dtype), g_pre, w_in_b)

    u = u5.reshape(S5_SLABS, S5_CHUNK_ROWS * S5_CHUNK, V7X_LANES)
    u_meta = jnp.pad(um5.reshape(S5_SLABS, META_CHUNKS, S5_CHUNK_COLS),
                     ((0, 0), (0, V7X_SUBLANES - META_CHUNKS), (0, 0)))
    y = _s5(u, u_meta, b_maps, c_maps, ptab, d_skip[l].reshape(S5_SLABS, 1, V7X_LANES))

    h1, w_ffn_out_b = _mixer(x, y, q, q_meta, bc, gate_s, gate_c, w_glu_b,
                             b_glu[l].reshape(1, S5_WIDTH), w_s_up_b, conv_w[l],
                             conv_b[l].reshape(1, CONV_WIDTH), w_c_up_b, w_o_b,
                             g_mix_post[l].reshape(1, D_MODEL), w_ffn_out[l])

    h2 = _ffn(h1.reshape(ROWS, D_MODEL), g_ffn_pre[l].reshape(1, D_MODEL),
              w_ffn_in_b, w_ffn_out_b, g_ffn_post[l].reshape(1, D_MODEL))
    return h2.reshape(BATCH, SEQ, D_MODEL)
```

```python
import functools
import math

import jax
import jax.numpy as jnp
from jax import lax
from jax.experimental import pallas as pl
from jax.experimental.pallas import tpu as pltpu

D_MODEL = 2048
BATCH = 4
SEQ = 2048
N_META = 16
S5_WIDTH = 1024
S5_GROUP = 16
S5_GROUPS = S5_WIDTH // S5_GROUP
S5_STATE = 64
N_DIR = 2
CONV_WIDTH = 1024
FFN_HIDDEN = ((math.ceil(8 * D_MODEL / 3) + 255) // 256) * 256
IN_COLS = S5_WIDTH + 3 * CONV_WIDTH + 2 * D_MODEL
RMS_EPS = 1e-6
LAM_RE_MAX = -1e-4

ROWS = SEQ * BATCH

V7X_SUBLANES = 8
V7X_LANES = 128
V7X_MXU_DIM = 256
V7X_VMEM_LIMIT_BYTES = 56 * 1024 * 1024

S5_SLABS = S5_WIDTH // V7X_LANES
SLAB_GROUPS = V7X_LANES // S5_GROUP
SLAB_STATES = SLAB_GROUPS * S5_STATE
PAIR_LANES = 2 * S5_GROUP
SLAB_PAIRS = V7X_LANES // PAIR_LANES
PAIR_STATES = 2 * S5_STATE
S5_CHUNK = V7X_SUBLANES
PAIR_COLS = S5_CHUNK * PAIR_LANES
S5_CHUNKS = SEQ // S5_CHUNK
S5_CHUNK_ROWS = S5_CHUNKS * BATCH
S5_CHUNK_COLS = S5_CHUNK * V7X_LANES
META_CHUNKS = N_META // S5_CHUNK
STATE_REPEATS = V7X_LANES // S5_STATE

IN_TILE_ROWS = 1024
IN_STEPS = 8
IN_SEG_COLS = S5_WIDTH // IN_STEPS
CAST_TILE_ROWS = 128
MIX_TILE_T = 64
FFN_TILE_ROWS = 1024
FFN_TILE_HIDDEN = 256
HALO_ROWS = 16

_f32 = jnp.float32
_bf16 = jnp.bfloat16


def _rms_norm(xf, g):
    r = lax.rsqrt(jnp.mean(xf * xf, axis=-1, keepdims=True) + RMS_EPS)
    return xf * r * g


def _params(sem):
    return pltpu.CompilerParams(dimension_semantics=sem,
                                vmem_limit_bytes=V7X_VMEM_LIMIT_BYTES)


def _dot(a, b):
    return jnp.dot(a, b, preferred_element_type=_f32)


def _dot_nt(a, b):
    return lax.dot_general(a, b, (((1,), (1,)), ((), ())), preferred_element_type=_f32)


def _discretise_kernel(lam_re_ref, lam_im_ref, log_dt_ref, b_re_ref, b_im_ref,
                       pow_ref, bz_re_ref, bz_im_ref):
    lr = jnp.minimum(lam_re_ref[...], LAM_RE_MAX)
    li = lam_im_ref[...]
    dt = jnp.exp(log_dt_ref[...])
    mag = jnp.exp(lr * dt)
    ar = mag * jnp.cos(li * dt)
    ai = mag * jnp.sin(li * dt)
    den = lr * lr + li * li
    nr = ar - 1.0
    zr = (nr * lr + ai * li) / den
    zi = (ai * lr - nr * li) / den
    pr = jnp.ones_like(ar)
    pi = jnp.zeros_like(ar)
    for m in range(S5_CHUNK + 1):
        pow_ref[m, 0] = pr
        pow_ref[m, 1] = pi
        pr, pi = pr * ar - pi * ai, pr * ai + pi * ar
    b_re = b_re_ref[...]
    b_im = b_im_ref[...]
    bz_re_ref[...] = zr[:, None, :] * b_re - zi[:, None, :] * b_im
    bz_im_ref[...] = zr[:, None, :] * b_im + zi[:, None, :] * b_re


def _discretise(lam_re, lam_im, log_dt, b_re, b_im):
    dg = N_DIR * S5_GROUPS
    b_re_t = jnp.swapaxes(b_re, -1, -2).reshape(dg, S5_GROUP, S5_STATE)
    b_im_t = jnp.swapaxes(b_im, -1, -2).reshape(dg, S5_GROUP, S5_STATE)
    return pl.pallas_call(
        _discretise_kernel,
        out_shape=(jax.ShapeDtypeStruct((S5_CHUNK + 1, 2, dg, S5_STATE), _f32),
                   jax.ShapeDtypeStruct((dg, S5_GROUP, S5_STATE), _f32),
                   jax.ShapeDtypeStruct((dg, S5_GROUP, S5_STATE), _f32)),
        name="s5_discretise",
    )(lam_re.reshape(dg, S5_STATE), lam_im.reshape(dg, S5_STATE),
      log_dt.reshape(dg, 1), b_re_t, b_im_t)


def _compact_maps(w_re, w_im):
    def part(w):
        w = w.reshape(N_DIR, S5_SLABS, V7X_LANES, S5_STATE)
        return jnp.concatenate([w] * STATE_REPEATS, axis=-1)
    return jnp.stack([part(w_re), part(w_im)], axis=2)


def _w_in_pieces(j):
    seg, gseg = IN_SEG_COLS, 2 * IN_SEG_COLS
    pieces = [(n * S5_WIDTH + j * seg, seg) for n in range(4)]
    pieces += [(4 * S5_WIDTH + n * D_MODEL + j * gseg, gseg) for n in range(2)]
    return pieces


def _cast_w_in_kernel(w_ref, o_ref):
    dst = 0
    for j in range(IN_STEPS):
        for src, width in _w_in_pieces(j):
            o_ref[:, dst:dst + width] = w_ref[:, src:src + width].astype(_bf16)
            dst += width


def _cast_w_in(w):
    spec = pl.BlockSpec((CAST_TILE_ROWS, IN_COLS), lambda i: (i, 0))
    return pl.pallas_call(
        _cast_w_in_kernel,
        grid=(D_MODEL // CAST_TILE_ROWS,),
        in_specs=[spec],
        out_specs=spec,
        out_shape=jax.ShapeDtypeStruct(w.shape, _bf16),
        compiler_params=_params(("arbitrary",)),
        name="cast_w_in",
    )(w)


def _ffn_in_plan():
    th = FFN_TILE_HIDDEN
    plan = []
    for j in range(FFN_HIDDEN // th):
        plan.append((j * th, 2 * j * th, th))
        plan.append((FFN_HIDDEN + j * th, (2 * j + 1) * th, th))
    return plan


def _in_proj_kernel(x_ref, g_ref, w_ref, gbs_ref, gbc_ref, *rest, cast_plans):
    n_cast = len(cast_plans)
    cast_in = rest[:n_cast]
    u_ref, bc_ref, q_ref, gs_ref, gc_ref = rest[n_cast:n_cast + 5]
    cast_out = rest[n_cast + 5:2 * n_cast + 5]
    h_scr = rest[2 * n_cast + 5]
    seg = IN_SEG_COLS

    @pl.when(pl.program_id(2) == 0)
    def _():
        h_scr[...] = _rms_norm(x_ref[...], g_ref[...]).astype(_bf16)

    proj = _dot(h_scr[...], w_ref[...])
    for s in range(seg // V7X_LANES):
        u_ref[s] = proj[:, s * V7X_LANES:(s + 1) * V7X_LANES].reshape(-1, S5_CHUNK, V7X_LANES)
    bc_ref[...] = proj[:, 2 * seg:3 * seg].astype(_bf16)
    q_ref[...] = (proj[:, 3 * seg:4 * seg] * proj[:, seg:2 * seg]).astype(_bf16)
    gs_ref[...] = jax.nn.sigmoid(proj[:, 4 * seg:6 * seg] + gbs_ref[0]).astype(_bf16)
    gc_ref[...] = jax.nn.sigmoid(proj[:, 6 * seg:8 * seg] + gbc_ref[0]).astype(_bf16)
    for src, dst, plan in zip(cast_in, cast_out, cast_plans):
        for s0, d0, width in plan:
            dst[:, d0:d0 + width] = src[:, s0:s0 + width].astype(_bf16)


def _in_proj(x3, g, w_in_b, gate_b, tile_rows, cast=()):
    nb, t_len, _ = x3.shape
    n_t = t_len // tile_rows
    n_steps = nb * n_t * IN_STEPS
    seg, gseg = IN_SEG_COLS, 2 * IN_SEG_COLS
    tile = lambda b, t, j: (b, t, j)
    step = lambda b, t, j: ((b * n_t + t) * IN_STEPS + j, 0)
    cast_w = [w for w, _ in cast]
    cast_plans = tuple(tuple(plan) if plan else ((0, 0, w.shape[1]),) for w, plan in cast)
    cast_specs = [pl.BlockSpec((w.shape[0] // n_steps, w.shape[1]), step) for w in cast_w]
    gb = lambda n: pl.BlockSpec((1, 1, gseg), lambda b, t, j: (n, 0, j))
    return pl.pallas_call(
        functools.partial(_in_proj_kernel, cast_plans=cast_plans),
        grid=(nb, n_t, IN_STEPS),
        in_specs=[
            pl.BlockSpec((None, tile_rows, D_MODEL), lambda b, t, j: (b, t, 0)),
            pl.BlockSpec((1, D_MODEL), lambda b, t, j: (0, 0)),
            pl.BlockSpec((D_MODEL, IN_COLS // IN_STEPS), lambda b, t, j: (0, j)),
            gb(0), gb(1),
        ] + cast_specs,
        out_specs=[
            pl.BlockSpec((seg // V7X_LANES, tile_rows // S5_CHUNK, None, S5_CHUNK, V7X_LANES),
                         lambda b, t, j: (j, t, b, 0, 0)),
            pl.BlockSpec((None, tile_rows, seg), tile),
            pl.BlockSpec((None, tile_rows, seg), tile),
            pl.BlockSpec((None, tile_rows, gseg), tile),
            pl.BlockSpec((None, tile_rows, gseg), tile),
        ] + cast_specs,
        out_shape=[jax.ShapeDtypeStruct(
                       (S5_SLABS, t_len // S5_CHUNK, nb, S5_CHUNK, V7X_LANES), _f32),
                   jax.ShapeDtypeStruct((nb, t_len, CONV_WIDTH), _bf16),
                   jax.ShapeDtypeStruct((nb, t_len, CONV_WIDTH), _bf16),
                   jax.ShapeDtypeStruct((nb, t_len, D_MODEL), _bf16),
                   jax.ShapeDtypeStruct((nb, t_len, D_MODEL), _bf16)]
                  + [jax.ShapeDtypeStruct(w.shape, _bf16) for w in cast_w],
        scratch_shapes=[pltpu.VMEM((tile_rows, D_MODEL), _bf16)],
        compiler_params=_params(("arbitrary", "arbitrary", "arbitrary")),
        name="in_proj",
    )(x3, g, w_in_b, gate_b, gate_b, *cast_w)


def _meta_proj_kernel(x_ref, g_ref, w_ref, u_ref, q_ref, h_scr):
    seg = IN_SEG_COLS

    @pl.when(pl.program_id(0) == 0)
    def _():
        h_scr[...] = _rms_norm(x_ref[...], g_ref[...]).astype(_bf16)

    proj = _dot(h_scr[...], w_ref[...])
    for s in range(seg // V7X_LANES):
        u_ref[s] = proj[:, s * V7X_LANES:(s + 1) * V7X_LANES].reshape(-1, S5_CHUNK, V7X_LANES)
    q_ref[...] = (proj[:, 3 * seg:4 * seg] * proj[:, seg:2 * seg]).astype(_bf16)


def _meta_proj(meta, g, w_in_b):
    seg = IN_SEG_COLS
    return pl.pallas_call(
        _meta_proj_kernel,
        grid=(IN_STEPS,),
        in_specs=[
            pl.BlockSpec((N_META, D_MODEL), lambda j: (0, 0)),
            pl.BlockSpec((1, D_MODEL), lambda j: (0, 0)),
            pl.BlockSpec((D_MODEL, 4 * seg), lambda j: (0, 2 * j)),
        ],
        out_specs=[
            pl.BlockSpec((seg // V7X_LANES, META_CHUNKS, S5_CHUNK, V7X_LANES),
                         lambda j: (j, 0, 0, 0)),
            pl.BlockSpec((N_META, seg), lambda j: (0, j)),
        ],
        out_shape=[jax.ShapeDtypeStruct((S5_SLABS, META_CHUNKS, S5_CHUNK, V7X_LANES), _f32),
                   jax.ShapeDtypeStruct((N_META, CONV_WIDTH), _bf16)],
        scratch_shapes=[pltpu.VMEM((N_META, D_MODEL), _bf16)],
        compiler_params=_params(("arbitrary",)),
        name="meta_proj",
    )(meta, g, w_in_b)


def _pair_map(ref, d, part, q):
    rows = lax.broadcasted_iota(jnp.int32, (PAIR_LANES, V7X_LANES), 0)
    cols = lax.broadcasted_iota(jnp.int32, (PAIR_LANES, V7X_LANES), 1)
    same_group = (rows // S5_GROUP) == (cols // S5_STATE)
    return jnp.where(same_group, ref[d, 0, part, q * PAIR_LANES:(q + 1) * PAIR_LANES, :], 0.0)


def _transpose_pieces(src):
    piece = lax.broadcasted_iota(jnp.int32, src[0].shape, 1) // PAIR_LANES
    out = []
    for i in range(SLAB_PAIRS):
        acc = None
        for j in reversed(range(SLAB_PAIRS)):
            shift = ((j - i) * PAIR_LANES) % V7X_LANES
            moved = pltpu.roll(src[j], shift, 1) if shift else src[j]
            acc = moved if acc is None else jnp.where(piece == j, moved, acc)
        out.append(acc)
    return out


def _boundary_scan(x_scr, a_re, a_im, init):
    n_tiles = S5_CHUNK_ROWS // V7X_SUBLANES
    row = lax.broadcasted_iota(jnp.int32, (V7X_SUBLANES, V7X_LANES), 0)

    def step(ar, ai, sr, si, xr, xi):
        return ar * sr - ai * si + xr, ar * si + ai * sr + xi

    def body(i, carry):
        new = []
        for d in range(N_DIR):
            first = (row < BATCH) if d == 0 else (row >= BATCH)
            ii = i if d == 0 else (n_tiles - 1 - i)
            r0 = pl.multiple_of(ii * V7X_SUBLANES, V7X_SUBLANES)
            for q in range(SLAB_PAIRS):
                n = d * SLAB_PAIRS + q
                re = slice(q * PAIR_COLS, q * PAIR_COLS + PAIR_STATES)
                im = slice(q * PAIR_COLS + PAIR_STATES, (q + 1) * PAIR_COLS)
                cr, ci = carry[2 * n], carry[2 * n + 1]
                xr = x_scr[d, pl.ds(r0, V7X_SUBLANES), re]
                xi = x_scr[d, pl.ds(r0, V7X_SUBLANES), im]
                tr, ti = step(a_re[n], a_im[n], cr, ci, xr, xi)
                outr = jnp.where(first, cr, pltpu.roll(tr, BATCH, 0))
                outi = jnp.where(first, ci, pltpu.roll(ti, BATCH, 0))
                x_scr[d, pl.ds(r0, V7X_SUBLANES), re] = outr
                x_scr[d, pl.ds(r0, V7X_SUBLANES), im] = outi
                tr, ti = step(a_re[n], a_im[n], outr, outi, xr, xi)
                new.append(jnp.where(first, pltpu.roll(tr, BATCH, 0), tr))
                new.append(jnp.where(first, pltpu.roll(ti, BATCH, 0), ti))
        return tuple(new)

    lax.fori_loop(0, n_tiles, body, init)


def _s5_kernel(u_ref, um_ref, b_ref, c_ref, p_ref, d_ref, y_ref,
               ub_scr, uf_scr, w_scr, x_scr, sp_scr, tap_scr, toep_scr, xm_scr):
    half_steps = S5_CHUNK // 2
    blk = PAIR_LANES

    def pair_cols(q):
        return slice(q * PAIR_COLS, (q + 1) * PAIR_COLS)

    def pair_states(q):
        return slice(q * PAIR_STATES, (q + 1) * PAIR_STATES)

    def strided_u(j):
        return u_ref[0, pl.ds(j, S5_CHUNK_ROWS, stride=S5_CHUNK), :]

    um = um_ref[0]
    um_q = [[] for _ in range(SLAB_PAIRS)]
    for h in range(2):
        steps = range(h * half_steps, (h + 1) * half_steps)
        for q, t in enumerate(_transpose_pieces([strided_u(j) for j in steps])):
            uf_scr[q, :, h * V7X_LANES:(h + 1) * V7X_LANES] = t
            ub_scr[q, :, h * V7X_LANES:(h + 1) * V7X_LANES] = t.astype(_bf16)
        meta = [um[:, j * V7X_LANES:(j + 1) * V7X_LANES] for j in steps]
        for q, t in enumerate(_transpose_pieces(meta)):
            um_q[q].append(t)

    for d in range(N_DIR):
        for q in range(SLAB_PAIRS):
            w_re = _pair_map(b_ref, d, 0, q)
            w_im = _pair_map(b_ref, d, 1, q)
            for j in range(S5_CHUNK):
                m = (S5_CHUNK - 1 - j) if d == 0 else j
                pr = p_ref[0, d, m, 0:1, pair_states(q)]
                pi = p_ref[0, d, m, 1:2, pair_states(q)]
                rows = slice(j * blk, (j + 1) * blk)
                w_scr[d, q, rows, 0:PAIR_STATES] = (w_re * pr - w_im * pi).astype(_bf16)
                w_scr[d, q, rows, PAIR_STATES:] = (w_re * pi + w_im * pr).astype(_bf16)
            x_scr[d, :, pair_cols(q)] = _dot(ub_scr[q], w_scr[d, q])
            if d == 0:
                xm_scr[:, pair_cols(q)] = _dot(
                    jnp.concatenate(um_q[q], axis=1).astype(_bf16), w_scr[0, q])

    a_re, a_im, init = [], [], []
    for d in range(N_DIR):
        for q in range(SLAB_PAIRS):
            re = slice(q * PAIR_COLS, q * PAIR_COLS + PAIR_STATES)
            im = slice(q * PAIR_COLS + PAIR_STATES, (q + 1) * PAIR_COLS)
            ar = p_ref[0, d, S5_CHUNK, 0:1, pair_states(q)]
            ai = p_ref[0, d, S5_CHUNK, 1:2, pair_states(q)]
            a_re.append(jnp.broadcast_to(ar, (V7X_SUBLANES, V7X_LANES)))
            a_im.append(jnp.broadcast_to(ai, (V7X_SUBLANES, V7X_LANES)))
            sr = jnp.zeros((1, V7X_LANES), _f32)
            si = jnp.zeros((1, V7X_LANES), _f32)
            if d == 0:
                for c in range(META_CHUNKS):
                    xr = xm_scr[c:c + 1, re]
                    xi = xm_scr[c:c + 1, im]
                    sr, si = ar * sr - ai * si + xr, ar * si + ai * sr + xi
            init.append(jnp.broadcast_to(sr, (V7X_SUBLANES, V7X_LANES)))
            init.append(jnp.broadcast_to(si, (V7X_SUBLANES, V7X_LANES)))
    _boundary_scan(x_scr, a_re, a_im, tuple(init))
    for d in range(N_DIR):
        sp_scr[d] = x_scr[d].astype(_bf16)

    def out_weights(c_re, c_imn, d, m, q):
        pr = p_ref[0, d, m, 0:1, pair_states(q)]
        pi = p_ref[0, d, m, 1:2, pair_states(q)]
        return jnp.concatenate([c_re * pr + c_imn * pi, c_imn * pr - c_re * pi],
                               axis=1).astype(_bf16)

    for q in range(SLAB_PAIRS):
        wy = []
        for d in range(N_DIR):
            c_re = _pair_map(c_ref, d, 0, q)
            c_imn = _pair_map(c_ref, d, 1, q)
            c_map = jnp.concatenate([c_re, c_imn], axis=1).astype(_bf16)
            tap_scr[d] = _dot_nt(w_scr[d, q], jnp.concatenate([c_map] * S5_CHUNK, axis=0))
            powers = [o + 1 if d == 0 else S5_CHUNK - o for o in range(S5_CHUNK)]
            wy.append(jnp.concatenate([out_weights(c_re, c_imn, d, m, q) for m in powers],
                                      axis=0))
        for o in range(S5_CHUNK):
            lanes = slice(o * blk, (o + 1) * blk)
            if o:
                toep_scr[0:o * blk, lanes] = tap_scr[0, (S5_CHUNK - 1 - o) * blk:
                                                     (S5_CHUNK - 1) * blk, lanes]
            toep_scr[o * blk:(o + 1) * blk, lanes] = (
                tap_scr[0, (S5_CHUNK - 1) * blk:, lanes] + tap_scr[1, 0:blk, lanes])
            if o < S5_CHUNK - 1:
                toep_scr[(o + 1) * blk:, lanes] = tap_scr[1, blk:(S5_CHUNK - o) * blk, lanes]
        y = _dot_nt(sp_scr[0, :, pair_cols(q)], wy[0])
        y += _dot_nt(sp_scr[1, :, pair_cols(q)], wy[1])
        y += _dot(ub_scr[q], toep_scr[...].astype(_bf16))
        y += uf_scr[q] * d_ref[0, q]
        for h in range(2):
            y_ref[0, q, h] = y[:, h * V7X_LANES:(h + 1) * V7X_LANES]


def _s5(u, u_meta, b_maps, c_maps, ptab, d_skip):
    maps = pl.BlockSpec((N_DIR, 1, 2, V7X_LANES, V7X_LANES), lambda s: (0, s, 0, 0, 0))
    return pl.pallas_call(
        _s5_kernel,
        grid=(S5_SLABS,),
        in_specs=[
            pl.BlockSpec((1, S5_CHUNK_ROWS * S5_CHUNK, V7X_LANES), lambda s: (s, 0, 0)),
            pl.BlockSpec((1, V7X_SUBLANES, S5_CHUNK_COLS), lambda s: (s, 0, 0)),
            maps,
            maps,
            pl.BlockSpec((1, N_DIR, S5_CHUNK + 1, 2, SLAB_STATES), lambda s: (s, 0, 0, 0, 0)),
            pl.BlockSpec((1, SLAB_PAIRS, 1, PAIR_COLS), lambda s: (s, 0, 0, 0)),
        ],
        out_specs=pl.BlockSpec((1, SLAB_PAIRS, 2, S5_CHUNK_ROWS, V7X_LANES),
                               lambda s: (s, 0, 0, 0, 0)),
        out_shape=jax.ShapeDtypeStruct((S5_SLABS, SLAB_PAIRS, 2, S5_CHUNK_ROWS, V7X_LANES),
                                       _f32),
        scratch_shapes=[
            pltpu.VMEM((SLAB_PAIRS, S5_CHUNK_ROWS, PAIR_COLS), _bf16),
            pltpu.VMEM((SLAB_PAIRS, S5_CHUNK_ROWS, PAIR_COLS), _f32),
            pltpu.VMEM((N_DIR, SLAB_PAIRS, PAIR_COLS, PAIR_COLS), _bf16),
            pltpu.VMEM((N_DIR, S5_CHUNK_ROWS, 2 * SLAB_STATES), _f32),
            pltpu.VMEM((N_DIR, S5_CHUNK_ROWS, 2 * SLAB_STATES), _bf16),
            pltpu.VMEM((N_DIR, PAIR_COLS, PAIR_COLS), _f32),
            pltpu.VMEM((PAIR_COLS, PAIR_COLS), _f32),
            pltpu.VMEM((V7X_SUBLANES, 2 * SLAB_STATES), _f32),
        ],
        compiler_params=_params(("arbitrary",)),
        name="s5",
    )(u, u_meta, b_maps, c_maps, ptab, d_skip)


def _mixer_kernel(h_ref, y_ref, q_ref, qp_ref, qn_ref, qm_ref, bc_ref, gs_ref, gc_ref,
                  wglu_ref, bglu_ref, wsup_ref, cw_ref, cb_ref, wcup_ref, wo_ref,
                  g_ref, wcast_ref, o_ref, wcast_out_ref, ys_scr, q_scr):
    t = pl.program_id(0)
    last = pl.num_programs(0) - 1
    tt = MIX_TILE_T
    tm = BATCH * tt
    n_c = tt // S5_CHUNK

    vs = []
    for b in range(BATCH):
        prev = jnp.where(t == 0, qm_ref[...], qp_ref[b]).astype(_f32)
        nxt = jnp.where(t == last, 0.0, qn_ref[b].astype(_f32))
        q_scr[b, 0:HALO_ROWS, :] = prev
        q_scr[b, HALO_ROWS:HALO_ROWS + tt, :] = q_ref[b].astype(_f32)
        q_scr[b, HALO_ROWS + tt:, :] = nxt
        vs.append(cw_ref[0:1, :] * q_scr[b, HALO_ROWS - 1:HALO_ROWS - 1 + tt, :]
                  + cw_ref[1:2, :] * q_scr[b, HALO_ROWS:HALO_ROWS + tt, :]
                  + cw_ref[2:3, :] * q_scr[b, HALO_ROWS + 1:HALO_ROWS + 1 + tt, :]
                  + cb_ref[...])
    v = jnp.concatenate(vs, axis=0)
    bc = bc_ref[...].reshape(tm, CONV_WIDTH).astype(_f32)
    y_c = _dot((bc * v).astype(_bf16), wcup_ref[...])

    half_steps = S5_CHUNK // 2
    for s in range(S5_SLABS):
        for h in range(2):
            for b in range(BATCH):
                tiles = [y_ref[s, q, h, pl.ds(b, n_c, stride=BATCH), :]
                         for q in range(SLAB_PAIRS)]
                for k, tile in enumerate(_transpose_pieces(tiles)):
                    o = h * half_steps + k
                    ys_scr[s, pl.ds(b * tt + o, n_c, stride=S5_CHUNK), :] = tile
    ys = jnp.concatenate([ys_scr[s] for s in range(S5_SLABS)], axis=1)
    ys = jax.nn.gelu(ys)
    glu = _dot(ys.astype(_bf16), wglu_ref[...]) + bglu_ref[...]
    ys = ys * jax.nn.sigmoid(glu)
    y_s = _dot(ys.astype(_bf16), wsup_ref[...])

    merged = (gs_ref[...].reshape(tm, D_MODEL).astype(_f32) * y_s
              + gc_ref[...].reshape(tm, D_MODEL).astype(_f32) * y_c)
    mixed = _dot(merged.astype(_bf16), wo_ref[...])
    out = h_ref[...].reshape(tm, D_MODEL) + _rms_norm(mixed, g_ref[...])
    o_ref[...] = out.reshape(BATCH, tt, D_MODEL)
    wcast_out_ref[...] = wcast_ref[...].astype(_bf16)


def _mixer(x, y, q, q_meta, bc, gate_s, gate_c, w_glu, b_glu, w_s_up, conv_w, conv_b, w_c_up,
           w_o, g_post, w_cast):
    tt = MIX_TILE_T
    n_t = SEQ // tt
    cast_spec = pl.BlockSpec((w_cast.shape[0] // n_t, w_cast.shape[1]), lambda t: (t, 0))
    halo_per_tile = tt // HALO_ROWS
    n_halo = SEQ // HALO_ROWS
    tile = lambda t: (0, t, 0)
    resident = functools.partial(pl.BlockSpec, index_map=lambda t: (0, 0),
                                 pipeline_mode=pl.Buffered(1))
    return pl.pallas_call(
        _mixer_kernel,
        grid=(n_t,),
        in_specs=[
            pl.BlockSpec((BATCH, tt, D_MODEL), tile),
            pl.BlockSpec((S5_SLABS, SLAB_PAIRS, 2, tt // S5_CHUNK * BATCH, V7X_LANES),
                         lambda t: (0, 0, 0, t, 0)),
            pl.BlockSpec((BATCH, tt, CONV_WIDTH), tile),
            pl.BlockSpec((BATCH, HALO_ROWS, CONV_WIDTH),
                         lambda t: (0, jnp.maximum(t * halo_per_tile - 1, 0), 0)),
            pl.BlockSpec((BATCH, HALO_ROWS, CONV_WIDTH),
                         lambda t: (0, jnp.minimum((t + 1) * halo_per_tile, n_halo - 1), 0)),
            resident((HALO_ROWS, CONV_WIDTH)),
            pl.BlockSpec((BATCH, tt, CONV_WIDTH), tile),
            pl.BlockSpec((BATCH, tt, D_MODEL), tile),
            pl.BlockSpec((BATCH, tt, D_MODEL), tile),
            resident((S5_WIDTH, S5_WIDTH)),
            resident((1, S5_WIDTH)),
            resident((S5_WIDTH, D_MODEL)),
            resident((3, CONV_WIDTH)),
            resident((1, CONV_WIDTH)),
            resident((CONV_WIDTH, D_MODEL)),
            resident((D_MODEL, D_MODEL)),
            resident((1, D_MODEL)),
            cast_spec,
        ],
        out_specs=[pl.BlockSpec((BATCH, tt, D_MODEL), tile), cast_spec],
        out_shape=[jax.ShapeDtypeStruct((BATCH, SEQ, D_MODEL), _f32),
                   jax.ShapeDtypeStruct(w_cast.shape, _bf16)],
        scratch_shapes=[pltpu.VMEM((S5_SLABS, BATCH * tt, V7X_LANES), _f32),
                        pltpu.VMEM((BATCH, tt + 2 * HALO_ROWS, CONV_WIDTH), _f32)],
        compiler_params=_params(("arbitrary",)),
        name="mixer_tail",
    )(x, y, q, q, q, q_meta, bc, gate_s, gate_c, w_glu, b_glu, w_s_up, conv_w, conv_b, w_c_up,
      w_o, g_post, w_cast)


def _ffn_kernel(h_ref, gpre_ref, wgu_ref, wout_ref, gpost_ref, o_ref, hb_scr):
    j = pl.program_id(1)

    @pl.when(j == 0)
    def _():
        hb_scr[...] = _rms_norm(h_ref[...], gpre_ref[...]).astype(_bf16)
        o_ref[...] = jnp.zeros_like(o_ref)

    gu = _dot(hb_scr[...], wgu_ref[...])
    gate, up = gu[:, :FFN_TILE_HIDDEN], gu[:, FFN_TILE_HIDDEN:]
    act = (jax.nn.silu(gate) * up).astype(_bf16)
    o_ref[...] += _dot(act, wout_ref[...])

    @pl.when(j == pl.num_programs(1) - 1)
    def _():
        o_ref[...] = h_ref[...] + _rms_norm(o_ref[...], gpost_ref[...])


def _ffn(h1, g_pre, w_in, w_out, g_post):
    tm, th = FFN_TILE_ROWS, FFN_TILE_HIDDEN
    n_hidden = FFN_HIDDEN // th
    return pl.pallas_call(
        _ffn_kernel,
        grid=(ROWS // tm, n_hidden),
        in_specs=[
            pl.BlockSpec((tm, D_MODEL), lambda i, j: (i, 0)),
            pl.BlockSpec((1, D_MODEL), lambda i, j: (0, 0)),
            pl.BlockSpec((D_MODEL, 2 * th), lambda i, j: (0, j)),
            pl.BlockSpec((th, D_MODEL), lambda i, j: (j, 0)),
            pl.BlockSpec((1, D_MODEL), lambda i, j: (0, 0)),
        ],
        out_specs=pl.BlockSpec((tm, D_MODEL), lambda i, j: (i, 0)),
        out_shape=jax.ShapeDtypeStruct((ROWS, D_MODEL), _f32),
        scratch_shapes=[pltpu.VMEM((tm, D_MODEL), _bf16)],
        compiler_params=_params(("arbitrary", "arbitrary")),
        name="ffn",
    )(h1, g_pre, w_in, w_out, g_post)


def kernel(x, meta, g_mix_pre, g_mix_post, g_ffn_pre, g_ffn_post, w_in, gate_b, lam_re, lam_im,
           log_dt, b_re, b_im, c_re, c_im, d_skip, w_glu, b_glu, w_s_up, conv_w, conv_b, w_c_up,
           w_o, w_ffn_in, w_ffn_out):
    l = 0
    w_in_b = _cast_w_in(w_in[l])
    gate_b2 = gate_b[l].reshape(2, 1, D_MODEL)
    g_pre = g_mix_pre[l].reshape(1, D_MODEL)

    ptab, bz_re, bz_im = _discretise(lam_re[l], lam_im[l], log_dt[l], b_re[l], b_im[l])
    dg = N_DIR * S5_GROUPS
    b_maps = _compact_maps(bz_re, bz_im)
    c_maps = _compact_maps(c_re[l].reshape(dg, S5_GROUP, S5_STATE),
                           -c_im[l].reshape(dg, S5_GROUP, S5_STATE))
    ptab = ptab.reshape(S5_CHUNK + 1, 2, N_DIR, S5_SLABS, SLAB_STATES)
    ptab = ptab.transpose(3, 2, 0, 1, 4)

    (u5, bc, q, gate_s, gate_c, w_ffn_in_b, w_glu_b, w_s_up_b, w_c_up_b, w_o_b) = _in_proj(
        x, g_pre, w_in_b, gate_b2, IN_TILE_ROWS,
        cast=((w_ffn_in[l], _ffn_in_plan()), (w_glu[l], None), (w_s_up[l], None),
              (w_c_up[l], None), (w_o[l], None)))
    um5, q_meta = _meta_proj(meta.astype(x.dtype), g_pre, w_in_b)

    u = u5.reshape(S5_SLABS, S5_CHUNK_ROWS * S5_CHUNK, V7X_LANES)
    u_meta = jnp.pad(um5.reshape(S5_SLABS, META_CHUNKS, S5_CHUNK_COLS),
                     ((0, 0), (0, V7X_SUBLANES - META_CHUNKS), (0, 0)))
    d_pairs = jnp.tile(d_skip[l].reshape(S5_SLABS, SLAB_PAIRS, 1, PAIR_LANES),
                       (1, 1, 1, S5_CHUNK))
    y = _s5(u, u_meta, b_maps, c_maps, ptab, d_pairs)

    h1, w_ffn_out_b = _mixer(x, y, q, q_meta, bc, gate_s, gate_c, w_glu_b,
                             b_glu[l].reshape(1, S5_WIDTH), w_s_up_b, conv_w[l],
                             conv_b[l].reshape(1, CONV_WIDTH), w_c_up_b, w_o_b,
                             g_mix_post[l].reshape(1, D_MODEL), w_ffn_out[l])

    h2 = _ffn(h1.reshape(ROWS, D_MODEL), g_ffn_pre[l].reshape(1, D_MODEL),
              w_ffn_in_b, w_ffn_out_b, g_ffn_post[l].reshape(1, D_MODEL))
    return h2.reshape(BATCH, SEQ, D_MODEL)
```

```python
import functools
import math

import jax
import jax.numpy as jnp
from jax import lax
from jax.experimental import pallas as pl
from jax.experimental.pallas import tpu as pltpu

D_MODEL = 2048
BATCH = 4
SEQ = 2048
N_META = 16
S5_WIDTH = 1024
S5_GROUP = 16
S5_GROUPS = S5_WIDTH // S5_GROUP
S5_STATE = 64
N_DIR = 2
CONV_WIDTH = 1024
FFN_HIDDEN = ((math.ceil(8 * D_MODEL / 3) + 255) // 256) * 256
IN_COLS = S5_WIDTH + 3 * CONV_WIDTH + 2 * D_MODEL
RMS_EPS = 1e-6
LAM_RE_MAX = -1e-4

ROWS = SEQ * BATCH

V7X_SUBLANES = 8
V7X_LANES = 128
V7X_MXU_DIM = 256
V7X_VMEM_LIMIT_BYTES = 56 * 1024 * 1024

S5_SLABS = S5_WIDTH // V7X_LANES
SLAB_GROUPS = V7X_LANES // S5_GROUP
SLAB_STATES = SLAB_GROUPS * S5_STATE
PAIR_LANES = 2 * S5_GROUP
SLAB_PAIRS = V7X_LANES // PAIR_LANES
PAIR_STATES = 2 * S5_STATE
S5_CHUNK = V7X_SUBLANES
PAIR_COLS = S5_CHUNK * PAIR_LANES
S5_CHUNKS = SEQ // S5_CHUNK
S5_CHUNK_ROWS = S5_CHUNKS * BATCH
S5_CHUNK_COLS = S5_CHUNK * V7X_LANES
META_CHUNKS = N_META // S5_CHUNK
STATE_REPEATS = V7X_LANES // S5_STATE

IN_TILE_ROWS = 1024
IN_STEPS = 8
IN_SEG_COLS = S5_WIDTH // IN_STEPS
CAST_TILE_ROWS = 128
MIX_TILE_T = 64
FFN_TILE_ROWS = 1024
FFN_TILE_HIDDEN = 256
HALO_ROWS = 16

_f32 = jnp.float32
_bf16 = jnp.bfloat16


def _rms_norm(xf, g):
    r = lax.rsqrt(jnp.mean(xf * xf, axis=-1, keepdims=True) + RMS_EPS)
    return xf * r * g


def _params(sem):
    return pltpu.CompilerParams(dimension_semantics=sem,
                                vmem_limit_bytes=V7X_VMEM_LIMIT_BYTES)


def _dot(a, b):
    return jnp.dot(a, b, preferred_element_type=_f32)


def _dot_nt(a, b):
    return lax.dot_general(a, b, (((1,), (1,)), ((), ())), preferred_element_type=_f32)


def _discretise_kernel(lam_re_ref, lam_im_ref, log_dt_ref, b_re_ref, b_im_ref,
                       pow_ref, bz_re_ref, bz_im_ref):
    lr = jnp.minimum(lam_re_ref[...], LAM_RE_MAX)
    li = lam_im_ref[...]
    dt = jnp.exp(log_dt_ref[...])
    mag = jnp.exp(lr * dt)
    ar = mag * jnp.cos(li * dt)
    ai = mag * jnp.sin(li * dt)
    den = lr * lr + li * li
    nr = ar - 1.0
    zr = (nr * lr + ai * li) / den
    zi = (ai * lr - nr * li) / den
    pr = jnp.ones_like(ar)
    pi = jnp.zeros_like(ar)
    for m in range(S5_CHUNK + 1):
        pow_ref[m, 0] = pr
        pow_ref[m, 1] = pi
        pr, pi = pr * ar - pi * ai, pr * ai + pi * ar
    b_re = b_re_ref[...]
    b_im = b_im_ref[...]
    bz_re_ref[...] = zr[:, None, :] * b_re - zi[:, None, :] * b_im
    bz_im_ref[...] = zr[:, None, :] * b_im + zi[:, None, :] * b_re


def _discretise(lam_re, lam_im, log_dt, b_re, b_im):
    dg = N_DIR * S5_GROUPS
    b_re_t = jnp.swapaxes(b_re, -1, -2).reshape(dg, S5_GROUP, S5_STATE)
    b_im_t = jnp.swapaxes(b_im, -1, -2).reshape(dg, S5_GROUP, S5_STATE)
    return pl.pallas_call(
        _discretise_kernel,
        out_shape=(jax.ShapeDtypeStruct((S5_CHUNK + 1, 2, dg, S5_STATE), _f32),
                   jax.ShapeDtypeStruct((dg, S5_GROUP, S5_STATE), _f32),
                   jax.ShapeDtypeStruct((dg, S5_GROUP, S5_STATE), _f32)),
        name="s5_discretise",
    )(lam_re.reshape(dg, S5_STATE), lam_im.reshape(dg, S5_STATE),
      log_dt.reshape(dg, 1), b_re_t, b_im_t)


def _compact_maps(w_re, w_im):
    def part(w):
        w = w.reshape(N_DIR, S5_SLABS, V7X_LANES, S5_STATE)
        return jnp.concatenate([w] * STATE_REPEATS, axis=-1)
    return jnp.stack([part(w_re), part(w_im)], axis=2)


def _w_in_pieces(j):
    seg, gseg = IN_SEG_COLS, 2 * IN_SEG_COLS
    pieces = [(n * S5_WIDTH + j * seg, seg) for n in range(4)]
    pieces += [(4 * S5_WIDTH + n * D_MODEL + j * gseg, gseg) for n in range(2)]
    return pieces


def _cast_w_in_kernel(w_ref, o_ref):
    dst = 0
    for j in range(IN_STEPS):
        for src, width in _w_in_pieces(j):
            o_ref[:, dst:dst + width] = w_ref[:, src:src + width].astype(_bf16)
            dst += width


def _cast_w_in(w):
    spec = pl.BlockSpec((CAST_TILE_ROWS, IN_COLS), lambda i: (i, 0))
    return pl.pallas_call(
        _cast_w_in_kernel,
        grid=(D_MODEL // CAST_TILE_ROWS,),
        in_specs=[spec],
        out_specs=spec,
        out_shape=jax.ShapeDtypeStruct(w.shape, _bf16),
        compiler_params=_params(("arbitrary",)),
        name="cast_w_in",
    )(w)


def _ffn_in_plan():
    th = FFN_TILE_HIDDEN
    plan = []
    for j in range(FFN_HIDDEN // th):
        plan.append((j * th, 2 * j * th, th))
        plan.append((FFN_HIDDEN + j * th, (2 * j + 1) * th, th))
    return plan


def _in_proj_kernel(x_ref, g_ref, w_ref, gbs_ref, gbc_ref, *rest, cast_plans):
    n_cast = len(cast_plans)
    cast_in = rest[:n_cast]
    u_ref, bc_ref, q_ref, gs_ref, gc_ref = rest[n_cast:n_cast + 5]
    cast_out = rest[n_cast + 5:2 * n_cast + 5]
    h_scr = rest[2 * n_cast + 5]
    seg = IN_SEG_COLS

    @pl.when(pl.program_id(2) == 0)
    def _():
        h_scr[...] = _rms_norm(x_ref[...], g_ref[...]).astype(_bf16)

    proj = _dot(h_scr[...], w_ref[...])
    for s in range(seg // V7X_LANES):
        u_ref[s] = proj[:, s * V7X_LANES:(s + 1) * V7X_LANES].reshape(-1, S5_CHUNK, V7X_LANES)
    bc_ref[...] = proj[:, 2 * seg:3 * seg].astype(_bf16)
    q_ref[...] = (proj[:, 3 * seg:4 * seg] * proj[:, seg:2 * seg]).astype(_bf16)
    gs_ref[...] = jax.nn.sigmoid(proj[:, 4 * seg:6 * seg] + gbs_ref[0]).astype(_bf16)
    gc_ref[...] = jax.nn.sigmoid(proj[:, 6 * seg:8 * seg] + gbc_ref[0]).astype(_bf16)
    for src, dst, plan in zip(cast_in, cast_out, cast_plans):
        for s0, d0, width in plan:
            dst[:, d0:d0 + width] = src[:, s0:s0 + width].astype(_bf16)


def _in_proj(x3, g, w_in_b, gate_b, tile_rows, cast=()):
    nb, t_len, _ = x3.shape
    n_t = t_len // tile_rows
    n_steps = nb * n_t * IN_STEPS
    seg, gseg = IN_SEG_COLS, 2 * IN_SEG_COLS
    tile = lambda b, t, j: (b, t, j)
    step = lambda b, t, j: ((b * n_t + t) * IN_STEPS + j, 0)
    cast_w = [w for w, _ in cast]
    cast_plans = tuple(tuple(plan) if plan else ((0, 0, w.shape[1]),) for w, plan in cast)
    cast_specs = [pl.BlockSpec((w.shape[0] // n_steps, w.shape[1]), step) for w in cast_w]
    gb = lambda n: pl.BlockSpec((1, 1, gseg), lambda b, t, j: (n, 0, j))
    return pl.pallas_call(
        functools.partial(_in_proj_kernel, cast_plans=cast_plans),
        grid=(nb, n_t, IN_STEPS),
        in_specs=[
            pl.BlockSpec((None, tile_rows, D_MODEL), lambda b, t, j: (b, t, 0)),
            pl.BlockSpec((1, D_MODEL), lambda b, t, j: (0, 0)),
            pl.BlockSpec((D_MODEL, IN_COLS // IN_STEPS), lambda b, t, j: (0, j)),
            gb(0), gb(1),
        ] + cast_specs,
        out_specs=[
            pl.BlockSpec((seg // V7X_LANES, tile_rows // S5_CHUNK, None, S5_CHUNK, V7X_LANES),
                         lambda b, t, j: (j, t, b, 0, 0)),
            pl.BlockSpec((None, tile_rows, seg), tile),
            pl.BlockSpec((None, tile_rows, seg), tile),
            pl.BlockSpec((None, tile_rows, gseg), tile),
            pl.BlockSpec((None, tile_rows, gseg), tile),
        ] + cast_specs,
        out_shape=[jax.ShapeDtypeStruct(
                       (S5_SLABS, t_len // S5_CHUNK, nb, S5_CHUNK, V7X_LANES), _f32),
                   jax.ShapeDtypeStruct((nb, t_len, CONV_WIDTH), _bf16),
                   jax.ShapeDtypeStruct((nb, t_len, CONV_WIDTH), _bf16),
                   jax.ShapeDtypeStruct((nb, t_len, D_MODEL), _bf16),
                   jax.ShapeDtypeStruct((nb, t_len, D_MODEL), _bf16)]
                  + [jax.ShapeDtypeStruct(w.shape, _bf16) for w in cast_w],
        scratch_shapes=[pltpu.VMEM((tile_rows, D_MODEL), _bf16)],
        compiler_params=_params(("arbitrary", "arbitrary", "arbitrary")),
        name="in_proj",
    )(x3, g, w_in_b, gate_b, gate_b, *cast_w)


def _meta_proj_kernel(x_ref, g_ref, w_ref, u_ref, q_ref, h_scr):
    seg = IN_SEG_COLS

    @pl.when(pl.program_id(0) == 0)
    def _():
        h_scr[...] = _rms_norm(x_ref[...], g_ref[...]).astype(_bf16)

    proj = _dot(h_scr[...], w_ref[...])
    for s in range(seg // V7X_LANES):
        u_ref[s] = proj[:, s * V7X_LANES:(s + 1) * V7X_LANES].reshape(-1, S5_CHUNK, V7X_LANES)
    q_ref[...] = (proj[:, 3 * seg:4 * seg] * proj[:, seg:2 * seg]).astype(_bf16)


def _meta_proj(meta, g, w_in_b):
    seg = IN_SEG_COLS
    return pl.pallas_call(
        _meta_proj_kernel,
        grid=(IN_STEPS,),
        in_specs=[
            pl.BlockSpec((N_META, D_MODEL), lambda j: (0, 0)),
            pl.BlockSpec((1, D_MODEL), lambda j: (0, 0)),
            pl.BlockSpec((D_MODEL, 4 * seg), lambda j: (0, 2 * j)),
        ],
        out_specs=[
            pl.BlockSpec((seg // V7X_LANES, META_CHUNKS, S5_CHUNK, V7X_LANES),
                         lambda j: (j, 0, 0, 0)),
            pl.BlockSpec((N_META, seg), lambda j: (0, j)),
        ],
        out_shape=[jax.ShapeDtypeStruct((S5_SLABS, META_CHUNKS, S5_CHUNK, V7X_LANES), _f32),
                   jax.ShapeDtypeStruct((N_META, CONV_WIDTH), _bf16)],
        scratch_shapes=[pltpu.VMEM((N_META, D_MODEL), _bf16)],
        compiler_params=_params(("arbitrary",)),
        name="meta_proj",
    )(meta, g, w_in_b)


def _pair_map(ref, d, part, q):
    rows = lax.broadcasted_iota(jnp.int32, (PAIR_LANES, V7X_LANES), 0)
    cols = lax.broadcasted_iota(jnp.int32, (PAIR_LANES, V7X_LANES), 1)
    same_group = (rows // S5_GROUP) == (cols // S5_STATE)
    return jnp.where(same_group, ref[d, 0, part, q * PAIR_LANES:(q + 1) * PAIR_LANES, :], 0.0)


def _transpose_pieces(src):
    assert SLAB_PAIRS == 4 and len(src) == SLAB_PAIRS
    lane = lax.broadcasted_iota(jnp.int32, src[0].shape, 1)

    def exchange(a, b, width):
        keep = (lane // width) % 2 == 0
        return (jnp.where(keep, a, pltpu.roll(b, width, 1)),
                jnp.where(keep, pltpu.roll(a, V7X_LANES - width, 1), b))

    b0, b2 = exchange(src[0], src[2], 2 * PAIR_LANES)
    b1, b3 = exchange(src[1], src[3], 2 * PAIR_LANES)
    c0, c1 = exchange(b0, b1, PAIR_LANES)
    c2, c3 = exchange(b2, b3, PAIR_LANES)
    return [c0, c1, c2, c3]


def _boundary_scan(x_scr, a_re, a_im, init):
    n_tiles = S5_CHUNK_ROWS // V7X_SUBLANES
    low = lax.broadcasted_iota(jnp.int32, (V7X_SUBLANES, V7X_LANES), 0) < BATCH

    def step(ar, ai, sr, si, xr, xi):
        return ar * sr - ai * si + xr, ar * si + ai * sr + xi

    def swap(v):
        return pltpu.roll(v, BATCH, 0)

    def body(i, carry):
        rf = pl.multiple_of(i * V7X_SUBLANES, V7X_SUBLANES)
        rb = pl.multiple_of((n_tiles - 1 - i) * V7X_SUBLANES, V7X_SUBLANES)
        loaded = []
        for q in range(SLAB_PAIRS):
            for part in range(2):
                lanes = slice(q * PAIR_COLS + part * PAIR_STATES,
                              q * PAIR_COLS + (part + 1) * PAIR_STATES)
                loaded.append((lanes, x_scr[0, pl.ds(rf, V7X_SUBLANES), lanes],
                               x_scr[1, pl.ds(rb, V7X_SUBLANES), lanes]))
        new, stores = [], []
        for q in range(SLAB_PAIRS):
            (re, xfr, xbr), (im, xfi, xbi) = loaded[2 * q], loaded[2 * q + 1]
            w1r, w1i = jnp.where(low, xfr, xbr), jnp.where(low, xfi, xbi)
            w2r, w2i = swap(jnp.where(low, xbr, xfr)), swap(jnp.where(low, xbi, xfi))
            cr, ci = carry[2 * q], carry[2 * q + 1]
            s1r, s1i = step(a_re[q], a_im[q], cr, ci, w1r, w1i)
            s2r, s2i = step(a_re[q], a_im[q], s1r, s1i, w2r, w2i)
            t1r, t1i = swap(s1r), swap(s1i)
            stores += [(0, rf, re, jnp.where(low, cr, t1r)), (0, rf, im, jnp.where(low, ci, t1i)),
                       (1, rb, re, jnp.where(low, t1r, cr)), (1, rb, im, jnp.where(low, t1i, ci))]
            new += [s2r, s2i]
        for d, r0, lanes, value in stores:
            x_scr[d, pl.ds(r0, V7X_SUBLANES), lanes] = value
        return tuple(new)

    lax.fori_loop(0, n_tiles, body, init)


def _s5_kernel(u_ref, um_ref, b_ref, c_ref, p_ref, d_ref, y_ref,
               ub_scr, uf_scr, w_scr, x_scr, sp_scr, tap_scr, toep_scr, xm_scr):
    half_steps = S5_CHUNK // 2
    blk = PAIR_LANES

    def pair_cols(q):
        return slice(q * PAIR_COLS, (q + 1) * PAIR_COLS)

    def pair_states(q):
        return slice(q * PAIR_STATES, (q + 1) * PAIR_STATES)

    def strided_u(j):
        return u_ref[0, pl.ds(j, S5_CHUNK_ROWS, stride=S5_CHUNK), :]

    um = um_ref[0]
    um_q = [[] for _ in range(SLAB_PAIRS)]
    for h in range(2):
        steps = range(h * half_steps, (h + 1) * half_steps)
        for q, t in enumerate(_transpose_pieces([strided_u(j) for j in steps])):
            uf_scr[q, :, h * V7X_LANES:(h + 1) * V7X_LANES] = t
            ub_scr[q, :, h * V7X_LANES:(h + 1) * V7X_LANES] = t.astype(_bf16)
        meta = [um[:, j * V7X_LANES:(j + 1) * V7X_LANES] for j in steps]
        for q, t in enumerate(_transpose_pieces(meta)):
            um_q[q].append(t)

    for d in range(N_DIR):
        for q in range(SLAB_PAIRS):
            w_re = _pair_map(b_ref, d, 0, q)
            w_im = _pair_map(b_ref, d, 1, q)
            for j in range(S5_CHUNK):
                m = (S5_CHUNK - 1 - j) if d == 0 else j
                pr = p_ref[0, d, m, 0:1, pair_states(q)]
                pi = p_ref[0, d, m, 1:2, pair_states(q)]
                rows = slice(j * blk, (j + 1) * blk)
                w_scr[d, q, rows, 0:PAIR_STATES] = (w_re * pr - w_im * pi).astype(_bf16)
                w_scr[d, q, rows, PAIR_STATES:] = (w_re * pi + w_im * pr).astype(_bf16)
            x_scr[d, :, pair_cols(q)] = _dot(ub_scr[q], w_scr[d, q])
            if d == 0:
                xm_scr[:, pair_cols(q)] = _dot(
                    jnp.concatenate(um_q[q], axis=1).astype(_bf16), w_scr[0, q])

    low = lax.broadcasted_iota(jnp.int32, (V7X_SUBLANES, V7X_LANES), 0) < BATCH
    a_re, a_im, init = [], [], []
    for q in range(SLAB_PAIRS):
        re = slice(q * PAIR_COLS, q * PAIR_COLS + PAIR_STATES)
        im = slice(q * PAIR_COLS + PAIR_STATES, (q + 1) * PAIR_COLS)
        ar = p_ref[0, 0, S5_CHUNK, 0:1, pair_states(q)]
        ai = p_ref[0, 0, S5_CHUNK, 1:2, pair_states(q)]
        a_re.append(jnp.where(low, ar, p_ref[0, 1, S5_CHUNK, 0:1, pair_states(q)]))
        a_im.append(jnp.where(low, ai, p_ref[0, 1, S5_CHUNK, 1:2, pair_states(q)]))
        sr = jnp.zeros((1, V7X_LANES), _f32)
        si = jnp.zeros((1, V7X_LANES), _f32)
        for c in range(META_CHUNKS):
            xr = xm_scr[c:c + 1, re]
            xi = xm_scr[c:c + 1, im]
            sr, si = ar * sr - ai * si + xr, ar * si + ai * sr + xi
        init.append(jnp.where(low, sr, 0.0))
        init.append(jnp.where(low, si, 0.0))
    _boundary_scan(x_scr, a_re, a_im, tuple(init))
    for d in range(N_DIR):
        sp_scr[d] = x_scr[d].astype(_bf16)

    def out_weights(c_re, c_imn, d, m, q):
        pr = p_ref[0, d, m, 0:1, pair_states(q)]
        pi = p_ref[0, d, m, 1:2, pair_states(q)]
        return jnp.concatenate([c_re * pr + c_imn * pi, c_imn * pr - c_re * pi],
                               axis=1).astype(_bf16)

    for q in range(SLAB_PAIRS):
        wy = []
        for d in range(N_DIR):
            c_re = _pair_map(c_ref, d, 0, q)
            c_imn = _pair_map(c_ref, d, 1, q)
            c_map = jnp.concatenate([c_re, c_imn], axis=1).astype(_bf16)
            tap_scr[d] = _dot_nt(w_scr[d, q], jnp.concatenate([c_map] * S5_CHUNK, axis=0))
            powers = [o + 1 if d == 0 else S5_CHUNK - o for o in range(S5_CHUNK)]
            wy.append(jnp.concatenate([out_weights(c_re, c_imn, d, m, q) for m in powers],
                                      axis=0))
        for o in range(S5_CHUNK):
            lanes = slice(o * blk, (o + 1) * blk)
            if o:
                toep_scr[0:o * blk, lanes] = tap_scr[0, (S5_CHUNK - 1 - o) * blk:
                                                     (S5_CHUNK - 1) * blk, lanes]
            toep_scr[o * blk:(o + 1) * blk, lanes] = (
                tap_scr[0, (S5_CHUNK - 1) * blk:, lanes] + tap_scr[1, 0:blk, lanes])
            if o < S5_CHUNK - 1:
                toep_scr[(o + 1) * blk:, lanes] = tap_scr[1, blk:(S5_CHUNK - o) * blk, lanes]
        y = _dot_nt(sp_scr[0, :, pair_cols(q)], wy[0])
        y += _dot_nt(sp_scr[1, :, pair_cols(q)], wy[1])
        y += _dot(ub_scr[q], toep_scr[...].astype(_bf16))
        y += uf_scr[q] * d_ref[0, q]
        for h in range(2):
            y_ref[0, q, h] = y[:, h * V7X_LANES:(h + 1) * V7X_LANES]


def _s5(u, u_meta, b_maps, c_maps, ptab, d_skip):
    maps = pl.BlockSpec((N_DIR, 1, 2, V7X_LANES, V7X_LANES), lambda s: (0, s, 0, 0, 0))
    return pl.pallas_call(
        _s5_kernel,
        grid=(S5_SLABS,),
        in_specs=[
            pl.BlockSpec((1, S5_CHUNK_ROWS * S5_CHUNK, V7X_LANES), lambda s: (s, 0, 0)),
            pl.BlockSpec((1, V7X_SUBLANES, S5_CHUNK_COLS), lambda s: (s, 0, 0)),
            maps,
            maps,
            pl.BlockSpec((1, N_DIR, S5_CHUNK + 1, 2, SLAB_STATES), lambda s: (s, 0, 0, 0, 0)),
            pl.BlockSpec((1, SLAB_PAIRS, 1, PAIR_COLS), lambda s: (s, 0, 0, 0)),
        ],
        out_specs=pl.BlockSpec((1, SLAB_PAIRS, 2, S5_CHUNK_ROWS, V7X_LANES),
                               lambda s: (s, 0, 0, 0, 0)),
        out_shape=jax.ShapeDtypeStruct((S5_SLABS, SLAB_PAIRS, 2, S5_CHUNK_ROWS, V7X_LANES),
                                       _f32),
        scratch_shapes=[
            pltpu.VMEM((SLAB_PAIRS, S5_CHUNK_ROWS, PAIR_COLS), _bf16),
            pltpu.VMEM((SLAB_PAIRS, S5_CHUNK_ROWS, PAIR_COLS), _f32),
            pltpu.VMEM((N_DIR, SLAB_PAIRS, PAIR_COLS, PAIR_COLS), _bf16),
            pltpu.VMEM((N_DIR, S5_CHUNK_ROWS, 2 * SLAB_STATES), _f32),
            pltpu.VMEM((N_DIR, S5_CHUNK_ROWS, 2 * SLAB_STATES), _bf16),
            pltpu.VMEM((N_DIR, PAIR_COLS, PAIR_COLS), _f32),
            pltpu.VMEM((PAIR_COLS, PAIR_COLS), _f32),
            pltpu.VMEM((V7X_SUBLANES, 2 * SLAB_STATES), _f32),
        ],
        compiler_params=_params(("arbitrary",)),
        name="s5",
    )(u, u_meta, b_maps, c_maps, ptab, d_skip)


def _mixer_kernel(h_ref, y_ref, q_ref, qp_ref, qn_ref, qm_ref, bc_ref, gs_ref, gc_ref,
                  wglu_ref, bglu_ref, wsup_ref, cw_ref, cb_ref, wcup_ref, wo_ref,
                  g_ref, wcast_ref, o_ref, wcast_out_ref, ys_scr, q_scr):
    t = pl.program_id(0)
    last = pl.num_programs(0) - 1
    tt = MIX_TILE_T
    tm = BATCH * tt
    n_c = tt // S5_CHUNK

    vs = []
    for b in range(BATCH):
        prev = jnp.where(t == 0, qm_ref[...], qp_ref[b]).astype(_f32)
        nxt = jnp.where(t == last, 0.0, qn_ref[b].astype(_f32))
        q_scr[b, 0:HALO_ROWS, :] = prev
        q_scr[b, HALO_ROWS:HALO_ROWS + tt, :] = q_ref[b].astype(_f32)
        q_scr[b, HALO_ROWS + tt:, :] = nxt
        vs.append(cw_ref[0:1, :] * q_scr[b, HALO_ROWS - 1:HALO_ROWS - 1 + tt, :]
                  + cw_ref[1:2, :] * q_scr[b, HALO_ROWS:HALO_ROWS + tt, :]
                  + cw_ref[2:3, :] * q_scr[b, HALO_ROWS + 1:HALO_ROWS + 1 + tt, :]
                  + cb_ref[...])
    v = jnp.concatenate(vs, axis=0)
    bc = bc_ref[...].reshape(tm, CONV_WIDTH).astype(_f32)
    y_c = _dot((bc * v).astype(_bf16), wcup_ref[...])

    half_steps = S5_CHUNK // 2
    for s in range(S5_SLABS):
        for h in range(2):
            for b in range(BATCH):
                tiles = [y_ref[s, q, h, pl.ds(b, n_c, stride=BATCH), :]
                         for q in range(SLAB_PAIRS)]
                for k, tile in enumerate(_transpose_pieces(tiles)):
                    o = h * half_steps + k
                    ys_scr[s, pl.ds(b * tt + o, n_c, stride=S5_CHUNK), :] = tile
    ys = jnp.concatenate([ys_scr[s] for s in range(S5_SLABS)], axis=1)
    ys = jax.nn.gelu(ys)
    glu = _dot(ys.astype(_bf16), wglu_ref[...]) + bglu_ref[...]
    ys = ys * jax.nn.sigmoid(glu)
    y_s = _dot(ys.astype(_bf16), wsup_ref[...])

    merged = (gs_ref[...].reshape(tm, D_MODEL).astype(_f32) * y_s
              + gc_ref[...].reshape(tm, D_MODEL).astype(_f32) * y_c)
    mixed = _dot(merged.astype(_bf16), wo_ref[...])
    out = h_ref[...].reshape(tm, D_MODEL) + _rms_norm(mixed, g_ref[...])
    o_ref[...] = out.reshape(BATCH, tt, D_MODEL)
    wcast_out_ref[...] = wcast_ref[...].astype(_bf16)


def _mixer(x, y, q, q_meta, bc, gate_s, gate_c, w_glu, b_glu, w_s_up, conv_w, conv_b, w_c_up,
           w_o, g_post, w_cast):
    tt = MIX_TILE_T
    n_t = SEQ // tt
    cast_spec = pl.BlockSpec((w_cast.shape[0] // n_t, w_cast.shape[1]), lambda t: (t, 0))
    halo_per_tile = tt // HALO_ROWS
    n_halo = SEQ // HALO_ROWS
    tile = lambda t: (0, t, 0)
    resident = functools.partial(pl.BlockSpec, index_map=lambda t: (0, 0),
                                 pipeline_mode=pl.Buffered(1))
    return pl.pallas_call(
        _mixer_kernel,
        grid=(n_t,),
        in_specs=[
            pl.BlockSpec((BATCH, tt, D_MODEL), tile),
            pl.BlockSpec((S5_SLABS, SLAB_PAIRS, 2, tt // S5_CHUNK * BATCH, V7X_LANES),
                         lambda t: (0, 0, 0, t, 0)),
            pl.BlockSpec((BATCH, tt, CONV_WIDTH), tile),
            pl.BlockSpec((BATCH, HALO_ROWS, CONV_WIDTH),
                         lambda t: (0, jnp.maximum(t * halo_per_tile - 1, 0), 0)),
            pl.BlockSpec((BATCH, HALO_ROWS, CONV_WIDTH),
                         lambda t: (0, jnp.minimum((t + 1) * halo_per_tile, n_halo - 1), 0)),
            resident((HALO_ROWS, CONV_WIDTH)),
            pl.BlockSpec((BATCH, tt, CONV_WIDTH), tile),
            pl.BlockSpec((BATCH, tt, D_MODEL), tile),
            pl.BlockSpec((BATCH, tt, D_MODEL), tile),
            resident((S5_WIDTH, S5_WIDTH)),
            resident((1, S5_WIDTH)),
            resident((S5_WIDTH, D_MODEL)),
            resident((3, CONV_WIDTH)),
            resident((1, CONV_WIDTH)),
            resident((CONV_WIDTH, D_MODEL)),
            resident((D_MODEL, D_MODEL)),
            resident((1, D_MODEL)),
            cast_spec,
        ],
        out_specs=[pl.BlockSpec((BATCH, tt, D_MODEL), tile), cast_spec],
        out_shape=[jax.ShapeDtypeStruct((BATCH, SEQ, D_MODEL), _f32),
                   jax.ShapeDtypeStruct(w_cast.shape, _bf16)],
        scratch_shapes=[pltpu.VMEM((S5_SLABS, BATCH * tt, V7X_LANES), _f32),
                        pltpu.VMEM((BATCH, tt + 2 * HALO_ROWS, CONV_WIDTH), _f32)],
        compiler_params=_params(("arbitrary",)),
        name="mixer_tail",
    )(x, y, q, q, q, q_meta, bc, gate_s, gate_c, w_glu, b_glu, w_s_up, conv_w, conv_b, w_c_up,
      w_o, g_post, w_cast)


def _ffn_kernel(h_ref, gpre_ref, wgu_ref, wout_ref, gpost_ref, o_ref, hb_scr):
    j = pl.program_id(1)

    @pl.when(j == 0)
    def _():
        hb_scr[...] = _rms_norm(h_ref[...], gpre_ref[...]).astype(_bf16)
        o_ref[...] = jnp.zeros_like(o_ref)

    gu = _dot(hb_scr[...], wgu_ref[...])
    gate, up = gu[:, :FFN_TILE_HIDDEN], gu[:, FFN_TILE_HIDDEN:]
    act = (jax.nn.silu(gate) * up).astype(_bf16)
    o_ref[...] += _dot(act, wout_ref[...])

    @pl.when(j == pl.num_programs(1) - 1)
    def _():
        o_ref[...] = h_ref[...] + _rms_norm(o_ref[...], gpost_ref[...])


def _ffn(h1, g_pre, w_in, w_out, g_post):
    tm, th = FFN_TILE_ROWS, FFN_TILE_HIDDEN
    n_hidden = FFN_HIDDEN // th
    return pl.pallas_call(
        _ffn_kernel,
        grid=(ROWS // tm, n_hidden),
        in_specs=[
            pl.BlockSpec((tm, D_MODEL), lambda i, j: (i, 0)),
            pl.BlockSpec((1, D_MODEL), lambda i, j: (0, 0)),
            pl.BlockSpec((D_MODEL, 2 * th), lambda i, j: (0, j)),
            pl.BlockSpec((th, D_MODEL), lambda i, j: (j, 0)),
            pl.BlockSpec((1, D_MODEL), lambda i, j: (0, 0)),
        ],
        out_specs=pl.BlockSpec((tm, D_MODEL), lambda i, j: (i, 0)),
        out_shape=jax.ShapeDtypeStruct((ROWS, D_MODEL), _f32),
        scratch_shapes=[pltpu.VMEM((tm, D_MODEL), _bf16)],
        compiler_params=_params(("arbitrary", "arbitrary")),
        name="ffn",
    )(h1, g_pre, w_in, w_out, g_post)


def kernel(x, meta, g_mix_pre, g_mix_post, g_ffn_pre, g_ffn_post, w_in, gate_b, lam_re, lam_im,
           log_dt, b_re, b_im, c_re, c_im, d_skip, w_glu, b_glu, w_s_up, conv_w, conv_b, w_c_up,
           w_o, w_ffn_in, w_ffn_out):
    l = 0
    w_in_b = _cast_w_in(w_in[l])
    gate_b2 = gate_b[l].reshape(2, 1, D_MODEL)
    g_pre = g_mix_pre[l].reshape(1, D_MODEL)

    ptab, bz_re, bz_im = _discretise(lam_re[l], lam_im[l], log_dt[l], b_re[l], b_im[l])
    dg = N_DIR * S5_GROUPS
    b_maps = _compact_maps(bz_re, bz_im)
    c_maps = _compact_maps(c_re[l].reshape(dg, S5_GROUP, S5_STATE),
                           -c_im[l].reshape(dg, S5_GROUP, S5_STATE))
    ptab = ptab.reshape(S5_CHUNK + 1, 2, N_DIR, S5_SLABS, SLAB_STATES)
    ptab = ptab.transpose(3, 2, 0, 1, 4)

    (u5, bc, q, gate_s, gate_c, w_ffn_in_b, w_glu_b, w_s_up_b, w_c_up_b, w_o_b) = _in_proj(
        x, g_pre, w_in_b, gate_b2, IN_TILE_ROWS,
        cast=((w_ffn_in[l], _ffn_in_plan()), (w_glu[l], None), (w_s_up[l], None),
              (w_c_up[l], None), (w_o[l], None)))
    um5, q_meta = _meta_proj(meta.astype(x.dtype), g_pre, w_in_b)

    u = u5.reshape(S5_SLABS, S5_CHUNK_ROWS * S5_CHUNK, V7X_LANES)
    u_meta = jnp.pad(um5.reshape(S5_SLABS, META_CHUNKS, S5_CHUNK_COLS),
                     ((0, 0), (0, V7X_SUBLANES - META_CHUNKS), (0, 0)))
    d_pairs = jnp.tile(d_skip[l].reshape(S5_SLABS, SLAB_PAIRS, 1, PAIR_LANES),
                       (1, 1, 1, S5_CHUNK))
    y = _s5(u, u_meta, b_maps, c_maps, ptab, d_pairs)

    h1, w_ffn_out_b = _mixer(x, y, q, q_meta, bc, gate_s, gate_c, w_glu_b,
                             b_glu[l].reshape(1, S5_WIDTH), w_s_up_b, conv_w[l],
                             conv_b[l].reshape(1, CONV_WIDTH), w_c_up_b, w_o_b,
                             g_mix_post[l].reshape(1, D_MODEL), w_ffn_out[l])

    h2 = _ffn(h1.reshape(ROWS, D_MODEL), g_ffn_pre[l].reshape(1, D_MODEL),
              w_ffn_in_b, w_ffn_out_b, g_ffn_post[l].reshape(1, D_MODEL))
    return h2.reshape(BATCH, SEQ, D_MODEL)
```

```python
import functools
import math

import jax
import jax.numpy as jnp
from jax import lax
from jax.experimental import pallas as pl
from jax.experimental.pallas import tpu as pltpu

D_MODEL = 2048
BATCH = 4
SEQ = 2048
N_META = 16
S5_WIDTH = 1024
S5_GROUP = 16
S5_GROUPS = S5_WIDTH // S5_GROUP
S5_STATE = 64
N_DIR = 2
CONV_WIDTH = 1024
FFN_HIDDEN = ((math.ceil(8 * D_MODEL / 3) + 255) // 256) * 256
IN_COLS = S5_WIDTH + 3 * CONV_WIDTH + 2 * D_MODEL
RMS_EPS = 1e-6
LAM_RE_MAX = -1e-4

ROWS = SEQ * BATCH

V7X_SUBLANES = 8
V7X_LANES = 128
V7X_MXU_DIM = 256
V7X_VMEM_LIMIT_BYTES = 56 * 1024 * 1024

S5_SLABS = S5_WIDTH // V7X_LANES
SLAB_GROUPS = V7X_LANES // S5_GROUP
SLAB_STATES = SLAB_GROUPS * S5_STATE
PAIR_LANES = 2 * S5_GROUP
SLAB_PAIRS = V7X_LANES // PAIR_LANES
PAIR_STATES = 2 * S5_STATE
S5_CHUNK = V7X_SUBLANES
PAIR_COLS = S5_CHUNK * PAIR_LANES
S5_CHUNKS = SEQ // S5_CHUNK
S5_CHUNK_ROWS = S5_CHUNKS * BATCH
S5_CHUNK_COLS = S5_CHUNK * V7X_LANES
META_CHUNKS = N_META // S5_CHUNK
STATE_REPEATS = V7X_LANES // S5_STATE

IN_TILE_ROWS = 1024
IN_STEPS = 8
IN_SEG_COLS = S5_WIDTH // IN_STEPS
CAST_TILE_ROWS = 128
MIX_TILE_T = 64
FFN_TILE_ROWS = 1024
FFN_TILE_HIDDEN = 256
HALO_ROWS = 16

_f32 = jnp.float32
_bf16 = jnp.bfloat16


def _rms_norm(xf, g):
    r = lax.rsqrt(jnp.mean(xf * xf, axis=-1, keepdims=True) + RMS_EPS)
    return xf * r * g


def _params(sem):
    return pltpu.CompilerParams(dimension_semantics=sem,
                                vmem_limit_bytes=V7X_VMEM_LIMIT_BYTES)


def _dot(a, b):
    return jnp.dot(a, b, preferred_element_type=_f32)


def _dot_nt(a, b):
    return lax.dot_general(a, b, (((1,), (1,)), ((), ())), preferred_element_type=_f32)


def _discretise_kernel(lam_re_ref, lam_im_ref, log_dt_ref, b_re_ref, b_im_ref,
                       pow_ref, bz_re_ref, bz_im_ref):
    lr = jnp.minimum(lam_re_ref[...], LAM_RE_MAX)
    li = lam_im_ref[...]
    dt = jnp.exp(log_dt_ref[...])
    mag = jnp.exp(lr * dt)
    ar = mag * jnp.cos(li * dt)
    ai = mag * jnp.sin(li * dt)
    den = lr * lr + li * li
    nr = ar - 1.0
    zr = (nr * lr + ai * li) / den
    zi = (ai * lr - nr * li) / den
    pr = jnp.ones_like(ar)
    pi = jnp.zeros_like(ar)
    for m in range(S5_CHUNK + 1):
        pow_ref[m, 0] = pr
        pow_ref[m, 1] = pi
        pr, pi = pr * ar - pi * ai, pr * ai + pi * ar
    b_re = b_re_ref[...]
    b_im = b_im_ref[...]
    bz_re_ref[...] = zr[:, None, :] * b_re - zi[:, None, :] * b_im
    bz_im_ref[...] = zr[:, None, :] * b_im + zi[:, None, :] * b_re


def _discretise(lam_re, lam_im, log_dt, b_re, b_im):
    dg = N_DIR * S5_GROUPS
    b_re_t = jnp.swapaxes(b_re, -1, -2).reshape(dg, S5_GROUP, S5_STATE)
    b_im_t = jnp.swapaxes(b_im, -1, -2).reshape(dg, S5_GROUP, S5_STATE)
    return pl.pallas_call(
        _discretise_kernel,
        out_shape=(jax.ShapeDtypeStruct((S5_CHUNK + 1, 2, dg, S5_STATE), _f32),
                   jax.ShapeDtypeStruct((dg, S5_GROUP, S5_STATE), _f32),
                   jax.ShapeDtypeStruct((dg, S5_GROUP, S5_STATE), _f32)),
        name="s5_discretise",
    )(lam_re.reshape(dg, S5_STATE), lam_im.reshape(dg, S5_STATE),
      log_dt.reshape(dg, 1), b_re_t, b_im_t)


def _compact_maps(w_re, w_im):
    def part(w):
        w = w.reshape(N_DIR, S5_SLABS, V7X_LANES, S5_STATE)
        return jnp.concatenate([w] * STATE_REPEATS, axis=-1)
    return jnp.stack([part(w_re), part(w_im)], axis=2)


def _w_in_pieces(j):
    seg, gseg = IN_SEG_COLS, 2 * IN_SEG_COLS
    pieces = [(n * S5_WIDTH + j * seg, seg) for n in range(4)]
    pieces += [(4 * S5_WIDTH + n * D_MODEL + j * gseg, gseg) for n in range(2)]
    return pieces


def _cast_w_in_kernel(w_ref, o_ref):
    dst = 0
    for j in range(IN_STEPS):
        for src, width in _w_in_pieces(j):
            o_ref[:, dst:dst + width] = w_ref[:, src:src + width].astype(_bf16)
            dst += width


def _cast_w_in(w):
    spec = pl.BlockSpec((CAST_TILE_ROWS, IN_COLS), lambda i: (i, 0))
    return pl.pallas_call(
        _cast_w_in_kernel,
        grid=(D_MODEL // CAST_TILE_ROWS,),
        in_specs=[spec],
        out_specs=spec,
        out_shape=jax.ShapeDtypeStruct(w.shape, _bf16),
        compiler_params=_params(("arbitrary",)),
        name="cast_w_in",
    )(w)


def _ffn_in_plan():
    th = FFN_TILE_HIDDEN
    plan = []
    for j in range(FFN_HIDDEN // th):
        plan.append((j * th, 2 * j * th, th))
        plan.append((FFN_HIDDEN + j * th, (2 * j + 1) * th, th))
    return plan


def _in_proj_kernel(x_ref, g_ref, w_ref, gbs_ref, gbc_ref, *rest, cast_plans):
    n_cast = len(cast_plans)
    cast_in = rest[:n_cast]
    u_ref, bc_ref, q_ref, gs_ref, gc_ref = rest[n_cast:n_cast + 5]
    cast_out = rest[n_cast + 5:2 * n_cast + 5]
    h_scr = rest[2 * n_cast + 5]
    seg = IN_SEG_COLS

    def column_step(h):
        proj = _dot(h, w_ref[...])
        for s in range(seg // V7X_LANES):
            u_ref[s] = proj[:, s * V7X_LANES:(s + 1) * V7X_LANES].reshape(
                -1, S5_CHUNK, V7X_LANES)
        bc_ref[...] = proj[:, 2 * seg:3 * seg].astype(_bf16)
        q_ref[...] = (proj[:, 3 * seg:4 * seg] * proj[:, seg:2 * seg]).astype(_bf16)
        gs_ref[...] = jax.nn.sigmoid(proj[:, 4 * seg:6 * seg] + gbs_ref[0]).astype(_bf16)
        gc_ref[...] = jax.nn.sigmoid(proj[:, 6 * seg:8 * seg] + gbc_ref[0]).astype(_bf16)
        for src, dst, plan in zip(cast_in, cast_out, cast_plans):
            for s0, d0, width in plan:
                dst[:, d0:d0 + width] = src[:, s0:s0 + width].astype(_bf16)

    @pl.when(pl.program_id(2) == 0)
    def _():
        h = _rms_norm(x_ref[...], g_ref[...]).astype(_bf16)
        h_scr[...] = h
        column_step(h)

    @pl.when(pl.program_id(2) > 0)
    def _():
        column_step(h_scr[...])


def _in_proj(x3, g, w_in_b, gate_b, tile_rows, cast=()):
    nb, t_len, _ = x3.shape
    n_t = t_len // tile_rows
    n_steps = nb * n_t * IN_STEPS
    seg, gseg = IN_SEG_COLS, 2 * IN_SEG_COLS
    tile = lambda b, t, j: (b, t, j)
    step = lambda b, t, j: ((b * n_t + t) * IN_STEPS + j, 0)
    cast_w = [w for w, _ in cast]
    cast_plans = tuple(tuple(plan) if plan else ((0, 0, w.shape[1]),) for w, plan in cast)
    cast_specs = [pl.BlockSpec((w.shape[0] // n_steps, w.shape[1]), step) for w in cast_w]
    gb = lambda n: pl.BlockSpec((1, 1, gseg), lambda b, t, j: (n, 0, j))
    return pl.pallas_call(
        functools.partial(_in_proj_kernel, cast_plans=cast_plans),
        grid=(nb, n_t, IN_STEPS),
        in_specs=[
            pl.BlockSpec((None, tile_rows, D_MODEL), lambda b, t, j: (b, t, 0)),
            pl.BlockSpec((1, D_MODEL), lambda b, t, j: (0, 0)),
            pl.BlockSpec((D_MODEL, IN_COLS // IN_STEPS), lambda b, t, j: (0, j)),
            gb(0), gb(1),
        ] + cast_specs,
        out_specs=[
            pl.BlockSpec((seg // V7X_LANES, tile_rows // S5_CHUNK, None, S5_CHUNK, V7X_LANES),
                         lambda b, t, j: (j, t, b, 0, 0)),
            pl.BlockSpec((None, tile_rows, seg), tile),
            pl.BlockSpec((None, tile_rows, seg), tile),
            pl.BlockSpec((None, tile_rows, gseg), tile),
            pl.BlockSpec((None, tile_rows, gseg), tile),
        ] + cast_specs,
        out_shape=[jax.ShapeDtypeStruct(
                       (S5_SLABS, t_len // S5_CHUNK, nb, S5_CHUNK, V7X_LANES), _f32),
                   jax.ShapeDtypeStruct((nb, t_len, CONV_WIDTH), _bf16),
                   jax.ShapeDtypeStruct((nb, t_len, CONV_WIDTH), _bf16),
                   jax.ShapeDtypeStruct((nb, t_len, D_MODEL), _bf16),
                   jax.ShapeDtypeStruct((nb, t_len, D_MODEL), _bf16)]
                  + [jax.ShapeDtypeStruct(w.shape, _bf16) for w in cast_w],
        scratch_shapes=[pltpu.VMEM((tile_rows, D_MODEL), _bf16)],
        compiler_params=_params(("arbitrary", "arbitrary", "arbitrary")),
        name="in_proj",
    )(x3, g, w_in_b, gate_b, gate_b, *cast_w)


def _meta_proj_kernel(x_ref, g_ref, w_ref, u_ref, q_ref, h_scr):
    seg = IN_SEG_COLS

    @pl.when(pl.program_id(0) == 0)
    def _():
        h_scr[...] = _rms_norm(x_ref[...], g_ref[...]).astype(_bf16)

    proj = _dot(h_scr[...], w_ref[...])
    for s in range(seg // V7X_LANES):
        u_ref[s] = proj[:, s * V7X_LANES:(s + 1) * V7X_LANES].reshape(-1, S5_CHUNK, V7X_LANES)
    q_ref[...] = (proj[:, 3 * seg:4 * seg] * proj[:, seg:2 * seg]).astype(_bf16)


def _meta_proj(meta, g, w_in_b):
    seg = IN_SEG_COLS
    return pl.pallas_call(
        _meta_proj_kernel,
        grid=(IN_STEPS,),
        in_specs=[
            pl.BlockSpec((N_META, D_MODEL), lambda j: (0, 0)),
            pl.BlockSpec((1, D_MODEL), lambda j: (0, 0)),
            pl.BlockSpec((D_MODEL, 4 * seg), lambda j: (0, 2 * j)),
        ],
        out_specs=[
            pl.BlockSpec((seg // V7X_LANES, META_CHUNKS, S5_CHUNK, V7X_LANES),
                         lambda j: (j, 0, 0, 0)),
            pl.BlockSpec((N_META, seg), lambda j: (0, j)),
        ],
        out_shape=[jax.ShapeDtypeStruct((S5_SLABS, META_CHUNKS, S5_CHUNK, V7X_LANES), _f32),
                   jax.ShapeDtypeStruct((N_META, CONV_WIDTH), _bf16)],
        scratch_shapes=[pltpu.VMEM((N_META, D_MODEL), _bf16)],
        compiler_params=_params(("arbitrary",)),
        name="meta_proj",
    )(meta, g, w_in_b)


def _pair_map(ref, d, part, q):
    rows = lax.broadcasted_iota(jnp.int32, (PAIR_LANES, V7X_LANES), 0)
    cols = lax.broadcasted_iota(jnp.int32, (PAIR_LANES, V7X_LANES), 1)
    same_group = (rows // S5_GROUP) == (cols // S5_STATE)
    return jnp.where(same_group, ref[d, 0, part, q * PAIR_LANES:(q + 1) * PAIR_LANES, :], 0.0)


def _transpose_pieces(src):
    assert SLAB_PAIRS == 4 and len(src) == SLAB_PAIRS
    lane = lax.broadcasted_iota(jnp.int32, src[0].shape, 1)

    def exchange(a, b, width):
        keep = (lane // width) % 2 == 0
        return (jnp.where(keep, a, pltpu.roll(b, width, 1)),
                jnp.where(keep, pltpu.roll(a, V7X_LANES - width, 1), b))

    b0, b2 = exchange(src[0], src[2], 2 * PAIR_LANES)
    b1, b3 = exchange(src[1], src[3], 2 * PAIR_LANES)
    c0, c1 = exchange(b0, b1, PAIR_LANES)
    c2, c3 = exchange(b2, b3, PAIR_LANES)
    return [c0, c1, c2, c3]


def _boundary_scan(x_scr, a_re, a_im, init):
    n_tiles = S5_CHUNK_ROWS // V7X_SUBLANES
    low = lax.broadcasted_iota(jnp.int32, (V7X_SUBLANES, V7X_LANES), 0) < BATCH

    def step(ar, ai, sr, si, xr, xi):
        return ar * sr - ai * si + xr, ar * si + ai * sr + xi

    def swap(v):
        return pltpu.roll(v, BATCH, 0)

    def body(i, carry):
        rf = pl.multiple_of(i * V7X_SUBLANES, V7X_SUBLANES)
        rb = pl.multiple_of((n_tiles - 1 - i) * V7X_SUBLANES, V7X_SUBLANES)
        loaded = []
        for q in range(SLAB_PAIRS):
            for part in range(2):
                lanes = slice(q * PAIR_COLS + part * PAIR_STATES,
                              q * PAIR_COLS + (part + 1) * PAIR_STATES)
                loaded.append((lanes, x_scr[0, pl.ds(rf, V7X_SUBLANES), lanes],
                               x_scr[1, pl.ds(rb, V7X_SUBLANES), lanes]))
        new, stores = [], []
        for q in range(SLAB_PAIRS):
            (re, xfr, xbr), (im, xfi, xbi) = loaded[2 * q], loaded[2 * q + 1]
            w1r, w1i = jnp.where(low, xfr, xbr), jnp.where(low, xfi, xbi)
            w2r, w2i = swap(jnp.where(low, xbr, xfr)), swap(jnp.where(low, xbi, xfi))
            cr, ci = carry[2 * q], carry[2 * q + 1]
            s1r, s1i = step(a_re[q], a_im[q], cr, ci, w1r, w1i)
            s2r, s2i = step(a_re[q], a_im[q], s1r, s1i, w2r, w2i)
            t1r, t1i = swap(s1r), swap(s1i)
            stores += [(0, rf, re, jnp.where(low, cr, t1r)), (0, rf, im, jnp.where(low, ci, t1i)),
                       (1, rb, re, jnp.where(low, t1r, cr)), (1, rb, im, jnp.where(low, t1i, ci))]
            new += [s2r, s2i]
        for d, r0, lanes, value in stores:
            x_scr[d, pl.ds(r0, V7X_SUBLANES), lanes] = value
        return tuple(new)

    lax.fori_loop(0, n_tiles, body, init)


def _s5_kernel(u_ref, um_ref, b_ref, c_ref, p_ref, d_ref, y_ref,
               ub_scr, uf_scr, w_scr, x_scr, sp_scr, tap_scr, toep_scr, xm_scr):
    half_steps = S5_CHUNK // 2
    blk = PAIR_LANES

    def pair_cols(q):
        return slice(q * PAIR_COLS, (q + 1) * PAIR_COLS)

    def pair_states(q):
        return slice(q * PAIR_STATES, (q + 1) * PAIR_STATES)

    def strided_u(j):
        return u_ref[0, pl.ds(j, S5_CHUNK_ROWS, stride=S5_CHUNK), :]

    um = um_ref[0]
    um_q = [[] for _ in range(SLAB_PAIRS)]
    for h in range(2):
        steps = range(h * half_steps, (h + 1) * half_steps)
        for q, t in enumerate(_transpose_pieces([strided_u(j) for j in steps])):
            uf_scr[q, :, h * V7X_LANES:(h + 1) * V7X_LANES] = t
            ub_scr[q, :, h * V7X_LANES:(h + 1) * V7X_LANES] = t.astype(_bf16)
        meta = [um[:, j * V7X_LANES:(j + 1) * V7X_LANES] for j in steps]
        for q, t in enumerate(_transpose_pieces(meta)):
            um_q[q].append(t)

    for d in range(N_DIR):
        for q in range(SLAB_PAIRS):
            w_re = _pair_map(b_ref, d, 0, q)
            w_im = _pair_map(b_ref, d, 1, q)
            for j in range(S5_CHUNK):
                m = (S5_CHUNK - 1 - j) if d == 0 else j
                pr = p_ref[0, d, m, 0:1, pair_states(q)]
                pi = p_ref[0, d, m, 1:2, pair_states(q)]
                rows = slice(j * blk, (j + 1) * blk)
                w_scr[d, q, rows, 0:PAIR_STATES] = (w_re * pr - w_im * pi).astype(_bf16)
                w_scr[d, q, rows, PAIR_STATES:] = (w_re * pi + w_im * pr).astype(_bf16)
            x_scr[d, :, pair_cols(q)] = _dot(ub_scr[q], w_scr[d, q])
            if d == 0:
                xm_scr[:, pair_cols(q)] = _dot(
                    jnp.concatenate(um_q[q], axis=1).astype(_bf16), w_scr[0, q])

    low = lax.broadcasted_iota(jnp.int32, (V7X_SUBLANES, V7X_LANES), 0) < BATCH
    a_re, a_im, init = [], [], []
    for q in range(SLAB_PAIRS):
        re = slice(q * PAIR_COLS, q * PAIR_COLS + PAIR_STATES)
        im = slice(q * PAIR_COLS + PAIR_STATES, (q + 1) * PAIR_COLS)
        ar = p_ref[0, 0, S5_CHUNK, 0:1, pair_states(q)]
        ai = p_ref[0, 0, S5_CHUNK, 1:2, pair_states(q)]
        a_re.append(jnp.where(low, ar, p_ref[0, 1, S5_CHUNK, 0:1, pair_states(q)]))
        a_im.append(jnp.where(low, ai, p_ref[0, 1, S5_CHUNK, 1:2, pair_states(q)]))
        sr = jnp.zeros((1, V7X_LANES), _f32)
        si = jnp.zeros((1, V7X_LANES), _f32)
        for c in range(META_CHUNKS):
            xr = xm_scr[c:c + 1, re]
            xi = xm_scr[c:c + 1, im]
            sr, si = ar * sr - ai * si + xr, ar * si + ai * sr + xi
        init.append(jnp.where(low, sr, 0.0))
        init.append(jnp.where(low, si, 0.0))
    _boundary_scan(x_scr, a_re, a_im, tuple(init))
    for d in range(N_DIR):
        sp_scr[d] = x_scr[d].astype(_bf16)

    def out_weights(c_re, c_imn, d, m, q):
        pr = p_ref[0, d, m, 0:1, pair_states(q)]
        pi = p_ref[0, d, m, 1:2, pair_states(q)]
        return jnp.concatenate([c_re * pr + c_imn * pi, c_imn * pr - c_re * pi],
                               axis=1).astype(_bf16)

    for q in range(SLAB_PAIRS):
        wy = []
        for d in range(N_DIR):
            c_re = _pair_map(c_ref, d, 0, q)
            c_imn = _pair_map(c_ref, d, 1, q)
            c_map = jnp.concatenate([c_re, c_imn], axis=1).astype(_bf16)
            tap_scr[d] = _dot_nt(w_scr[d, q], jnp.concatenate([c_map] * S5_CHUNK, axis=0))
            powers = [o + 1 if d == 0 else S5_CHUNK - o for o in range(S5_CHUNK)]
            wy.append(jnp.concatenate([out_weights(c_re, c_imn, d, m, q) for m in powers],
                                      axis=0))
        for o in range(S5_CHUNK):
            lanes = slice(o * blk, (o + 1) * blk)
            if o:
                toep_scr[0:o * blk, lanes] = tap_scr[0, (S5_CHUNK - 1 - o) * blk:
                                                     (S5_CHUNK - 1) * blk, lanes]
            toep_scr[o * blk:(o + 1) * blk, lanes] = (
                tap_scr[0, (S5_CHUNK - 1) * blk:, lanes] + tap_scr[1, 0:blk, lanes])
            if o < S5_CHUNK - 1:
                toep_scr[(o + 1) * blk:, lanes] = tap_scr[1, blk:(S5_CHUNK - o) * blk, lanes]
        y = _dot_nt(sp_scr[0, :, pair_cols(q)], wy[0])
        y += _dot_nt(sp_scr[1, :, pair_cols(q)], wy[1])
        y += _dot(ub_scr[q], toep_scr[...].astype(_bf16))
        y += uf_scr[q] * d_ref[0, q]
        for h in range(2):
            y_ref[0, q, h] = y[:, h * V7X_LANES:(h + 1) * V7X_LANES]


def _s5(u, u_meta, b_maps, c_maps, ptab, d_skip):
    maps = pl.BlockSpec((N_DIR, 1, 2, V7X_LANES, V7X_LANES), lambda s: (0, s, 0, 0, 0))
    return pl.pallas_call(
        _s5_kernel,
        grid=(S5_SLABS,),
        in_specs=[
            pl.BlockSpec((1, S5_CHUNK_ROWS * S5_CHUNK, V7X_LANES), lambda s: (s, 0, 0)),
            pl.BlockSpec((1, V7X_SUBLANES, S5_CHUNK_COLS), lambda s: (s, 0, 0)),
            maps,
            maps,
            pl.BlockSpec((1, N_DIR, S5_CHUNK + 1, 2, SLAB_STATES), lambda s: (s, 0, 0, 0, 0)),
            pl.BlockSpec((1, SLAB_PAIRS, 1, PAIR_COLS), lambda s: (s, 0, 0, 0)),
        ],
        out_specs=pl.BlockSpec((1, SLAB_PAIRS, 2, S5_CHUNK_ROWS, V7X_LANES),
                               lambda s: (s, 0, 0, 0, 0)),
        out_shape=jax.ShapeDtypeStruct((S5_SLABS, SLAB_PAIRS, 2, S5_CHUNK_ROWS, V7X_LANES),
                                       _f32),
        scratch_shapes=[
            pltpu.VMEM((SLAB_PAIRS, S5_CHUNK_ROWS, PAIR_COLS), _bf16),
            pltpu.VMEM((SLAB_PAIRS, S5_CHUNK_ROWS, PAIR_COLS), _f32),
            pltpu.VMEM((N_DIR, SLAB_PAIRS, PAIR_COLS, PAIR_COLS), _bf16),
            pltpu.VMEM((N_DIR, S5_CHUNK_ROWS, 2 * SLAB_STATES), _f32),
            pltpu.VMEM((N_DIR, S5_CHUNK_ROWS, 2 * SLAB_STATES), _bf16),
            pltpu.VMEM((N_DIR, PAIR_COLS, PAIR_COLS), _f32),
            pltpu.VMEM((PAIR_COLS, PAIR_COLS), _f32),
            pltpu.VMEM((V7X_SUBLANES, 2 * SLAB_STATES), _f32),
        ],
        compiler_params=_params(("arbitrary",)),
        name="s5",
    )(u, u_meta, b_maps, c_maps, ptab, d_skip)


def _mixer_kernel(h_ref, y_ref, q_ref, qp_ref, qn_ref, qm_ref, bc_ref, gs_ref, gc_ref,
                  wglu_ref, bglu_ref, wsup_ref, cw_ref, cb_ref, wcup_ref, wo_ref,
                  g_ref, wcast_ref, o_ref, wcast_out_ref, ys_scr, q_scr):
    t = pl.program_id(0)
    last = pl.num_programs(0) - 1
    tt = MIX_TILE_T
    tm = BATCH * tt
    n_c = tt // S5_CHUNK

    vs = []
    for b in range(BATCH):
        prev = jnp.where(t == 0, qm_ref[...], qp_ref[b]).astype(_f32)
        nxt = jnp.where(t == last, 0.0, qn_ref[b].astype(_f32))
        q_scr[b, 0:HALO_ROWS, :] = prev
        q_scr[b, HALO_ROWS:HALO_ROWS + tt, :] = q_ref[b].astype(_f32)
        q_scr[b, HALO_ROWS + tt:, :] = nxt
        vs.append(cw_ref[0:1, :] * q_scr[b, HALO_ROWS - 1:HALO_ROWS - 1 + tt, :]
                  + cw_ref[1:2, :] * q_scr[b, HALO_ROWS:HALO_ROWS + tt, :]
                  + cw_ref[2:3, :] * q_scr[b, HALO_ROWS + 1:HALO_ROWS + 1 + tt, :]
                  + cb_ref[...])
    v = jnp.concatenate(vs, axis=0)
    bc = bc_ref[...].reshape(tm, CONV_WIDTH).astype(_f32)
    y_c = _dot((bc * v).astype(_bf16), wcup_ref[...])

    half_steps = S5_CHUNK // 2
    for s in range(S5_SLABS):
        for h in range(2):
            for b in range(BATCH):
                tiles = [y_ref[s, q, h, pl.ds(b, n_c, stride=BATCH), :]
                         for q in range(SLAB_PAIRS)]
                for k, tile in enumerate(_transpose_pieces(tiles)):
                    o = h * half_steps + k
                    ys_scr[s, pl.ds(b * tt + o, n_c, stride=S5_CHUNK), :] = tile
    ys = jnp.concatenate([ys_scr[s] for s in range(S5_SLABS)], axis=1)
    ys = jax.nn.gelu(ys)
    glu = _dot(ys.astype(_bf16), wglu_ref[...]) + bglu_ref[...]
    ys = ys * jax.nn.sigmoid(glu)
    y_s = _dot(ys.astype(_bf16), wsup_ref[...])

    merged = (gs_ref[...].reshape(tm, D_MODEL).astype(_f32) * y_s
              + gc_ref[...].reshape(tm, D_MODEL).astype(_f32) * y_c)
    mixed = _dot(merged.astype(_bf16), wo_ref[...])
    out = h_ref[...].reshape(tm, D_MODEL) + _rms_norm(mixed, g_ref[...])
    o_ref[...] = out.reshape(BATCH, tt, D_MODEL)
    wcast_out_ref[...] = wcast_ref[...].astype(_bf16)


def _mixer(x, y, q, q_meta, bc, gate_s, gate_c, w_glu, b_glu, w_s_up, conv_w, conv_b, w_c_up,
           w_o, g_post, w_cast):
    tt = MIX_TILE_T
    n_t = SEQ // tt
    cast_spec = pl.BlockSpec((w_cast.shape[0] // n_t, w_cast.shape[1]), lambda t: (t, 0))
    halo_per_tile = tt // HALO_ROWS
    n_halo = SEQ // HALO_ROWS
    tile = lambda t: (0, t, 0)
    resident = functools.partial(pl.BlockSpec, index_map=lambda t: (0, 0),
                                 pipeline_mode=pl.Buffered(1))
    return pl.pallas_call(
        _mixer_kernel,
        grid=(n_t,),
        in_specs=[
            pl.BlockSpec((BATCH, tt, D_MODEL), tile),
            pl.BlockSpec((S5_SLABS, SLAB_PAIRS, 2, tt // S5_CHUNK * BATCH, V7X_LANES),
                         lambda t: (0, 0, 0, t, 0)),
            pl.BlockSpec((BATCH, tt, CONV_WIDTH), tile),
            pl.BlockSpec((BATCH, HALO_ROWS, CONV_WIDTH),
                         lambda t: (0, jnp.maximum(t * halo_per_tile - 1, 0), 0)),
            pl.BlockSpec((BATCH, HALO_ROWS, CONV_WIDTH),
                         lambda t: (0, jnp.minimum((t + 1) * halo_per_tile, n_halo - 1), 0)),
            resident((HALO_ROWS, CONV_WIDTH)),
            pl.BlockSpec((BATCH, tt, CONV_WIDTH), tile),
            pl.BlockSpec((BATCH, tt, D_MODEL), tile),
            pl.BlockSpec((BATCH, tt, D_MODEL), tile),
            resident((S5_WIDTH, S5_WIDTH)),
            resident((1, S5_WIDTH)),
            resident((S5_WIDTH, D_MODEL)),
            resident((3, CONV_WIDTH)),
            resident((1, CONV_WIDTH)),
            resident((CONV_WIDTH, D_MODEL)),
            resident((D_MODEL, D_MODEL)),
            resident((1, D_MODEL)),
            cast_spec,
        ],
        out_specs=[pl.BlockSpec((BATCH, tt, D_MODEL), tile), cast_spec],
        out_shape=[jax.ShapeDtypeStruct((BATCH, SEQ, D_MODEL), _f32),
                   jax.ShapeDtypeStruct(w_cast.shape, _bf16)],
        scratch_shapes=[pltpu.VMEM((S5_SLABS, BATCH * tt, V7X_LANES), _f32),
                        pltpu.VMEM((BATCH, tt + 2 * HALO_ROWS, CONV_WIDTH), _f32)],
        compiler_params=_params(("arbitrary",)),
        name="mixer_tail",
    )(x, y, q, q, q, q_meta, bc, gate_s, gate_c, w_glu, b_glu, w_s_up, conv_w, conv_b, w_c_up,
      w_o, g_post, w_cast)


def _ffn_kernel(h_ref, gpre_ref, wgu_ref, wout_ref, gpost_ref, o_ref, hb_scr):
    j = pl.program_id(1)
    last = pl.num_programs(1) - 1

    def chunk(hb):
        gu = _dot(hb, wgu_ref[...])
        gate, up = gu[:, :FFN_TILE_HIDDEN], gu[:, FFN_TILE_HIDDEN:]
        return _dot((jax.nn.silu(gate) * up).astype(_bf16), wout_ref[...])

    @pl.when(j == 0)
    def _():
        hb = _rms_norm(h_ref[...], gpre_ref[...]).astype(_bf16)
        hb_scr[...] = hb
        o_ref[...] = chunk(hb)

    @pl.when(jnp.logical_and(j > 0, j < last))
    def _():
        o_ref[...] += chunk(hb_scr[...])

    @pl.when(j == last)
    def _():
        f = o_ref[...] + chunk(hb_scr[...])
        o_ref[...] = h_ref[...] + _rms_norm(f, gpost_ref[...])


def _ffn(h1, g_pre, w_in, w_out, g_post):
    tm, th = FFN_TILE_ROWS, FFN_TILE_HIDDEN
    n_hidden = FFN_HIDDEN // th
    return pl.pallas_call(
        _ffn_kernel,
        grid=(ROWS // tm, n_hidden),
        in_specs=[
            pl.BlockSpec((tm, D_MODEL), lambda i, j: (i, 0)),
            pl.BlockSpec((1, D_MODEL), lambda i, j: (0, 0)),
            pl.BlockSpec((D_MODEL, 2 * th), lambda i, j: (0, j)),
            pl.BlockSpec((th, D_MODEL), lambda i, j: (j, 0)),
            pl.BlockSpec((1, D_MODEL), lambda i, j: (0, 0)),
        ],
        out_specs=pl.BlockSpec((tm, D_MODEL), lambda i, j: (i, 0)),
        out_shape=jax.ShapeDtypeStruct((ROWS, D_MODEL), _f32),
        scratch_shapes=[pltpu.VMEM((tm, D_MODEL), _bf16)],
        compiler_params=_params(("arbitrary", "arbitrary")),
        name="ffn",
    )(h1, g_pre, w_in, w_out, g_post)


def kernel(x, meta, g_mix_pre, g_mix_post, g_ffn_pre, g_ffn_post, w_in, gate_b, lam_re, lam_im,
           log_dt, b_re, b_im, c_re, c_im, d_skip, w_glu, b_glu, w_s_up, conv_w, conv_b, w_c_up,
           w_o, w_ffn_in, w_ffn_out):
    l = 0
    w_in_b = _cast_w_in(w_in[l])
    gate_b2 = gate_b[l].reshape(2, 1, D_MODEL)
    g_pre = g_mix_pre[l].reshape(1, D_MODEL)

    ptab, bz_re, bz_im = _discretise(lam_re[l], lam_im[l], log_dt[l], b_re[l], b_im[l])
    dg = N_DIR * S5_GROUPS
    b_maps = _compact_maps(bz_re, bz_im)
    c_maps = _compact_maps(c_re[l].reshape(dg, S5_GROUP, S5_STATE),
                           -c_im[l].reshape(dg, S5_GROUP, S5_STATE))
    ptab = ptab.reshape(S5_CHUNK + 1, 2, N_DIR, S5_SLABS, SLAB_STATES)
    ptab = ptab.transpose(3, 2, 0, 1, 4)

    (u5, bc, q, gate_s, gate_c, w_ffn_in_b, w_glu_b, w_s_up_b, w_c_up_b, w_o_b) = _in_proj(
        x, g_pre, w_in_b, gate_b2, IN_TILE_ROWS,
        cast=((w_ffn_in[l], _ffn_in_plan()), (w_glu[l], None), (w_s_up[l], None),
              (w_c_up[l], None), (w_o[l], None)))
    um5, q_meta = _meta_proj(meta.astype(x.dtype), g_pre, w_in_b)

    u = u5.reshape(S5_SLABS, S5_CHUNK_ROWS * S5_CHUNK, V7X_LANES)
    u_meta = jnp.pad(um5.reshape(S5_SLABS, META_CHUNKS, S5_CHUNK_COLS),
                     ((0, 0), (0, V7X_SUBLANES - META_CHUNKS), (0, 0)))
    d_pairs = jnp.tile(d_skip[l].reshape(S5_SLABS, SLAB_PAIRS, 1, PAIR_LANES),
                       (1, 1, 1, S5_CHUNK))
    y = _s5(u, u_meta, b_maps, c_maps, ptab, d_pairs)

    h1, w_ffn_out_b = _mixer(x, y, q, q_meta, bc, gate_s, gate_c, w_glu_b,
                             b_glu[l].reshape(1, S5_WIDTH), w_s_up_b, conv_w[l],
                             conv_b[l].reshape(1, CONV_WIDTH), w_c_up_b, w_o_b,
                             g_mix_post[l].reshape(1, D_MODEL), w_ffn_out[l])

    h2 = _ffn(h1.reshape(ROWS, D_MODEL), g_ffn_pre[l].reshape(1, D_MODEL),
              w_ffn_in_b, w_ffn_out_b, g_ffn_post[l].reshape(1, D_MODEL))
    return h2.reshape(BATCH, SEQ, D_MODEL)
```

```python
import functools
import math

import jax
import jax.numpy as jnp
from jax import lax
from jax.experimental import pallas as pl
from jax.experimental.pallas import tpu as pltpu

D_MODEL = 2048
BATCH = 4
SEQ = 2048
N_META = 16
S5_WIDTH = 1024
S5_GROUP = 16
S5_GROUPS = S5_WIDTH // S5_GROUP
S5_STATE = 64
N_DIR = 2
CONV_WIDTH = 1024
FFN_HIDDEN = ((math.ceil(8 * D_MODEL / 3) + 255) // 256) * 256
IN_COLS = S5_WIDTH + 3 * CONV_WIDTH + 2 * D_MODEL
RMS_EPS = 1e-6
LAM_RE_MAX = -1e-4

ROWS = SEQ * BATCH

V7X_SUBLANES = 8
V7X_LANES = 128
V7X_MXU_DIM = 256
V7X_VMEM_LIMIT_BYTES = 56 * 1024 * 1024

S5_SLABS = S5_WIDTH // V7X_LANES
SLAB_GROUPS = V7X_LANES // S5_GROUP
SLAB_STATES = SLAB_GROUPS * S5_STATE
PAIR_LANES = 2 * S5_GROUP
SLAB_PAIRS = V7X_LANES // PAIR_LANES
PAIR_STATES = 2 * S5_STATE
S5_CHUNK = V7X_SUBLANES
PAIR_COLS = S5_CHUNK * PAIR_LANES
S5_CHUNKS = SEQ // S5_CHUNK
S5_CHUNK_ROWS = S5_CHUNKS * BATCH
S5_CHUNK_COLS = S5_CHUNK * V7X_LANES
META_CHUNKS = N_META // S5_CHUNK
STATE_REPEATS = V7X_LANES // S5_STATE

IN_TILE_ROWS = 1024
IN_STEPS = 8
IN_SEG_COLS = S5_WIDTH // IN_STEPS
CAST_TILE_ROWS = 128
MIX_TILE_T = 64
FFN_TILE_ROWS = 1024
FFN_TILE_HIDDEN = 256
HALO_ROWS = 16

_f32 = jnp.float32
_bf16 = jnp.bfloat16


def _rms_norm(xf, g):
    r = lax.rsqrt(jnp.mean(xf * xf, axis=-1, keepdims=True) + RMS_EPS)
    return xf * r * g


def _params(sem):
    return pltpu.CompilerParams(dimension_semantics=sem,
                                vmem_limit_bytes=V7X_VMEM_LIMIT_BYTES)


def _dot(a, b):
    return jnp.dot(a, b, preferred_element_type=_f32)


def _dot_nt(a, b):
    return lax.dot_general(a, b, (((1,), (1,)), ((), ())), preferred_element_type=_f32)


def _discretise_kernel(lam_re_ref, lam_im_ref, log_dt_ref, b_re_ref, b_im_ref,
                       pow_ref, bz_re_ref, bz_im_ref):
    lr = jnp.minimum(lam_re_ref[...], LAM_RE_MAX)
    li = lam_im_ref[...]
    dt = jnp.exp(log_dt_ref[...])
    mag = jnp.exp(lr * dt)
    ar = mag * jnp.cos(li * dt)
    ai = mag * jnp.sin(li * dt)
    den = lr * lr + li * li
    nr = ar - 1.0
    zr = (nr * lr + ai * li) / den
    zi = (ai * lr - nr * li) / den
    pr = jnp.ones_like(ar)
    pi = jnp.zeros_like(ar)
    for m in range(S5_CHUNK + 1):
        pow_ref[m, 0] = pr
        pow_ref[m, 1] = pi
        pr, pi = pr * ar - pi * ai, pr * ai + pi * ar
    b_re = b_re_ref[...]
    b_im = b_im_ref[...]
    bz_re_ref[...] = zr[:, None, :] * b_re - zi[:, None, :] * b_im
    bz_im_ref[...] = zr[:, None, :] * b_im + zi[:, None, :] * b_re


def _discretise(lam_re, lam_im, log_dt, b_re, b_im):
    dg = N_DIR * S5_GROUPS
    b_re_t = jnp.swapaxes(b_re, -1, -2).reshape(dg, S5_GROUP, S5_STATE)
    b_im_t = jnp.swapaxes(b_im, -1, -2).reshape(dg, S5_GROUP, S5_STATE)
    return pl.pallas_call(
        _discretise_kernel,
        out_shape=(jax.ShapeDtypeStruct((S5_CHUNK + 1, 2, dg, S5_STATE), _f32),
                   jax.ShapeDtypeStruct((dg, S5_GROUP, S5_STATE), _f32),
                   jax.ShapeDtypeStruct((dg, S5_GROUP, S5_STATE), _f32)),
        name="s5_discretise",
    )(lam_re.reshape(dg, S5_STATE), lam_im.reshape(dg, S5_STATE),
      log_dt.reshape(dg, 1), b_re_t, b_im_t)


def _compact_maps(w_re, w_im):
    def part(w):
        w = w.reshape(N_DIR, S5_SLABS, V7X_LANES, S5_STATE)
        return jnp.concatenate([w] * STATE_REPEATS, axis=-1)
    return jnp.stack([part(w_re), part(w_im)], axis=2)


def _w_in_pieces(j):
    seg, gseg = IN_SEG_COLS, 2 * IN_SEG_COLS
    pieces = [(4 * S5_WIDTH + n * D_MODEL + j * gseg, gseg) for n in range(2)]
    pieces += [(n * S5_WIDTH + j * seg, seg) for n in range(4)]
    return pieces


def _cast_w_in_kernel(w_ref, o_ref):
    dst = 0
    for j in range(IN_STEPS):
        for src, width in _w_in_pieces(j):
            o_ref[:, dst:dst + width] = w_ref[:, src:src + width].astype(_bf16)
            dst += width


def _cast_w_in(w):
    spec = pl.BlockSpec((CAST_TILE_ROWS, IN_COLS), lambda i: (i, 0))
    return pl.pallas_call(
        _cast_w_in_kernel,
        grid=(D_MODEL // CAST_TILE_ROWS,),
        in_specs=[spec],
        out_specs=spec,
        out_shape=jax.ShapeDtypeStruct(w.shape, _bf16),
        compiler_params=_params(("arbitrary",)),
        name="cast_w_in",
    )(w)


def _ffn_in_plan():
    th = FFN_TILE_HIDDEN
    plan = []
    for j in range(FFN_HIDDEN // th):
        plan.append((j * th, 2 * j * th, th))
        plan.append((FFN_HIDDEN + j * th, (2 * j + 1) * th, th))
    return plan


def _in_proj_kernel(x_ref, g_ref, w_ref, gbs_ref, gbc_ref, *rest, cast_plans):
    n_cast = len(cast_plans)
    cast_in = rest[:n_cast]
    u_ref, bc_ref, q_ref, gs_ref, gc_ref = rest[n_cast:n_cast + 5]
    cast_out = rest[n_cast + 5:2 * n_cast + 5]
    h_scr = rest[2 * n_cast + 5]
    seg = IN_SEG_COLS

    def column_step(h):
        proj = _dot(h, w_ref[...])
        gs_ref[...] = jax.nn.sigmoid(proj[:, 0:2 * seg] + gbs_ref[0]).astype(_bf16)
        gc_ref[...] = jax.nn.sigmoid(proj[:, 2 * seg:4 * seg] + gbc_ref[0]).astype(_bf16)
        for s in range(seg // V7X_LANES):
            u_ref[s] = proj[:, 4 * seg + s * V7X_LANES:4 * seg + (s + 1) * V7X_LANES].reshape(
                -1, S5_CHUNK, V7X_LANES)
        bc_ref[...] = proj[:, 6 * seg:7 * seg].astype(_bf16)
        q_ref[...] = (proj[:, 7 * seg:8 * seg] * proj[:, 5 * seg:6 * seg]).astype(_bf16)
        for src, dst, plan in zip(cast_in, cast_out, cast_plans):
            for s0, d0, width in plan:
                dst[:, d0:d0 + width] = src[:, s0:s0 + width].astype(_bf16)

    @pl.when(pl.program_id(2) == 0)
    def _():
        h = _rms_norm(x_ref[...], g_ref[...]).astype(_bf16)
        h_scr[...] = h
        column_step(h)

    @pl.when(pl.program_id(2) > 0)
    def _():
        column_step(h_scr[...])


def _in_proj(x3, g, w_in_b, gate_b, tile_rows, cast=()):
    nb, t_len, _ = x3.shape
    n_t = t_len // tile_rows
    n_steps = nb * n_t * IN_STEPS
    seg, gseg = IN_SEG_COLS, 2 * IN_SEG_COLS
    tile = lambda b, t, j: (b, t, j)
    step = lambda b, t, j: ((b * n_t + t) * IN_STEPS + j, 0)
    cast_w = [w for w, _ in cast]
    cast_plans = tuple(tuple(plan) if plan else ((0, 0, w.shape[1]),) for w, plan in cast)
    cast_specs = [pl.BlockSpec((w.shape[0] // n_steps, w.shape[1]), step) for w in cast_w]
    gb = lambda n: pl.BlockSpec((1, 1, gseg), lambda b, t, j: (n, 0, j))
    return pl.pallas_call(
        functools.partial(_in_proj_kernel, cast_plans=cast_plans),
        grid=(nb, n_t, IN_STEPS),
        in_specs=[
            pl.BlockSpec((None, tile_rows, D_MODEL), lambda b, t, j: (b, t, 0)),
            pl.BlockSpec((1, D_MODEL), lambda b, t, j: (0, 0)),
            pl.BlockSpec((D_MODEL, IN_COLS // IN_STEPS), lambda b, t, j: (0, j)),
            gb(0), gb(1),
        ] + cast_specs,
        out_specs=[
            pl.BlockSpec((seg // V7X_LANES, tile_rows // S5_CHUNK, None, S5_CHUNK, V7X_LANES),
                         lambda b, t, j: (j, t, b, 0, 0)),
            pl.BlockSpec((None, tile_rows, seg), tile),
            pl.BlockSpec((None, tile_rows, seg), tile),
            pl.BlockSpec((None, tile_rows, gseg), tile),
            pl.BlockSpec((None, tile_rows, gseg), tile),
        ] + cast_specs,
        out_shape=[jax.ShapeDtypeStruct(
                       (S5_SLABS, t_len // S5_CHUNK, nb, S5_CHUNK, V7X_LANES), _f32),
                   jax.ShapeDtypeStruct((nb, t_len, CONV_WIDTH), _bf16),
                   jax.ShapeDtypeStruct((nb, t_len, CONV_WIDTH), _bf16),
                   jax.ShapeDtypeStruct((nb, t_len, D_MODEL), _bf16),
                   jax.ShapeDtypeStruct((nb, t_len, D_MODEL), _bf16)]
                  + [jax.ShapeDtypeStruct(w.shape, _bf16) for w in cast_w],
        scratch_shapes=[pltpu.VMEM((tile_rows, D_MODEL), _bf16)],
        compiler_params=_params(("arbitrary", "arbitrary", "arbitrary")),
        name="in_proj",
    )(x3, g, w_in_b, gate_b, gate_b, *cast_w)


def _meta_proj_kernel(x_ref, g_ref, w_ref, u_ref, q_ref, h_scr):
    seg = IN_SEG_COLS

    @pl.when(pl.program_id(0) == 0)
    def _():
        h_scr[...] = _rms_norm(x_ref[...], g_ref[...]).astype(_bf16)

    proj = _dot(h_scr[...], w_ref[...])
    for s in range(seg // V7X_LANES):
        u_ref[s] = proj[:, s * V7X_LANES:(s + 1) * V7X_LANES].reshape(-1, S5_CHUNK, V7X_LANES)
    q_ref[...] = (proj[:, 3 * seg:4 * seg] * proj[:, seg:2 * seg]).astype(_bf16)


def _meta_proj(meta, g, w_in_b):
    seg = IN_SEG_COLS
    return pl.pallas_call(
        _meta_proj_kernel,
        grid=(IN_STEPS,),
        in_specs=[
            pl.BlockSpec((N_META, D_MODEL), lambda j: (0, 0)),
            pl.BlockSpec((1, D_MODEL), lambda j: (0, 0)),
            pl.BlockSpec((D_MODEL, 4 * seg), lambda j: (0, 2 * j + 1)),
        ],
        out_specs=[
            pl.BlockSpec((seg // V7X_LANES, META_CHUNKS, S5_CHUNK, V7X_LANES),
                         lambda j: (j, 0, 0, 0)),
            pl.BlockSpec((N_META, seg), lambda j: (0, j)),
        ],
        out_shape=[jax.ShapeDtypeStruct((S5_SLABS, META_CHUNKS, S5_CHUNK, V7X_LANES), _f32),
                   jax.ShapeDtypeStruct((N_META, CONV_WIDTH), _bf16)],
        scratch_shapes=[pltpu.VMEM((N_META, D_MODEL), _bf16)],
        compiler_params=_params(("arbitrary",)),
        name="meta_proj",
    )(meta, g, w_in_b)


def _pair_map(ref, d, part, q):
    rows = lax.broadcasted_iota(jnp.int32, (PAIR_LANES, V7X_LANES), 0)
    cols = lax.broadcasted_iota(jnp.int32, (PAIR_LANES, V7X_LANES), 1)
    same_group = (rows // S5_GROUP) == (cols // S5_STATE)
    return jnp.where(same_group, ref[d, 0, part, q * PAIR_LANES:(q + 1) * PAIR_LANES, :], 0.0)


def _transpose_pieces(src):
    assert SLAB_PAIRS == 4 and len(src) == SLAB_PAIRS
    lane = lax.broadcasted_iota(jnp.int32, src[0].shape, 1)

    def exchange(a, b, width):
        keep = (lane // width) % 2 == 0
        return (jnp.where(keep, a, pltpu.roll(b, width, 1)),
                jnp.where(keep, pltpu.roll(a, V7X_LANES - width, 1), b))

    b0, b2 = exchange(src[0], src[2], 2 * PAIR_LANES)
    b1, b3 = exchange(src[1], src[3], 2 * PAIR_LANES)
    c0, c1 = exchange(b0, b1, PAIR_LANES)
    c2, c3 = exchange(b2, b3, PAIR_LANES)
    return [c0, c1, c2, c3]


def _boundary_scan(x_scr, a_re, a_im, init):
    n_tiles = S5_CHUNK_ROWS // V7X_SUBLANES
    low = lax.broadcasted_iota(jnp.int32, (V7X_SUBLANES, V7X_LANES), 0) < BATCH

    def step(ar, ai, sr, si, xr, xi):
        return ar * sr - ai * si + xr, ar * si + ai * sr + xi

    def swap(v):
        return pltpu.roll(v, BATCH, 0)

    def body(i, carry):
        rf = pl.multiple_of(i * V7X_SUBLANES, V7X_SUBLANES)
        rb = pl.multiple_of((n_tiles - 1 - i) * V7X_SUBLANES, V7X_SUBLANES)
        loaded = []
        for q in range(SLAB_PAIRS):
            for part in range(2):
                lanes = slice(q * PAIR_COLS + part * PAIR_STATES,
                              q * PAIR_COLS + (part + 1) * PAIR_STATES)
                loaded.append((lanes, x_scr[0, pl.ds(rf, V7X_SUBLANES), lanes],
                               x_scr[1, pl.ds(rb, V7X_SUBLANES), lanes]))
        new, stores = [], []
        for q in range(SLAB_PAIRS):
            (re, xfr, xbr), (im, xfi, xbi) = loaded[2 * q], loaded[2 * q + 1]
            w1r, w1i = jnp.where(low, xfr, xbr), jnp.where(low, xfi, xbi)
            w2r, w2i = swap(jnp.where(low, xbr, xfr)), swap(jnp.where(low, xbi, xfi))
            cr, ci = carry[2 * q], carry[2 * q + 1]
            s1r, s1i = step(a_re[q], a_im[q], cr, ci, w1r, w1i)
            s2r, s2i = step(a_re[q], a_im[q], s1r, s1i, w2r, w2i)
            t1r, t1i = swap(s1r), swap(s1i)
            stores += [(0, rf, re, jnp.where(low, cr, t1r)), (0, rf, im, jnp.where(low, ci, t1i)),
                       (1, rb, re, jnp.where(low, t1r, cr)), (1, rb, im, jnp.where(low, t1i, ci))]
            new += [s2r, s2i]
        for d, r0, lanes, value in stores:
            x_scr[d, pl.ds(r0, V7X_SUBLANES), lanes] = value
        return tuple(new)

    lax.fori_loop(0, n_tiles, body, init)


def _s5_kernel(u_ref, um_ref, b_ref, c_ref, p_ref, d_ref, y_ref,
               ub_scr, uf_scr, w_scr, x_scr, sp_scr, tap_scr, toep_scr, xm_scr):
    half_steps = S5_CHUNK // 2
    blk = PAIR_LANES

    def pair_cols(q):
        return slice(q * PAIR_COLS, (q + 1) * PAIR_COLS)

    def pair_states(q):
        return slice(q * PAIR_STATES, (q + 1) * PAIR_STATES)

    def strided_u(j):
        return u_ref[0, pl.ds(j, S5_CHUNK_ROWS, stride=S5_CHUNK), :]

    um = um_ref[0]
    um_q = [[] for _ in range(SLAB_PAIRS)]
    for h in range(2):
        steps = range(h * half_steps, (h + 1) * half_steps)
        for q, t in enumerate(_transpose_pieces([strided_u(j) for j in steps])):
            uf_scr[q, :, h * V7X_LANES:(h + 1) * V7X_LANES] = t
            ub_scr[q, :, h * V7X_LANES:(h + 1) * V7X_LANES] = t.astype(_bf16)
        meta = [um[:, j * V7X_LANES:(j + 1) * V7X_LANES] for j in steps]
        for q, t in enumerate(_transpose_pieces(meta)):
            um_q[q].append(t)

    for d in range(N_DIR):
        for q in range(SLAB_PAIRS):
            w_re = _pair_map(b_ref, d, 0, q)
            w_im = _pair_map(b_ref, d, 1, q)
            for j in range(S5_CHUNK):
                m = (S5_CHUNK - 1 - j) if d == 0 else j
                pr = p_ref[0, d, m, 0:1, pair_states(q)]
                pi = p_ref[0, d, m, 1:2, pair_states(q)]
                rows = slice(j * blk, (j + 1) * blk)
                w_scr[d, q, rows, 0:PAIR_STATES] = (w_re * pr - w_im * pi).astype(_bf16)
                w_scr[d, q, rows, PAIR_STATES:] = (w_re * pi + w_im * pr).astype(_bf16)
            x_scr[d, :, pair_cols(q)] = _dot(ub_scr[q], w_scr[d, q])
            if d == 0:
                xm_scr[:, pair_cols(q)] = _dot(
                    jnp.concatenate(um_q[q], axis=1).astype(_bf16), w_scr[0, q])

    low = lax.broadcasted_iota(jnp.int32, (V7X_SUBLANES, V7X_LANES), 0) < BATCH
    a_re, a_im, init = [], [], []
    for q in range(SLAB_PAIRS):
        re = slice(q * PAIR_COLS, q * PAIR_COLS + PAIR_STATES)
        im = slice(q * PAIR_COLS + PAIR_STATES, (q + 1) * PAIR_COLS)
        ar = p_ref[0, 0, S5_CHUNK, 0:1, pair_states(q)]
        ai = p_ref[0, 0, S5_CHUNK, 1:2, pair_states(q)]
        a_re.append(jnp.where(low, ar, p_ref[0, 1, S5_CHUNK, 0:1, pair_states(q)]))
        a_im.append(jnp.where(low, ai, p_ref[0, 1, S5_CHUNK, 1:2, pair_states(q)]))
        sr = jnp.zeros((1, V7X_LANES), _f32)
        si = jnp.zeros((1, V7X_LANES), _f32)
        for c in range(META_CHUNKS):
            xr = xm_scr[c:c + 1, re]
            xi = xm_scr[c:c + 1, im]
            sr, si = ar * sr - ai * si + xr, ar * si + ai * sr + xi
        init.append(jnp.where(low, sr, 0.0))
        init.append(jnp.where(low, si, 0.0))
    _boundary_scan(x_scr, a_re, a_im, tuple(init))
    for d in range(N_DIR):
        sp_scr[d] = x_scr[d].astype(_bf16)

    def out_weights(c_re, c_imn, d, m, q):
        pr = p_ref[0, d, m, 0:1, pair_states(q)]
        pi = p_ref[0, d, m, 1:2, pair_states(q)]
        return jnp.concatenate([c_re * pr + c_imn * pi, c_imn * pr - c_re * pi],
                               axis=1).astype(_bf16)

    for q in range(SLAB_PAIRS):
        wy = []
        for d in range(N_DIR):
            c_re = _pair_map(c_ref, d, 0, q)
            c_imn = _pair_map(c_ref, d, 1, q)
            c_map = jnp.concatenate([c_re, c_imn], axis=1).astype(_bf16)
            tap_scr[d] = _dot_nt(w_scr[d, q], jnp.concatenate([c_map] * S5_CHUNK, axis=0))
            powers = [o + 1 if d == 0 else S5_CHUNK - o for o in range(S5_CHUNK)]
            wy.append(jnp.concatenate([out_weights(c_re, c_imn, d, m, q) for m in powers],
                                      axis=0))
        for o in range(S5_CHUNK):
            lanes = slice(o * blk, (o + 1) * blk)
            if o:
                toep_scr[0:o * blk, lanes] = tap_scr[0, (S5_CHUNK - 1 - o) * blk:
                                                     (S5_CHUNK - 1) * blk, lanes]
            toep_scr[o * blk:(o + 1) * blk, lanes] = (
                tap_scr[0, (S5_CHUNK - 1) * blk:, lanes] + tap_scr[1, 0:blk, lanes])
            if o < S5_CHUNK - 1:
                toep_scr[(o + 1) * blk:, lanes] = tap_scr[1, blk:(S5_CHUNK - o) * blk, lanes]
        y = _dot_nt(sp_scr[0, :, pair_cols(q)], wy[0])
        y += _dot_nt(sp_scr[1, :, pair_cols(q)], wy[1])
        y += _dot(ub_scr[q], toep_scr[...].astype(_bf16))
        y += uf_scr[q] * d_ref[0, q]
        for h in range(2):
            y_ref[0, q, h] = y[:, h * V7X_LANES:(h + 1) * V7X_LANES]


def _s5(u, u_meta, b_maps, c_maps, ptab, d_skip):
    maps = pl.BlockSpec((N_DIR, 1, 2, V7X_LANES, V7X_LANES), lambda s: (0, s, 0, 0, 0))
    return pl.pallas_call(
        _s5_kernel,
        grid=(S5_SLABS,),
        in_specs=[
            pl.BlockSpec((1, S5_CHUNK_ROWS * S5_CHUNK, V7X_LANES), lambda s: (s, 0, 0)),
            pl.BlockSpec((1, V7X_SUBLANES, S5_CHUNK_COLS), lambda s: (s, 0, 0)),
            maps,
            maps,
            pl.BlockSpec((1, N_DIR, S5_CHUNK + 1, 2, SLAB_STATES), lambda s: (s, 0, 0, 0, 0)),
            pl.BlockSpec((1, SLAB_PAIRS, 1, PAIR_COLS), lambda s: (s, 0, 0, 0)),
        ],
        out_specs=pl.BlockSpec((1, SLAB_PAIRS, 2, S5_CHUNK_ROWS, V7X_LANES),
                               lambda s: (s, 0, 0, 0, 0)),
        out_shape=jax.ShapeDtypeStruct((S5_SLABS, SLAB_PAIRS, 2, S5_CHUNK_ROWS, V7X_LANES),
                                       _f32),
        scratch_shapes=[
            pltpu.VMEM((SLAB_PAIRS, S5_CHUNK_ROWS, PAIR_COLS), _bf16),
            pltpu.VMEM((SLAB_PAIRS, S5_CHUNK_ROWS, PAIR_COLS), _f32),
            pltpu.VMEM((N_DIR, SLAB_PAIRS, PAIR_COLS, PAIR_COLS), _bf16),
            pltpu.VMEM((N_DIR, S5_CHUNK_ROWS, 2 * SLAB_STATES), _f32),
            pltpu.VMEM((N_DIR, S5_CHUNK_ROWS, 2 * SLAB_STATES), _bf16),
            pltpu.VMEM((N_DIR, PAIR_COLS, PAIR_COLS), _f32),
            pltpu.VMEM((PAIR_COLS, PAIR_COLS), _f32),
            pltpu.VMEM((V7X_SUBLANES, 2 * SLAB_STATES), _f32),
        ],
        compiler_params=_params(("arbitrary",)),
        name="s5",
    )(u, u_meta, b_maps, c_maps, ptab, d_skip)


def _mixer_kernel(h_ref, y_ref, q_ref, qp_ref, qn_ref, qm_ref, bc_ref, gs_ref, gc_ref,
                  wglu_ref, bglu_ref, wsup_ref, cw_ref, cb_ref, wcup_ref, wo_ref,
                  g_ref, wcast_ref, o_ref, wcast_out_ref, ys_scr, q_scr):
    t = pl.program_id(0)
    last = pl.num_programs(0) - 1
    tt = MIX_TILE_T
    tm = BATCH * tt
    n_c = tt // S5_CHUNK

    vs = []
    for b in range(BATCH):
        prev = jnp.where(t == 0, qm_ref[...], qp_ref[b]).astype(_f32)
        nxt = jnp.where(t == last, 0.0, qn_ref[b].astype(_f32))
        q_scr[b, 0:HALO_ROWS, :] = prev
        q_scr[b, HALO_ROWS:HALO_ROWS + tt, :] = q_ref[b].astype(_f32)
        q_scr[b, HALO_ROWS + tt:, :] = nxt
        vs.append(cw_ref[0:1, :] * q_scr[b, HALO_ROWS - 1:HALO_ROWS - 1 + tt, :]
                  + cw_ref[1:2, :] * q_scr[b, HALO_ROWS:HALO_ROWS + tt, :]
                  + cw_ref[2:3, :] * q_scr[b, HALO_ROWS + 1:HALO_ROWS + 1 + tt, :]
                  + cb_ref[...])
    v = jnp.concatenate(vs, axis=0)
    bc = bc_ref[...].reshape(tm, CONV_WIDTH).astype(_f32)
    y_c = _dot((bc * v).astype(_bf16), wcup_ref[...])

    half_steps = S5_CHUNK // 2
    for s in range(S5_SLABS):
        for h in range(2):
            for b in range(BATCH):
                tiles = [y_ref[s, q, h, pl.ds(b, n_c, stride=BATCH), :]
                         for q in range(SLAB_PAIRS)]
                for k, tile in enumerate(_transpose_pieces(tiles)):
                    o = h * half_steps + k
                    ys_scr[s, pl.ds(b * tt + o, n_c, stride=S5_CHUNK), :] = tile
    ys = jnp.concatenate([ys_scr[s] for s in range(S5_SLABS)], axis=1)
    ys = jax.nn.gelu(ys)
    glu = _dot(ys.astype(_bf16), wglu_ref[...]) + bglu_ref[...]
    ys = ys * jax.nn.sigmoid(glu)
    y_s = _dot(ys.astype(_bf16), wsup_ref[...])

    merged = (gs_ref[...].reshape(tm, D_MODEL).astype(_f32) * y_s
              + gc_ref[...].reshape(tm, D_MODEL).astype(_f32) * y_c)
    mixed = _dot(merged.astype(_bf16), wo_ref[...])
    out = h_ref[...].reshape(tm, D_MODEL) + _rms_norm(mixed, g_ref[...])
    o_ref[...] = out.reshape(BATCH, tt, D_MODEL)
    wcast_out_ref[...] = wcast_ref[...].astype(_bf16)


def _mixer(x, y, q, q_meta, bc, gate_s, gate_c, w_glu, b_glu, w_s_up, conv_w, conv_b, w_c_up,
           w_o, g_post, w_cast):
    tt = MIX_TILE_T
    n_t = SEQ // tt
    cast_spec = pl.BlockSpec((w_cast.shape[0] // n_t, w_cast.shape[1]), lambda t: (t, 0))
    halo_per_tile = tt // HALO_ROWS
    n_halo = SEQ // HALO_ROWS
    tile = lambda t: (0, t, 0)
    resident = functools.partial(pl.BlockSpec, index_map=lambda t: (0, 0),
                                 pipeline_mode=pl.Buffered(1))
    return pl.pallas_call(
        _mixer_kernel,
        grid=(n_t,),
        in_specs=[
            pl.BlockSpec((BATCH, tt, D_MODEL), tile),
            pl.BlockSpec((S5_SLABS, SLAB_PAIRS, 2, tt // S5_CHUNK * BATCH, V7X_LANES),
                         lambda t: (0, 0, 0, t, 0)),
            pl.BlockSpec((BATCH, tt, CONV_WIDTH), tile),
            pl.BlockSpec((BATCH, HALO_ROWS, CONV_WIDTH),
                         lambda t: (0, jnp.maximum(t * halo_per_tile - 1, 0), 0)),
            pl.BlockSpec((BATCH, HALO_ROWS, CONV_WIDTH),
                         lambda t: (0, jnp.minimum((t + 1) * halo_per_tile, n_halo - 1), 0)),
            resident((HALO_ROWS, CONV_WIDTH)),
            pl.BlockSpec((BATCH, tt, CONV_WIDTH), tile),
            pl.BlockSpec((BATCH, tt, D_MODEL), tile),
            pl.BlockSpec((BATCH, tt, D_MODEL), tile),
            resident((S5_WIDTH, S5_WIDTH)),
            resident((1, S5_WIDTH)),
            resident((S5_WIDTH, D_MODEL)),
            resident((3, CONV_WIDTH)),
            resident((1, CONV_WIDTH)),
            resident((CONV_WIDTH, D_MODEL)),
            resident((D_MODEL, D_MODEL)),
            resident((1, D_MODEL)),
            cast_spec,
        ],
        out_specs=[pl.BlockSpec((BATCH, tt, D_MODEL), tile), cast_spec],
        out_shape=[jax.ShapeDtypeStruct((BATCH, SEQ, D_MODEL), _f32),
                   jax.ShapeDtypeStruct(w_cast.shape, _bf16)],
        scratch_shapes=[pltpu.VMEM((S5_SLABS, BATCH * tt, V7X_LANES), _f32),
                        pltpu.VMEM((BATCH, tt + 2 * HALO_ROWS, CONV_WIDTH), _f32)],
        compiler_params=_params(("arbitrary",)),
        name="mixer_tail",
    )(x, y, q, q, q, q_meta, bc, gate_s, gate_c, w_glu, b_glu, w_s_up, conv_w, conv_b, w_c_up,
      w_o, g_post, w_cast)


def _ffn_kernel(h_hbm, gpre_ref, wgu_ref, wout_ref, gpost_ref, o_ref, hb_scr, h_buf, h_sem):
    i = pl.program_id(0)
    j = pl.program_id(1)
    last = pl.num_programs(1) - 1
    slot = i % 2

    def h_copy(tile, buf):
        rows = pl.ds(pl.multiple_of(tile * FFN_TILE_ROWS, FFN_TILE_ROWS), FFN_TILE_ROWS)
        return pltpu.make_async_copy(h_hbm.at[rows, :], h_buf.at[buf], h_sem.at[buf])

    def chunk(hb):
        gu = _dot(hb, wgu_ref[...])
        gate, up = gu[:, :FFN_TILE_HIDDEN], gu[:, FFN_TILE_HIDDEN:]
        return _dot((jax.nn.silu(gate) * up).astype(_bf16), wout_ref[...])

    @pl.when(j == 0)
    def _():
        @pl.when(i == 0)
        def _():
            h_copy(0, 0).start()

        h_copy(i, slot).wait()

        @pl.when(i + 1 < pl.num_programs(0))
        def _():
            h_copy(i + 1, 1 - slot).start()

        hb = _rms_norm(h_buf[slot], gpre_ref[...]).astype(_bf16)
        hb_scr[...] = hb
        o_ref[...] = chunk(hb)

    @pl.when(jnp.logical_and(j > 0, j < last))
    def _():
        o_ref[...] += chunk(hb_scr[...])

    @pl.when(j == last)
    def _():
        f = o_ref[...] + chunk(hb_scr[...])
        o_ref[...] = h_buf[slot] + _rms_norm(f, gpost_ref[...])


def _ffn(h1, g_pre, w_in, w_out, g_post):
    tm, th = FFN_TILE_ROWS, FFN_TILE_HIDDEN
    n_hidden = FFN_HIDDEN // th
    return pl.pallas_call(
        _ffn_kernel,
        grid=(ROWS // tm, n_hidden),
        in_specs=[
            pl.BlockSpec(memory_space=pl.ANY),
            pl.BlockSpec((1, D_MODEL), lambda i, j: (0, 0)),
            pl.BlockSpec((D_MODEL, 2 * th), lambda i, j: (0, j)),
            pl.BlockSpec((th, D_MODEL), lambda i, j: (j, 0)),
            pl.BlockSpec((1, D_MODEL), lambda i, j: (0, 0)),
        ],
        out_specs=pl.BlockSpec((tm, D_MODEL), lambda i, j: (i, 0)),
        out_shape=jax.ShapeDtypeStruct((ROWS, D_MODEL), _f32),
        scratch_shapes=[pltpu.VMEM((tm, D_MODEL), _bf16),
                        pltpu.VMEM((2, tm, D_MODEL), _f32),
                        pltpu.SemaphoreType.DMA((2,))],
        compiler_params=_params(("arbitrary", "arbitrary")),
        name="ffn",
    )(h1, g_pre, w_in, w_out, g_post)


def kernel(x, meta, g_mix_pre, g_mix_post, g_ffn_pre, g_ffn_post, w_in, gate_b, lam_re, lam_im,
           log_dt, b_re, b_im, c_re, c_im, d_skip, w_glu, b_glu, w_s_up, conv_w, conv_b, w_c_up,
           w_o, w_ffn_in, w_ffn_out):
    l = 0
    w_in_b = _cast_w_in(w_in[l])
    gate_b2 = gate_b[l].reshape(2, 1, D_MODEL)
    g_pre = g_mix_pre[l].reshape(1, D_MODEL)

    ptab, bz_re, bz_im = _discretise(lam_re[l], lam_im[l], log_dt[l], b_re[l], b_im[l])
    dg = N_DIR * S5_GROUPS
    b_maps = _compact_maps(bz_re, bz_im)
    c_maps = _compact_maps(c_re[l].reshape(dg, S5_GROUP, S5_STATE),
                           -c_im[l].reshape(dg, S5_GROUP, S5_STATE))
    ptab = ptab.reshape(S5_CHUNK + 1, 2, N_DIR, S5_SLABS, SLAB_STATES)
    ptab = ptab.transpose(3, 2, 0, 1, 4)

    (u5, bc, q, gate_s, gate_c, w_ffn_in_b, w_glu_b, w_s_up_b, w_c_up_b, w_o_b) = _in_proj(
        x, g_pre, w_in_b, gate_b2, IN_TILE_ROWS,
        cast=((w_ffn_in[l], _ffn_in_plan()), (w_glu[l], None), (w_s_up[l], None),
              (w_c_up[l], None), (w_o[l], None)))
    um5, q_meta = _meta_proj(meta.astype(x.dtype), g_pre, w_in_b)

    u = u5.reshape(S5_SLABS, S5_CHUNK_ROWS * S5_CHUNK, V7X_LANES)
    u_meta = jnp.pad(um5.reshape(S5_SLABS, META_CHUNKS, S5_CHUNK_COLS),
                     ((0, 0), (0, V7X_SUBLANES - META_CHUNKS), (0, 0)))
    d_pairs = jnp.tile(d_skip[l].reshape(S5_SLABS, SLAB_PAIRS, 1, PAIR_LANES),
                       (1, 1, 1, S5_CHUNK))
    y = _s5(u, u_meta, b_maps, c_maps, ptab, d_pairs)

    h1, w_ffn_out_b = _mixer(x, y, q, q_meta, bc, gate_s, gate_c, w_glu_b,
                             b_glu[l].reshape(1, S5_WIDTH), w_s_up_b, conv_w[l],
                             conv_b[l].reshape(1, CONV_WIDTH), w_c_up_b, w_o_b,
                             g_mix_post[l].reshape(1, D_MODEL), w_ffn_out[l])

    h2 = _ffn(h1.reshape(ROWS, D_MODEL), g_ffn_pre[l].reshape(1, D_MODEL),
              w_ffn_in_b, w_ffn_out_b, g_ffn_post[l].reshape(1, D_MODEL))
    return h2.reshape(BATCH, SEQ, D_MODEL)
```

```python
import functools
import math

import jax
import jax.numpy as jnp
from jax import lax
from jax.experimental import pallas as pl
from jax.experimental.pallas import tpu as pltpu

D_MODEL = 2048
BATCH = 4
SEQ = 2048
N_META = 16
S5_WIDTH = 1024
S5_GROUP = 16
S5_GROUPS = S5_WIDTH // S5_GROUP
S5_STATE = 64
N_DIR = 2
CONV_WIDTH = 1024
FFN_HIDDEN = ((math.ceil(8 * D_MODEL / 3) + 255) // 256) * 256
IN_COLS = S5_WIDTH + 3 * CONV_WIDTH + 2 * D_MODEL
RMS_EPS = 1e-6
LAM_RE_MAX = -1e-4

ROWS = SEQ * BATCH

V7X_SUBLANES = 8
V7X_LANES = 128
V7X_MXU_DIM = 256
V7X_VMEM_LIMIT_BYTES = 56 * 1024 * 1024

S5_SLABS = S5_WIDTH // V7X_LANES
SLAB_GROUPS = V7X_LANES // S5_GROUP
SLAB_STATES = SLAB_GROUPS * S5_STATE
PAIR_LANES = 2 * S5_GROUP
SLAB_PAIRS = V7X_LANES // PAIR_LANES
PAIR_STATES = 2 * S5_STATE
S5_CHUNK = V7X_SUBLANES
PAIR_COLS = S5_CHUNK * PAIR_LANES
S5_CHUNKS = SEQ // S5_CHUNK
S5_CHUNK_ROWS = S5_CHUNKS * BATCH
S5_CHUNK_COLS = S5_CHUNK * V7X_LANES
META_CHUNKS = N_META // S5_CHUNK
STATE_REPEATS = V7X_LANES // S5_STATE

IN_TILE_ROWS = 1024
IN_STEPS = 8
IN_SEG_COLS = S5_WIDTH // IN_STEPS
CAST_TILE_ROWS = 128
MIX_TILE_T = 64
FFN_TILE_ROWS = 1024
FFN_TILE_HIDDEN = 256
HALO_ROWS = 16

_f32 = jnp.float32
_bf16 = jnp.bfloat16


def _rms_norm(xf, g):
    r = lax.rsqrt(jnp.mean(xf * xf, axis=-1, keepdims=True) + RMS_EPS)
    return xf * r * g


def _params(sem):
    return pltpu.CompilerParams(dimension_semantics=sem,
                                vmem_limit_bytes=V7X_VMEM_LIMIT_BYTES)


def _dot(a, b):
    return jnp.dot(a, b, preferred_element_type=_f32)


def _dot_nt(a, b):
    return lax.dot_general(a, b, (((1,), (1,)), ((), ())), preferred_element_type=_f32)


def _discretise_kernel(lam_re_ref, lam_im_ref, log_dt_ref, b_re_ref, b_im_ref,
                       pow_ref, bz_re_ref, bz_im_ref):
    lr = jnp.minimum(lam_re_ref[...], LAM_RE_MAX)
    li = lam_im_ref[...]
    dt = jnp.exp(log_dt_ref[...])
    mag = jnp.exp(lr * dt)
    ar = mag * jnp.cos(li * dt)
    ai = mag * jnp.sin(li * dt)
    den = lr * lr + li * li
    nr = ar - 1.0
    zr = (nr * lr + ai * li) / den
    zi = (ai * lr - nr * li) / den
    pr = jnp.ones_like(ar)
    pi = jnp.zeros_like(ar)
    for m in range(S5_CHUNK + 1):
        pow_ref[m, 0] = pr
        pow_ref[m, 1] = pi
        pr, pi = pr * ar - pi * ai, pr * ai + pi * ar
    b_re = b_re_ref[...]
    b_im = b_im_ref[...]
    bz_re_ref[...] = zr[:, None, :] * b_re - zi[:, None, :] * b_im
    bz_im_ref[...] = zr[:, None, :] * b_im + zi[:, None, :] * b_re


def _discretise(lam_re, lam_im, log_dt, b_re, b_im):
    dg = N_DIR * S5_GROUPS
    b_re_t = jnp.swapaxes(b_re, -1, -2).reshape(dg, S5_GROUP, S5_STATE)
    b_im_t = jnp.swapaxes(b_im, -1, -2).reshape(dg, S5_GROUP, S5_STATE)
    return pl.pallas_call(
        _discretise_kernel,
        out_shape=(jax.ShapeDtypeStruct((S5_CHUNK + 1, 2, dg, S5_STATE), _f32),
                   jax.ShapeDtypeStruct((dg, S5_GROUP, S5_STATE), _f32),
                   jax.ShapeDtypeStruct((dg, S5_GROUP, S5_STATE), _f32)),
        name="s5_discretise",
    )(lam_re.reshape(dg, S5_STATE), lam_im.reshape(dg, S5_STATE),
      log_dt.reshape(dg, 1), b_re_t, b_im_t)


def _compact_maps(w_re, w_im):
    def part(w):
        w = w.reshape(N_DIR, S5_SLABS, V7X_LANES, S5_STATE)
        return jnp.concatenate([w] * STATE_REPEATS, axis=-1)
    return jnp.stack([part(w_re), part(w_im)], axis=2)


def _w_in_pieces(j):
    seg, gseg = IN_SEG_COLS, 2 * IN_SEG_COLS
    pieces = [(4 * S5_WIDTH + n * D_MODEL + j * gseg, gseg) for n in range(2)]
    pieces += [(n * S5_WIDTH + j * seg, seg) for n in range(4)]
    return pieces


def _cast_w_in_kernel(w_ref, o_ref):
    dst = 0
    for j in range(IN_STEPS):
        for src, width in _w_in_pieces(j):
            o_ref[:, dst:dst + width] = w_ref[:, src:src + width].astype(_bf16)
            dst += width


def _cast_w_in(w):
    spec = pl.BlockSpec((CAST_TILE_ROWS, IN_COLS), lambda i: (i, 0))
    return pl.pallas_call(
        _cast_w_in_kernel,
        grid=(D_MODEL // CAST_TILE_ROWS,),
        in_specs=[spec],
        out_specs=spec,
        out_shape=jax.ShapeDtypeStruct(w.shape, _bf16),
        compiler_params=_params(("arbitrary",)),
        name="cast_w_in",
    )(w)


def _ffn_in_plan():
    th = FFN_TILE_HIDDEN
    plan = []
    for j in range(FFN_HIDDEN // th):
        plan.append((j * th, 2 * j * th, th))
        plan.append((FFN_HIDDEN + j * th, (2 * j + 1) * th, th))
    return plan


def _in_proj_kernel(x_ref, g_ref, w_ref, gbs_ref, gbc_ref, *rest, cast_plans):
    n_cast = len(cast_plans)
    cast_in = rest[:n_cast]
    u_ref, bc_ref, q_ref, gs_ref, gc_ref = rest[n_cast:n_cast + 5]
    cast_out = rest[n_cast + 5:2 * n_cast + 5]
    h_scr = rest[2 * n_cast + 5]
    seg = IN_SEG_COLS

    def column_step(h):
        proj = _dot(h, w_ref[...])
        gs_ref[...] = jax.nn.sigmoid(proj[:, 0:2 * seg] + gbs_ref[0]).astype(_bf16)
        gc_ref[...] = jax.nn.sigmoid(proj[:, 2 * seg:4 * seg] + gbc_ref[0]).astype(_bf16)
        for s in range(seg // V7X_LANES):
            u_ref[s] = proj[:, 4 * seg + s * V7X_LANES:4 * seg + (s + 1) * V7X_LANES].reshape(
                -1, S5_CHUNK, V7X_LANES)
        bc_ref[...] = proj[:, 6 * seg:7 * seg].astype(_bf16)
        q_ref[...] = (proj[:, 7 * seg:8 * seg] * proj[:, 5 * seg:6 * seg]).astype(_bf16)
        for src, dst, plan in zip(cast_in, cast_out, cast_plans):
            for s0, d0, width in plan:
                dst[:, d0:d0 + width] = src[:, s0:s0 + width].astype(_bf16)

    @pl.when(pl.program_id(2) == 0)
    def _():
        h = _rms_norm(x_ref[...], g_ref[...]).astype(_bf16)
        h_scr[...] = h
        column_step(h)

    @pl.when(pl.program_id(2) > 0)
    def _():
        column_step(h_scr[...])


def _in_proj(x3, g, w_in_b, gate_b, tile_rows, cast=()):
    nb, t_len, _ = x3.shape
    n_t = t_len // tile_rows
    n_steps = nb * n_t * IN_STEPS
    seg, gseg = IN_SEG_COLS, 2 * IN_SEG_COLS
    by_step = lambda b, t, j: (j, b, t, 0)
    step = lambda b, t, j: ((b * n_t + t) * IN_STEPS + j, 0)
    cast_w = [w for w, _ in cast]
    cast_plans = tuple(tuple(plan) if plan else ((0, 0, w.shape[1]),) for w, plan in cast)
    cast_specs = [pl.BlockSpec((w.shape[0] // n_steps, w.shape[1]), step) for w in cast_w]
    gb = lambda n: pl.BlockSpec((1, 1, gseg), lambda b, t, j: (n, 0, j))
    return pl.pallas_call(
        functools.partial(_in_proj_kernel, cast_plans=cast_plans),
        grid=(nb, n_t, IN_STEPS),
        in_specs=[
            pl.BlockSpec((None, tile_rows, D_MODEL), lambda b, t, j: (b, t, 0)),
            pl.BlockSpec((1, D_MODEL), lambda b, t, j: (0, 0)),
            pl.BlockSpec((D_MODEL, IN_COLS // IN_STEPS), lambda b, t, j: (0, j)),
            gb(0), gb(1),
        ] + cast_specs,
        out_specs=[
            pl.BlockSpec((seg // V7X_LANES, tile_rows // S5_CHUNK, None, S5_CHUNK, V7X_LANES),
                         lambda b, t, j: (j, t, b, 0, 0)),
            pl.BlockSpec((None, None, tile_rows, seg), by_step),
            pl.BlockSpec((None, None, tile_rows, seg), by_step),
            pl.BlockSpec((None, None, tile_rows, gseg), by_step),
            pl.BlockSpec((None, None, tile_rows, gseg), by_step),
        ] + cast_specs,
        out_shape=[jax.ShapeDtypeStruct(
                       (S5_SLABS, t_len // S5_CHUNK, nb, S5_CHUNK, V7X_LANES), _f32),
                   jax.ShapeDtypeStruct((IN_STEPS, nb, t_len, seg), _bf16),
                   jax.ShapeDtypeStruct((IN_STEPS, nb, t_len, seg), _bf16),
                   jax.ShapeDtypeStruct((IN_STEPS, nb, t_len, gseg), _bf16),
                   jax.ShapeDtypeStruct((IN_STEPS, nb, t_len, gseg), _bf16)]
                  + [jax.ShapeDtypeStruct(w.shape, _bf16) for w in cast_w],
        scratch_shapes=[pltpu.VMEM((tile_rows, D_MODEL), _bf16)],
        compiler_params=_params(("arbitrary", "arbitrary", "arbitrary")),
        name="in_proj",
    )(x3, g, w_in_b, gate_b, gate_b, *cast_w)


def _meta_proj_kernel(x_ref, g_ref, w_ref, u_ref, q_ref, h_scr):
    seg = IN_SEG_COLS

    @pl.when(pl.program_id(0) == 0)
    def _():
        h_scr[...] = _rms_norm(x_ref[...], g_ref[...]).astype(_bf16)

    proj = _dot(h_scr[...], w_ref[...])
    for s in range(seg // V7X_LANES):
        u_ref[s] = proj[:, s * V7X_LANES:(s + 1) * V7X_LANES].reshape(-1, S5_CHUNK, V7X_LANES)
    q_ref[...] = (proj[:, 3 * seg:4 * seg] * proj[:, seg:2 * seg]).astype(_bf16)


def _meta_proj(meta, g, w_in_b):
    seg = IN_SEG_COLS
    return pl.pallas_call(
        _meta_proj_kernel,
        grid=(IN_STEPS,),
        in_specs=[
            pl.BlockSpec((N_META, D_MODEL), lambda j: (0, 0)),
            pl.BlockSpec((1, D_MODEL), lambda j: (0, 0)),
            pl.BlockSpec((D_MODEL, 4 * seg), lambda j: (0, 2 * j + 1)),
        ],
        out_specs=[
            pl.BlockSpec((seg // V7X_LANES, META_CHUNKS, S5_CHUNK, V7X_LANES),
                         lambda j: (j, 0, 0, 0)),
            pl.BlockSpec((N_META, seg), lambda j: (0, j)),
        ],
        out_shape=[jax.ShapeDtypeStruct((S5_SLABS, META_CHUNKS, S5_CHUNK, V7X_LANES), _f32),
                   jax.ShapeDtypeStruct((N_META, CONV_WIDTH), _bf16)],
        scratch_shapes=[pltpu.VMEM((N_META, D_MODEL), _bf16)],
        compiler_params=_params(("arbitrary",)),
        name="meta_proj",
    )(meta, g, w_in_b)


def _pair_map(ref, d, part, q):
    rows = lax.broadcasted_iota(jnp.int32, (PAIR_LANES, V7X_LANES), 0)
    cols = lax.broadcasted_iota(jnp.int32, (PAIR_LANES, V7X_LANES), 1)
    same_group = (rows // S5_GROUP) == (cols // S5_STATE)
    return jnp.where(same_group, ref[d, 0, part, q * PAIR_LANES:(q + 1) * PAIR_LANES, :], 0.0)


def _transpose_pieces(src):
    assert SLAB_PAIRS == 4 and len(src) == SLAB_PAIRS
    lane = lax.broadcasted_iota(jnp.int32, src[0].shape, 1)

    def exchange(a, b, width):
        keep = (lane // width) % 2 == 0
        return (jnp.where(keep, a, pltpu.roll(b, width, 1)),
                jnp.where(keep, pltpu.roll(a, V7X_LANES - width, 1), b))

    b0, b2 = exchange(src[0], src[2], 2 * PAIR_LANES)
    b1, b3 = exchange(src[1], src[3], 2 * PAIR_LANES)
    c0, c1 = exchange(b0, b1, PAIR_LANES)
    c2, c3 = exchange(b2, b3, PAIR_LANES)
    return [c0, c1, c2, c3]


def _boundary_scan(x_scr, a_re, a_im, init):
    n_tiles = S5_CHUNK_ROWS // V7X_SUBLANES
    low = lax.broadcasted_iota(jnp.int32, (V7X_SUBLANES, V7X_LANES), 0) < BATCH

    def step(ar, ai, sr, si, xr, xi):
        return ar * sr - ai * si + xr, ar * si + ai * sr + xi

    def swap(v):
        return pltpu.roll(v, BATCH, 0)

    def body(i, carry):
        rf = pl.multiple_of(i * V7X_SUBLANES, V7X_SUBLANES)
        rb = pl.multiple_of((n_tiles - 1 - i) * V7X_SUBLANES, V7X_SUBLANES)
        loaded = []
        for q in range(SLAB_PAIRS):
            for part in range(2):
                lanes = slice(q * PAIR_COLS + part * PAIR_STATES,
                              q * PAIR_COLS + (part + 1) * PAIR_STATES)
                loaded.append((lanes, x_scr[0, pl.ds(rf, V7X_SUBLANES), lanes],
                               x_scr[1, pl.ds(rb, V7X_SUBLANES), lanes]))
        new, stores = [], []
        for q in range(SLAB_PAIRS):
            (re, xfr, xbr), (im, xfi, xbi) = loaded[2 * q], loaded[2 * q + 1]
            w1r, w1i = jnp.where(low, xfr, xbr), jnp.where(low, xfi, xbi)
            w2r, w2i = swap(jnp.where(low, xbr, xfr)), swap(jnp.where(low, xbi, xfi))
            cr, ci = carry[2 * q], carry[2 * q + 1]
            s1r, s1i = step(a_re[q], a_im[q], cr, ci, w1r, w1i)
            s2r, s2i = step(a_re[q], a_im[q], s1r, s1i, w2r, w2i)
            t1r, t1i = swap(s1r), swap(s1i)
            stores += [(0, rf, re, jnp.where(low, cr, t1r)), (0, rf, im, jnp.where(low, ci, t1i)),
                       (1, rb, re, jnp.where(low, t1r, cr)), (1, rb, im, jnp.where(low, t1i, ci))]
            new += [s2r, s2i]
        for d, r0, lanes, value in stores:
            x_scr[d, pl.ds(r0, V7X_SUBLANES), lanes] = value
        return tuple(new)

    lax.fori_loop(0, n_tiles, body, init)


def _s5_kernel(u_ref, um_ref, b_ref, c_ref, p_ref, d_ref, y_ref,
               ub_scr, uf_scr, w_scr, x_scr, sp_scr, tap_scr, toep_scr, xm_scr):
    half_steps = S5_CHUNK // 2
    blk = PAIR_LANES

    def pair_cols(q):
        return slice(q * PAIR_COLS, (q + 1) * PAIR_COLS)

    def pair_states(q):
        return slice(q * PAIR_STATES, (q + 1) * PAIR_STATES)

    def strided_u(j):
        return u_ref[0, pl.ds(j, S5_CHUNK_ROWS, stride=S5_CHUNK), :]

    um = um_ref[0]
    um_q = [[] for _ in range(SLAB_PAIRS)]
    for h in range(2):
        steps = range(h * half_steps, (h + 1) * half_steps)
        for q, t in enumerate(_transpose_pieces([strided_u(j) for j in steps])):
            uf_scr[q, :, h * V7X_LANES:(h + 1) * V7X_LANES] = t
            ub_scr[q, :, h * V7X_LANES:(h + 1) * V7X_LANES] = t.astype(_bf16)
        meta = [um[:, j * V7X_LANES:(j + 1) * V7X_LANES] for j in steps]
        for q, t in enumerate(_transpose_pieces(meta)):
            um_q[q].append(t)

    for d in range(N_DIR):
        for q in range(SLAB_PAIRS):
            w_re = _pair_map(b_ref, d, 0, q)
            w_im = _pair_map(b_ref, d, 1, q)
            for j in range(S5_CHUNK):
                m = (S5_CHUNK - 1 - j) if d == 0 else j
                pr = p_ref[0, d, m, 0:1, pair_states(q)]
                pi = p_ref[0, d, m, 1:2, pair_states(q)]
                rows = slice(j * blk, (j + 1) * blk)
                w_scr[d, q, rows, 0:PAIR_STATES] = (w_re * pr - w_im * pi).astype(_bf16)
                w_scr[d, q, rows, PAIR_STATES:] = (w_re * pi + w_im * pr).astype(_bf16)
            x_scr[d, :, pair_cols(q)] = _dot(ub_scr[q], w_scr[d, q])
            if d == 0:
                xm_scr[:, pair_cols(q)] = _dot(
                    jnp.concatenate(um_q[q], axis=1).astype(_bf16), w_scr[0, q])

    low = lax.broadcasted_iota(jnp.int32, (V7X_SUBLANES, V7X_LANES), 0) < BATCH
    a_re, a_im, init = [], [], []
    for q in range(SLAB_PAIRS):
        re = slice(q * PAIR_COLS, q * PAIR_COLS + PAIR_STATES)
        im = slice(q * PAIR_COLS + PAIR_STATES, (q + 1) * PAIR_COLS)
        ar = p_ref[0, 0, S5_CHUNK, 0:1, pair_states(q)]
        ai = p_ref[0, 0, S5_CHUNK, 1:2, pair_states(q)]
        a_re.append(jnp.where(low, ar, p_ref[0, 1, S5_CHUNK, 0:1, pair_states(q)]))
        a_im.append(jnp.where(low, ai, p_ref[0, 1, S5_CHUNK, 1:2, pair_states(q)]))
        sr = jnp.zeros((1, V7X_LANES), _f32)
        si = jnp.zeros((1, V7X_LANES), _f32)
        for c in range(META_CHUNKS):
            xr = xm_scr[c:c + 1, re]
            xi = xm_scr[c:c + 1, im]
            sr, si = ar * sr - ai * si + xr, ar * si + ai * sr + xi
        init.append(jnp.where(low, sr, 0.0))
        init.append(jnp.where(low, si, 0.0))
    _boundary_scan(x_scr, a_re, a_im, tuple(init))
    for d in range(N_DIR):
        sp_scr[d] = x_scr[d].astype(_bf16)

    def out_weights(c_re, c_imn, d, m, q):
        pr = p_ref[0, d, m, 0:1, pair_states(q)]
        pi = p_ref[0, d, m, 1:2, pair_states(q)]
        return jnp.concatenate([c_re * pr + c_imn * pi, c_imn * pr - c_re * pi],
                               axis=1).astype(_bf16)

    for q in range(SLAB_PAIRS):
        wy = []
        for d in range(N_DIR):
            c_re = _pair_map(c_ref, d, 0, q)
            c_imn = _pair_map(c_ref, d, 1, q)
            c_map = jnp.concatenate([c_re, c_imn], axis=1).astype(_bf16)
            tap_scr[d] = _dot_nt(w_scr[d, q], jnp.concatenate([c_map] * S5_CHUNK, axis=0))
            powers = [o + 1 if d == 0 else S5_CHUNK - o for o in range(S5_CHUNK)]
            wy.append(jnp.concatenate([out_weights(c_re, c_imn, d, m, q) for m in powers],
                                      axis=0))
        for o in range(S5_CHUNK):
            lanes = slice(o * blk, (o + 1) * blk)
            if o:
                toep_scr[0:o * blk, lanes] = tap_scr[0, (S5_CHUNK - 1 - o) * blk:
                                                     (S5_CHUNK - 1) * blk, lanes]
            toep_scr[o * blk:(o + 1) * blk, lanes] = (
                tap_scr[0, (S5_CHUNK - 1) * blk:, lanes] + tap_scr[1, 0:blk, lanes])
            if o < S5_CHUNK - 1:
                toep_scr[(o + 1) * blk:, lanes] = tap_scr[1, blk:(S5_CHUNK - o) * blk, lanes]
        y = _dot_nt(sp_scr[0, :, pair_cols(q)], wy[0])
        y += _dot_nt(sp_scr[1, :, pair_cols(q)], wy[1])
        y += _dot(ub_scr[q], toep_scr[...].astype(_bf16))
        y += uf_scr[q] * d_ref[0, q]
        for h in range(2):
            y_ref[0, q, h] = y[:, h * V7X_LANES:(h + 1) * V7X_LANES]


def _s5(u, u_meta, b_maps, c_maps, ptab, d_skip):
    maps = pl.BlockSpec((N_DIR, 1, 2, V7X_LANES, V7X_LANES), lambda s: (0, s, 0, 0, 0))
    return pl.pallas_call(
        _s5_kernel,
        grid=(S5_SLABS,),
        in_specs=[
            pl.BlockSpec((1, S5_CHUNK_ROWS * S5_CHUNK, V7X_LANES), lambda s: (s, 0, 0)),
            pl.BlockSpec((1, V7X_SUBLANES, S5_CHUNK_COLS), lambda s: (s, 0, 0)),
            maps,
            maps,
            pl.BlockSpec((1, N_DIR, S5_CHUNK + 1, 2, SLAB_STATES), lambda s: (s, 0, 0, 0, 0)),
            pl.BlockSpec((1, SLAB_PAIRS, 1, PAIR_COLS), lambda s: (s, 0, 0, 0)),
        ],
        out_specs=pl.BlockSpec((1, SLAB_PAIRS, 2, S5_CHUNK_ROWS, V7X_LANES),
                               lambda s: (s, 0, 0, 0, 0)),
        out_shape=jax.ShapeDtypeStruct((S5_SLABS, SLAB_PAIRS, 2, S5_CHUNK_ROWS, V7X_LANES),
                                       _f32),
        scratch_shapes=[
            pltpu.VMEM((SLAB_PAIRS, S5_CHUNK_ROWS, PAIR_COLS), _bf16),
            pltpu.VMEM((SLAB_PAIRS, S5_CHUNK_ROWS, PAIR_COLS), _f32),
            pltpu.VMEM((N_DIR, SLAB_PAIRS, PAIR_COLS, PAIR_COLS), _bf16),
            pltpu.VMEM((N_DIR, S5_CHUNK_ROWS, 2 * SLAB_STATES), _f32),
            pltpu.VMEM((N_DIR, S5_CHUNK_ROWS, 2 * SLAB_STATES), _bf16),
            pltpu.VMEM((N_DIR, PAIR_COLS, PAIR_COLS), _f32),
            pltpu.VMEM((PAIR_COLS, PAIR_COLS), _f32),
            pltpu.VMEM((V7X_SUBLANES, 2 * SLAB_STATES), _f32),
        ],
        compiler_params=_params(("arbitrary",)),
        name="s5",
    )(u, u_meta, b_maps, c_maps, ptab, d_skip)


def _mixer_kernel(h_ref, y_ref, q_ref, qp_ref, qn_ref, qm_ref, bc_ref, gs_ref, gc_ref,
                  wglu_ref, bglu_ref, wsup_ref, cw_ref, cb_ref, wcup_ref, wo_ref,
                  g_ref, wcast_ref, o_ref, wcast_out_ref, ys_scr, q_scr):
    t = pl.program_id(0)
    last = pl.num_programs(0) - 1
    tt = MIX_TILE_T
    tm = BATCH * tt
    n_c = tt // S5_CHUNK

    def columns(ref, b):
        return jnp.concatenate([ref[j, b] for j in range(IN_STEPS)], axis=1)

    def all_rows(ref):
        return jnp.concatenate([ref[j].reshape(tm, -1) for j in range(IN_STEPS)], axis=1)

    vs = []
    for b in range(BATCH):
        prev = jnp.where(t == 0, qm_ref[...], columns(qp_ref, b)).astype(_f32)
        nxt = jnp.where(t == last, 0.0, columns(qn_ref, b).astype(_f32))
        q_scr[b, 0:HALO_ROWS, :] = prev
        q_scr[b, HALO_ROWS:HALO_ROWS + tt, :] = columns(q_ref, b).astype(_f32)
        q_scr[b, HALO_ROWS + tt:, :] = nxt
        vs.append(cw_ref[0:1, :] * q_scr[b, HALO_ROWS - 1:HALO_ROWS - 1 + tt, :]
                  + cw_ref[1:2, :] * q_scr[b, HALO_ROWS:HALO_ROWS + tt, :]
                  + cw_ref[2:3, :] * q_scr[b, HALO_ROWS + 1:HALO_ROWS + 1 + tt, :]
                  + cb_ref[...])
    v = jnp.concatenate(vs, axis=0)
    bc = all_rows(bc_ref).astype(_f32)
    y_c = _dot((bc * v).astype(_bf16), wcup_ref[...])

    half_steps = S5_CHUNK // 2
    for s in range(S5_SLABS):
        for h in range(2):
            for b in range(BATCH):
                tiles = [y_ref[s, q, h, pl.ds(b, n_c, stride=BATCH), :]
                         for q in range(SLAB_PAIRS)]
                for k, tile in enumerate(_transpose_pieces(tiles)):
                    o = h * half_steps + k
                    ys_scr[s, pl.ds(b * tt + o, n_c, stride=S5_CHUNK), :] = tile
    ys = jnp.concatenate([ys_scr[s] for s in range(S5_SLABS)], axis=1)
    ys = jax.nn.gelu(ys)
    glu = _dot(ys.astype(_bf16), wglu_ref[...]) + bglu_ref[...]
    ys = ys * jax.nn.sigmoid(glu)
    y_s = _dot(ys.astype(_bf16), wsup_ref[...])

    merged = all_rows(gs_ref).astype(_f32) * y_s + all_rows(gc_ref).astype(_f32) * y_c
    mixed = _dot(merged.astype(_bf16), wo_ref[...])
    out = h_ref[...].reshape(tm, D_MODEL) + _rms_norm(mixed, g_ref[...])
    o_ref[...] = out.reshape(BATCH, tt, D_MODEL)
    wcast_out_ref[...] = wcast_ref[...].astype(_bf16)


def _mixer(x, y, q, q_meta, bc, gate_s, gate_c, w_glu, b_glu, w_s_up, conv_w, conv_b, w_c_up,
           w_o, g_post, w_cast):
    tt = MIX_TILE_T
    n_t = SEQ // tt
    cast_spec = pl.BlockSpec((w_cast.shape[0] // n_t, w_cast.shape[1]), lambda t: (t, 0))
    halo_per_tile = tt // HALO_ROWS
    n_halo = SEQ // HALO_ROWS
    tile = lambda t: (0, t, 0)
    stepped = lambda rows, cols: pl.BlockSpec((IN_STEPS, BATCH, rows, cols),
                                              lambda t: (0, 0, t, 0))
    resident = functools.partial(pl.BlockSpec, index_map=lambda t: (0, 0),
                                 pipeline_mode=pl.Buffered(1))
    return pl.pallas_call(
        _mixer_kernel,
        grid=(n_t,),
        in_specs=[
            pl.BlockSpec((BATCH, tt, D_MODEL), tile),
            pl.BlockSpec((S5_SLABS, SLAB_PAIRS, 2, tt // S5_CHUNK * BATCH, V7X_LANES),
                         lambda t: (0, 0, 0, t, 0)),
            stepped(tt, IN_SEG_COLS),
            pl.BlockSpec((IN_STEPS, BATCH, HALO_ROWS, IN_SEG_COLS),
                         lambda t: (0, 0, jnp.maximum(t * halo_per_tile - 1, 0), 0)),
            pl.BlockSpec((IN_STEPS, BATCH, HALO_ROWS, IN_SEG_COLS),
                         lambda t: (0, 0, jnp.minimum((t + 1) * halo_per_tile, n_halo - 1), 0)),
            resident((HALO_ROWS, CONV_WIDTH)),
            stepped(tt, IN_SEG_COLS),
            stepped(tt, 2 * IN_SEG_COLS),
            stepped(tt, 2 * IN_SEG_COLS),
            resident((S5_WIDTH, S5_WIDTH)),
            resident((1, S5_WIDTH)),
            resident((S5_WIDTH, D_MODEL)),
            resident((3, CONV_WIDTH)),
            resident((1, CONV_WIDTH)),
            resident((CONV_WIDTH, D_MODEL)),
            resident((D_MODEL, D_MODEL)),
            resident((1, D_MODEL)),
            cast_spec,
        ],
        out_specs=[pl.BlockSpec((BATCH, tt, D_MODEL), tile), cast_spec],
        out_shape=[jax.ShapeDtypeStruct((BATCH, SEQ, D_MODEL), _f32),
                   jax.ShapeDtypeStruct(w_cast.shape, _bf16)],
        scratch_shapes=[pltpu.VMEM((S5_SLABS, BATCH * tt, V7X_LANES), _f32),
                        pltpu.VMEM((BATCH, tt + 2 * HALO_ROWS, CONV_WIDTH), _f32)],
        compiler_params=_params(("arbitrary",)),
        name="mixer_tail",
    )(x, y, q, q, q, q_meta, bc, gate_s, gate_c, w_glu, b_glu, w_s_up, conv_w, conv_b, w_c_up,
      w_o, g_post, w_cast)


def _ffn_kernel(h_ref, gpre_ref, wgu_ref, wout_ref, gpost_ref, o_ref, hb_scr):
    j = pl.program_id(1)
    last = pl.num_programs(1) - 1

    def chunk(hb):
        gu = _dot(hb, wgu_ref[...])
        gate, up = gu[:, :FFN_TILE_HIDDEN], gu[:, FFN_TILE_HIDDEN:]
        return _dot((jax.nn.silu(gate) * up).astype(_bf16), wout_ref[...])

    @pl.when(j == 0)
    def _():
        hb = _rms_norm(h_ref[...], gpre_ref[...]).astype(_bf16)
        hb_scr[...] = hb
        o_ref[...] = chunk(hb)

    @pl.when(jnp.logical_and(j > 0, j < last))
    def _():
        o_ref[...] += chunk(hb_scr[...])

    @pl.when(j == last)
    def _():
        f = o_ref[...] + chunk(hb_scr[...])
        o_ref[...] = h_ref[...] + _rms_norm(f, gpost_ref[...])


def _ffn(h1, g_pre, w_in, w_out, g_post):
    tm, th = FFN_TILE_ROWS, FFN_TILE_HIDDEN
    n_hidden = FFN_HIDDEN // th
    return pl.pallas_call(
        _ffn_kernel,
        grid=(ROWS // tm, n_hidden),
        in_specs=[
            pl.BlockSpec((tm, D_MODEL), lambda i, j: (i, 0)),
            pl.BlockSpec((1, D_MODEL), lambda i, j: (0, 0)),
            pl.BlockSpec((D_MODEL, 2 * th), lambda i, j: (0, j)),
            pl.BlockSpec((th, D_MODEL), lambda i, j: (j, 0)),
            pl.BlockSpec((1, D_MODEL), lambda i, j: (0, 0)),
        ],
        out_specs=pl.BlockSpec((tm, D_MODEL), lambda i, j: (i, 0)),
        out_shape=jax.ShapeDtypeStruct((ROWS, D_MODEL), _f32),
        scratch_shapes=[pltpu.VMEM((tm, D_MODEL), _bf16)],
        compiler_params=_params(("arbitrary", "arbitrary")),
        name="ffn",
    )(h1, g_pre, w_in, w_out, g_post)


def kernel(x, meta, g_mix_pre, g_mix_post, g_ffn_pre, g_ffn_post, w_in, gate_b, lam_re, lam_im,
           log_dt, b_re, b_im, c_re, c_im, d_skip, w_glu, b_glu, w_s_up, conv_w, conv_b, w_c_up,
           w_o, w_ffn_in, w_ffn_out):
    l = 0
    w_in_b = _cast_w_in(w_in[l])
    gate_b2 = gate_b[l].reshape(2, 1, D_MODEL)
    g_pre = g_mix_pre[l].reshape(1, D_MODEL)

    ptab, bz_re, bz_im = _discretise(lam_re[l], lam_im[l], log_dt[l], b_re[l], b_im[l])
    dg = N_DIR * S5_GROUPS
    b_maps = _compact_maps(bz_re, bz_im)
    c_maps = _compact_maps(c_re[l].reshape(dg, S5_GROUP, S5_STATE),
                           -c_im[l].reshape(dg, S5_GROUP, S5_STATE))
    ptab = ptab.reshape(S5_CHUNK + 1, 2, N_DIR, S5_SLABS, SLAB_STATES)
    ptab = ptab.transpose(3, 2, 0, 1, 4)

    (u5, bc, q, gate_s, gate_c, w_ffn_in_b, w_glu_b, w_s_up_b, w_c_up_b, w_o_b) = _in_proj(
        x, g_pre, w_in_b, gate_b2, IN_TILE_ROWS,
        cast=((w_ffn_in[l], _ffn_in_plan()), (w_glu[l], None), (w_s_up[l], None),
              (w_c_up[l], None), (w_o[l], None)))
    um5, q_meta = _meta_proj(meta.astype(x.dtype), g_pre, w_in_b)

    u = u5.reshape(S5_SLABS, S5_CHUNK_ROWS * S5_CHUNK, V7X_LANES)
    u_meta = jnp.pad(um5.reshape(S5_SLABS, META_CHUNKS, S5_CHUNK_COLS),
                     ((0, 0), (0, V7X_SUBLANES - META_CHUNKS), (0, 0)))
    d_pairs = jnp.tile(d_skip[l].reshape(S5_SLABS, SLAB_PAIRS, 1, PAIR_LANES),
                       (1, 1, 1, S5_CHUNK))
    y = _s5(u, u_meta, b_maps, c_maps, ptab, d_pairs)

    h1, w_ffn_out_b = _mixer(x, y, q, q_meta, bc, gate_s, gate_c, w_glu_b,
                             b_glu[l].reshape(1, S5_WIDTH), w_s_up_b, conv_w[l],
                             conv_b[l].reshape(1, CONV_WIDTH), w_c_up_b, w_o_b,
                             g_mix_post[l].reshape(1, D_MODEL), w_ffn_out[l])

    h2 = _ffn(h1.reshape(ROWS, D_MODEL), g_ffn_pre[l].reshape(1, D_MODEL),
              w_ffn_in_b, w_ffn_out_b, g_ffn_post[l].reshape(1, D_MODEL))
    return h2.reshape(BATCH, SEQ, D_MODEL)
```

```python
import functools
import math

import jax
import jax.numpy as jnp
from jax import lax
from jax.experimental import pallas as pl
from jax.experimental.pallas import tpu as pltpu

D_MODEL = 2048
BATCH = 4
SEQ = 2048
N_META = 16
S5_WIDTH = 1024
S5_GROUP = 16
S5_GROUPS = S5_WIDTH // S5_GROUP
S5_STATE = 64
N_DIR = 2
CONV_WIDTH = 1024
FFN_HIDDEN = ((math.ceil(8 * D_MODEL / 3) + 255) // 256) * 256
IN_COLS = S5_WIDTH + 3 * CONV_WIDTH + 2 * D_MODEL
RMS_EPS = 1e-6
LAM_RE_MAX = -1e-4

ROWS = SEQ * BATCH

V7X_SUBLANES = 8
V7X_LANES = 128
V7X_MXU_DIM = 256
V7X_VMEM_LIMIT_BYTES = 56 * 1024 * 1024

S5_SLABS = S5_WIDTH // V7X_LANES
SLAB_GROUPS = V7X_LANES // S5_GROUP
SLAB_STATES = SLAB_GROUPS * S5_STATE
PAIR_LANES = 2 * S5_GROUP
SLAB_PAIRS = V7X_LANES // PAIR_LANES
PAIR_STATES = 2 * S5_STATE
S5_CHUNK = V7X_SUBLANES
PAIR_COLS = S5_CHUNK * PAIR_LANES
S5_CHUNKS = SEQ // S5_CHUNK
S5_CHUNK_ROWS = S5_CHUNKS * BATCH
S5_CHUNK_COLS = S5_CHUNK * V7X_LANES
META_CHUNKS = N_META // S5_CHUNK
STATE_REPEATS = V7X_LANES // S5_STATE

IN_TILE_ROWS = 1024
IN_STEPS = 8
IN_SEG_COLS = S5_WIDTH // IN_STEPS
CAST_TILE_ROWS = 128
MIX_TILE_T = 64
FFN_TILE_ROWS = 1024
FFN_TILE_HIDDEN = 256
HALO_ROWS = 16

_f32 = jnp.float32
_bf16 = jnp.bfloat16


def _rms_norm(xf, g):
    r = lax.rsqrt(jnp.mean(xf * xf, axis=-1, keepdims=True) + RMS_EPS)
    return xf * r * g


def _params(sem):
    return pltpu.CompilerParams(dimension_semantics=sem,
                                vmem_limit_bytes=V7X_VMEM_LIMIT_BYTES)


def _dot(a, b):
    return jnp.dot(a, b, preferred_element_type=_f32)


def _dot_nt(a, b):
    return lax.dot_general(a, b, (((1,), (1,)), ((), ())), preferred_element_type=_f32)


def _discretise_kernel(lam_re_ref, lam_im_ref, log_dt_ref, b_re_ref, b_im_ref,
                       pow_ref, bz_re_ref, bz_im_ref):
    lr = jnp.minimum(lam_re_ref[...], LAM_RE_MAX)
    li = lam_im_ref[...]
    dt = jnp.exp(log_dt_ref[...])
    mag = jnp.exp(lr * dt)
    ar = mag * jnp.cos(li * dt)
    ai = mag * jnp.sin(li * dt)
    den = lr * lr + li * li
    nr = ar - 1.0
    zr = (nr * lr + ai * li) / den
    zi = (ai * lr - nr * li) / den
    pr = jnp.ones_like(ar)
    pi = jnp.zeros_like(ar)
    for m in range(S5_CHUNK + 1):
        pow_ref[m, 0] = pr
        pow_ref[m, 1] = pi
        pr, pi = pr * ar - pi * ai, pr * ai + pi * ar
    b_re = b_re_ref[...]
    b_im = b_im_ref[...]
    bz_re_ref[...] = zr[:, None, :] * b_re - zi[:, None, :] * b_im
    bz_im_ref[...] = zr[:, None, :] * b_im + zi[:, None, :] * b_re


def _discretise(lam_re, lam_im, log_dt, b_re, b_im):
    dg = N_DIR * S5_GROUPS
    b_re_t = jnp.swapaxes(b_re, -1, -2).reshape(dg, S5_GROUP, S5_STATE)
    b_im_t = jnp.swapaxes(b_im, -1, -2).reshape(dg, S5_GROUP, S5_STATE)
    return pl.pallas_call(
        _discretise_kernel,
        out_shape=(jax.ShapeDtypeStruct((S5_CHUNK + 1, 2, dg, S5_STATE), _f32),
                   jax.ShapeDtypeStruct((dg, S5_GROUP, S5_STATE), _f32),
                   jax.ShapeDtypeStruct((dg, S5_GROUP, S5_STATE), _f32)),
        name="s5_discretise",
    )(lam_re.reshape(dg, S5_STATE), lam_im.reshape(dg, S5_STATE),
      log_dt.reshape(dg, 1), b_re_t, b_im_t)


def _compact_maps(w_re, w_im):
    def part(w):
        w = w.reshape(N_DIR, S5_SLABS, V7X_LANES, S5_STATE)
        return jnp.concatenate([w] * STATE_REPEATS, axis=-1)
    return jnp.stack([part(w_re), part(w_im)], axis=2)


def _w_in_pieces(j):
    seg, gseg = IN_SEG_COLS, 2 * IN_SEG_COLS
    pieces = [(4 * S5_WIDTH + n * D_MODEL + j * gseg, gseg) for n in range(2)]
    pieces += [(n * S5_WIDTH + j * seg, seg) for n in range(4)]
    return pieces


def _cast_w_in_kernel(w_ref, o_ref):
    dst = 0
    for j in range(IN_STEPS):
        for src, width in _w_in_pieces(j):
            o_ref[:, dst:dst + width] = w_ref[:, src:src + width].astype(_bf16)
            dst += width


def _cast_w_in(w):
    spec = pl.BlockSpec((CAST_TILE_ROWS, IN_COLS), lambda i: (i, 0))
    return pl.pallas_call(
        _cast_w_in_kernel,
        grid=(D_MODEL // CAST_TILE_ROWS,),
        in_specs=[spec],
        out_specs=spec,
        out_shape=jax.ShapeDtypeStruct(w.shape, _bf16),
        compiler_params=_params(("arbitrary",)),
        name="cast_w_in",
    )(w)


def _ffn_in_plan():
    th = FFN_TILE_HIDDEN
    plan = []
    for j in range(FFN_HIDDEN // th):
        plan.append((j * th, 2 * j * th, th))
        plan.append((FFN_HIDDEN + j * th, (2 * j + 1) * th, th))
    return plan


def _in_proj_kernel(x_ref, g_ref, w_ref, gbs_ref, gbc_ref, *rest, cast_plans):
    n_cast = len(cast_plans)
    cast_in = rest[:n_cast]
    u_ref, bc_ref, q_ref, gs_ref, gc_ref = rest[n_cast:n_cast + 5]
    cast_out = rest[n_cast + 5:2 * n_cast + 5]
    h_scr = rest[2 * n_cast + 5]
    seg = IN_SEG_COLS

    def column_step(h):
        proj = _dot(h, w_ref[...])
        gs_ref[...] = jax.nn.sigmoid(proj[:, 0:2 * seg] + gbs_ref[0]).astype(_bf16)
        gc_ref[...] = jax.nn.sigmoid(proj[:, 2 * seg:4 * seg] + gbc_ref[0]).astype(_bf16)
        for s in range(seg // V7X_LANES):
            u_ref[s] = proj[:, 4 * seg + s * V7X_LANES:4 * seg + (s + 1) * V7X_LANES].reshape(
                -1, S5_CHUNK, V7X_LANES)
        bc_ref[...] = proj[:, 6 * seg:7 * seg].astype(_bf16)
        q_ref[...] = (proj[:, 7 * seg:8 * seg] * proj[:, 5 * seg:6 * seg]).astype(_bf16)
        for src, dst, plan in zip(cast_in, cast_out, cast_plans):
            for s0, d0, width in plan:
                dst[:, d0:d0 + width] = src[:, s0:s0 + width].astype(_bf16)

    @pl.when(pl.program_id(2) == 0)
    def _():
        h = _rms_norm(x_ref[...], g_ref[...]).astype(_bf16)
        h_scr[...] = h
        column_step(h)

    @pl.when(pl.program_id(2) > 0)
    def _():
        column_step(h_scr[...])


def _in_proj(x3, g, w_in_b, gate_b, tile_rows, cast=()):
    nb, t_len, _ = x3.shape
    n_t = t_len // tile_rows
    n_steps = nb * n_t * IN_STEPS
    seg, gseg = IN_SEG_COLS, 2 * IN_SEG_COLS
    by_step = lambda b, t, j: (j, b, t, 0)
    step = lambda b, t, j: ((b * n_t + t) * IN_STEPS + j, 0)
    cast_w = [w for w, _ in cast]
    cast_plans = tuple(tuple(plan) if plan else ((0, 0, w.shape[1]),) for w, plan in cast)
    cast_specs = [pl.BlockSpec((w.shape[0] // n_steps, w.shape[1]), step) for w in cast_w]
    gb = lambda n: pl.BlockSpec((1, 1, gseg), lambda b, t, j: (n, 0, j))
    return pl.pallas_call(
        functools.partial(_in_proj_kernel, cast_plans=cast_plans),
        grid=(nb, n_t, IN_STEPS),
        in_specs=[
            pl.BlockSpec((None, tile_rows, D_MODEL), lambda b, t, j: (b, t, 0)),
            pl.BlockSpec((1, D_MODEL), lambda b, t, j: (0, 0)),
            pl.BlockSpec((D_MODEL, IN_COLS // IN_STEPS), lambda b, t, j: (0, j)),
            gb(0), gb(1),
        ] + cast_specs,
        out_specs=[
            pl.BlockSpec((seg // V7X_LANES, tile_rows // S5_CHUNK, None, S5_CHUNK, V7X_LANES),
                         lambda b, t, j: (j, t, b, 0, 0)),
            pl.BlockSpec((None, None, tile_rows, seg), by_step),
            pl.BlockSpec((None, None, tile_rows, seg), by_step),
            pl.BlockSpec((None, None, tile_rows, gseg), by_step),
            pl.BlockSpec((None, None, tile_rows, gseg), by_step),
        ] + cast_specs,
        out_shape=[jax.ShapeDtypeStruct(
                       (S5_SLABS, t_len // S5_CHUNK, nb, S5_CHUNK, V7X_LANES), _f32),
                   jax.ShapeDtypeStruct((IN_STEPS, nb, t_len, seg), _bf16),
                   jax.ShapeDtypeStruct((IN_STEPS, nb, t_len, seg), _bf16),
                   jax.ShapeDtypeStruct((IN_STEPS, nb, t_len, gseg), _bf16),
                   jax.ShapeDtypeStruct((IN_STEPS, nb, t_len, gseg), _bf16)]
                  + [jax.ShapeDtypeStruct(w.shape, _bf16) for w in cast_w],
        scratch_shapes=[pltpu.VMEM((tile_rows, D_MODEL), _bf16)],
        compiler_params=_params(("arbitrary", "arbitrary", "arbitrary")),
        name="in_proj",
    )(x3, g, w_in_b, gate_b, gate_b, *cast_w)


def _meta_proj_kernel(x_ref, g_ref, w_ref, u_ref, q_ref, h_scr):
    seg = IN_SEG_COLS

    @pl.when(pl.program_id(0) == 0)
    def _():
        h_scr[...] = _rms_norm(x_ref[...], g_ref[...]).astype(_bf16)

    proj = _dot(h_scr[...], w_ref[...])
    for s in range(seg // V7X_LANES):
        u_ref[s] = proj[:, s * V7X_LANES:(s + 1) * V7X_LANES].reshape(-1, S5_CHUNK, V7X_LANES)
    q_ref[...] = (proj[:, 3 * seg:4 * seg] * proj[:, seg:2 * seg]).astype(_bf16)


def _meta_proj(meta, g, w_in_b):
    seg = IN_SEG_COLS
    return pl.pallas_call(
        _meta_proj_kernel,
        grid=(IN_STEPS,),
        in_specs=[
            pl.BlockSpec((N_META, D_MODEL), lambda j: (0, 0)),
            pl.BlockSpec((1, D_MODEL), lambda j: (0, 0)),
            pl.BlockSpec((D_MODEL, 4 * seg), lambda j: (0, 2 * j + 1)),
        ],
        out_specs=[
            pl.BlockSpec((seg // V7X_LANES, META_CHUNKS, S5_CHUNK, V7X_LANES),
                         lambda j: (j, 0, 0, 0)),
            pl.BlockSpec((N_META, seg), lambda j: (0, j)),
        ],
        out_shape=[jax.ShapeDtypeStruct((S5_SLABS, META_CHUNKS, S5_CHUNK, V7X_LANES), _f32),
                   jax.ShapeDtypeStruct((N_META, CONV_WIDTH), _bf16)],
        scratch_shapes=[pltpu.VMEM((N_META, D_MODEL), _bf16)],
        compiler_params=_params(("arbitrary",)),
        name="meta_proj",
    )(meta, g, w_in_b)


def _pair_map(ref, d, part, q):
    rows = lax.broadcasted_iota(jnp.int32, (PAIR_LANES, V7X_LANES), 0)
    cols = lax.broadcasted_iota(jnp.int32, (PAIR_LANES, V7X_LANES), 1)
    same_group = (rows // S5_GROUP) == (cols // S5_STATE)
    return jnp.where(same_group, ref[d, 0, part, q * PAIR_LANES:(q + 1) * PAIR_LANES, :], 0.0)


def _transpose_pieces(src):
    assert SLAB_PAIRS == 4 and len(src) == SLAB_PAIRS
    lane = lax.broadcasted_iota(jnp.int32, src[0].shape, 1)

    def exchange(a, b, width):
        keep = (lane // width) % 2 == 0
        return (jnp.where(keep, a, pltpu.roll(b, width, 1)),
                jnp.where(keep, pltpu.roll(a, V7X_LANES - width, 1), b))

    b0, b2 = exchange(src[0], src[2], 2 * PAIR_LANES)
    b1, b3 = exchange(src[1], src[3], 2 * PAIR_LANES)
    c0, c1 = exchange(b0, b1, PAIR_LANES)
    c2, c3 = exchange(b2, b3, PAIR_LANES)
    return [c0, c1, c2, c3]


def _boundary_scan(x_scr, a_re, a_im, init):
    n_tiles = S5_CHUNK_ROWS // V7X_SUBLANES
    low = lax.broadcasted_iota(jnp.int32, (V7X_SUBLANES, V7X_LANES), 0) < BATCH

    def step(ar, ai, sr, si, xr, xi):
        return ar * sr - ai * si + xr, ar * si + ai * sr + xi

    def swap(v):
        return pltpu.roll(v, BATCH, 0)

    def body(i, carry):
        rf = pl.multiple_of(i * V7X_SUBLANES, V7X_SUBLANES)
        rb = pl.multiple_of((n_tiles - 1 - i) * V7X_SUBLANES, V7X_SUBLANES)
        loaded = []
        for q in range(SLAB_PAIRS):
            for part in range(2):
                lanes = slice(q * PAIR_COLS + part * PAIR_STATES,
                              q * PAIR_COLS + (part + 1) * PAIR_STATES)
                loaded.append((lanes, x_scr[0, pl.ds(rf, V7X_SUBLANES), lanes],
                               x_scr[1, pl.ds(rb, V7X_SUBLANES), lanes]))
        new, stores = [], []
        for q in range(SLAB_PAIRS):
            (re, xfr, xbr), (im, xfi, xbi) = loaded[2 * q], loaded[2 * q + 1]
            w1r, w1i = jnp.where(low, xfr, xbr), jnp.where(low, xfi, xbi)
            w2r, w2i = swap(jnp.where(low, xbr, xfr)), swap(jnp.where(low, xbi, xfi))
            cr, ci = carry[2 * q], carry[2 * q + 1]
            s1r, s1i = step(a_re[q], a_im[q], cr, ci, w1r, w1i)
            s2r, s2i = step(a_re[q], a_im[q], s1r, s1i, w2r, w2i)
            t1r, t1i = swap(s1r), swap(s1i)
            stores += [(0, rf, re, jnp.where(low, cr, t1r)), (0, rf, im, jnp.where(low, ci, t1i)),
                       (1, rb, re, jnp.where(low, t1r, cr)), (1, rb, im, jnp.where(low, t1i, ci))]
            new += [s2r, s2i]
        for d, r0, lanes, value in stores:
            x_scr[d, pl.ds(r0, V7X_SUBLANES), lanes] = value
        return tuple(new)

    lax.fori_loop(0, n_tiles, body, init)


def _s5_kernel(u_ref, um_ref, b_ref, c_ref, p_ref, d_ref, *rest, n_cast):
    cast_in = rest[:n_cast]
    y_ref = rest[n_cast]
    cast_out = rest[n_cast + 1:2 * n_cast + 1]
    ub_scr, uf_scr, w_scr, x_scr, sp_scr, tap_scr, toep_scr, xm_scr = rest[2 * n_cast + 1:]
    for src, dst in zip(cast_in, cast_out):
        dst[...] = src[...].astype(_bf16)

    half_steps = S5_CHUNK // 2
    blk = PAIR_LANES

    def pair_cols(q):
        return slice(q * PAIR_COLS, (q + 1) * PAIR_COLS)

    def pair_states(q):
        return slice(q * PAIR_STATES, (q + 1) * PAIR_STATES)

    def strided_u(j):
        return u_ref[0, pl.ds(j, S5_CHUNK_ROWS, stride=S5_CHUNK), :]

    um = um_ref[0]
    um_q = [[] for _ in range(SLAB_PAIRS)]
    for h in range(2):
        steps = range(h * half_steps, (h + 1) * half_steps)
        for q, t in enumerate(_transpose_pieces([strided_u(j) for j in steps])):
            uf_scr[q, :, h * V7X_LANES:(h + 1) * V7X_LANES] = t
            ub_scr[q, :, h * V7X_LANES:(h + 1) * V7X_LANES] = t.astype(_bf16)
        meta = [um[:, j * V7X_LANES:(j + 1) * V7X_LANES] for j in steps]
        for q, t in enumerate(_transpose_pieces(meta)):
            um_q[q].append(t)

    for d in range(N_DIR):
        for q in range(SLAB_PAIRS):
            w_re = _pair_map(b_ref, d, 0, q)
            w_im = _pair_map(b_ref, d, 1, q)
            for j in range(S5_CHUNK):
                m = (S5_CHUNK - 1 - j) if d == 0 else j
                pr = p_ref[0, d, m, 0:1, pair_states(q)]
                pi = p_ref[0, d, m, 1:2, pair_states(q)]
                rows = slice(j * blk, (j + 1) * blk)
                w_scr[d, q, rows, 0:PAIR_STATES] = (w_re * pr - w_im * pi).astype(_bf16)
                w_scr[d, q, rows, PAIR_STATES:] = (w_re * pi + w_im * pr).astype(_bf16)
            x_scr[d, :, pair_cols(q)] = _dot(ub_scr[q], w_scr[d, q])
            if d == 0:
                xm_scr[:, pair_cols(q)] = _dot(
                    jnp.concatenate(um_q[q], axis=1).astype(_bf16), w_scr[0, q])

    low = lax.broadcasted_iota(jnp.int32, (V7X_SUBLANES, V7X_LANES), 0) < BATCH
    a_re, a_im, init = [], [], []
    for q in range(SLAB_PAIRS):
        re = slice(q * PAIR_COLS, q * PAIR_COLS + PAIR_STATES)
        im = slice(q * PAIR_COLS + PAIR_STATES, (q + 1) * PAIR_COLS)
        ar = p_ref[0, 0, S5_CHUNK, 0:1, pair_states(q)]
        ai = p_ref[0, 0, S5_CHUNK, 1:2, pair_states(q)]
        a_re.append(jnp.where(low, ar, p_ref[0, 1, S5_CHUNK, 0:1, pair_states(q)]))
        a_im.append(jnp.where(low, ai, p_ref[0, 1, S5_CHUNK, 1:2, pair_states(q)]))
        sr = jnp.zeros((1, V7X_LANES), _f32)
        si = jnp.zeros((1, V7X_LANES), _f32)
        for c in range(META_CHUNKS):
            xr = xm_scr[c:c + 1, re]
            xi = xm_scr[c:c + 1, im]
            sr, si = ar * sr - ai * si + xr, ar * si + ai * sr + xi
        init.append(jnp.where(low, sr, 0.0))
        init.append(jnp.where(low, si, 0.0))
    _boundary_scan(x_scr, a_re, a_im, tuple(init))
    for d in range(N_DIR):
        sp_scr[d] = x_scr[d].astype(_bf16)

    def out_weights(c_re, c_imn, d, m, q):
        pr = p_ref[0, d, m, 0:1, pair_states(q)]
        pi = p_ref[0, d, m, 1:2, pair_states(q)]
        return jnp.concatenate([c_re * pr + c_imn * pi, c_imn * pr - c_re * pi],
                               axis=1).astype(_bf16)

    for q in range(SLAB_PAIRS):
        wy = []
        for d in range(N_DIR):
            c_re = _pair_map(c_ref, d, 0, q)
            c_imn = _pair_map(c_ref, d, 1, q)
            c_map = jnp.concatenate([c_re, c_imn], axis=1).astype(_bf16)
            tap_scr[d] = _dot_nt(w_scr[d, q], jnp.concatenate([c_map] * S5_CHUNK, axis=0))
            powers = [o + 1 if d == 0 else S5_CHUNK - o for o in range(S5_CHUNK)]
            wy.append(jnp.concatenate([out_weights(c_re, c_imn, d, m, q) for m in powers],
                                      axis=0))
        for o in range(S5_CHUNK):
            lanes = slice(o * blk, (o + 1) * blk)
            if o:
                toep_scr[0:o * blk, lanes] = tap_scr[0, (S5_CHUNK - 1 - o) * blk:
                                                     (S5_CHUNK - 1) * blk, lanes]
            toep_scr[o * blk:(o + 1) * blk, lanes] = (
                tap_scr[0, (S5_CHUNK - 1) * blk:, lanes] + tap_scr[1, 0:blk, lanes])
            if o < S5_CHUNK - 1:
                toep_scr[(o + 1) * blk:, lanes] = tap_scr[1, blk:(S5_CHUNK - o) * blk, lanes]
        y = _dot_nt(sp_scr[0, :, pair_cols(q)], wy[0])
        y += _dot_nt(sp_scr[1, :, pair_cols(q)], wy[1])
        y += _dot(ub_scr[q], toep_scr[...].astype(_bf16))
        y += uf_scr[q] * d_ref[0, q]
        for h in range(2):
            y_ref[0, q, h] = y[:, h * V7X_LANES:(h + 1) * V7X_LANES]


def _s5(u, u_meta, b_maps, c_maps, ptab, d_skip, cast=()):
    maps = pl.BlockSpec((N_DIR, 1, 2, V7X_LANES, V7X_LANES), lambda s: (0, s, 0, 0, 0))
    cast_specs = [pl.BlockSpec((w.shape[0] // S5_SLABS, w.shape[1]), lambda s: (s, 0))
                  for w in cast]
    return pl.pallas_call(
        functools.partial(_s5_kernel, n_cast=len(cast)),
        grid=(S5_SLABS,),
        in_specs=[
            pl.BlockSpec((1, S5_CHUNK_ROWS * S5_CHUNK, V7X_LANES), lambda s: (s, 0, 0)),
            pl.BlockSpec((1, V7X_SUBLANES, S5_CHUNK_COLS), lambda s: (s, 0, 0)),
            maps,
            maps,
            pl.BlockSpec((1, N_DIR, S5_CHUNK + 1, 2, SLAB_STATES), lambda s: (s, 0, 0, 0, 0)),
            pl.BlockSpec((1, SLAB_PAIRS, 1, PAIR_COLS), lambda s: (s, 0, 0, 0)),
        ] + cast_specs,
        out_specs=[pl.BlockSpec((1, SLAB_PAIRS, 2, S5_CHUNK_ROWS, V7X_LANES),
                                lambda s: (s, 0, 0, 0, 0))] + cast_specs,
        out_shape=[jax.ShapeDtypeStruct((S5_SLABS, SLAB_PAIRS, 2, S5_CHUNK_ROWS, V7X_LANES),
                                        _f32)]
                  + [jax.ShapeDtypeStruct(w.shape, _bf16) for w in cast],
        scratch_shapes=[
            pltpu.VMEM((SLAB_PAIRS, S5_CHUNK_ROWS, PAIR_COLS), _bf16),
            pltpu.VMEM((SLAB_PAIRS, S5_CHUNK_ROWS, PAIR_COLS), _f32),
            pltpu.VMEM((N_DIR, SLAB_PAIRS, PAIR_COLS, PAIR_COLS), _bf16),
            pltpu.VMEM((N_DIR, S5_CHUNK_ROWS, 2 * SLAB_STATES), _f32),
            pltpu.VMEM((N_DIR, S5_CHUNK_ROWS, 2 * SLAB_STATES), _bf16),
            pltpu.VMEM((N_DIR, PAIR_COLS, PAIR_COLS), _f32),
            pltpu.VMEM((PAIR_COLS, PAIR_COLS), _f32),
            pltpu.VMEM((V7X_SUBLANES, 2 * SLAB_STATES), _f32),
        ],
        compiler_params=_params(("arbitrary",)),
        name="s5",
    )(u, u_meta, b_maps, c_maps, ptab, d_skip, *cast)


def _mixer_kernel(h_ref, y_ref, q_ref, qp_ref, qn_ref, qm_ref, bc_ref, gs_ref, gc_ref,
                  wglu_ref, bglu_ref, wsup_ref, cw_ref, cb_ref, wcup_ref, wo_ref,
                  g_ref, wcast_ref, o_ref, wcast_out_ref, ys_scr, q_scr):
    t = pl.program_id(0)
    last = pl.num_programs(0) - 1
    tt = MIX_TILE_T
    tm = BATCH * tt
    n_c = tt // S5_CHUNK

    def columns(ref, b):
        return jnp.concatenate([ref[j, b] for j in range(IN_STEPS)], axis=1)

    def all_rows(ref):
        return jnp.concatenate([ref[j].reshape(tm, -1) for j in range(IN_STEPS)], axis=1)

    vs = []
    for b in range(BATCH):
        prev = jnp.where(t == 0, qm_ref[...], columns(qp_ref, b)).astype(_f32)
        nxt = jnp.where(t == last, 0.0, columns(qn_ref, b).astype(_f32))
        q_scr[b, 0:HALO_ROWS, :] = prev
        q_scr[b, HALO_ROWS:HALO_ROWS + tt, :] = columns(q_ref, b).astype(_f32)
        q_scr[b, HALO_ROWS + tt:, :] = nxt
        vs.append(cw_ref[0:1, :] * q_scr[b, HALO_ROWS - 1:HALO_ROWS - 1 + tt, :]
                  + cw_ref[1:2, :] * q_scr[b, HALO_ROWS:HALO_ROWS + tt, :]
                  + cw_ref[2:3, :] * q_scr[b, HALO_ROWS + 1:HALO_ROWS + 1 + tt, :]
                  + cb_ref[...])
    v = jnp.concatenate(vs, axis=0)
    bc = all_rows(bc_ref).astype(_f32)
    y_c = _dot((bc * v).astype(_bf16), wcup_ref[...])

    half_steps = S5_CHUNK // 2
    for s in range(S5_SLABS):
        for h in range(2):
            for b in range(BATCH):
                tiles = [y_ref[s, q, h, pl.ds(b, n_c, stride=BATCH), :]
                         for q in range(SLAB_PAIRS)]
                for k, tile in enumerate(_transpose_pieces(tiles)):
                    o = h * half_steps + k
                    ys_scr[s, pl.ds(b * tt + o, n_c, stride=S5_CHUNK), :] = tile
    ys = jnp.concatenate([ys_scr[s] for s in range(S5_SLABS)], axis=1)
    ys = jax.nn.gelu(ys)
    glu = _dot(ys.astype(_bf16), wglu_ref[...]) + bglu_ref[...]
    ys = ys * jax.nn.sigmoid(glu)
    y_s = _dot(ys.astype(_bf16), wsup_ref[...])

    merged = all_rows(gs_ref).astype(_f32) * y_s + all_rows(gc_ref).astype(_f32) * y_c
    mixed = _dot(merged.astype(_bf16), wo_ref[...])
    out = h_ref[...].reshape(tm, D_MODEL) + _rms_norm(mixed, g_ref[...])
    o_ref[...] = out.reshape(BATCH, tt, D_MODEL)
    wcast_out_ref[...] = wcast_ref[...].astype(_bf16)


def _mixer(x, y, q, q_meta, bc, gate_s, gate_c, w_glu, b_glu, w_s_up, conv_w, conv_b, w_c_up,
           w_o, g_post, w_cast):
    tt = MIX_TILE_T
    n_t = SEQ // tt
    cast_spec = pl.BlockSpec((w_cast.shape[0] // n_t, w_cast.shape[1]), lambda t: (t, 0))
    halo_per_tile = tt // HALO_ROWS
    n_halo = SEQ // HALO_ROWS
    tile = lambda t: (0, t, 0)
    stepped = lambda rows, cols: pl.BlockSpec((IN_STEPS, BATCH, rows, cols),
                                              lambda t: (0, 0, t, 0))
    resident = functools.partial(pl.BlockSpec, index_map=lambda t: (0, 0),
                                 pipeline_mode=pl.Buffered(1))
    return pl.pallas_call(
        _mixer_kernel,
        grid=(n_t,),
        in_specs=[
            pl.BlockSpec((BATCH, tt, D_MODEL), tile),
            pl.BlockSpec((S5_SLABS, SLAB_PAIRS, 2, tt // S5_CHUNK * BATCH, V7X_LANES),
                         lambda t: (0, 0, 0, t, 0)),
            stepped(tt, IN_SEG_COLS),
            pl.BlockSpec((IN_STEPS, BATCH, HALO_ROWS, IN_SEG_COLS),
                         lambda t: (0, 0, jnp.maximum(t * halo_per_tile - 1, 0), 0)),
            pl.BlockSpec((IN_STEPS, BATCH, HALO_ROWS, IN_SEG_COLS),
                         lambda t: (0, 0, jnp.minimum((t + 1) * halo_per_tile, n_halo - 1), 0)),
            resident((HALO_ROWS, CONV_WIDTH)),
            stepped(tt, IN_SEG_COLS),
            stepped(tt, 2 * IN_SEG_COLS),
            stepped(tt, 2 * IN_SEG_COLS),
            resident((S5_WIDTH, S5_WIDTH)),
            resident((1, S5_WIDTH)),
            resident((S5_WIDTH, D_MODEL)),
            resident((3, CONV_WIDTH)),
            resident((1, CONV_WIDTH)),
            resident((CONV_WIDTH, D_MODEL)),
            resident((D_MODEL, D_MODEL)),
            resident((1, D_MODEL)),
            cast_spec,
        ],
        out_specs=[pl.BlockSpec((BATCH, tt, D_MODEL), tile), cast_spec],
        out_shape=[jax.ShapeDtypeStruct((BATCH, SEQ, D_MODEL), _f32),
                   jax.ShapeDtypeStruct(w_cast.shape, _bf16)],
        scratch_shapes=[pltpu.VMEM((S5_SLABS, BATCH * tt, V7X_LANES), _f32),
                        pltpu.VMEM((BATCH, tt + 2 * HALO_ROWS, CONV_WIDTH), _f32)],
        compiler_params=_params(("arbitrary",)),
        name="mixer_tail",
    )(x, y, q, q, q, q_meta, bc, gate_s, gate_c, w_glu, b_glu, w_s_up, conv_w, conv_b, w_c_up,
      w_o, g_post, w_cast)


def _ffn_kernel(h_ref, gpre_ref, wgu_ref, wout_ref, gpost_ref, o_ref, hb_scr):
    j = pl.program_id(1)
    last = pl.num_programs(1) - 1

    def chunk(hb):
        gu = _dot(hb, wgu_ref[...])
        gate, up = gu[:, :FFN_TILE_HIDDEN], gu[:, FFN_TILE_HIDDEN:]
        return _dot((jax.nn.silu(gate) * up).astype(_bf16), wout_ref[...])

    @pl.when(j == 0)
    def _():
        hb = _rms_norm(h_ref[...], gpre_ref[...]).astype(_bf16)
        hb_scr[...] = hb
        o_ref[...] = chunk(hb)

    @pl.when(jnp.logical_and(j > 0, j < last))
    def _():
        o_ref[...] += chunk(hb_scr[...])

    @pl.when(j == last)
    def _():
        f = o_ref[...] + chunk(hb_scr[...])
        o_ref[...] = h_ref[...] + _rms_norm(f, gpost_ref[...])


def _ffn(h1, g_pre, w_in, w_out, g_post):
    tm, th = FFN_TILE_ROWS, FFN_TILE_HIDDEN
    n_hidden = FFN_HIDDEN // th
    return pl.pallas_call(
        _ffn_kernel,
        grid=(ROWS // tm, n_hidden),
        in_specs=[
            pl.BlockSpec((tm, D_MODEL), lambda i, j: (i, 0)),
            pl.BlockSpec((1, D_MODEL), lambda i, j: (0, 0)),
            pl.BlockSpec((D_MODEL, 2 * th), lambda i, j: (0, j)),
            pl.BlockSpec((th, D_MODEL), lambda i, j: (j, 0)),
            pl.BlockSpec((1, D_MODEL), lambda i, j: (0, 0)),
        ],
        out_specs=pl.BlockSpec((tm, D_MODEL), lambda i, j: (i, 0)),
        out_shape=jax.ShapeDtypeStruct((ROWS, D_MODEL), _f32),
        scratch_shapes=[pltpu.VMEM((tm, D_MODEL), _bf16)],
        compiler_params=_params(("arbitrary", "arbitrary")),
        name="ffn",
    )(h1, g_pre, w_in, w_out, g_post)


def kernel(x, meta, g_mix_pre, g_mix_post, g_ffn_pre, g_ffn_post, w_in, gate_b, lam_re, lam_im,
           log_dt, b_re, b_im, c_re, c_im, d_skip, w_glu, b_glu, w_s_up, conv_w, conv_b, w_c_up,
           w_o, w_ffn_in, w_ffn_out):
    l = 0
    w_in_b = _cast_w_in(w_in[l])
    gate_b2 = gate_b[l].reshape(2, 1, D_MODEL)
    g_pre = g_mix_pre[l].reshape(1, D_MODEL)

    ptab, bz_re, bz_im = _discretise(lam_re[l], lam_im[l], log_dt[l], b_re[l], b_im[l])
    dg = N_DIR * S5_GROUPS
    b_maps = _compact_maps(bz_re, bz_im)
    c_maps = _compact_maps(c_re[l].reshape(dg, S5_GROUP, S5_STATE),
                           -c_im[l].reshape(dg, S5_GROUP, S5_STATE))
    ptab = ptab.reshape(S5_CHUNK + 1, 2, N_DIR, S5_SLABS, SLAB_STATES)
    ptab = ptab.transpose(3, 2, 0, 1, 4)

    u5, bc, q, gate_s, gate_c, w_ffn_in_b = _in_proj(
        x, g_pre, w_in_b, gate_b2, IN_TILE_ROWS, cast=((w_ffn_in[l], _ffn_in_plan()),))
    um5, q_meta = _meta_proj(meta.astype(x.dtype), g_pre, w_in_b)

    u = u5.reshape(S5_SLABS, S5_CHUNK_ROWS * S5_CHUNK, V7X_LANES)
    u_meta = jnp.pad(um5.reshape(S5_SLABS, META_CHUNKS, S5_CHUNK_COLS),
                     ((0, 0), (0, V7X_SUBLANES - META_CHUNKS), (0, 0)))
    d_pairs = jnp.tile(d_skip[l].reshape(S5_SLABS, SLAB_PAIRS, 1, PAIR_LANES),
                       (1, 1, 1, S5_CHUNK))
    y, w_glu_b, w_s_up_b, w_c_up_b, w_o_b = _s5(
        u, u_meta, b_maps, c_maps, ptab, d_pairs,
        cast=(w_glu[l], w_s_up[l], w_c_up[l], w_o[l]))

    h1, w_ffn_out_b = _mixer(x, y, q, q_meta, bc, gate_s, gate_c, w_glu_b,
                             b_glu[l].reshape(1, S5_WIDTH), w_s_up_b, conv_w[l],
                             conv_b[l].reshape(1, CONV_WIDTH), w_c_up_b, w_o_b,
                             g_mix_post[l].reshape(1, D_MODEL), w_ffn_out[l])

    h2 = _ffn(h1.reshape(ROWS, D_MODEL), g_ffn_pre[l].reshape(1, D_MODEL),
              w_ffn_in_b, w_ffn_out_b, g_ffn_post[l].reshape(1, D_MODEL))
    return h2.reshape(BATCH, SEQ, D_MODEL)
```

```python
import functools
import math

import jax
import jax.numpy as jnp
from jax import lax
from jax.experimental import pallas as pl
from jax.experimental.pallas import tpu as pltpu

D_MODEL = 2048
BATCH = 4
SEQ = 2048
N_META = 16
S5_WIDTH = 1024
S5_GROUP = 16
S5_GROUPS = S5_WIDTH // S5_GROUP
S5_STATE = 64
N_DIR = 2
CONV_WIDTH = 1024
FFN_HIDDEN = ((math.ceil(8 * D_MODEL / 3) + 255) // 256) * 256
IN_COLS = S5_WIDTH + 3 * CONV_WIDTH + 2 * D_MODEL
RMS_EPS = 1e-6
LAM_RE_MAX = -1e-4

ROWS = SEQ * BATCH

V7X_SUBLANES = 8
V7X_LANES = 128
V7X_MXU_DIM = 256
V7X_VMEM_LIMIT_BYTES = 56 * 1024 * 1024

S5_SLABS = S5_WIDTH // V7X_LANES
SLAB_GROUPS = V7X_LANES // S5_GROUP
SLAB_STATES = SLAB_GROUPS * S5_STATE
PAIR_LANES = 2 * S5_GROUP
SLAB_PAIRS = V7X_LANES // PAIR_LANES
PAIR_STATES = 2 * S5_STATE
S5_CHUNK = V7X_SUBLANES
PAIR_COLS = S5_CHUNK * PAIR_LANES
S5_CHUNKS = SEQ // S5_CHUNK
S5_CHUNK_ROWS = S5_CHUNKS * BATCH
S5_CHUNK_COLS = S5_CHUNK * V7X_LANES
META_CHUNKS = N_META // S5_CHUNK
STATE_REPEATS = V7X_LANES // S5_STATE

IN_TILE_ROWS = 1024
IN_STEPS = 8
IN_SEG_COLS = S5_WIDTH // IN_STEPS
IN_RING_SLOTS = 3
CAST_TILE_ROWS = 128
MIX_TILE_T = 64
FFN_TILE_ROWS = 1024
FFN_TILE_HIDDEN = 256
HALO_ROWS = 16

_f32 = jnp.float32
_bf16 = jnp.bfloat16


def _rms_norm(xf, g):
    r = lax.rsqrt(jnp.mean(xf * xf, axis=-1, keepdims=True) + RMS_EPS)
    return xf * r * g


def _params(sem):
    return pltpu.CompilerParams(dimension_semantics=sem,
                                vmem_limit_bytes=V7X_VMEM_LIMIT_BYTES)


def _dot(a, b):
    return jnp.dot(a, b, preferred_element_type=_f32)


def _dot_nt(a, b):
    return lax.dot_general(a, b, (((1,), (1,)), ((), ())), preferred_element_type=_f32)


def _discretise_kernel(lam_re_ref, lam_im_ref, log_dt_ref, b_re_ref, b_im_ref,
                       pow_ref, bz_re_ref, bz_im_ref):
    lr = jnp.minimum(lam_re_ref[...], LAM_RE_MAX)
    li = lam_im_ref[...]
    dt = jnp.exp(log_dt_ref[...])
    mag = jnp.exp(lr * dt)
    ar = mag * jnp.cos(li * dt)
    ai = mag * jnp.sin(li * dt)
    den = lr * lr + li * li
    nr = ar - 1.0
    zr = (nr * lr + ai * li) / den
    zi = (ai * lr - nr * li) / den
    pr = jnp.ones_like(ar)
    pi = jnp.zeros_like(ar)
    for m in range(S5_CHUNK + 1):
        pow_ref[m, 0] = pr
        pow_ref[m, 1] = pi
        pr, pi = pr * ar - pi * ai, pr * ai + pi * ar
    b_re = b_re_ref[...]
    b_im = b_im_ref[...]
    bz_re_ref[...] = zr[:, None, :] * b_re - zi[:, None, :] * b_im
    bz_im_ref[...] = zr[:, None, :] * b_im + zi[:, None, :] * b_re


def _discretise(lam_re, lam_im, log_dt, b_re, b_im):
    dg = N_DIR * S5_GROUPS
    b_re_t = jnp.swapaxes(b_re, -1, -2).reshape(dg, S5_GROUP, S5_STATE)
    b_im_t = jnp.swapaxes(b_im, -1, -2).reshape(dg, S5_GROUP, S5_STATE)
    return pl.pallas_call(
        _discretise_kernel,
        out_shape=(jax.ShapeDtypeStruct((S5_CHUNK + 1, 2, dg, S5_STATE), _f32),
                   jax.ShapeDtypeStruct((dg, S5_GROUP, S5_STATE), _f32),
                   jax.ShapeDtypeStruct((dg, S5_GROUP, S5_STATE), _f32)),
        name="s5_discretise",
    )(lam_re.reshape(dg, S5_STATE), lam_im.reshape(dg, S5_STATE),
      log_dt.reshape(dg, 1), b_re_t, b_im_t)


def _compact_maps(w_re, w_im):
    def part(w):
        w = w.reshape(N_DIR, S5_SLABS, V7X_LANES, S5_STATE)
        return jnp.concatenate([w] * STATE_REPEATS, axis=-1)
    return jnp.stack([part(w_re), part(w_im)], axis=2)


def _w_in_pieces(j):
    seg, gseg = IN_SEG_COLS, 2 * IN_SEG_COLS
    pieces = [(4 * S5_WIDTH + n * D_MODEL + j * gseg, gseg) for n in range(2)]
    pieces += [(n * S5_WIDTH + j * seg, seg) for n in range(4)]
    return pieces


def _cast_w_in_kernel(w_ref, o_ref):
    dst = 0
    for j in range(IN_STEPS):
        for src, width in _w_in_pieces(j):
            o_ref[:, dst:dst + width] = w_ref[:, src:src + width].astype(_bf16)
            dst += width


def _cast_w_in(w):
    spec = pl.BlockSpec((CAST_TILE_ROWS, IN_COLS), lambda i: (i, 0))
    return pl.pallas_call(
        _cast_w_in_kernel,
        grid=(D_MODEL // CAST_TILE_ROWS,),
        in_specs=[spec],
        out_specs=spec,
        out_shape=jax.ShapeDtypeStruct(w.shape, _bf16),
        compiler_params=_params(("arbitrary",)),
        name="cast_w_in",
    )(w)


def _ffn_in_plan():
    th = FFN_TILE_HIDDEN
    plan = []
    for j in range(FFN_HIDDEN // th):
        plan.append((j * th, 2 * j * th, th))
        plan.append((FFN_HIDDEN + j * th, (2 * j + 1) * th, th))
    return plan


def _in_proj_kernel(x_ref, g_ref, w_hbm, gbs_ref, gbc_ref, *rest, cast_plans):
    n_cast = len(cast_plans)
    cast_in = rest[:n_cast]
    u_ref, bc_ref, q_ref, gs_ref, gc_ref = rest[n_cast:n_cast + 5]
    cast_out = rest[n_cast + 5:2 * n_cast + 5]
    h_scr, w_buf, w_sem = rest[2 * n_cast + 5:]
    seg = IN_SEG_COLS
    tile_cols = IN_COLS // IN_STEPS

    step = ((pl.program_id(0) * pl.num_programs(1) + pl.program_id(1)) * IN_STEPS
            + pl.program_id(2))
    n_steps = pl.num_programs(0) * pl.num_programs(1) * IN_STEPS

    def w_copy(s):
        cols = pl.ds(pl.multiple_of((s % IN_STEPS) * tile_cols, tile_cols), tile_cols)
        slot = s % IN_RING_SLOTS
        return pltpu.make_async_copy(w_hbm.at[:, cols], w_buf.at[slot], w_sem.at[slot])

    @pl.when(step == 0)
    def _():
        for s in range(IN_RING_SLOTS - 1):
            w_copy(s).start()

    @pl.when(step + IN_RING_SLOTS - 1 < n_steps)
    def _():
        w_copy(step + IN_RING_SLOTS - 1).start()

    w_copy(step).wait()
    w_ref = w_buf.at[step % IN_RING_SLOTS]

    def column_step(h):
        proj = _dot(h, w_ref[...])
        gs_ref[...] = jax.nn.sigmoid(proj[:, 0:2 * seg] + gbs_ref[0]).astype(_bf16)
        gc_ref[...] = jax.nn.sigmoid(proj[:, 2 * seg:4 * seg] + gbc_ref[0]).astype(_bf16)
        for s in range(seg // V7X_LANES):
            u_ref[s] = proj[:, 4 * seg + s * V7X_LANES:4 * seg + (s + 1) * V7X_LANES].reshape(
                -1, S5_CHUNK, V7X_LANES)
        bc_ref[...] = proj[:, 6 * seg:7 * seg].astype(_bf16)
        q_ref[...] = (proj[:, 7 * seg:8 * seg] * proj[:, 5 * seg:6 * seg]).astype(_bf16)
        for src, dst, plan in zip(cast_in, cast_out, cast_plans):
            for s0, d0, width in plan:
                dst[:, d0:d0 + width] = src[:, s0:s0 + width].astype(_bf16)

    @pl.when(pl.program_id(2) == 0)
    def _():
        h = _rms_norm(x_ref[...], g_ref[...]).astype(_bf16)
        h_scr[...] = h
        column_step(h)

    @pl.when(pl.program_id(2) > 0)
    def _():
        column_step(h_scr[...])


def _in_proj(x3, g, w_in_b, gate_b, tile_rows, cast=()):
    nb, t_len, _ = x3.shape
    n_t = t_len // tile_rows
    n_steps = nb * n_t * IN_STEPS
    seg, gseg = IN_SEG_COLS, 2 * IN_SEG_COLS
    by_step = lambda b, t, j: (j, b, t, 0)
    step = lambda b, t, j: ((b * n_t + t) * IN_STEPS + j, 0)
    cast_w = [w for w, _ in cast]
    cast_plans = tuple(tuple(plan) if plan else ((0, 0, w.shape[1]),) for w, plan in cast)
    cast_specs = [pl.BlockSpec((w.shape[0] // n_steps, w.shape[1]), step) for w in cast_w]
    gb = lambda n: pl.BlockSpec((1, 1, gseg), lambda b, t, j: (n, 0, j))
    return pl.pallas_call(
        functools.partial(_in_proj_kernel, cast_plans=cast_plans),
        grid=(nb, n_t, IN_STEPS),
        in_specs=[
            pl.BlockSpec((None, tile_rows, D_MODEL), lambda b, t, j: (b, t, 0)),
            pl.BlockSpec((1, D_MODEL), lambda b, t, j: (0, 0)),
            pl.BlockSpec(memory_space=pl.ANY),
            gb(0), gb(1),
        ] + cast_specs,
        out_specs=[
            pl.BlockSpec((seg // V7X_LANES, tile_rows // S5_CHUNK, None, S5_CHUNK, V7X_LANES),
                         lambda b, t, j: (j, t, b, 0, 0)),
            pl.BlockSpec((None, None, tile_rows, seg), by_step),
            pl.BlockSpec((None, None, tile_rows, seg), by_step),
            pl.BlockSpec((None, None, tile_rows, gseg), by_step),
            pl.BlockSpec((None, None, tile_rows, gseg), by_step),
        ] + cast_specs,
        out_shape=[jax.ShapeDtypeStruct(
                       (S5_SLABS, t_len // S5_CHUNK, nb, S5_CHUNK, V7X_LANES), _f32),
                   jax.ShapeDtypeStruct((IN_STEPS, nb, t_len, seg), _bf16),
                   jax.ShapeDtypeStruct((IN_STEPS, nb, t_len, seg), _bf16),
                   jax.ShapeDtypeStruct((IN_STEPS, nb, t_len, gseg), _bf16),
                   jax.ShapeDtypeStruct((IN_STEPS, nb, t_len, gseg), _bf16)]
                  + [jax.ShapeDtypeStruct(w.shape, _bf16) for w in cast_w],
        scratch_shapes=[pltpu.VMEM((tile_rows, D_MODEL), _bf16),
                        pltpu.VMEM((IN_RING_SLOTS, D_MODEL, IN_COLS // IN_STEPS), _bf16),
                        pltpu.SemaphoreType.DMA((IN_RING_SLOTS,))],
        compiler_params=_params(("arbitrary", "arbitrary", "arbitrary")),
        name="in_proj",
    )(x3, g, w_in_b, gate_b, gate_b, *cast_w)


def _meta_proj_kernel(x_ref, g_ref, w_ref, u_ref, q_ref, h_scr):
    seg = IN_SEG_COLS

    @pl.when(pl.program_id(0) == 0)
    def _():
        h_scr[...] = _rms_norm(x_ref[...], g_ref[...]).astype(_bf16)

    proj = _dot(h_scr[...], w_ref[...])
    for s in range(seg // V7X_LANES):
        u_ref[s] = proj[:, s * V7X_LANES:(s + 1) * V7X_LANES].reshape(-1, S5_CHUNK, V7X_LANES)
    q_ref[...] = (proj[:, 3 * seg:4 * seg] * proj[:, seg:2 * seg]).astype(_bf16)


def _meta_proj(meta, g, w_in_b):
    seg = IN_SEG_COLS
    return pl.pallas_call(
        _meta_proj_kernel,
        grid=(IN_STEPS,),
        in_specs=[
            pl.BlockSpec((N_META, D_MODEL), lambda j: (0, 0)),
            pl.BlockSpec((1, D_MODEL), lambda j: (0, 0)),
            pl.BlockSpec((D_MODEL, 4 * seg), lambda j: (0, 2 * j + 1)),
        ],
        out_specs=[
            pl.BlockSpec((seg // V7X_LANES, META_CHUNKS, S5_CHUNK, V7X_LANES),
                         lambda j: (j, 0, 0, 0)),
            pl.BlockSpec((N_META, seg), lambda j: (0, j)),
        ],
        out_shape=[jax.ShapeDtypeStruct((S5_SLABS, META_CHUNKS, S5_CHUNK, V7X_LANES), _f32),
                   jax.ShapeDtypeStruct((N_META, CONV_WIDTH), _bf16)],
        scratch_shapes=[pltpu.VMEM((N_META, D_MODEL), _bf16)],
        compiler_params=_params(("arbitrary",)),
        name="meta_proj",
    )(meta, g, w_in_b)


def _pair_map(ref, d, part, q):
    rows = lax.broadcasted_iota(jnp.int32, (PAIR_LANES, V7X_LANES), 0)
    cols = lax.broadcasted_iota(jnp.int32, (PAIR_LANES, V7X_LANES), 1)
    same_group = (rows // S5_GROUP) == (cols // S5_STATE)
    return jnp.where(same_group, ref[d, 0, part, q * PAIR_LANES:(q + 1) * PAIR_LANES, :], 0.0)


def _transpose_pieces(src):
    assert SLAB_PAIRS == 4 and len(src) == SLAB_PAIRS
    lane = lax.broadcasted_iota(jnp.int32, src[0].shape, 1)

    def exchange(a, b, width):
        keep = (lane // width) % 2 == 0
        return (jnp.where(keep, a, pltpu.roll(b, width, 1)),
                jnp.where(keep, pltpu.roll(a, V7X_LANES - width, 1), b))

    b0, b2 = exchange(src[0], src[2], 2 * PAIR_LANES)
    b1, b3 = exchange(src[1], src[3], 2 * PAIR_LANES)
    c0, c1 = exchange(b0, b1, PAIR_LANES)
    c2, c3 = exchange(b2, b3, PAIR_LANES)
    return [c0, c1, c2, c3]


def _boundary_scan(x_scr, a_re, a_im, init):
    n_tiles = S5_CHUNK_ROWS // V7X_SUBLANES
    low = lax.broadcasted_iota(jnp.int32, (V7X_SUBLANES, V7X_LANES), 0) < BATCH

    def step(ar, ai, sr, si, xr, xi):
        return ar * sr - ai * si + xr, ar * si + ai * sr + xi

    def swap(v):
        return pltpu.roll(v, BATCH, 0)

    def body(i, carry):
        rf = pl.multiple_of(i * V7X_SUBLANES, V7X_SUBLANES)
        rb = pl.multiple_of((n_tiles - 1 - i) * V7X_SUBLANES, V7X_SUBLANES)
        loaded = []
        for q in range(SLAB_PAIRS):
            for part in range(2):
                lanes = slice(q * PAIR_COLS + part * PAIR_STATES,
                              q * PAIR_COLS + (part + 1) * PAIR_STATES)
                loaded.append((lanes, x_scr[0, pl.ds(rf, V7X_SUBLANES), lanes],
                               x_scr[1, pl.ds(rb, V7X_SUBLANES), lanes]))
        new, stores = [], []
        for q in range(SLAB_PAIRS):
            (re, xfr, xbr), (im, xfi, xbi) = loaded[2 * q], loaded[2 * q + 1]
            w1r, w1i = jnp.where(low, xfr, xbr), jnp.where(low, xfi, xbi)
            w2r, w2i = swap(jnp.where(low, xbr, xfr)), swap(jnp.where(low, xbi, xfi))
            cr, ci = carry[2 * q], carry[2 * q + 1]
            s1r, s1i = step(a_re[q], a_im[q], cr, ci, w1r, w1i)
            s2r, s2i = step(a_re[q], a_im[q], s1r, s1i, w2r, w2i)
            t1r, t1i = swap(s1r), swap(s1i)
            stores += [(0, rf, re, jnp.where(low, cr, t1r)), (0, rf, im, jnp.where(low, ci, t1i)),
                       (1, rb, re, jnp.where(low, t1r, cr)), (1, rb, im, jnp.where(low, t1i, ci))]
            new += [s2r, s2i]
        for d, r0, lanes, value in stores:
            x_scr[d, pl.ds(r0, V7X_SUBLANES), lanes] = value
        return tuple(new)

    lax.fori_loop(0, n_tiles, body, init)


def _s5_kernel(u_ref, um_ref, b_ref, c_ref, p_ref, d_ref, *rest, n_cast):
    cast_in = rest[:n_cast]
    y_ref = rest[n_cast]
    cast_out = rest[n_cast + 1:2 * n_cast + 1]
    ub_scr, uf_scr, w_scr, x_scr, sp_scr, tap_scr, toep_scr, xm_scr = rest[2 * n_cast + 1:]
    for src, dst in zip(cast_in, cast_out):
        dst[...] = src[...].astype(_bf16)

    half_steps = S5_CHUNK // 2
    blk = PAIR_LANES

    def pair_cols(q):
        return slice(q * PAIR_COLS, (q + 1) * PAIR_COLS)

    def pair_states(q):
        return slice(q * PAIR_STATES, (q + 1) * PAIR_STATES)

    def strided_u(j):
        return u_ref[0, pl.ds(j, S5_CHUNK_ROWS, stride=S5_CHUNK), :]

    um = um_ref[0]
    um_q = [[] for _ in range(SLAB_PAIRS)]
    for h in range(2):
        steps = range(h * half_steps, (h + 1) * half_steps)
        for q, t in enumerate(_transpose_pieces([strided_u(j) for j in steps])):
            uf_scr[q, :, h * V7X_LANES:(h + 1) * V7X_LANES] = t
            ub_scr[q, :, h * V7X_LANES:(h + 1) * V7X_LANES] = t.astype(_bf16)
        meta = [um[:, j * V7X_LANES:(j + 1) * V7X_LANES] for j in steps]
        for q, t in enumerate(_transpose_pieces(meta)):
            um_q[q].append(t)

    for d in range(N_DIR):
        for q in range(SLAB_PAIRS):
            w_re = _pair_map(b_ref, d, 0, q)
            w_im = _pair_map(b_ref, d, 1, q)
            for j in range(S5_CHUNK):
                m = (S5_CHUNK - 1 - j) if d == 0 else j
                pr = p_ref[0, d, m, 0:1, pair_states(q)]
                pi = p_ref[0, d, m, 1:2, pair_states(q)]
                rows = slice(j * blk, (j + 1) * blk)
                w_scr[d, q, rows, 0:PAIR_STATES] = (w_re * pr - w_im * pi).astype(_bf16)
                w_scr[d, q, rows, PAIR_STATES:] = (w_re * pi + w_im * pr).astype(_bf16)
            x_scr[d, :, pair_cols(q)] = _dot(ub_scr[q], w_scr[d, q])
            if d == 0:
                xm_scr[:, pair_cols(q)] = _dot(
                    jnp.concatenate(um_q[q], axis=1).astype(_bf16), w_scr[0, q])

    low = lax.broadcasted_iota(jnp.int32, (V7X_SUBLANES, V7X_LANES), 0) < BATCH
    a_re, a_im, init = [], [], []
    for q in range(SLAB_PAIRS):
        re = slice(q * PAIR_COLS, q * PAIR_COLS + PAIR_STATES)
        im = slice(q * PAIR_COLS + PAIR_STATES, (q + 1) * PAIR_COLS)
        ar = p_ref[0, 0, S5_CHUNK, 0:1, pair_states(q)]
        ai = p_ref[0, 0, S5_CHUNK, 1:2, pair_states(q)]
        a_re.append(jnp.where(low, ar, p_ref[0, 1, S5_CHUNK, 0:1, pair_states(q)]))
        a_im.append(jnp.where(low, ai, p_ref[0, 1, S5_CHUNK, 1:2, pair_states(q)]))
        sr = jnp.zeros((1, V7X_LANES), _f32)
        si = jnp.zeros((1, V7X_LANES), _f32)
        for c in range(META_CHUNKS):
            xr = xm_scr[c:c + 1, re]
            xi = xm_scr[c:c + 1, im]
            sr, si = ar * sr - ai * si + xr, ar * si + ai * sr + xi
        init.append(jnp.where(low, sr, 0.0))
        init.append(jnp.where(low, si, 0.0))
    _boundary_scan(x_scr, a_re, a_im, tuple(init))
    for d in range(N_DIR):
        sp_scr[d] = x_scr[d].astype(_bf16)

    def out_weights(c_re, c_imn, d, m, q):
        pr = p_ref[0, d, m, 0:1, pair_states(q)]
        pi = p_ref[0, d, m, 1:2, pair_states(q)]
        return jnp.concatenate([c_re * pr + c_imn * pi, c_imn * pr - c_re * pi],
                               axis=1).astype(_bf16)

    for q in range(SLAB_PAIRS):
        wy = []
        for d in range(N_DIR):
            c_re = _pair_map(c_ref, d, 0, q)
            c_imn = _pair_map(c_ref, d, 1, q)
            c_map = jnp.concatenate([c_re, c_imn], axis=1).astype(_bf16)
            tap_scr[d] = _dot_nt(w_scr[d, q], jnp.concatenate([c_map] * S5_CHUNK, axis=0))
            powers = [o + 1 if d == 0 else S5_CHUNK - o for o in range(S5_CHUNK)]
            wy.append(jnp.concatenate([out_weights(c_re, c_imn, d, m, q) for m in powers],
                                      axis=0))
        for o in range(S5_CHUNK):
            lanes = slice(o * blk, (o + 1) * blk)
            if o:
                toep_scr[0:o * blk, lanes] = tap_scr[0, (S5_CHUNK - 1 - o) * blk:
                                                     (S5_CHUNK - 1) * blk, lanes]
            toep_scr[o * blk:(o + 1) * blk, lanes] = (
                tap_scr[0, (S5_CHUNK - 1) * blk:, lanes] + tap_scr[1, 0:blk, lanes])
            if o < S5_CHUNK - 1:
                toep_scr[(o + 1) * blk:, lanes] = tap_scr[1, blk:(S5_CHUNK - o) * blk, lanes]
        y = _dot_nt(sp_scr[0, :, pair_cols(q)], wy[0])
        y += _dot_nt(sp_scr[1, :, pair_cols(q)], wy[1])
        y += _dot(ub_scr[q], toep_scr[...].astype(_bf16))
        y += uf_scr[q] * d_ref[0, q]
        for h in range(2):
            y_ref[0, q, h] = y[:, h * V7X_LANES:(h + 1) * V7X_LANES]


def _s5(u, u_meta, b_maps, c_maps, ptab, d_skip, cast=()):
    maps = pl.BlockSpec((N_DIR, 1, 2, V7X_LANES, V7X_LANES), lambda s: (0, s, 0, 0, 0))
    cast_specs = [pl.BlockSpec((w.shape[0] // S5_SLABS, w.shape[1]), lambda s: (s, 0))
                  for w in cast]
    return pl.pallas_call(
        functools.partial(_s5_kernel, n_cast=len(cast)),
        grid=(S5_SLABS,),
        in_specs=[
            pl.BlockSpec((1, S5_CHUNK_ROWS * S5_CHUNK, V7X_LANES), lambda s: (s, 0, 0)),
            pl.BlockSpec((1, V7X_SUBLANES, S5_CHUNK_COLS), lambda s: (s, 0, 0)),
            maps,
            maps,
            pl.BlockSpec((1, N_DIR, S5_CHUNK + 1, 2, SLAB_STATES), lambda s: (s, 0, 0, 0, 0)),
            pl.BlockSpec((1, SLAB_PAIRS, 1, PAIR_COLS), lambda s: (s, 0, 0, 0)),
        ] + cast_specs,
        out_specs=[pl.BlockSpec((1, SLAB_PAIRS, 2, S5_CHUNK_ROWS, V7X_LANES),
                                lambda s: (s, 0, 0, 0, 0))] + cast_specs,
        out_shape=[jax.ShapeDtypeStruct((S5_SLABS, SLAB_PAIRS, 2, S5_CHUNK_ROWS, V7X_LANES),
                                        _f32)]
                  + [jax.ShapeDtypeStruct(w.shape, _bf16) for w in cast],
        scratch_shapes=[
            pltpu.VMEM((SLAB_PAIRS, S5_CHUNK_ROWS, PAIR_COLS), _bf16),
            pltpu.VMEM((SLAB_PAIRS, S5_CHUNK_ROWS, PAIR_COLS), _f32),
            pltpu.VMEM((N_DIR, SLAB_PAIRS, PAIR_COLS, PAIR_COLS), _bf16),
            pltpu.VMEM((N_DIR, S5_CHUNK_ROWS, 2 * SLAB_STATES), _f32),
            pltpu.VMEM((N_DIR, S5_CHUNK_ROWS, 2 * SLAB_STATES), _bf16),
            pltpu.VMEM((N_DIR, PAIR_COLS, PAIR_COLS), _f32),
            pltpu.VMEM((PAIR_COLS, PAIR_COLS), _f32),
            pltpu.VMEM((V7X_SUBLANES, 2 * SLAB_STATES), _f32),
        ],
        compiler_params=_params(("arbitrary",)),
        name="s5",
    )(u, u_meta, b_maps, c_maps, ptab, d_skip, *cast)


def _mixer_kernel(h_ref, y_ref, q_ref, qp_ref, qn_ref, qm_ref, bc_ref, gs_ref, gc_ref,
                  wglu_ref, bglu_ref, wsup_ref, cw_ref, cb_ref, wcup_ref, wo_ref,
                  g_ref, wcast_ref, o_ref, wcast_out_ref, ys_scr, q_scr):
    t = pl.program_id(0)
    last = pl.num_programs(0) - 1
    tt = MIX_TILE_T
    tm = BATCH * tt
    n_c = tt // S5_CHUNK

    def columns(ref, b):
        return jnp.concatenate([ref[j, b] for j in range(IN_STEPS)], axis=1)

    def all_rows(ref):
        return jnp.concatenate([ref[j].reshape(tm, -1) for j in range(IN_STEPS)], axis=1)

    vs = []
    for b in range(BATCH):
        prev = jnp.where(t == 0, qm_ref[...], columns(qp_ref, b)).astype(_f32)
        nxt = jnp.where(t == last, 0.0, columns(qn_ref, b).astype(_f32))
        q_scr[b, 0:HALO_ROWS, :] = prev
        q_scr[b, HALO_ROWS:HALO_ROWS + tt, :] = columns(q_ref, b).astype(_f32)
        q_scr[b, HALO_ROWS + tt:, :] = nxt
        vs.append(cw_ref[0:1, :] * q_scr[b, HALO_ROWS - 1:HALO_ROWS - 1 + tt, :]
                  + cw_ref[1:2, :] * q_scr[b, HALO_ROWS:HALO_ROWS + tt, :]
                  + cw_ref[2:3, :] * q_scr[b, HALO_ROWS + 1:HALO_ROWS + 1 + tt, :]
                  + cb_ref[...])
    v = jnp.concatenate(vs, axis=0)
    bc = all_rows(bc_ref).astype(_f32)
    y_c = _dot((bc * v).astype(_bf16), wcup_ref[...])

    half_steps = S5_CHUNK // 2
    for s in range(S5_SLABS):
        for h in range(2):
            for b in range(BATCH):
                tiles = [y_ref[s, q, h, pl.ds(b, n_c, stride=BATCH), :]
                         for q in range(SLAB_PAIRS)]
                for k, tile in enumerate(_transpose_pieces(tiles)):
                    o = h * half_steps + k
                    ys_scr[s, pl.ds(b * tt + o, n_c, stride=S5_CHUNK), :] = tile
    ys = jnp.concatenate([ys_scr[s] for s in range(S5_SLABS)], axis=1)
    ys = jax.nn.gelu(ys)
    glu = _dot(ys.astype(_bf16), wglu_ref[...]) + bglu_ref[...]
    ys = ys * jax.nn.sigmoid(glu)
    y_s = _dot(ys.astype(_bf16), wsup_ref[...])

    merged = all_rows(gs_ref).astype(_f32) * y_s + all_rows(gc_ref).astype(_f32) * y_c
    mixed = _dot(merged.astype(_bf16), wo_ref[...])
    out = h_ref[...].reshape(tm, D_MODEL) + _rms_norm(mixed, g_ref[...])
    o_ref[...] = out.reshape(BATCH, tt, D_MODEL)
    wcast_out_ref[...] = wcast_ref[...].astype(_bf16)


def _mixer(x, y, q, q_meta, bc, gate_s, gate_c, w_glu, b_glu, w_s_up, conv_w, conv_b, w_c_up,
           w_o, g_post, w_cast):
    tt = MIX_TILE_T
    n_t = SEQ // tt
    cast_spec = pl.BlockSpec((w_cast.shape[0] // n_t, w_cast.shape[1]), lambda t: (t, 0))
    halo_per_tile = tt // HALO_ROWS
    n_halo = SEQ // HALO_ROWS
    tile = lambda t: (0, t, 0)
    stepped = lambda rows, cols: pl.BlockSpec((IN_STEPS, BATCH, rows, cols),
                                              lambda t: (0, 0, t, 0))
    resident = functools.partial(pl.BlockSpec, index_map=lambda t: (0, 0),
                                 pipeline_mode=pl.Buffered(1))
    return pl.pallas_call(
        _mixer_kernel,
        grid=(n_t,),
        in_specs=[
            pl.BlockSpec((BATCH, tt, D_MODEL), tile),
            pl.BlockSpec((S5_SLABS, SLAB_PAIRS, 2, tt // S5_CHUNK * BATCH, V7X_LANES),
                         lambda t: (0, 0, 0, t, 0)),
            stepped(tt, IN_SEG_COLS),
            pl.BlockSpec((IN_STEPS, BATCH, HALO_ROWS, IN_SEG_COLS),
                         lambda t: (0, 0, jnp.maximum(t * halo_per_tile - 1, 0), 0)),
            pl.BlockSpec((IN_STEPS, BATCH, HALO_ROWS, IN_SEG_COLS),
                         lambda t: (0, 0, jnp.minimum((t + 1) * halo_per_tile, n_halo - 1), 0)),
            resident((HALO_ROWS, CONV_WIDTH)),
            stepped(tt, IN_SEG_COLS),
            stepped(tt, 2 * IN_SEG_COLS),
            stepped(tt, 2 * IN_SEG_COLS),
            resident((S5_WIDTH, S5_WIDTH)),
            resident((1, S5_WIDTH)),
            resident((S5_WIDTH, D_MODEL)),
            resident((3, CONV_WIDTH)),
            resident((1, CONV_WIDTH)),
            resident((CONV_WIDTH, D_MODEL)),
            resident((D_MODEL, D_MODEL)),
            resident((1, D_MODEL)),
            cast_spec,
        ],
        out_specs=[pl.BlockSpec((BATCH, tt, D_MODEL), tile), cast_spec],
        out_shape=[jax.ShapeDtypeStruct((BATCH, SEQ, D_MODEL), _f32),
                   jax.ShapeDtypeStruct(w_cast.shape, _bf16)],
        scratch_shapes=[pltpu.VMEM((S5_SLABS, BATCH * tt, V7X_LANES), _f32),
                        pltpu.VMEM((BATCH, tt + 2 * HALO_ROWS, CONV_WIDTH), _f32)],
        compiler_params=_params(("arbitrary",)),
        name="mixer_tail",
    )(x, y, q, q, q, q_meta, bc, gate_s, gate_c, w_glu, b_glu, w_s_up, conv_w, conv_b, w_c_up,
      w_o, g_post, w_cast)


def _ffn_kernel(h_ref, gpre_ref, wgu_ref, wout_ref, gpost_ref, o_ref, hb_scr):
    j = pl.program_id(1)
    last = pl.num_programs(1) - 1

    def chunk(hb):
        gu = _dot(hb, wgu_ref[...])
        gate, up = gu[:, :FFN_TILE_HIDDEN], gu[:, FFN_TILE_HIDDEN:]
        return _dot((jax.nn.silu(gate) * up).astype(_bf16), wout_ref[...])

    @pl.when(j == 0)
    def _():
        hb = _rms_norm(h_ref[...], gpre_ref[...]).astype(_bf16)
        hb_scr[...] = hb
        o_ref[...] = chunk(hb)

    @pl.when(jnp.logical_and(j > 0, j < last))
    def _():
        o_ref[...] += chunk(hb_scr[...])

    @pl.when(j == last)
    def _():
        f = o_ref[...] + chunk(hb_scr[...])
        o_ref[...] = h_ref[...] + _rms_norm(f, gpost_ref[...])


def _ffn(h1, g_pre, w_in, w_out, g_post):
    tm, th = FFN_TILE_ROWS, FFN_TILE_HIDDEN
    n_hidden = FFN_HIDDEN // th
    return pl.pallas_call(
        _ffn_kernel,
        grid=(ROWS // tm, n_hidden),
        in_specs=[
            pl.BlockSpec((tm, D_MODEL), lambda i, j: (i, 0)),
            pl.BlockSpec((1, D_MODEL), lambda i, j: (0, 0)),
            pl.BlockSpec((D_MODEL, 2 * th), lambda i, j: (0, j)),
            pl.BlockSpec((th, D_MODEL), lambda i, j: (j, 0)),
            pl.BlockSpec((1, D_MODEL), lambda i, j: (0, 0)),
        ],
        out_specs=pl.BlockSpec((tm, D_MODEL), lambda i, j: (i, 0)),
        out_shape=jax.ShapeDtypeStruct((ROWS, D_MODEL), _f32),
        scratch_shapes=[pltpu.VMEM((tm, D_MODEL), _bf16)],
        compiler_params=_params(("arbitrary", "arbitrary")),
        name="ffn",
    )(h1, g_pre, w_in, w_out, g_post)


def kernel(x, meta, g_mix_pre, g_mix_post, g_ffn_pre, g_ffn_post, w_in, gate_b, lam_re, lam_im,
           log_dt, b_re, b_im, c_re, c_im, d_skip, w_glu, b_glu, w_s_up, conv_w, conv_b, w_c_up,
           w_o, w_ffn_in, w_ffn_out):
    l = 0
    w_in_b = _cast_w_in(w_in[l])
    gate_b2 = gate_b[l].reshape(2, 1, D_MODEL)
    g_pre = g_mix_pre[l].reshape(1, D_MODEL)

    ptab, bz_re, bz_im = _discretise(lam_re[l], lam_im[l], log_dt[l], b_re[l], b_im[l])
    dg = N_DIR * S5_GROUPS
    b_maps = _compact_maps(bz_re, bz_im)
    c_maps = _compact_maps(c_re[l].reshape(dg, S5_GROUP, S5_STATE),
                           -c_im[l].reshape(dg, S5_GROUP, S5_STATE))
    ptab = ptab.reshape(S5_CHUNK + 1, 2, N_DIR, S5_SLABS, SLAB_STATES)
    ptab = ptab.transpose(3, 2, 0, 1, 4)

    u5, bc, q, gate_s, gate_c, w_ffn_in_b = _in_proj(
        x, g_pre, w_in_b, gate_b2, IN_TILE_ROWS, cast=((w_ffn_in[l], _ffn_in_plan()),))
    um5, q_meta = _meta_proj(meta.astype(x.dtype), g_pre, w_in_b)

    u = u5.reshape(S5_SLABS, S5_CHUNK_ROWS * S5_CHUNK, V7X_LANES)
    u_meta = jnp.pad(um5.reshape(S5_SLABS, META_CHUNKS, S5_CHUNK_COLS),
                     ((0, 0), (0, V7X_SUBLANES - META_CHUNKS), (0, 0)))
    d_pairs = jnp.tile(d_skip[l].reshape(S5_SLABS, SLAB_PAIRS, 1, PAIR_LANES),
                       (1, 1, 1, S5_CHUNK))
    y, w_glu_b, w_s_up_b, w_c_up_b, w_o_b = _s5(
        u, u_meta, b_maps, c_maps, ptab, d_pairs,
        cast=(w_glu[l], w_s_up[l], w_c_up[l], w_o[l]))

    h1, w_ffn_out_b = _mixer(x, y, q, q_meta, bc, gate_s, gate_c, w_glu_b,
                             b_glu[l].reshape(1, S5_WIDTH), w_s_up_b, conv_w[l],
                             conv_b[l].reshape(1, CONV_WIDTH), w_c_up_b, w_o_b,
                             g_mix_post[l].reshape(1, D_MODEL), w_ffn_out[l])

    h2 = _ffn(h1.reshape(ROWS, D_MODEL), g_ffn_pre[l].reshape(1, D_MODEL),
              w_ffn_in_b, w_ffn_out_b, g_ffn_post[l].reshape(1, D_MODEL))
    return h2.reshape(BATCH, SEQ, D_MODEL)
```

```python
import functools
import math

import jax
import jax.numpy as jnp
from jax import lax
from jax.experimental import pallas as pl
from jax.experimental.pallas import tpu as pltpu

D_MODEL = 2048
BATCH = 4
SEQ = 2048
N_META = 16
S5_WIDTH = 1024
S5_GROUP = 16
S5_GROUPS = S5_WIDTH // S5_GROUP
S5_STATE = 64
N_DIR = 2
CONV_WIDTH = 1024
FFN_HIDDEN = ((math.ceil(8 * D_MODEL / 3) + 255) // 256) * 256
IN_COLS = S5_WIDTH + 3 * CONV_WIDTH + 2 * D_MODEL
RMS_EPS = 1e-6
LAM_RE_MAX = -1e-4

ROWS = SEQ * BATCH

V7X_SUBLANES = 8
V7X_LANES = 128
V7X_MXU_DIM = 256
V7X_VMEM_LIMIT_BYTES = 56 * 1024 * 1024

S5_SLABS = S5_WIDTH // V7X_LANES
SLAB_GROUPS = V7X_LANES // S5_GROUP
SLAB_STATES = SLAB_GROUPS * S5_STATE
PAIR_LANES = 2 * S5_GROUP
SLAB_PAIRS = V7X_LANES // PAIR_LANES
PAIR_STATES = 2 * S5_STATE
S5_CHUNK = V7X_SUBLANES
PAIR_COLS = S5_CHUNK * PAIR_LANES
S5_CHUNKS = SEQ // S5_CHUNK
S5_CHUNK_ROWS = S5_CHUNKS * BATCH
S5_CHUNK_COLS = S5_CHUNK * V7X_LANES
META_CHUNKS = N_META // S5_CHUNK
S5_RING_SLOTS = 3
STATE_REPEATS = V7X_LANES // S5_STATE

IN_TILE_ROWS = 1024
IN_STEPS = 8
IN_SEG_COLS = S5_WIDTH // IN_STEPS
IN_RING_SLOTS = 3
CAST_TILE_ROWS = 128
MIX_TILE_T = 64
FFN_TILE_ROWS = 1024
FFN_TILE_HIDDEN = 256
HALO_ROWS = 16

_f32 = jnp.float32
_bf16 = jnp.bfloat16


def _rms_norm(xf, g):
    r = lax.rsqrt(jnp.mean(xf * xf, axis=-1, keepdims=True) + RMS_EPS)
    return xf * r * g


def _params(sem):
    return pltpu.CompilerParams(dimension_semantics=sem,
                                vmem_limit_bytes=V7X_VMEM_LIMIT_BYTES)


def _dot(a, b):
    return jnp.dot(a, b, preferred_element_type=_f32)


def _dot_nt(a, b):
    return lax.dot_general(a, b, (((1,), (1,)), ((), ())), preferred_element_type=_f32)


def _discretise_kernel(lam_re_ref, lam_im_ref, log_dt_ref, b_re_ref, b_im_ref,
                       pow_ref, bz_re_ref, bz_im_ref):
    lr = jnp.minimum(lam_re_ref[...], LAM_RE_MAX)
    li = lam_im_ref[...]
    dt = jnp.exp(log_dt_ref[...])
    mag = jnp.exp(lr * dt)
    ar = mag * jnp.cos(li * dt)
    ai = mag * jnp.sin(li * dt)
    den = lr * lr + li * li
    nr = ar - 1.0
    zr = (nr * lr + ai * li) / den
    zi = (ai * lr - nr * li) / den
    pr = jnp.ones_like(ar)
    pi = jnp.zeros_like(ar)
    for m in range(S5_CHUNK + 1):
        pow_ref[m, 0] = pr
        pow_ref[m, 1] = pi
        pr, pi = pr * ar - pi * ai, pr * ai + pi * ar
    b_re = b_re_ref[...]
    b_im = b_im_ref[...]
    bz_re_ref[...] = zr[:, None, :] * b_re - zi[:, None, :] * b_im
    bz_im_ref[...] = zr[:, None, :] * b_im + zi[:, None, :] * b_re


def _discretise(lam_re, lam_im, log_dt, b_re, b_im):
    dg = N_DIR * S5_GROUPS
    b_re_t = jnp.swapaxes(b_re, -1, -2).reshape(dg, S5_GROUP, S5_STATE)
    b_im_t = jnp.swapaxes(b_im, -1, -2).reshape(dg, S5_GROUP, S5_STATE)
    return pl.pallas_call(
        _discretise_kernel,
        out_shape=(jax.ShapeDtypeStruct((S5_CHUNK + 1, 2, dg, S5_STATE), _f32),
                   jax.ShapeDtypeStruct((dg, S5_GROUP, S5_STATE), _f32),
                   jax.ShapeDtypeStruct((dg, S5_GROUP, S5_STATE), _f32)),
        name="s5_discretise",
    )(lam_re.reshape(dg, S5_STATE), lam_im.reshape(dg, S5_STATE),
      log_dt.reshape(dg, 1), b_re_t, b_im_t)


def _compact_maps(w_re, w_im):
    def part(w):
        w = w.reshape(N_DIR, S5_SLABS, V7X_LANES, S5_STATE)
        return jnp.concatenate([w] * STATE_REPEATS, axis=-1)
    return jnp.stack([part(w_re), part(w_im)], axis=2)


def _w_in_pieces(j):
    seg, gseg = IN_SEG_COLS, 2 * IN_SEG_COLS
    pieces = [(4 * S5_WIDTH + n * D_MODEL + j * gseg, gseg) for n in range(2)]
    pieces += [(n * S5_WIDTH + j * seg, seg) for n in range(4)]
    return pieces


def _cast_w_in_kernel(w_ref, o_ref):
    dst = 0
    for j in range(IN_STEPS):
        for src, width in _w_in_pieces(j):
            o_ref[:, dst:dst + width] = w_ref[:, src:src + width].astype(_bf16)
            dst += width


def _cast_w_in(w):
    spec = pl.BlockSpec((CAST_TILE_ROWS, IN_COLS), lambda i: (i, 0))
    return pl.pallas_call(
        _cast_w_in_kernel,
        grid=(D_MODEL // CAST_TILE_ROWS,),
        in_specs=[spec],
        out_specs=spec,
        out_shape=jax.ShapeDtypeStruct(w.shape, _bf16),
        compiler_params=_params(("arbitrary",)),
        name="cast_w_in",
    )(w)


def _ffn_in_plan():
    th = FFN_TILE_HIDDEN
    plan = []
    for j in range(FFN_HIDDEN // th):
        plan.append((j * th, 2 * j * th, th))
        plan.append((FFN_HIDDEN + j * th, (2 * j + 1) * th, th))
    return plan


def _in_proj_kernel(x_ref, g_ref, w_hbm, gbs_ref, gbc_ref, *rest, cast_plans):
    n_cast = len(cast_plans)
    cast_in = rest[:n_cast]
    u_ref, bc_ref, q_ref, gs_ref, gc_ref = rest[n_cast:n_cast + 5]
    cast_out = rest[n_cast + 5:2 * n_cast + 5]
    h_scr, w_buf, w_sem = rest[2 * n_cast + 5:]
    seg = IN_SEG_COLS
    tile_cols = IN_COLS // IN_STEPS

    step = ((pl.program_id(0) * pl.num_programs(1) + pl.program_id(1)) * IN_STEPS
            + pl.program_id(2))
    n_steps = pl.num_programs(0) * pl.num_programs(1) * IN_STEPS

    def w_copy(s):
        cols = pl.ds(pl.multiple_of((s % IN_STEPS) * tile_cols, tile_cols), tile_cols)
        slot = s % IN_RING_SLOTS
        return pltpu.make_async_copy(w_hbm.at[:, cols], w_buf.at[slot], w_sem.at[slot])

    @pl.when(step == 0)
    def _():
        for s in range(IN_RING_SLOTS - 1):
            w_copy(s).start()

    @pl.when(step + IN_RING_SLOTS - 1 < n_steps)
    def _():
        w_copy(step + IN_RING_SLOTS - 1).start()

    w_copy(step).wait()
    w_ref = w_buf.at[step % IN_RING_SLOTS]

    def column_step(h):
        proj = _dot(h, w_ref[...])
        gs_ref[...] = jax.nn.sigmoid(proj[:, 0:2 * seg] + gbs_ref[0]).astype(_bf16)
        gc_ref[...] = jax.nn.sigmoid(proj[:, 2 * seg:4 * seg] + gbc_ref[0]).astype(_bf16)
        for s in range(seg // V7X_LANES):
            u_ref[s] = proj[:, 4 * seg + s * V7X_LANES:4 * seg + (s + 1) * V7X_LANES].reshape(
                -1, S5_CHUNK, V7X_LANES)
        bc_ref[...] = proj[:, 6 * seg:7 * seg].astype(_bf16)
        q_ref[...] = (proj[:, 7 * seg:8 * seg] * proj[:, 5 * seg:6 * seg]).astype(_bf16)
        for src, dst, plan in zip(cast_in, cast_out, cast_plans):
            for s0, d0, width in plan:
                dst[:, d0:d0 + width] = src[:, s0:s0 + width].astype(_bf16)

    @pl.when(pl.program_id(2) == 0)
    def _():
        h = _rms_norm(x_ref[...], g_ref[...]).astype(_bf16)
        h_scr[...] = h
        column_step(h)

    @pl.when(pl.program_id(2) > 0)
    def _():
        column_step(h_scr[...])


def _in_proj(x3, g, w_in_b, gate_b, tile_rows, cast=()):
    nb, t_len, _ = x3.shape
    n_t = t_len // tile_rows
    n_steps = nb * n_t * IN_STEPS
    seg, gseg = IN_SEG_COLS, 2 * IN_SEG_COLS
    by_step = lambda b, t, j: (j, b, t, 0)
    step = lambda b, t, j: ((b * n_t + t) * IN_STEPS + j, 0)
    cast_w = [w for w, _ in cast]
    cast_plans = tuple(tuple(plan) if plan else ((0, 0, w.shape[1]),) for w, plan in cast)
    cast_specs = [pl.BlockSpec((w.shape[0] // n_steps, w.shape[1]), step) for w in cast_w]
    gb = lambda n: pl.BlockSpec((1, 1, gseg), lambda b, t, j: (n, 0, j))
    return pl.pallas_call(
        functools.partial(_in_proj_kernel, cast_plans=cast_plans),
        grid=(nb, n_t, IN_STEPS),
        in_specs=[
            pl.BlockSpec((None, tile_rows, D_MODEL), lambda b, t, j: (b, t, 0)),
            pl.BlockSpec((1, D_MODEL), lambda b, t, j: (0, 0)),
            pl.BlockSpec(memory_space=pl.ANY),
            gb(0), gb(1),
        ] + cast_specs,
        out_specs=[
            pl.BlockSpec((seg // V7X_LANES, tile_rows // S5_CHUNK, None, S5_CHUNK, V7X_LANES),
                         lambda b, t, j: (j, t, b, 0, 0)),
            pl.BlockSpec((None, None, tile_rows, seg), by_step),
            pl.BlockSpec((None, None, tile_rows, seg), by_step),
            pl.BlockSpec((None, None, tile_rows, gseg), by_step),
            pl.BlockSpec((None, None, tile_rows, gseg), by_step),
        ] + cast_specs,
        out_shape=[jax.ShapeDtypeStruct(
                       (S5_SLABS, t_len // S5_CHUNK, nb, S5_CHUNK, V7X_LANES), _f32),
                   jax.ShapeDtypeStruct((IN_STEPS, nb, t_len, seg), _bf16),
                   jax.ShapeDtypeStruct((IN_STEPS, nb, t_len, seg), _bf16),
                   jax.ShapeDtypeStruct((IN_STEPS, nb, t_len, gseg), _bf16),
                   jax.ShapeDtypeStruct((IN_STEPS, nb, t_len, gseg), _bf16)]
                  + [jax.ShapeDtypeStruct(w.shape, _bf16) for w in cast_w],
        scratch_shapes=[pltpu.VMEM((tile_rows, D_MODEL), _bf16),
                        pltpu.VMEM((IN_RING_SLOTS, D_MODEL, IN_COLS // IN_STEPS), _bf16),
                        pltpu.SemaphoreType.DMA((IN_RING_SLOTS,))],
        compiler_params=_params(("arbitrary", "arbitrary", "arbitrary")),
        name="in_proj",
    )(x3, g, w_in_b, gate_b, gate_b, *cast_w)


def _meta_proj_kernel(x_ref, g_ref, w_ref, u_ref, q_ref, h_scr):
    seg = IN_SEG_COLS

    @pl.when(pl.program_id(0) == 0)
    def _():
        h_scr[...] = _rms_norm(x_ref[...], g_ref[...]).astype(_bf16)

    proj = _dot(h_scr[...], w_ref[...])
    for s in range(seg // V7X_LANES):
        u_ref[s] = proj[:, s * V7X_LANES:(s + 1) * V7X_LANES].reshape(-1, S5_CHUNK, V7X_LANES)
    q_ref[...] = (proj[:, 3 * seg:4 * seg] * proj[:, seg:2 * seg]).astype(_bf16)


def _meta_proj(meta, g, w_in_b):
    seg = IN_SEG_COLS
    return pl.pallas_call(
        _meta_proj_kernel,
        grid=(IN_STEPS,),
        in_specs=[
            pl.BlockSpec((N_META, D_MODEL), lambda j: (0, 0)),
            pl.BlockSpec((1, D_MODEL), lambda j: (0, 0)),
            pl.BlockSpec((D_MODEL, 4 * seg), lambda j: (0, 2 * j + 1)),
        ],
        out_specs=[
            pl.BlockSpec((seg // V7X_LANES, META_CHUNKS, S5_CHUNK, V7X_LANES),
                         lambda j: (j, 0, 0, 0)),
            pl.BlockSpec((N_META, seg), lambda j: (0, j)),
        ],
        out_shape=[jax.ShapeDtypeStruct((S5_SLABS, META_CHUNKS, S5_CHUNK, V7X_LANES), _f32),
                   jax.ShapeDtypeStruct((N_META, CONV_WIDTH), _bf16)],
        scratch_shapes=[pltpu.VMEM((N_META, D_MODEL), _bf16)],
        compiler_params=_params(("arbitrary",)),
        name="meta_proj",
    )(meta, g, w_in_b)


def _pair_map(ref, d, part, q):
    rows = lax.broadcasted_iota(jnp.int32, (PAIR_LANES, V7X_LANES), 0)
    cols = lax.broadcasted_iota(jnp.int32, (PAIR_LANES, V7X_LANES), 1)
    same_group = (rows // S5_GROUP) == (cols // S5_STATE)
    return jnp.where(same_group, ref[d, 0, part, q * PAIR_LANES:(q + 1) * PAIR_LANES, :], 0.0)


def _transpose_pieces(src):
    assert SLAB_PAIRS == 4 and len(src) == SLAB_PAIRS
    lane = lax.broadcasted_iota(jnp.int32, src[0].shape, 1)

    def exchange(a, b, width):
        keep = (lane // width) % 2 == 0
        return (jnp.where(keep, a, pltpu.roll(b, width, 1)),
                jnp.where(keep, pltpu.roll(a, V7X_LANES - width, 1), b))

    b0, b2 = exchange(src[0], src[2], 2 * PAIR_LANES)
    b1, b3 = exchange(src[1], src[3], 2 * PAIR_LANES)
    c0, c1 = exchange(b0, b1, PAIR_LANES)
    c2, c3 = exchange(b2, b3, PAIR_LANES)
    return [c0, c1, c2, c3]


def _boundary_scan(x_scr, a_re, a_im, init):
    n_tiles = S5_CHUNK_ROWS // V7X_SUBLANES
    low = lax.broadcasted_iota(jnp.int32, (V7X_SUBLANES, V7X_LANES), 0) < BATCH

    def step(ar, ai, sr, si, xr, xi):
        return ar * sr - ai * si + xr, ar * si + ai * sr + xi

    def swap(v):
        return pltpu.roll(v, BATCH, 0)

    def body(i, carry):
        rf = pl.multiple_of(i * V7X_SUBLANES, V7X_SUBLANES)
        rb = pl.multiple_of((n_tiles - 1 - i) * V7X_SUBLANES, V7X_SUBLANES)
        loaded = []
        for q in range(SLAB_PAIRS):
            for part in range(2):
                lanes = slice(q * PAIR_COLS + part * PAIR_STATES,
                              q * PAIR_COLS + (part + 1) * PAIR_STATES)
                loaded.append((lanes, x_scr[0, pl.ds(rf, V7X_SUBLANES), lanes],
                               x_scr[1, pl.ds(rb, V7X_SUBLANES), lanes]))
        new, stores = [], []
        for q in range(SLAB_PAIRS):
            (re, xfr, xbr), (im, xfi, xbi) = loaded[2 * q], loaded[2 * q + 1]
            w1r, w1i = jnp.where(low, xfr, xbr), jnp.where(low, xfi, xbi)
            w2r, w2i = swap(jnp.where(low, xbr, xfr)), swap(jnp.where(low, xbi, xfi))
            cr, ci = carry[2 * q], carry[2 * q + 1]
            s1r, s1i = step(a_re[q], a_im[q], cr, ci, w1r, w1i)
            s2r, s2i = step(a_re[q], a_im[q], s1r, s1i, w2r, w2i)
            t1r, t1i = swap(s1r), swap(s1i)
            stores += [(0, rf, re, jnp.where(low, cr, t1r)), (0, rf, im, jnp.where(low, ci, t1i)),
                       (1, rb, re, jnp.where(low, t1r, cr)), (1, rb, im, jnp.where(low, t1i, ci))]
            new += [s2r, s2i]
        for d, r0, lanes, value in stores:
            x_scr[d, pl.ds(r0, V7X_SUBLANES), lanes] = value
        return tuple(new)

    lax.fori_loop(0, n_tiles, body, init)


def _s5_kernel(u_hbm, um_ref, b_ref, c_ref, p_ref, d_ref, *rest, n_cast):
    cast_in = rest[:n_cast]
    y_ref = rest[n_cast]
    cast_out = rest[n_cast + 1:2 * n_cast + 1]
    (ub_scr, uf_scr, w_scr, x_scr, sp_scr, tap_scr, toep_scr, xm_scr,
     u_buf, u_sem) = rest[2 * n_cast + 1:]
    for src, dst in zip(cast_in, cast_out):
        dst[...] = src[...].astype(_bf16)

    slab = pl.program_id(0)

    def u_copy(s):
        slot = s % S5_RING_SLOTS
        return pltpu.make_async_copy(u_hbm.at[s], u_buf.at[slot], u_sem.at[slot])

    @pl.when(slab == 0)
    def _():
        for s in range(S5_RING_SLOTS - 1):
            u_copy(s).start()

    @pl.when(slab + S5_RING_SLOTS - 1 < pl.num_programs(0))
    def _():
        u_copy(slab + S5_RING_SLOTS - 1).start()

    u_copy(slab).wait()
    u_ref = u_buf.at[slab % S5_RING_SLOTS]

    half_steps = S5_CHUNK // 2
    blk = PAIR_LANES

    def pair_cols(q):
        return slice(q * PAIR_COLS, (q + 1) * PAIR_COLS)

    def pair_states(q):
        return slice(q * PAIR_STATES, (q + 1) * PAIR_STATES)

    def strided_u(j):
        return u_ref[pl.ds(j, S5_CHUNK_ROWS, stride=S5_CHUNK), :]

    um = um_ref[0]
    um_q = [[] for _ in range(SLAB_PAIRS)]
    for h in range(2):
        steps = range(h * half_steps, (h + 1) * half_steps)
        for q, t in enumerate(_transpose_pieces([strided_u(j) for j in steps])):
            uf_scr[q, :, h * V7X_LANES:(h + 1) * V7X_LANES] = t
            ub_scr[q, :, h * V7X_LANES:(h + 1) * V7X_LANES] = t.astype(_bf16)
        meta = [um[:, j * V7X_LANES:(j + 1) * V7X_LANES] for j in steps]
        for q, t in enumerate(_transpose_pieces(meta)):
            um_q[q].append(t)

    for d in range(N_DIR):
        for q in range(SLAB_PAIRS):
            w_re = _pair_map(b_ref, d, 0, q)
            w_im = _pair_map(b_ref, d, 1, q)
            for j in range(S5_CHUNK):
                m = (S5_CHUNK - 1 - j) if d == 0 else j
                pr = p_ref[0, d, m, 0:1, pair_states(q)]
                pi = p_ref[0, d, m, 1:2, pair_states(q)]
                rows = slice(j * blk, (j + 1) * blk)
                w_scr[d, q, rows, 0:PAIR_STATES] = (w_re * pr - w_im * pi).astype(_bf16)
                w_scr[d, q, rows, PAIR_STATES:] = (w_re * pi + w_im * pr).astype(_bf16)
            x_scr[d, :, pair_cols(q)] = _dot(ub_scr[q], w_scr[d, q])
            if d == 0:
                xm_scr[:, pair_cols(q)] = _dot(
                    jnp.concatenate(um_q[q], axis=1).astype(_bf16), w_scr[0, q])

    low = lax.broadcasted_iota(jnp.int32, (V7X_SUBLANES, V7X_LANES), 0) < BATCH
    a_re, a_im, init = [], [], []
    for q in range(SLAB_PAIRS):
        re = slice(q * PAIR_COLS, q * PAIR_COLS + PAIR_STATES)
        im = slice(q * PAIR_COLS + PAIR_STATES, (q + 1) * PAIR_COLS)
        ar = p_ref[0, 0, S5_CHUNK, 0:1, pair_states(q)]
        ai = p_ref[0, 0, S5_CHUNK, 1:2, pair_states(q)]
        a_re.append(jnp.where(low, ar, p_ref[0, 1, S5_CHUNK, 0:1, pair_states(q)]))
        a_im.append(jnp.where(low, ai, p_ref[0, 1, S5_CHUNK, 1:2, pair_states(q)]))
        sr = jnp.zeros((1, V7X_LANES), _f32)
        si = jnp.zeros((1, V7X_LANES), _f32)
        for c in range(META_CHUNKS):
            xr = xm_scr[c:c + 1, re]
            xi = xm_scr[c:c + 1, im]
            sr, si = ar * sr - ai * si + xr, ar * si + ai * sr + xi
        init.append(jnp.where(low, sr, 0.0))
        init.append(jnp.where(low, si, 0.0))
    _boundary_scan(x_scr, a_re, a_im, tuple(init))
    for d in range(N_DIR):
        sp_scr[d] = x_scr[d].astype(_bf16)

    def out_weights(c_re, c_imn, d, m, q):
        pr = p_ref[0, d, m, 0:1, pair_states(q)]
        pi = p_ref[0, d, m, 1:2, pair_states(q)]
        return jnp.concatenate([c_re * pr + c_imn * pi, c_imn * pr - c_re * pi],
                               axis=1).astype(_bf16)

    for q in range(SLAB_PAIRS):
        wy = []
        for d in range(N_DIR):
            c_re = _pair_map(c_ref, d, 0, q)
            c_imn = _pair_map(c_ref, d, 1, q)
            c_map = jnp.concatenate([c_re, c_imn], axis=1).astype(_bf16)
            tap_scr[d] = _dot_nt(w_scr[d, q], jnp.concatenate([c_map] * S5_CHUNK, axis=0))
            powers = [o + 1 if d == 0 else S5_CHUNK - o for o in range(S5_CHUNK)]
            wy.append(jnp.concatenate([out_weights(c_re, c_imn, d, m, q) for m in powers],
                                      axis=0))
        for o in range(S5_CHUNK):
            lanes = slice(o * blk, (o + 1) * blk)
            if o:
                toep_scr[0:o * blk, lanes] = tap_scr[0, (S5_CHUNK - 1 - o) * blk:
                                                     (S5_CHUNK - 1) * blk, lanes]
            toep_scr[o * blk:(o + 1) * blk, lanes] = (
                tap_scr[0, (S5_CHUNK - 1) * blk:, lanes] + tap_scr[1, 0:blk, lanes])
            if o < S5_CHUNK - 1:
                toep_scr[(o + 1) * blk:, lanes] = tap_scr[1, blk:(S5_CHUNK - o) * blk, lanes]
        y = _dot_nt(sp_scr[0, :, pair_cols(q)], wy[0])
        y += _dot_nt(sp_scr[1, :, pair_cols(q)], wy[1])
        y += _dot(ub_scr[q], toep_scr[...].astype(_bf16))
        y += uf_scr[q] * d_ref[0, q]
        for h in range(2):
            y_ref[0, q, h] = y[:, h * V7X_LANES:(h + 1) * V7X_LANES]


def _s5(u, u_meta, b_maps, c_maps, ptab, d_skip, cast=()):
    maps = pl.BlockSpec((N_DIR, 1, 2, V7X_LANES, V7X_LANES), lambda s: (0, s, 0, 0, 0))
    cast_specs = [pl.BlockSpec((w.shape[0] // S5_SLABS, w.shape[1]), lambda s: (s, 0))
                  for w in cast]
    return pl.pallas_call(
        functools.partial(_s5_kernel, n_cast=len(cast)),
        grid=(S5_SLABS,),
        in_specs=[
            pl.BlockSpec(memory_space=pl.ANY),
            pl.BlockSpec((1, V7X_SUBLANES, S5_CHUNK_COLS), lambda s: (s, 0, 0)),
            maps,
            maps,
            pl.BlockSpec((1, N_DIR, S5_CHUNK + 1, 2, SLAB_STATES), lambda s: (s, 0, 0, 0, 0)),
            pl.BlockSpec((1, SLAB_PAIRS, 1, PAIR_COLS), lambda s: (s, 0, 0, 0)),
        ] + cast_specs,
        out_specs=[pl.BlockSpec((1, SLAB_PAIRS, 2, S5_CHUNK_ROWS, V7X_LANES),
                                lambda s: (s, 0, 0, 0, 0))] + cast_specs,
        out_shape=[jax.ShapeDtypeStruct((S5_SLABS, SLAB_PAIRS, 2, S5_CHUNK_ROWS, V7X_LANES),
                                        _f32)]
                  + [jax.ShapeDtypeStruct(w.shape, _bf16) for w in cast],
        scratch_shapes=[
            pltpu.VMEM((SLAB_PAIRS, S5_CHUNK_ROWS, PAIR_COLS), _bf16),
            pltpu.VMEM((SLAB_PAIRS, S5_CHUNK_ROWS, PAIR_COLS), _f32),
            pltpu.VMEM((N_DIR, SLAB_PAIRS, PAIR_COLS, PAIR_COLS), _bf16),
            pltpu.VMEM((N_DIR, S5_CHUNK_ROWS, 2 * SLAB_STATES), _f32),
            pltpu.VMEM((N_DIR, S5_CHUNK_ROWS, 2 * SLAB_STATES), _bf16),
            pltpu.VMEM((N_DIR, PAIR_COLS, PAIR_COLS), _f32),
            pltpu.VMEM((PAIR_COLS, PAIR_COLS), _f32),
            pltpu.VMEM((V7X_SUBLANES, 2 * SLAB_STATES), _f32),
            pltpu.VMEM((S5_RING_SLOTS, S5_CHUNK_ROWS * S5_CHUNK, V7X_LANES), _f32),
            pltpu.SemaphoreType.DMA((S5_RING_SLOTS,)),
        ],
        compiler_params=_params(("arbitrary",)),
        name="s5",
    )(u, u_meta, b_maps, c_maps, ptab, d_skip, *cast)


def _mixer_kernel(h_ref, y_ref, q_ref, qp_ref, qn_ref, qm_ref, bc_ref, gs_ref, gc_ref,
                  wglu_ref, bglu_ref, wsup_ref, cw_ref, cb_ref, wcup_ref, wo_ref,
                  g_ref, wcast_ref, o_ref, wcast_out_ref, ys_scr, q_scr):
    t = pl.program_id(0)
    last = pl.num_programs(0) - 1
    tt = MIX_TILE_T
    tm = BATCH * tt
    n_c = tt // S5_CHUNK

    def columns(ref, b):
        return jnp.concatenate([ref[j, b] for j in range(IN_STEPS)], axis=1)

    def all_rows(ref):
        return jnp.concatenate([ref[j].reshape(tm, -1) for j in range(IN_STEPS)], axis=1)

    vs = []
    for b in range(BATCH):
        prev = jnp.where(t == 0, qm_ref[...], columns(qp_ref, b)).astype(_f32)
        nxt = jnp.where(t == last, 0.0, columns(qn_ref, b).astype(_f32))
        q_scr[b, 0:HALO_ROWS, :] = prev
        q_scr[b, HALO_ROWS:HALO_ROWS + tt, :] = columns(q_ref, b).astype(_f32)
        q_scr[b, HALO_ROWS + tt:, :] = nxt
        vs.append(cw_ref[0:1, :] * q_scr[b, HALO_ROWS - 1:HALO_ROWS - 1 + tt, :]
                  + cw_ref[1:2, :] * q_scr[b, HALO_ROWS:HALO_ROWS + tt, :]
                  + cw_ref[2:3, :] * q_scr[b, HALO_ROWS + 1:HALO_ROWS + 1 + tt, :]
                  + cb_ref[...])
    v = jnp.concatenate(vs, axis=0)
    bc = all_rows(bc_ref).astype(_f32)
    y_c = _dot((bc * v).astype(_bf16), wcup_ref[...])

    half_steps = S5_CHUNK // 2
    for s in range(S5_SLABS):
        for h in range(2):
            for b in range(BATCH):
                tiles = [y_ref[s, q, h, pl.ds(b, n_c, stride=BATCH), :]
                         for q in range(SLAB_PAIRS)]
                for k, tile in enumerate(_transpose_pieces(tiles)):
                    o = h * half_steps + k
                    ys_scr[s, pl.ds(b * tt + o, n_c, stride=S5_CHUNK), :] = tile
    ys = jnp.concatenate([ys_scr[s] for s in range(S5_SLABS)], axis=1)
    ys = jax.nn.gelu(ys)
    glu = _dot(ys.astype(_bf16), wglu_ref[...]) + bglu_ref[...]
    ys = ys * jax.nn.sigmoid(glu)
    y_s = _dot(ys.astype(_bf16), wsup_ref[...])

    merged = all_rows(gs_ref).astype(_f32) * y_s + all_rows(gc_ref).astype(_f32) * y_c
    mixed = _dot(merged.astype(_bf16), wo_ref[...])
    out = h_ref[...].reshape(tm, D_MODEL) + _rms_norm(mixed, g_ref[...])
    o_ref[...] = out.reshape(BATCH, tt, D_MODEL)
    wcast_out_ref[...] = wcast_ref[...].astype(_bf16)


def _mixer(x, y, q, q_meta, bc, gate_s, gate_c, w_glu, b_glu, w_s_up, conv_w, conv_b, w_c_up,
           w_o, g_post, w_cast):
    tt = MIX_TILE_T
    n_t = SEQ // tt
    cast_spec = pl.BlockSpec((w_cast.shape[0] // n_t, w_cast.shape[1]), lambda t: (t, 0))
    halo_per_tile = tt // HALO_ROWS
    n_halo = SEQ // HALO_ROWS
    tile = lambda t: (0, t, 0)
    stepped = lambda rows, cols: pl.BlockSpec((IN_STEPS, BATCH, rows, cols),
                                              lambda t: (0, 0, t, 0))
    resident = functools.partial(pl.BlockSpec, index_map=lambda t: (0, 0),
                                 pipeline_mode=pl.Buffered(1))
    return pl.pallas_call(
        _mixer_kernel,
        grid=(n_t,),
        in_specs=[
            pl.BlockSpec((BATCH, tt, D_MODEL), tile),
            pl.BlockSpec((S5_SLABS, SLAB_PAIRS, 2, tt // S5_CHUNK * BATCH, V7X_LANES),
                         lambda t: (0, 0, 0, t, 0)),
            stepped(tt, IN_SEG_COLS),
            pl.BlockSpec((IN_STEPS, BATCH, HALO_ROWS, IN_SEG_COLS),
                         lambda t: (0, 0, jnp.maximum(t * halo_per_tile - 1, 0), 0)),
            pl.BlockSpec((IN_STEPS, BATCH, HALO_ROWS, IN_SEG_COLS),
                         lambda t: (0, 0, jnp.minimum((t + 1) * halo_per_tile, n_halo - 1), 0)),
            resident((HALO_ROWS, CONV_WIDTH)),
            stepped(tt, IN_SEG_COLS),
            stepped(tt, 2 * IN_SEG_COLS),
            stepped(tt, 2 * IN_SEG_COLS),
            resident((S5_WIDTH, S5_WIDTH)),
            resident((1, S5_WIDTH)),
            resident((S5_WIDTH, D_MODEL)),
            resident((3, CONV_WIDTH)),
            resident((1, CONV_WIDTH)),
            resident((CONV_WIDTH, D_MODEL)),
            resident((D_MODEL, D_MODEL)),
            resident((1, D_MODEL)),
            cast_spec,
        ],
        out_specs=[pl.BlockSpec((BATCH, tt, D_MODEL), tile), cast_spec],
        out_shape=[jax.ShapeDtypeStruct((BATCH, SEQ, D_MODEL), _f32),
                   jax.ShapeDtypeStruct(w_cast.shape, _bf16)],
        scratch_shapes=[pltpu.VMEM((S5_SLABS, BATCH * tt, V7X_LANES), _f32),
                        pltpu.VMEM((BATCH, tt + 2 * HALO_ROWS, CONV_WIDTH), _f32)],
        compiler_params=_params(("arbitrary",)),
        name="mixer_tail",
    )(x, y, q, q, q, q_meta, bc, gate_s, gate_c, w_glu, b_glu, w_s_up, conv_w, conv_b, w_c_up,
      w_o, g_post, w_cast)


def _ffn_kernel(h_ref, gpre_ref, wgu_ref, wout_ref, gpost_ref, o_ref, hb_scr):
    j = pl.program_id(1)
    last = pl.num_programs(1) - 1

    def chunk(hb):
        gu = _dot(hb, wgu_ref[...])
        gate, up = gu[:, :FFN_TILE_HIDDEN], gu[:, FFN_TILE_HIDDEN:]
        return _dot((jax.nn.silu(gate) * up).astype(_bf16), wout_ref[...])

    @pl.when(j == 0)
    def _():
        hb = _rms_norm(h_ref[...], gpre_ref[...]).astype(_bf16)
        hb_scr[...] = hb
        o_ref[...] = chunk(hb)

    @pl.when(jnp.logical_and(j > 0, j < last))
    def _():
        o_ref[...] += chunk(hb_scr[...])

    @pl.when(j == last)
    def _():
        f = o_ref[...] + chunk(hb_scr[...])
        o_ref[...] = h_ref[...] + _rms_norm(f, gpost_ref[...])


def _ffn(h1, g_pre, w_in, w_out, g_post):
    tm, th = FFN_TILE_ROWS, FFN_TILE_HIDDEN
    n_hidden = FFN_HIDDEN // th
    return pl.pallas_call(
        _ffn_kernel,
        grid=(ROWS // tm, n_hidden),
        in_specs=[
            pl.BlockSpec((tm, D_MODEL), lambda i, j: (i, 0)),
            pl.BlockSpec((1, D_MODEL), lambda i, j: (0, 0)),
            pl.BlockSpec((D_MODEL, 2 * th), lambda i, j: (0, j)),
            pl.BlockSpec((th, D_MODEL), lambda i, j: (j, 0)),
            pl.BlockSpec((1, D_MODEL), lambda i, j: (0, 0)),
        ],
        out_specs=pl.BlockSpec((tm, D_MODEL), lambda i, j: (i, 0)),
        out_shape=jax.ShapeDtypeStruct((ROWS, D_MODEL), _f32),
        scratch_shapes=[pltpu.VMEM((tm, D_MODEL), _bf16)],
        compiler_params=_params(("arbitrary", "arbitrary")),
        name="ffn",
    )(h1, g_pre, w_in, w_out, g_post)


def kernel(x, meta, g_mix_pre, g_mix_post, g_ffn_pre, g_ffn_post, w_in, gate_b, lam_re, lam_im,
           log_dt, b_re, b_im, c_re, c_im, d_skip, w_glu, b_glu, w_s_up, conv_w, conv_b, w_c_up,
           w_o, w_ffn_in, w_ffn_out):
    l = 0
    w_in_b = _cast_w_in(w_in[l])
    gate_b2 = gate_b[l].reshape(2, 1, D_MODEL)
    g_pre = g_mix_pre[l].reshape(1, D_MODEL)

    ptab, bz_re, bz_im = _discretise(lam_re[l], lam_im[l], log_dt[l], b_re[l], b_im[l])
    dg = N_DIR * S5_GROUPS
    b_maps = _compact_maps(bz_re, bz_im)
    c_maps = _compact_maps(c_re[l].reshape(dg, S5_GROUP, S5_STATE),
                           -c_im[l].reshape(dg, S5_GROUP, S5_STATE))
    ptab = ptab.reshape(S5_CHUNK + 1, 2, N_DIR, S5_SLABS, SLAB_STATES)
    ptab = ptab.transpose(3, 2, 0, 1, 4)

    (u5, bc, q, gate_s, gate_c, w_ffn_in_b, w_glu_b, w_s_up_b, w_c_up_b, w_o_b) = _in_proj(
        x, g_pre, w_in_b, gate_b2, IN_TILE_ROWS,
        cast=((w_ffn_in[l], _ffn_in_plan()), (w_glu[l], None), (w_s_up[l], None),
              (w_c_up[l], None), (w_o[l], None)))
    um5, q_meta = _meta_proj(meta.astype(x.dtype), g_pre, w_in_b)

    u = u5.reshape(S5_SLABS, S5_CHUNK_ROWS * S5_CHUNK, V7X_LANES)
    u_meta = jnp.pad(um5.reshape(S5_SLABS, META_CHUNKS, S5_CHUNK_COLS),
                     ((0, 0), (0, V7X_SUBLANES - META_CHUNKS), (0, 0)))
    d_pairs = jnp.tile(d_skip[l].reshape(S5_SLABS, SLAB_PAIRS, 1, PAIR_LANES),
                       (1, 1, 1, S5_CHUNK))
    (y,) = _s5(u, u_meta, b_maps, c_maps, ptab, d_pairs)

    h1, w_ffn_out_b = _mixer(x, y, q, q_meta, bc, gate_s, gate_c, w_glu_b,
                             b_glu[l].reshape(1, S5_WIDTH), w_s_up_b, conv_w[l],
                             conv_b[l].reshape(1, CONV_WIDTH), w_c_up_b, w_o_b,
                             g_mix_post[l].reshape(1, D_MODEL), w_ffn_out[l])

    h2 = _ffn(h1.reshape(ROWS, D_MODEL), g_ffn_pre[l].reshape(1, D_MODEL),
              w_ffn_in_b, w_ffn_out_b, g_ffn_post[l].reshape(1, D_MODEL))
    return h2.reshape(BATCH, SEQ, D_MODEL)
```

```python
import functools
import math

import jax
import jax.numpy as jnp
from jax import lax
from jax.experimental import pallas as pl
from jax.experimental.pallas import tpu as pltpu

D_MODEL = 2048
BATCH = 4
SEQ = 2048
N_META = 16
S5_WIDTH = 1024
S5_GROUP = 16
S5_GROUPS = S5_WIDTH // S5_GROUP
S5_STATE = 64
N_DIR = 2
CONV_WIDTH = 1024
FFN_HIDDEN = ((math.ceil(8 * D_MODEL / 3) + 255) // 256) * 256
IN_COLS = S5_WIDTH + 3 * CONV_WIDTH + 2 * D_MODEL
RMS_EPS = 1e-6
LAM_RE_MAX = -1e-4

ROWS = SEQ * BATCH

V7X_SUBLANES = 8
V7X_LANES = 128
V7X_MXU_DIM = 256
V7X_VMEM_LIMIT_BYTES = 56 * 1024 * 1024

S5_SLABS = S5_WIDTH // V7X_LANES
SLAB_GROUPS = V7X_LANES // S5_GROUP
SLAB_STATES = SLAB_GROUPS * S5_STATE
PAIR_LANES = 2 * S5_GROUP
SLAB_PAIRS = V7X_LANES // PAIR_LANES
PAIR_STATES = 2 * S5_STATE
S5_CHUNK = V7X_SUBLANES
PAIR_COLS = S5_CHUNK * PAIR_LANES
assert PAIR_COLS == V7X_MXU_DIM
S5_CHUNKS = SEQ // S5_CHUNK
S5_CHUNK_ROWS = S5_CHUNKS * BATCH
S5_CHUNK_COLS = S5_CHUNK * V7X_LANES
META_CHUNKS = N_META // S5_CHUNK
STATE_REPEATS = V7X_LANES // S5_STATE

IN_TILE_ROWS = 1024
IN_STEPS = 8
IN_SEG_COLS = S5_WIDTH // IN_STEPS
IN_RING_SLOTS = 3
CAST_TILE_ROWS = 128
MIX_TILE_T = 64
FFN_TILE_ROWS = 1024
FFN_TILE_HIDDEN = 256
HALO_ROWS = 16

_f32 = jnp.float32
_bf16 = jnp.bfloat16


def _rms_norm(xf, g):
    r = lax.rsqrt(jnp.mean(xf * xf, axis=-1, keepdims=True) + RMS_EPS)
    return xf * r * g


def _params(sem):
    return pltpu.CompilerParams(dimension_semantics=sem,
                                vmem_limit_bytes=V7X_VMEM_LIMIT_BYTES)


def _dot(a, b):
    return jnp.dot(a, b, preferred_element_type=_f32)


def _dot_nt(a, b):
    return lax.dot_general(a, b, (((1,), (1,)), ((), ())), preferred_element_type=_f32)


def _discretise_kernel(lam_re_ref, lam_im_ref, log_dt_ref, b_re_ref, b_im_ref,
                       pow_ref, bz_re_ref, bz_im_ref):
    lr = jnp.minimum(lam_re_ref[...], LAM_RE_MAX)
    li = lam_im_ref[...]
    dt = jnp.exp(log_dt_ref[...])
    mag = jnp.exp(lr * dt)
    ar = mag * jnp.cos(li * dt)
    ai = mag * jnp.sin(li * dt)
    den = lr * lr + li * li
    nr = ar - 1.0
    zr = (nr * lr + ai * li) / den
    zi = (ai * lr - nr * li) / den
    pr = jnp.ones_like(ar)
    pi = jnp.zeros_like(ar)
    for m in range(S5_CHUNK + 1):
        pow_ref[m, 0] = pr
        pow_ref[m, 1] = pi
        pr, pi = pr * ar - pi * ai, pr * ai + pi * ar
    b_re = b_re_ref[...]
    b_im = b_im_ref[...]
    bz_re_ref[...] = zr[:, None, :] * b_re - zi[:, None, :] * b_im
    bz_im_ref[...] = zr[:, None, :] * b_im + zi[:, None, :] * b_re


def _discretise(lam_re, lam_im, log_dt, b_re, b_im):
    dg = N_DIR * S5_GROUPS
    b_re_t = jnp.swapaxes(b_re, -1, -2).reshape(dg, S5_GROUP, S5_STATE)
    b_im_t = jnp.swapaxes(b_im, -1, -2).reshape(dg, S5_GROUP, S5_STATE)
    return pl.pallas_call(
        _discretise_kernel,
        out_shape=(jax.ShapeDtypeStruct((S5_CHUNK + 1, 2, dg, S5_STATE), _f32),
                   jax.ShapeDtypeStruct((dg, S5_GROUP, S5_STATE), _f32),
                   jax.ShapeDtypeStruct((dg, S5_GROUP, S5_STATE), _f32)),
        name="s5_discretise",
    )(lam_re.reshape(dg, S5_STATE), lam_im.reshape(dg, S5_STATE),
      log_dt.reshape(dg, 1), b_re_t, b_im_t)


def _w_in_pieces(j):
    seg, gseg = IN_SEG_COLS, 2 * IN_SEG_COLS
    pieces = [(4 * S5_WIDTH + n * D_MODEL + j * gseg, gseg) for n in range(2)]
    pieces += [(n * S5_WIDTH + j * seg, seg) for n in range(4)]
    return pieces


def _cast_w_in_kernel(w_ref, o_ref):
    dst = 0
    for j in range(IN_STEPS):
        for src, width in _w_in_pieces(j):
            o_ref[:, dst:dst + width] = w_ref[:, src:src + width].astype(_bf16)
            dst += width


def _cast_w_in(w):
    spec = pl.BlockSpec((CAST_TILE_ROWS, IN_COLS), lambda i: (i, 0))
    return pl.pallas_call(
        _cast_w_in_kernel,
        grid=(D_MODEL // CAST_TILE_ROWS,),
        in_specs=[spec],
        out_specs=spec,
        out_shape=jax.ShapeDtypeStruct(w.shape, _bf16),
        compiler_params=_params(("arbitrary",)),
        name="cast_w_in",
    )(w)


def _ffn_in_plan():
    th = FFN_TILE_HIDDEN
    plan = []
    for j in range(FFN_HIDDEN // th):
        plan.append((j * th, 2 * j * th, th))
        plan.append((FFN_HIDDEN + j * th, (2 * j + 1) * th, th))
    return plan


def _in_proj_kernel(x_ref, m_ref, g_ref, w_hbm, gbs_ref, gbc_ref, *rest, cast_plans):
    n_cast = len(cast_plans)
    cast_in = rest[:n_cast]
    u_ref, bc_ref, q_ref, gs_ref, gc_ref, um_ref, qm_ref = rest[n_cast:n_cast + 7]
    cast_out = rest[n_cast + 7:2 * n_cast + 7]
    h_scr, w_buf, w_sem = rest[2 * n_cast + 7:]
    seg = IN_SEG_COLS
    tile_cols = IN_COLS // IN_STEPS
    tm = x_ref.shape[0]

    step = ((pl.program_id(0) * pl.num_programs(1) + pl.program_id(1)) * IN_STEPS
            + pl.program_id(2))
    n_steps = pl.num_programs(0) * pl.num_programs(1) * IN_STEPS

    def w_copy(s):
        cols = pl.ds(pl.multiple_of((s % IN_STEPS) * tile_cols, tile_cols), tile_cols)
        slot = s % IN_RING_SLOTS
        return pltpu.make_async_copy(w_hbm.at[:, cols], w_buf.at[slot], w_sem.at[slot])

    @pl.when(step == 0)
    def _():
        for s in range(IN_RING_SLOTS - 1):
            w_copy(s).start()

    @pl.when(step + IN_RING_SLOTS - 1 < n_steps)
    def _():
        w_copy(step + IN_RING_SLOTS - 1).start()

    w_copy(step).wait()
    w_ref = w_buf.at[step % IN_RING_SLOTS]

    def column_step(h):
        both = _dot(h, w_ref[...])
        proj, meta = both[:tm], both[tm:]
        gs_ref[...] = jax.nn.sigmoid(proj[:, 0:2 * seg] + gbs_ref[0]).astype(_bf16)
        gc_ref[...] = jax.nn.sigmoid(proj[:, 2 * seg:4 * seg] + gbc_ref[0]).astype(_bf16)
        for s in range(seg // V7X_LANES):
            lanes = slice(4 * seg + s * V7X_LANES, 4 * seg + (s + 1) * V7X_LANES)
            u_ref[s] = proj[:, lanes].reshape(-1, S5_CHUNK, V7X_LANES)
            um_ref[s] = meta[:, lanes].reshape(-1, S5_CHUNK, V7X_LANES)
        bc_ref[...] = proj[:, 6 * seg:7 * seg].astype(_bf16)
        q_ref[...] = (proj[:, 7 * seg:8 * seg] * proj[:, 5 * seg:6 * seg]).astype(_bf16)
        qm_ref[...] = (meta[:, 7 * seg:8 * seg] * meta[:, 5 * seg:6 * seg]).astype(_bf16)
        for src, dst, plan in zip(cast_in, cast_out, cast_plans):
            for s0, d0, width in plan:
                dst[:, d0:d0 + width] = src[:, s0:s0 + width].astype(_bf16)

    @pl.when(pl.program_id(2) == 0)
    def _():
        h = jnp.concatenate([_rms_norm(x_ref[...], g_ref[...]),
                             _rms_norm(m_ref[...], g_ref[...])], axis=0).astype(_bf16)
        h_scr[...] = h
        column_step(h)

    @pl.when(pl.program_id(2) > 0)
    def _():
        column_step(h_scr[...])


def _in_proj(x3, meta, g, w_in_b, gate_b, tile_rows, cast=()):
    nb, t_len, _ = x3.shape
    n_t = t_len // tile_rows
    n_steps = nb * n_t * IN_STEPS
    seg, gseg = IN_SEG_COLS, 2 * IN_SEG_COLS
    by_step = lambda b, t, j: (j, b, t, 0)
    step = lambda b, t, j: ((b * n_t + t) * IN_STEPS + j, 0)
    cast_w = [w for w, _ in cast]
    cast_plans = tuple(tuple(plan) if plan else ((0, 0, w.shape[1]),) for w, plan in cast)
    cast_specs = [pl.BlockSpec((w.shape[0] // n_steps, w.shape[1]), step) for w in cast_w]
    gb = lambda n: pl.BlockSpec((1, 1, gseg), lambda b, t, j: (n, 0, j))
    return pl.pallas_call(
        functools.partial(_in_proj_kernel, cast_plans=cast_plans),
        grid=(nb, n_t, IN_STEPS),
        in_specs=[
            pl.BlockSpec((None, tile_rows, D_MODEL), lambda b, t, j: (b, t, 0)),
            pl.BlockSpec((N_META, D_MODEL), lambda b, t, j: (0, 0)),
            pl.BlockSpec((1, D_MODEL), lambda b, t, j: (0, 0)),
            pl.BlockSpec(memory_space=pl.ANY),
            gb(0), gb(1),
        ] + cast_specs,
        out_specs=[
            pl.BlockSpec((seg // V7X_LANES, tile_rows // S5_CHUNK, None, S5_CHUNK, V7X_LANES),
                         lambda b, t, j: (j, t, b, 0, 0)),
            pl.BlockSpec((None, None, tile_rows, seg), by_step),
            pl.BlockSpec((None, None, tile_rows, seg), by_step),
            pl.BlockSpec((None, None, tile_rows, gseg), by_step),
            pl.BlockSpec((None, None, tile_rows, gseg), by_step),
            pl.BlockSpec((None, seg // V7X_LANES, META_CHUNKS, S5_CHUNK, V7X_LANES),
                         lambda b, t, j: (b * n_t + t, j, 0, 0, 0)),
            pl.BlockSpec((None, N_META, seg), lambda b, t, j: (b * n_t + t, 0, j)),
        ] + cast_specs,
        out_shape=[jax.ShapeDtypeStruct(
                       (S5_SLABS, t_len // S5_CHUNK, nb, S5_CHUNK, V7X_LANES), _f32),
                   jax.ShapeDtypeStruct((IN_STEPS, nb, t_len, seg), _bf16),
                   jax.ShapeDtypeStruct((IN_STEPS, nb, t_len, seg), _bf16),
                   jax.ShapeDtypeStruct((IN_STEPS, nb, t_len, gseg), _bf16),
                   jax.ShapeDtypeStruct((IN_STEPS, nb, t_len, gseg), _bf16),
                   jax.ShapeDtypeStruct((nb * n_t, S5_SLABS, META_CHUNKS, S5_CHUNK, V7X_LANES),
                                        _f32),
                   jax.ShapeDtypeStruct((nb * n_t, N_META, CONV_WIDTH), _bf16)]
                  + [jax.ShapeDtypeStruct(w.shape, _bf16) for w in cast_w],
        scratch_shapes=[pltpu.VMEM((tile_rows + N_META, D_MODEL), _bf16),
                        pltpu.VMEM((IN_RING_SLOTS, D_MODEL, IN_COLS // IN_STEPS), _bf16),
                        pltpu.SemaphoreType.DMA((IN_RING_SLOTS,))],
        compiler_params=_params(("arbitrary", "arbitrary", "arbitrary")),
        name="in_proj",
    )(x3, meta, g, w_in_b, gate_b, gate_b, *cast_w)


def _pair_map(ref, d, q):
    rows = lax.broadcasted_iota(jnp.int32, (PAIR_LANES, V7X_LANES), 0)
    cols = lax.broadcasted_iota(jnp.int32, (PAIR_LANES, V7X_LANES), 1)
    same_group = (rows // S5_GROUP) == (cols // S5_STATE)
    per_group = ref[d, q * PAIR_LANES:(q + 1) * PAIR_LANES, :]
    return jnp.where(same_group, jnp.concatenate([per_group] * STATE_REPEATS, axis=1), 0.0)


def _transpose_pieces(src):
    assert SLAB_PAIRS == 4 and len(src) == SLAB_PAIRS
    lane = lax.broadcasted_iota(jnp.int32, src[0].shape, 1)

    def exchange(a, b, width):
        keep = (lane // width) % 2 == 0
        return (jnp.where(keep, a, pltpu.roll(b, width, 1)),
                jnp.where(keep, pltpu.roll(a, V7X_LANES - width, 1), b))

    b0, b2 = exchange(src[0], src[2], 2 * PAIR_LANES)
    b1, b3 = exchange(src[1], src[3], 2 * PAIR_LANES)
    c0, c1 = exchange(b0, b1, PAIR_LANES)
    c2, c3 = exchange(b2, b3, PAIR_LANES)
    return [c0, c1, c2, c3]


def _boundary_scan(x_scr, a_re, a_im, init):
    n_tiles = S5_CHUNK_ROWS // V7X_SUBLANES
    low = lax.broadcasted_iota(jnp.int32, (V7X_SUBLANES, V7X_LANES), 0) < BATCH

    def step(ar, ai, sr, si, xr, xi):
        return ar * sr - ai * si + xr, ar * si + ai * sr + xi

    def swap(v):
        return pltpu.roll(v, BATCH, 0)

    def body(i, carry):
        rf = pl.multiple_of(i * V7X_SUBLANES, V7X_SUBLANES)
        rb = pl.multiple_of((n_tiles - 1 - i) * V7X_SUBLANES, V7X_SUBLANES)
        loaded = []
        for q in range(SLAB_PAIRS):
            for part in range(2):
                lanes = slice(q * PAIR_COLS + part * PAIR_STATES,
                              q * PAIR_COLS + (part + 1) * PAIR_STATES)
                loaded.append((lanes, x_scr[0, pl.ds(rf, V7X_SUBLANES), lanes],
                               x_scr[1, pl.ds(rb, V7X_SUBLANES), lanes]))
        new, stores = [], []
        for q in range(SLAB_PAIRS):
            (re, xfr, xbr), (im, xfi, xbi) = loaded[2 * q], loaded[2 * q + 1]
            w1r, w1i = jnp.where(low, xfr, xbr), jnp.where(low, xfi, xbi)
            w2r, w2i = swap(jnp.where(low, xbr, xfr)), swap(jnp.where(low, xbi, xfi))
            cr, ci = carry[2 * q], carry[2 * q + 1]
            s1r, s1i = step(a_re[q], a_im[q], cr, ci, w1r, w1i)
            s2r, s2i = step(a_re[q], a_im[q], s1r, s1i, w2r, w2i)
            t1r, t1i = swap(s1r), swap(s1i)
            stores += [(0, rf, re, jnp.where(low, cr, t1r)), (0, rf, im, jnp.where(low, ci, t1i)),
                       (1, rb, re, jnp.where(low, t1r, cr)), (1, rb, im, jnp.where(low, t1i, ci))]
            new += [s2r, s2i]
        for d, r0, lanes, value in stores:
            x_scr[d, pl.ds(r0, V7X_SUBLANES), lanes] = value
        return tuple(new)

    lax.fori_loop(0, n_tiles, body, init)


def _s5_kernel(u_ref, um_ref, bre_ref, bim_ref, cre_ref, cim_ref, p_ref, d_ref, *rest, n_cast):
    cast_in = rest[:n_cast]
    y_ref = rest[n_cast]
    cast_out = rest[n_cast + 1:2 * n_cast + 1]
    ub_scr, uf_scr, w_scr, x_scr, sp_scr, tap_scr, toep_scr, xm_scr = rest[2 * n_cast + 1:]
    for src, dst in zip(cast_in, cast_out):
        dst[...] = src[...].astype(_bf16)

    half_steps = S5_CHUNK // 2
    blk = PAIR_LANES

    def pair_cols(q):
        return slice(q * PAIR_COLS, (q + 1) * PAIR_COLS)

    @functools.cache
    def power(d, m, part, q):
        g = q * (PAIR_LANES // S5_GROUP)
        return jnp.concatenate([p_ref[m, part, d, g + k:g + k + 1, :]
                                for k in range(PAIR_LANES // S5_GROUP)], axis=1)

    def strided_u(j):
        return u_ref[0, pl.ds(j, S5_CHUNK_ROWS, stride=S5_CHUNK), :]

    um = um_ref[0]
    um_q = [[] for _ in range(SLAB_PAIRS)]
    for h in range(2):
        steps = range(h * half_steps, (h + 1) * half_steps)
        for q, t in enumerate(_transpose_pieces([strided_u(j) for j in steps])):
            uf_scr[q, :, h * V7X_LANES:(h + 1) * V7X_LANES] = t
            ub_scr[q, :, h * V7X_LANES:(h + 1) * V7X_LANES] = t.astype(_bf16)
        meta = [um[:, j * V7X_LANES:(j + 1) * V7X_LANES] for j in steps]
        for q, t in enumerate(_transpose_pieces(meta)):
            um_q[q].append(t)

    for d in range(N_DIR):
        for q in range(SLAB_PAIRS):
            w_re = _pair_map(bre_ref, d, q)
            w_im = _pair_map(bim_ref, d, q)
            for j in range(S5_CHUNK):
                m = (S5_CHUNK - 1 - j) if d == 0 else j
                pr = power(d, m, 0, q)
                pi = power(d, m, 1, q)
                rows = slice(j * blk, (j + 1) * blk)
                w_scr[d, q, rows, 0:PAIR_STATES] = (w_re * pr - w_im * pi).astype(_bf16)
                w_scr[d, q, rows, PAIR_STATES:] = (w_re * pi + w_im * pr).astype(_bf16)
            x_scr[d, :, pair_cols(q)] = _dot(ub_scr[q], w_scr[d, q])
            if d == 0:
                xm_scr[:, pair_cols(q)] = _dot(
                    jnp.concatenate(um_q[q], axis=1).astype(_bf16), w_scr[0, q])

    low = lax.broadcasted_iota(jnp.int32, (V7X_SUBLANES, V7X_LANES), 0) < BATCH
    a_re, a_im, init = [], [], []
    for q in range(SLAB_PAIRS):
        re = slice(q * PAIR_COLS, q * PAIR_COLS + PAIR_STATES)
        im = slice(q * PAIR_COLS + PAIR_STATES, (q + 1) * PAIR_COLS)
        ar = power(0, S5_CHUNK, 0, q)
        ai = power(0, S5_CHUNK, 1, q)
        a_re.append(jnp.where(low, ar, power(1, S5_CHUNK, 0, q)))
        a_im.append(jnp.where(low, ai, power(1, S5_CHUNK, 1, q)))
        sr = jnp.zeros((1, V7X_LANES), _f32)
        si = jnp.zeros((1, V7X_LANES), _f32)
        for c in range(META_CHUNKS):
            xr = xm_scr[c:c + 1, re]
            xi = xm_scr[c:c + 1, im]
            sr, si = ar * sr - ai * si + xr, ar * si + ai * sr + xi
        init.append(jnp.where(low, sr, 0.0))
        init.append(jnp.where(low, si, 0.0))
    _boundary_scan(x_scr, a_re, a_im, tuple(init))
    for d in range(N_DIR):
        sp_scr[d] = x_scr[d].astype(_bf16)

    def out_weights(c_re, c_imn, d, m, q):
        pr = power(d, m, 0, q)
        pi = power(d, m, 1, q)
        return jnp.concatenate([c_re * pr + c_imn * pi, c_imn * pr - c_re * pi],
                               axis=1).astype(_bf16)

    piece = lax.broadcasted_iota(jnp.int32, (1, V7X_LANES), 1) // PAIR_LANES

    for q in range(SLAB_PAIRS):
        wy = []
        for d in range(N_DIR):
            c_re = _pair_map(cre_ref, d, q)
            c_imn = -_pair_map(cim_ref, d, q)
            c_map = jnp.concatenate([c_re, c_imn], axis=1).astype(_bf16)
            tap_scr[d] = _dot_nt(w_scr[d, q], jnp.concatenate([c_map] * S5_CHUNK, axis=0))
            powers = [o + 1 if d == 0 else S5_CHUNK - o for o in range(S5_CHUNK)]
            wy.append(jnp.concatenate([out_weights(c_re, c_imn, d, m, q) for m in powers],
                                      axis=0))
        for o in range(S5_CHUNK):
            lanes = slice(o * blk, (o + 1) * blk)
            if o:
                toep_scr[0:o * blk, lanes] = tap_scr[0, (S5_CHUNK - 1 - o) * blk:
                                                     (S5_CHUNK - 1) * blk, lanes]
            toep_scr[o * blk:(o + 1) * blk, lanes] = (
                tap_scr[0, (S5_CHUNK - 1) * blk:, lanes] + tap_scr[1, 0:blk, lanes])
            if o < S5_CHUNK - 1:
                toep_scr[(o + 1) * blk:, lanes] = tap_scr[1, blk:(S5_CHUNK - o) * blk, lanes]
        y = _dot_nt(sp_scr[0, :, pair_cols(q)], wy[0])
        y += _dot_nt(sp_scr[1, :, pair_cols(q)], wy[1])
        y += _dot(ub_scr[q], toep_scr[...].astype(_bf16))
        d_q = jnp.where(piece == q, d_ref[0], 0.0)
        for shift in range(PAIR_LANES, V7X_LANES, PAIR_LANES):
            d_q = d_q + pltpu.roll(jnp.where(piece == q, d_ref[0], 0.0), shift, 1)
        y += uf_scr[q] * jnp.concatenate([d_q] * (PAIR_COLS // V7X_LANES), axis=1)
        for h in range(2):
            y_ref[0, q, h] = y[:, h * V7X_LANES:(h + 1) * V7X_LANES]


def _s5(u, u_meta, maps, powers, d_skip, cast=()):
    slab_map = pl.BlockSpec((N_DIR, V7X_LANES, S5_STATE), lambda s: (0, s, 0))
    cast_specs = [pl.BlockSpec((w.shape[0] // S5_SLABS, w.shape[1]), lambda s: (s, 0))
                  for w in cast]
    return pl.pallas_call(
        functools.partial(_s5_kernel, n_cast=len(cast)),
        grid=(S5_SLABS,),
        in_specs=[
            pl.BlockSpec((1, S5_CHUNK_ROWS * S5_CHUNK, V7X_LANES), lambda s: (s, 0, 0)),
            pl.BlockSpec((1, V7X_SUBLANES, S5_CHUNK_COLS), lambda s: (s, 0, 0)),
            slab_map, slab_map, slab_map, slab_map,
            pl.BlockSpec((S5_CHUNK + 1, 2, N_DIR, SLAB_GROUPS, S5_STATE),
                         lambda s: (0, 0, 0, s, 0)),
            pl.BlockSpec((1, 1, V7X_LANES), lambda s: (s, 0, 0)),
        ] + cast_specs,
        out_specs=[pl.BlockSpec((1, SLAB_PAIRS, 2, S5_CHUNK_ROWS, V7X_LANES),
                                lambda s: (s, 0, 0, 0, 0))] + cast_specs,
        out_shape=[jax.ShapeDtypeStruct((S5_SLABS, SLAB_PAIRS, 2, S5_CHUNK_ROWS, V7X_LANES),
                                        _f32)]
                  + [jax.ShapeDtypeStruct(w.shape, _bf16) for w in cast],
        scratch_shapes=[
            pltpu.VMEM((SLAB_PAIRS, S5_CHUNK_ROWS, PAIR_COLS), _bf16),
            pltpu.VMEM((SLAB_PAIRS, S5_CHUNK_ROWS, PAIR_COLS), _f32),
            pltpu.VMEM((N_DIR, SLAB_PAIRS, PAIR_COLS, PAIR_COLS), _bf16),
            pltpu.VMEM((N_DIR, S5_CHUNK_ROWS, 2 * SLAB_STATES), _f32),
            pltpu.VMEM((N_DIR, S5_CHUNK_ROWS, 2 * SLAB_STATES), _bf16),
            pltpu.VMEM((N_DIR, PAIR_COLS, PAIR_COLS), _f32),
            pltpu.VMEM((PAIR_COLS, PAIR_COLS), _f32),
            pltpu.VMEM((V7X_SUBLANES, 2 * SLAB_STATES), _f32),
        ],
        compiler_params=_params(("arbitrary",)),
        name="s5",
    )(u, u_meta, *maps, powers, d_skip, *cast)


def _mixer_kernel(h_ref, y_ref, q_ref, qp_ref, qn_ref, qm_ref, bc_ref, gs_ref, gc_ref,
                  wglu_ref, bglu_ref, wsup_ref, cw_ref, cb_ref, wcup_ref, wo_ref,
                  g_ref, wcast_ref, o_ref, wcast_out_ref, ys_scr, q_scr):
    t = pl.program_id(0)
    last = pl.num_programs(0) - 1
    tt = MIX_TILE_T
    tm = BATCH * tt
    n_c = tt // S5_CHUNK

    def columns(ref, b):
        return jnp.concatenate([ref[j, b] for j in range(IN_STEPS)], axis=1)

    def all_rows(ref):
        return jnp.concatenate([ref[j].reshape(tm, -1) for j in range(IN_STEPS)], axis=1)

    vs = []
    for b in range(BATCH):
        prev = jnp.where(t == 0, qm_ref[...], columns(qp_ref, b)).astype(_f32)
        nxt = jnp.where(t == last, 0.0, columns(qn_ref, b).astype(_f32))
        q_scr[b, 0:HALO_ROWS, :] = prev
        q_scr[b, HALO_ROWS:HALO_ROWS + tt, :] = columns(q_ref, b).astype(_f32)
        q_scr[b, HALO_ROWS + tt:, :] = nxt
        vs.append(cw_ref[0:1, :] * q_scr[b, HALO_ROWS - 1:HALO_ROWS - 1 + tt, :]
                  + cw_ref[1:2, :] * q_scr[b, HALO_ROWS:HALO_ROWS + tt, :]
                  + cw_ref[2:3, :] * q_scr[b, HALO_ROWS + 1:HALO_ROWS + 1 + tt, :]
                  + cb_ref[...])
    v = jnp.concatenate(vs, axis=0)
    bc = all_rows(bc_ref).astype(_f32)
    y_c = _dot((bc * v).astype(_bf16), wcup_ref[...])

    half_steps = S5_CHUNK // 2
    for s in range(S5_SLABS):
        for h in range(2):
            for b in range(BATCH):
                tiles = [y_ref[s, q, h, pl.ds(b, n_c, stride=BATCH), :]
                         for q in range(SLAB_PAIRS)]
                for k, tile in enumerate(_transpose_pieces(tiles)):
                    o = h * half_steps + k
                    ys_scr[s, pl.ds(b * tt + o, n_c, stride=S5_CHUNK), :] = tile
    ys = jnp.concatenate([ys_scr[s] for s in range(S5_SLABS)], axis=1)
    ys = jax.nn.gelu(ys)
    glu = _dot(ys.astype(_bf16), wglu_ref[...]) + bglu_ref[...]
    ys = ys * jax.nn.sigmoid(glu)
    y_s = _dot(ys.astype(_bf16), wsup_ref[...])

    merged = all_rows(gs_ref).astype(_f32) * y_s + all_rows(gc_ref).astype(_f32) * y_c
    mixed = _dot(merged.astype(_bf16), wo_ref[...])
    out = h_ref[...].reshape(tm, D_MODEL) + _rms_norm(mixed, g_ref[...])
    o_ref[...] = out.reshape(BATCH, tt, D_MODEL)
    wcast_out_ref[...] = wcast_ref[...].astype(_bf16)


def _mixer(x, y, q, q_meta, bc, gate_s, gate_c, w_glu, b_glu, w_s_up, conv_w, conv_b, w_c_up,
           w_o, g_post, w_cast):
    tt = MIX_TILE_T
    n_t = SEQ // tt
    cast_spec = pl.BlockSpec((w_cast.shape[0] // n_t, w_cast.shape[1]), lambda t: (t, 0))
    halo_per_tile = tt // HALO_ROWS
    n_halo = SEQ // HALO_ROWS
    tile = lambda t: (0, t, 0)
    stepped = lambda rows, cols: pl.BlockSpec((IN_STEPS, BATCH, rows, cols),
                                              lambda t: (0, 0, t, 0))
    resident = functools.partial(pl.BlockSpec, index_map=lambda t: (0, 0),
                                 pipeline_mode=pl.Buffered(1))
    return pl.pallas_call(
        _mixer_kernel,
        grid=(n_t,),
        in_specs=[
            pl.BlockSpec((BATCH, tt, D_MODEL), tile),
            pl.BlockSpec((S5_SLABS, SLAB_PAIRS, 2, tt // S5_CHUNK * BATCH, V7X_LANES),
                         lambda t: (0, 0, 0, t, 0)),
            stepped(tt, IN_SEG_COLS),
            pl.BlockSpec((IN_STEPS, BATCH, HALO_ROWS, IN_SEG_COLS),
                         lambda t: (0, 0, jnp.maximum(t * halo_per_tile - 1, 0), 0)),
            pl.BlockSpec((IN_STEPS, BATCH, HALO_ROWS, IN_SEG_COLS),
                         lambda t: (0, 0, jnp.minimum((t + 1) * halo_per_tile, n_halo - 1), 0)),
            resident((HALO_ROWS, CONV_WIDTH)),
            stepped(tt, IN_SEG_COLS),
            stepped(tt, 2 * IN_SEG_COLS),
            stepped(tt, 2 * IN_SEG_COLS),
            resident((S5_WIDTH, S5_WIDTH)),
            resident((1, S5_WIDTH)),
            resident((S5_WIDTH, D_MODEL)),
            resident((3, CONV_WIDTH)),
            resident((1, CONV_WIDTH)),
            resident((CONV_WIDTH, D_MODEL)),
            resident((D_MODEL, D_MODEL)),
            resident((1, D_MODEL)),
            cast_spec,
        ],
        out_specs=[pl.BlockSpec((BATCH, tt, D_MODEL), tile), cast_spec],
        out_shape=[jax.ShapeDtypeStruct((BATCH, SEQ, D_MODEL), _f32),
                   jax.ShapeDtypeStruct(w_cast.shape, _bf16)],
        scratch_shapes=[pltpu.VMEM((S5_SLABS, BATCH * tt, V7X_LANES), _f32),
                        pltpu.VMEM((BATCH, tt + 2 * HALO_ROWS, CONV_WIDTH), _f32)],
        compiler_params=_params(("arbitrary",)),
        name="mixer_tail",
    )(x, y, q, q, q, q_meta, bc, gate_s, gate_c, w_glu, b_glu, w_s_up, conv_w, conv_b, w_c_up,
      w_o, g_post, w_cast)


def _ffn_kernel(h_ref, gpre_ref, wgu_ref, wout_ref, gpost_ref, o_ref, hb_scr):
    j = pl.program_id(1)
    last = pl.num_programs(1) - 1

    def chunk(hb):
        gu = _dot(hb, wgu_ref[...])
        gate, up = gu[:, :FFN_TILE_HIDDEN], gu[:, FFN_TILE_HIDDEN:]
        return _dot((jax.nn.silu(gate) * up).astype(_bf16), wout_ref[...])

    @pl.when(j == 0)
    def _():
        hb = _rms_norm(h_ref[...], gpre_ref[...]).astype(_bf16)
        hb_scr[...] = hb
        o_ref[...] = chunk(hb)

    @pl.when(jnp.logical_and(j > 0, j < last))
    def _():
        o_ref[...] += chunk(hb_scr[...])

    @pl.when(j == last)
    def _():
        f = o_ref[...] + chunk(hb_scr[...])
        o_ref[...] = h_ref[...] + _rms_norm(f, gpost_ref[...])


def _ffn(h1, g_pre, w_in, w_out, g_post):
    tm, th = FFN_TILE_ROWS, FFN_TILE_HIDDEN
    n_hidden = FFN_HIDDEN // th
    return pl.pallas_call(
        _ffn_kernel,
        grid=(ROWS // tm, n_hidden),
        in_specs=[
            pl.BlockSpec((tm, D_MODEL), lambda i, j: (i, 0)),
            pl.BlockSpec((1, D_MODEL), lambda i, j: (0, 0)),
            pl.BlockSpec((D_MODEL, 2 * th), lambda i, j: (0, j)),
            pl.BlockSpec((th, D_MODEL), lambda i, j: (j, 0)),
            pl.BlockSpec((1, D_MODEL), lambda i, j: (0, 0)),
        ],
        out_specs=pl.BlockSpec((tm, D_MODEL), lambda i, j: (i, 0)),
        out_shape=jax.ShapeDtypeStruct((ROWS, D_MODEL), _f32),
        scratch_shapes=[pltpu.VMEM((tm, D_MODEL), _bf16)],
        compiler_params=_params(("arbitrary", "arbitrary")),
        name="ffn",
    )(h1, g_pre, w_in, w_out, g_post)


def kernel(x, meta, g_mix_pre, g_mix_post, g_ffn_pre, g_ffn_post, w_in, gate_b, lam_re, lam_im,
           log_dt, b_re, b_im, c_re, c_im, d_skip, w_glu, b_glu, w_s_up, conv_w, conv_b, w_c_up,
           w_o, w_ffn_in, w_ffn_out):
    l = 0
    w_in_b = _cast_w_in(w_in[l])
    gate_b2 = gate_b[l].reshape(2, 1, D_MODEL)
    g_pre = g_mix_pre[l].reshape(1, D_MODEL)

    powers, bz_re, bz_im = _discretise(lam_re[l], lam_im[l], log_dt[l], b_re[l], b_im[l])
    powers = powers.reshape(S5_CHUNK + 1, 2, N_DIR, S5_GROUPS, S5_STATE)
    maps = [m.reshape(N_DIR, S5_WIDTH, S5_STATE) for m in (bz_re, bz_im, c_re[l], c_im[l])]

    u5, bc, q, gate_s, gate_c, um_tiles, qm_tiles, w_ffn_in_b = _in_proj(
        x, meta.astype(x.dtype), g_pre, w_in_b, gate_b2, IN_TILE_ROWS,
        cast=((w_ffn_in[l], _ffn_in_plan()),))
    um5, q_meta = um_tiles[0], qm_tiles[0]

    u = u5.reshape(S5_SLABS, S5_CHUNK_ROWS * S5_CHUNK, V7X_LANES)
    u_meta = jnp.pad(um5.reshape(S5_SLABS, META_CHUNKS, S5_CHUNK_COLS),
                     ((0, 0), (0, V7X_SUBLANES - META_CHUNKS), (0, 0)))
    y, w_glu_b, w_s_up_b, w_c_up_b, w_o_b = _s5(
        u, u_meta, maps, powers, d_skip[l].reshape(S5_SLABS, 1, V7X_LANES),
        cast=(w_glu[l], w_s_up[l], w_c_up[l], w_o[l]))

    h1, w_ffn_out_b = _mixer(x, y, q, q_meta, bc, gate_s, gate_c, w_glu_b,
                             b_glu[l].reshape(1, S5_WIDTH), w_s_up_b, conv_w[l],
                             conv_b[l].reshape(1, CONV_WIDTH), w_c_up_b, w_o_b,
                             g_mix_post[l].reshape(1, D_MODEL), w_ffn_out[l])

    h2 = _ffn(h1.reshape(ROWS, D_MODEL), g_ffn_pre[l].reshape(1, D_MODEL),
              w_ffn_in_b, w_ffn_out_b, g_ffn_post[l].reshape(1, D_MODEL))
    return h2.reshape(BATCH, SEQ, D_MODEL)
```

```python
import functools
import math

import jax
import jax.numpy as jnp
from jax import lax
from jax.experimental import pallas as pl
from jax.experimental.pallas import tpu as pltpu

D_MODEL = 2048
BATCH = 4
SEQ = 2048
N_META = 16
S5_WIDTH = 1024
S5_GROUP = 16
S5_GROUPS = S5_WIDTH // S5_GROUP
S5_STATE = 64
N_DIR = 2
CONV_WIDTH = 1024
FFN_HIDDEN = ((math.ceil(8 * D_MODEL / 3) + 255) // 256) * 256
IN_COLS = S5_WIDTH + 3 * CONV_WIDTH + 2 * D_MODEL
RMS_EPS = 1e-6
LAM_RE_MAX = -1e-4

ROWS = SEQ * BATCH

V7X_SUBLANES = 8
V7X_LANES = 128
V7X_MXU_DIM = 256
V7X_VMEM_LIMIT_BYTES = 56 * 1024 * 1024

S5_SLABS = S5_WIDTH // V7X_LANES
SLAB_GROUPS = V7X_LANES // S5_GROUP
SLAB_STATES = SLAB_GROUPS * S5_STATE
PAIR_LANES = 2 * S5_GROUP
SLAB_PAIRS = V7X_LANES // PAIR_LANES
PAIR_STATES = 2 * S5_STATE
S5_CHUNK = V7X_SUBLANES
PAIR_COLS = S5_CHUNK * PAIR_LANES
assert PAIR_COLS == V7X_MXU_DIM
S5_CHUNKS = SEQ // S5_CHUNK
S5_CHUNK_ROWS = S5_CHUNKS * BATCH
S5_CHUNK_COLS = S5_CHUNK * V7X_LANES
META_CHUNKS = N_META // S5_CHUNK
STATE_REPEATS = V7X_LANES // S5_STATE

IN_TILE_ROWS = 1024
IN_STEPS = 8
IN_SEG_COLS = S5_WIDTH // IN_STEPS
IN_RING_SLOTS = 3
IN_RING_DMA_PRIORITY = 1
CAST_TILE_ROWS = 128
MIX_TILE_T = 64
FFN_TILE_ROWS = 1024
FFN_TILE_HIDDEN = 256
HALO_ROWS = 16

_f32 = jnp.float32
_bf16 = jnp.bfloat16


def _rms_norm(xf, g):
    r = lax.rsqrt(jnp.mean(xf * xf, axis=-1, keepdims=True) + RMS_EPS)
    return xf * r * g


def _params(sem):
    return pltpu.CompilerParams(dimension_semantics=sem,
                                vmem_limit_bytes=V7X_VMEM_LIMIT_BYTES)


def _dot(a, b):
    return jnp.dot(a, b, preferred_element_type=_f32)


def _dot_nt(a, b):
    return lax.dot_general(a, b, (((1,), (1,)), ((), ())), preferred_element_type=_f32)


def _discretise_kernel(lam_re_ref, lam_im_ref, log_dt_ref, b_re_ref, b_im_ref,
                       pow_ref, bz_re_ref, bz_im_ref):
    lr = jnp.minimum(lam_re_ref[...], LAM_RE_MAX)
    li = lam_im_ref[...]
    dt = jnp.exp(log_dt_ref[...])
    mag = jnp.exp(lr * dt)
    ar = mag * jnp.cos(li * dt)
    ai = mag * jnp.sin(li * dt)
    den = lr * lr + li * li
    nr = ar - 1.0
    zr = (nr * lr + ai * li) / den
    zi = (ai * lr - nr * li) / den
    pr = jnp.ones_like(ar)
    pi = jnp.zeros_like(ar)
    for m in range(S5_CHUNK + 1):
        pow_ref[m, 0] = pr
        pow_ref[m, 1] = pi
        pr, pi = pr * ar - pi * ai, pr * ai + pi * ar
    b_re = b_re_ref[...]
    b_im = b_im_ref[...]
    bz_re_ref[...] = zr[:, None, :] * b_re - zi[:, None, :] * b_im
    bz_im_ref[...] = zr[:, None, :] * b_im + zi[:, None, :] * b_re


def _discretise(lam_re, lam_im, log_dt, b_re, b_im):
    dg = N_DIR * S5_GROUPS
    b_re_t = jnp.swapaxes(b_re, -1, -2).reshape(dg, S5_GROUP, S5_STATE)
    b_im_t = jnp.swapaxes(b_im, -1, -2).reshape(dg, S5_GROUP, S5_STATE)
    return pl.pallas_call(
        _discretise_kernel,
        out_shape=(jax.ShapeDtypeStruct((S5_CHUNK + 1, 2, dg, S5_STATE), _f32),
                   jax.ShapeDtypeStruct((dg, S5_GROUP, S5_STATE), _f32),
                   jax.ShapeDtypeStruct((dg, S5_GROUP, S5_STATE), _f32)),
        name="s5_discretise",
    )(lam_re.reshape(dg, S5_STATE), lam_im.reshape(dg, S5_STATE),
      log_dt.reshape(dg, 1), b_re_t, b_im_t)


def _w_in_pieces(j):
    seg, gseg = IN_SEG_COLS, 2 * IN_SEG_COLS
    pieces = [(4 * S5_WIDTH + n * D_MODEL + j * gseg, gseg) for n in range(2)]
    pieces += [(n * S5_WIDTH + j * seg, seg) for n in range(4)]
    return pieces


def _cast_w_in_kernel(w_ref, o_ref):
    dst = 0
    for j in range(IN_STEPS):
        for src, width in _w_in_pieces(j):
            o_ref[:, dst:dst + width] = w_ref[:, src:src + width].astype(_bf16)
            dst += width


def _cast_w_in(w):
    spec = pl.BlockSpec((CAST_TILE_ROWS, IN_COLS), lambda i: (i, 0))
    return pl.pallas_call(
        _cast_w_in_kernel,
        grid=(D_MODEL // CAST_TILE_ROWS,),
        in_specs=[spec],
        out_specs=spec,
        out_shape=jax.ShapeDtypeStruct(w.shape, _bf16),
        compiler_params=_params(("arbitrary",)),
        name="cast_w_in",
    )(w)


def _ffn_in_plan():
    th = FFN_TILE_HIDDEN
    plan = []
    for j in range(FFN_HIDDEN // th):
        plan.append((j * th, 2 * j * th, th))
        plan.append((FFN_HIDDEN + j * th, (2 * j + 1) * th, th))
    return plan


def _in_proj_kernel(x_ref, m_ref, g_ref, w_hbm, gbs_ref, gbc_ref, *rest, cast_plans):
    n_cast = len(cast_plans)
    cast_in = rest[:n_cast]
    u_ref, bc_ref, q_ref, gs_ref, gc_ref, um_ref, qm_ref = rest[n_cast:n_cast + 7]
    cast_out = rest[n_cast + 7:2 * n_cast + 7]
    h_scr, w_buf, w_sem = rest[2 * n_cast + 7:]
    seg = IN_SEG_COLS
    tile_cols = IN_COLS // IN_STEPS
    tm = x_ref.shape[0]

    step = ((pl.program_id(0) * pl.num_programs(1) + pl.program_id(1)) * IN_STEPS
            + pl.program_id(2))
    n_steps = pl.num_programs(0) * pl.num_programs(1) * IN_STEPS

    def w_copy(s):
        cols = pl.ds(pl.multiple_of((s % IN_STEPS) * tile_cols, tile_cols), tile_cols)
        slot = s % IN_RING_SLOTS
        return pltpu.make_async_copy(w_hbm.at[:, cols], w_buf.at[slot], w_sem.at[slot])

    @pl.when(step == 0)
    def _():
        for s in range(IN_RING_SLOTS - 1):
            w_copy(s).start(priority=IN_RING_DMA_PRIORITY)

    @pl.when(step + IN_RING_SLOTS - 1 < n_steps)
    def _():
        w_copy(step + IN_RING_SLOTS - 1).start(priority=IN_RING_DMA_PRIORITY)

    w_copy(step).wait()
    w_ref = w_buf.at[step % IN_RING_SLOTS]

    def column_step(h):
        both = _dot(h, w_ref[...])
        proj, meta = both[:tm], both[tm:]
        gs_ref[...] = jax.nn.sigmoid(proj[:, 0:2 * seg] + gbs_ref[0]).astype(_bf16)
        gc_ref[...] = jax.nn.sigmoid(proj[:, 2 * seg:4 * seg] + gbc_ref[0]).astype(_bf16)
        for s in range(seg // V7X_LANES):
            lanes = slice(4 * seg + s * V7X_LANES, 4 * seg + (s + 1) * V7X_LANES)
            u_ref[s] = proj[:, lanes].reshape(-1, S5_CHUNK, V7X_LANES)
            um_ref[s] = meta[:, lanes].reshape(-1, S5_CHUNK, V7X_LANES)
        bc_ref[...] = proj[:, 6 * seg:7 * seg].astype(_bf16)
        q_ref[...] = (proj[:, 7 * seg:8 * seg] * proj[:, 5 * seg:6 * seg]).astype(_bf16)
        qm_ref[...] = (meta[:, 7 * seg:8 * seg] * meta[:, 5 * seg:6 * seg]).astype(_bf16)
        for src, dst, plan in zip(cast_in, cast_out, cast_plans):
            for s0, d0, width in plan:
                dst[:, d0:d0 + width] = src[:, s0:s0 + width].astype(_bf16)

    @pl.when(pl.program_id(2) == 0)
    def _():
        h = jnp.concatenate([_rms_norm(x_ref[...], g_ref[...]),
                             _rms_norm(m_ref[...], g_ref[...])], axis=0).astype(_bf16)
        h_scr[...] = h
        column_step(h)

    @pl.when(pl.program_id(2) > 0)
    def _():
        column_step(h_scr[...])


def _in_proj(x3, meta, g, w_in_b, gate_b, tile_rows, cast=()):
    nb, t_len, _ = x3.shape
    n_t = t_len // tile_rows
    n_steps = nb * n_t * IN_STEPS
    seg, gseg = IN_SEG_COLS, 2 * IN_SEG_COLS
    by_step = lambda b, t, j: (j, b, t, 0)
    step = lambda b, t, j: ((b * n_t + t) * IN_STEPS + j, 0)
    cast_w = [w for w, _ in cast]
    cast_plans = tuple(tuple(plan) if plan else ((0, 0, w.shape[1]),) for w, plan in cast)
    cast_specs = [pl.BlockSpec((w.shape[0] // n_steps, w.shape[1]), step) for w in cast_w]
    gb = lambda n: pl.BlockSpec((1, 1, gseg), lambda b, t, j: (n, 0, j))
    return pl.pallas_call(
        functools.partial(_in_proj_kernel, cast_plans=cast_plans),
        grid=(nb, n_t, IN_STEPS),
        in_specs=[
            pl.BlockSpec((None, tile_rows, D_MODEL), lambda b, t, j: (b, t, 0)),
            pl.BlockSpec((N_META, D_MODEL), lambda b, t, j: (0, 0)),
            pl.BlockSpec((1, D_MODEL), lambda b, t, j: (0, 0)),
            pl.BlockSpec(memory_space=pl.ANY),
            gb(0), gb(1),
        ] + cast_specs,
        out_specs=[
            pl.BlockSpec((seg // V7X_LANES, tile_rows // S5_CHUNK, None, S5_CHUNK, V7X_LANES),
                         lambda b, t, j: (j, t, b, 0, 0)),
            pl.BlockSpec((None, None, tile_rows, seg), by_step),
            pl.BlockSpec((None, None, tile_rows, seg), by_step),
            pl.BlockSpec((None, None, tile_rows, gseg), by_step),
            pl.BlockSpec((None, None, tile_rows, gseg), by_step),
            pl.BlockSpec((None, seg // V7X_LANES, META_CHUNKS, S5_CHUNK, V7X_LANES),
                         lambda b, t, j: (b * n_t + t, j, 0, 0, 0)),
            pl.BlockSpec((None, N_META, seg), lambda b, t, j: (b * n_t + t, 0, j)),
        ] + cast_specs,
        out_shape=[jax.ShapeDtypeStruct(
                       (S5_SLABS, t_len // S5_CHUNK, nb, S5_CHUNK, V7X_LANES), _f32),
                   jax.ShapeDtypeStruct((IN_STEPS, nb, t_len, seg), _bf16),
                   jax.ShapeDtypeStruct((IN_STEPS, nb, t_len, seg), _bf16),
                   jax.ShapeDtypeStruct((IN_STEPS, nb, t_len, gseg), _bf16),
                   jax.ShapeDtypeStruct((IN_STEPS, nb, t_len, gseg), _bf16),
                   jax.ShapeDtypeStruct((nb * n_t, S5_SLABS, META_CHUNKS, S5_CHUNK, V7X_LANES),
                                        _f32),
                   jax.ShapeDtypeStruct((nb * n_t, N_META, CONV_WIDTH), _bf16)]
                  + [jax.ShapeDtypeStruct(w.shape, _bf16) for w in cast_w],
        scratch_shapes=[pltpu.VMEM((tile_rows + N_META, D_MODEL), _bf16),
                        pltpu.VMEM((IN_RING_SLOTS, D_MODEL, IN_COLS // IN_STEPS), _bf16),
                        pltpu.SemaphoreType.DMA((IN_RING_SLOTS,))],
        compiler_params=_params(("arbitrary", "arbitrary", "arbitrary")),
        name="in_proj",
    )(x3, meta, g, w_in_b, gate_b, gate_b, *cast_w)


def _pair_map(ref, d, q):
    rows = lax.broadcasted_iota(jnp.int32, (PAIR_LANES, V7X_LANES), 0)
    cols = lax.broadcasted_iota(jnp.int32, (PAIR_LANES, V7X_LANES), 1)
    same_group = (rows // S5_GROUP) == (cols // S5_STATE)
    per_group = ref[d, q * PAIR_LANES:(q + 1) * PAIR_LANES, :]
    return jnp.where(same_group, jnp.concatenate([per_group] * STATE_REPEATS, axis=1), 0.0)


def _transpose_pieces(src):
    assert SLAB_PAIRS == 4 and len(src) == SLAB_PAIRS
    lane = lax.broadcasted_iota(jnp.int32, src[0].shape, 1)

    def exchange(a, b, width):
        keep = (lane // width) % 2 == 0
        return (jnp.where(keep, a, pltpu.roll(b, width, 1)),
                jnp.where(keep, pltpu.roll(a, V7X_LANES - width, 1), b))

    b0, b2 = exchange(src[0], src[2], 2 * PAIR_LANES)
    b1, b3 = exchange(src[1], src[3], 2 * PAIR_LANES)
    c0, c1 = exchange(b0, b1, PAIR_LANES)
    c2, c3 = exchange(b2, b3, PAIR_LANES)
    return [c0, c1, c2, c3]


def _boundary_scan(x_scr, a_re, a_im, init):
    n_tiles = S5_CHUNK_ROWS // V7X_SUBLANES
    low = lax.broadcasted_iota(jnp.int32, (V7X_SUBLANES, V7X_LANES), 0) < BATCH

    def step(ar, ai, sr, si, xr, xi):
        return ar * sr - ai * si + xr, ar * si + ai * sr + xi

    def swap(v):
        return pltpu.roll(v, BATCH, 0)

    def body(i, carry):
        rf = pl.multiple_of(i * V7X_SUBLANES, V7X_SUBLANES)
        rb = pl.multiple_of((n_tiles - 1 - i) * V7X_SUBLANES, V7X_SUBLANES)
        loaded = []
        for q in range(SLAB_PAIRS):
            for part in range(2):
                lanes = slice(q * PAIR_COLS + part * PAIR_STATES,
                              q * PAIR_COLS + (part + 1) * PAIR_STATES)
                loaded.append((lanes, x_scr[0, pl.ds(rf, V7X_SUBLANES), lanes],
                               x_scr[1, pl.ds(rb, V7X_SUBLANES), lanes]))
        new, stores = [], []
        for q in range(SLAB_PAIRS):
            (re, xfr, xbr), (im, xfi, xbi) = loaded[2 * q], loaded[2 * q + 1]
            w1r, w1i = jnp.where(low, xfr, xbr), jnp.where(low, xfi, xbi)
            w2r, w2i = swap(jnp.where(low, xbr, xfr)), swap(jnp.where(low, xbi, xfi))
            cr, ci = carry[2 * q], carry[2 * q + 1]
            s1r, s1i = step(a_re[q], a_im[q], cr, ci, w1r, w1i)
            s2r, s2i = step(a_re[q], a_im[q], s1r, s1i, w2r, w2i)
            t1r, t1i = swap(s1r), swap(s1i)
            stores += [(0, rf, re, jnp.where(low, cr, t1r)), (0, rf, im, jnp.where(low, ci, t1i)),
                       (1, rb, re, jnp.where(low, t1r, cr)), (1, rb, im, jnp.where(low, t1i, ci))]
            new += [s2r, s2i]
        for d, r0, lanes, value in stores:
            x_scr[d, pl.ds(r0, V7X_SUBLANES), lanes] = value
        return tuple(new)

    lax.fori_loop(0, n_tiles, body, init)


def _s5_kernel(u_ref, um_ref, bre_ref, bim_ref, cre_ref, cim_ref, p_ref, d_ref, *rest, n_cast):
    cast_in = rest[:n_cast]
    y_ref = rest[n_cast]
    cast_out = rest[n_cast + 1:2 * n_cast + 1]
    ub_scr, uf_scr, w_scr, x_scr, sp_scr, tap_scr, toep_scr, xm_scr = rest[2 * n_cast + 1:]
    for src, dst in zip(cast_in, cast_out):
        dst[...] = src[...].astype(_bf16)

    half_steps = S5_CHUNK // 2
    blk = PAIR_LANES

    def pair_cols(q):
        return slice(q * PAIR_COLS, (q + 1) * PAIR_COLS)

    @functools.cache
    def power(d, m, part, q):
        g = q * (PAIR_LANES // S5_GROUP)
        return jnp.concatenate([p_ref[m, part, d, g + k:g + k + 1, :]
                                for k in range(PAIR_LANES // S5_GROUP)], axis=1)

    def strided_u(j):
        return u_ref[0, pl.ds(j, S5_CHUNK_ROWS, stride=S5_CHUNK), :]

    um = um_ref[0]
    um_q = [[] for _ in range(SLAB_PAIRS)]
    for h in range(2):
        steps = range(h * half_steps, (h + 1) * half_steps)
        for q, t in enumerate(_transpose_pieces([strided_u(j) for j in steps])):
            uf_scr[q, :, h * V7X_LANES:(h + 1) * V7X_LANES] = t
            ub_scr[q, :, h * V7X_LANES:(h + 1) * V7X_LANES] = t.astype(_bf16)
        meta = [um[:, j * V7X_LANES:(j + 1) * V7X_LANES] for j in steps]
        for q, t in enumerate(_transpose_pieces(meta)):
            um_q[q].append(t)

    for d in range(N_DIR):
        for q in range(SLAB_PAIRS):
            w_re = _pair_map(bre_ref, d, q)
            w_im = _pair_map(bim_ref, d, q)
            for j in range(S5_CHUNK):
                m = (S5_CHUNK - 1 - j) if d == 0 else j
                pr = power(d, m, 0, q)
                pi = power(d, m, 1, q)
                rows = slice(j * blk, (j + 1) * blk)
                w_scr[d, q, rows, 0:PAIR_STATES] = (w_re * pr - w_im * pi).astype(_bf16)
                w_scr[d, q, rows, PAIR_STATES:] = (w_re * pi + w_im * pr).astype(_bf16)
            x_scr[d, :, pair_cols(q)] = _dot(ub_scr[q], w_scr[d, q])
            if d == 0:
                xm_scr[:, pair_cols(q)] = _dot(
                    jnp.concatenate(um_q[q], axis=1).astype(_bf16), w_scr[0, q])

    low = lax.broadcasted_iota(jnp.int32, (V7X_SUBLANES, V7X_LANES), 0) < BATCH
    a_re, a_im, init = [], [], []
    for q in range(SLAB_PAIRS):
        re = slice(q * PAIR_COLS, q * PAIR_COLS + PAIR_STATES)
        im = slice(q * PAIR_COLS + PAIR_STATES, (q + 1) * PAIR_COLS)
        ar = power(0, S5_CHUNK, 0, q)
        ai = power(0, S5_CHUNK, 1, q)
        a_re.append(jnp.where(low, ar, power(1, S5_CHUNK, 0, q)))
        a_im.append(jnp.where(low, ai, power(1, S5_CHUNK, 1, q)))
        sr = jnp.zeros((1, V7X_LANES), _f32)
        si = jnp.zeros((1, V7X_LANES), _f32)
        for c in range(META_CHUNKS):
            xr = xm_scr[c:c + 1, re]
            xi = xm_scr[c:c + 1, im]
            sr, si = ar * sr - ai * si + xr, ar * si + ai * sr + xi
        init.append(jnp.where(low, sr, 0.0))
        init.append(jnp.where(low, si, 0.0))
    _boundary_scan(x_scr, a_re, a_im, tuple(init))
    for d in range(N_DIR):
        sp_scr[d] = x_scr[d].astype(_bf16)

    def out_weights(c_re, c_imn, d, m, q):
        pr = power(d, m, 0, q)
        pi = power(d, m, 1, q)
        return jnp.concatenate([c_re * pr + c_imn * pi, c_imn * pr - c_re * pi],
                               axis=1).astype(_bf16)

    piece = lax.broadcasted_iota(jnp.int32, (1, V7X_LANES), 1) // PAIR_LANES

    for q in range(SLAB_PAIRS):
        wy = []
        for d in range(N_DIR):
            c_re = _pair_map(cre_ref, d, q)
            c_imn = -_pair_map(cim_ref, d, q)
            c_map = jnp.concatenate([c_re, c_imn], axis=1).astype(_bf16)
            tap_scr[d] = _dot_nt(w_scr[d, q], jnp.concatenate([c_map] * S5_CHUNK, axis=0))
            powers = [o + 1 if d == 0 else S5_CHUNK - o for o in range(S5_CHUNK)]
            wy.append(jnp.concatenate([out_weights(c_re, c_imn, d, m, q) for m in powers],
                                      axis=0))
        for o in range(S5_CHUNK):
            lanes = slice(o * blk, (o + 1) * blk)
            if o:
                toep_scr[0:o * blk, lanes] = tap_scr[0, (S5_CHUNK - 1 - o) * blk:
                                                     (S5_CHUNK - 1) * blk, lanes]
            toep_scr[o * blk:(o + 1) * blk, lanes] = (
                tap_scr[0, (S5_CHUNK - 1) * blk:, lanes] + tap_scr[1, 0:blk, lanes])
            if o < S5_CHUNK - 1:
                toep_scr[(o + 1) * blk:, lanes] = tap_scr[1, blk:(S5_CHUNK - o) * blk, lanes]
        y = _dot_nt(sp_scr[0, :, pair_cols(q)], wy[0])
        y += _dot_nt(sp_scr[1, :, pair_cols(q)], wy[1])
        y += _dot(ub_scr[q], toep_scr[...].astype(_bf16))
        d_q = jnp.where(piece == q, d_ref[0], 0.0)
        for shift in range(PAIR_LANES, V7X_LANES, PAIR_LANES):
            d_q = d_q + pltpu.roll(jnp.where(piece == q, d_ref[0], 0.0), shift, 1)
        y += uf_scr[q] * jnp.concatenate([d_q] * (PAIR_COLS // V7X_LANES), axis=1)
        for h in range(2):
            y_ref[0, q, h] = y[:, h * V7X_LANES:(h + 1) * V7X_LANES]


def _s5(u, u_meta, maps, powers, d_skip, cast=()):
    slab_map = pl.BlockSpec((N_DIR, V7X_LANES, S5_STATE), lambda s: (0, s, 0))
    cast_specs = [pl.BlockSpec((w.shape[0] // S5_SLABS, w.shape[1]), lambda s: (s, 0))
                  for w in cast]
    return pl.pallas_call(
        functools.partial(_s5_kernel, n_cast=len(cast)),
        grid=(S5_SLABS,),
        in_specs=[
            pl.BlockSpec((1, S5_CHUNK_ROWS * S5_CHUNK, V7X_LANES), lambda s: (s, 0, 0)),
            pl.BlockSpec((1, V7X_SUBLANES, S5_CHUNK_COLS), lambda s: (s, 0, 0)),
            slab_map, slab_map, slab_map, slab_map,
            pl.BlockSpec((S5_CHUNK + 1, 2, N_DIR, SLAB_GROUPS, S5_STATE),
                         lambda s: (0, 0, 0, s, 0)),
            pl.BlockSpec((1, 1, V7X_LANES), lambda s: (s, 0, 0)),
        ] + cast_specs,
        out_specs=[pl.BlockSpec((1, SLAB_PAIRS, 2, S5_CHUNK_ROWS, V7X_LANES),
                                lambda s: (s, 0, 0, 0, 0))] + cast_specs,
        out_shape=[jax.ShapeDtypeStruct((S5_SLABS, SLAB_PAIRS, 2, S5_CHUNK_ROWS, V7X_LANES),
                                        _f32)]
                  + [jax.ShapeDtypeStruct(w.shape, _bf16) for w in cast],
        scratch_shapes=[
            pltpu.VMEM((SLAB_PAIRS, S5_CHUNK_ROWS, PAIR_COLS), _bf16),
            pltpu.VMEM((SLAB_PAIRS, S5_CHUNK_ROWS, PAIR_COLS), _f32),
            pltpu.VMEM((N_DIR, SLAB_PAIRS, PAIR_COLS, PAIR_COLS), _bf16),
            pltpu.VMEM((N_DIR, S5_CHUNK_ROWS, 2 * SLAB_STATES), _f32),
            pltpu.VMEM((N_DIR, S5_CHUNK_ROWS, 2 * SLAB_STATES), _bf16),
            pltpu.VMEM((N_DIR, PAIR_COLS, PAIR_COLS), _f32),
            pltpu.VMEM((PAIR_COLS, PAIR_COLS), _f32),
            pltpu.VMEM((V7X_SUBLANES, 2 * SLAB_STATES), _f32),
        ],
        compiler_params=_params(("arbitrary",)),
        name="s5",
    )(u, u_meta, *maps, powers, d_skip, *cast)


def _mixer_kernel(h_ref, y_ref, q_ref, qp_ref, qn_ref, qm_ref, bc_ref, gs_ref, gc_ref,
                  wglu_ref, bglu_ref, wsup_ref, cw_ref, cb_ref, wcup_ref, wo_ref,
                  g_ref, wcast_ref, o_ref, wcast_out_ref, ys_scr, q_scr):
    t = pl.program_id(0)
    last = pl.num_programs(0) - 1
    tt = MIX_TILE_T
    tm = BATCH * tt
    n_c = tt // S5_CHUNK

    def columns(ref, b):
        return jnp.concatenate([ref[j, b] for j in range(IN_STEPS)], axis=1)

    def all_rows(ref):
        return jnp.concatenate([ref[j].reshape(tm, -1) for j in range(IN_STEPS)], axis=1)

    vs = []
    for b in range(BATCH):
        prev = jnp.where(t == 0, qm_ref[...], columns(qp_ref, b)).astype(_f32)
        nxt = jnp.where(t == last, 0.0, columns(qn_ref, b).astype(_f32))
        q_scr[b, 0:HALO_ROWS, :] = prev
        q_scr[b, HALO_ROWS:HALO_ROWS + tt, :] = columns(q_ref, b).astype(_f32)
        q_scr[b, HALO_ROWS + tt:, :] = nxt
        vs.append(cw_ref[0:1, :] * q_scr[b, HALO_ROWS - 1:HALO_ROWS - 1 + tt, :]
                  + cw_ref[1:2, :] * q_scr[b, HALO_ROWS:HALO_ROWS + tt, :]
                  + cw_ref[2:3, :] * q_scr[b, HALO_ROWS + 1:HALO_ROWS + 1 + tt, :]
                  + cb_ref[...])
    v = jnp.concatenate(vs, axis=0)
    bc = all_rows(bc_ref).astype(_f32)
    y_c = _dot((bc * v).astype(_bf16), wcup_ref[...])

    half_steps = S5_CHUNK // 2
    for s in range(S5_SLABS):
        for h in range(2):
            for b in range(BATCH):
                tiles = [y_ref[s, q, h, pl.ds(b, n_c, stride=BATCH), :]
                         for q in range(SLAB_PAIRS)]
                for k, tile in enumerate(_transpose_pieces(tiles)):
                    o = h * half_steps + k
                    ys_scr[s, pl.ds(b * tt + o, n_c, stride=S5_CHUNK), :] = tile
    ys = jnp.concatenate([ys_scr[s] for s in range(S5_SLABS)], axis=1)
    ys = jax.nn.gelu(ys)
    glu = _dot(ys.astype(_bf16), wglu_ref[...]) + bglu_ref[...]
    ys = ys * jax.nn.sigmoid(glu)
    y_s = _dot(ys.astype(_bf16), wsup_ref[...])

    merged = all_rows(gs_ref).astype(_f32) * y_s + all_rows(gc_ref).astype(_f32) * y_c
    mixed = _dot(merged.astype(_bf16), wo_ref[...])
    out = h_ref[...].reshape(tm, D_MODEL) + _rms_norm(mixed, g_ref[...])
    o_ref[...] = out.reshape(BATCH, tt, D_MODEL)
    wcast_out_ref[...] = wcast_ref[...].astype(_bf16)


def _mixer(x, y, q, q_meta, bc, gate_s, gate_c, w_glu, b_glu, w_s_up, conv_w, conv_b, w_c_up,
           w_o, g_post, w_cast):
    tt = MIX_TILE_T
    n_t = SEQ // tt
    cast_spec = pl.BlockSpec((w_cast.shape[0] // n_t, w_cast.shape[1]), lambda t: (t, 0))
    halo_per_tile = tt // HALO_ROWS
    n_halo = SEQ // HALO_ROWS
    tile = lambda t: (0, t, 0)
    stepped = lambda rows, cols: pl.BlockSpec((IN_STEPS, BATCH, rows, cols),
                                              lambda t: (0, 0, t, 0))
    resident = functools.partial(pl.BlockSpec, index_map=lambda t: (0, 0),
                                 pipeline_mode=pl.Buffered(1))
    return pl.pallas_call(
        _mixer_kernel,
        grid=(n_t,),
        in_specs=[
            pl.BlockSpec((BATCH, tt, D_MODEL), tile),
            pl.BlockSpec((S5_SLABS, SLAB_PAIRS, 2, tt // S5_CHUNK * BATCH, V7X_LANES),
                         lambda t: (0, 0, 0, t, 0)),
            stepped(tt, IN_SEG_COLS),
            pl.BlockSpec((IN_STEPS, BATCH, HALO_ROWS, IN_SEG_COLS),
                         lambda t: (0, 0, jnp.maximum(t * halo_per_tile - 1, 0), 0)),
            pl.BlockSpec((IN_STEPS, BATCH, HALO_ROWS, IN_SEG_COLS),
                         lambda t: (0, 0, jnp.minimum((t + 1) * halo_per_tile, n_halo - 1), 0)),
            resident((HALO_ROWS, CONV_WIDTH)),
            stepped(tt, IN_SEG_COLS),
            stepped(tt, 2 * IN_SEG_COLS),
            stepped(tt, 2 * IN_SEG_COLS),
            resident((S5_WIDTH, S5_WIDTH)),
            resident((1, S5_WIDTH)),
            resident((S5_WIDTH, D_MODEL)),
            resident((3, CONV_WIDTH)),
            resident((1, CONV_WIDTH)),
            resident((CONV_WIDTH, D_MODEL)),
            resident((D_MODEL, D_MODEL)),
            resident((1, D_MODEL)),
            cast_spec,
        ],
        out_specs=[pl.BlockSpec((BATCH, tt, D_MODEL), tile), cast_spec],
        out_shape=[jax.ShapeDtypeStruct((BATCH, SEQ, D_MODEL), _f32),
                   jax.ShapeDtypeStruct(w_cast.shape, _bf16)],
        scratch_shapes=[pltpu.VMEM((S5_SLABS, BATCH * tt, V7X_LANES), _f32),
                        pltpu.VMEM((BATCH, tt + 2 * HALO_ROWS, CONV_WIDTH), _f32)],
        compiler_params=_params(("arbitrary",)),
        name="mixer_tail",
    )(x, y, q, q, q, q_meta, bc, gate_s, gate_c, w_glu, b_glu, w_s_up, conv_w, conv_b, w_c_up,
      w_o, g_post, w_cast)


def _ffn_kernel(h_ref, gpre_ref, wgu_ref, wout_ref, gpost_ref, o_ref, hb_scr):
    j = pl.program_id(1)
    last = pl.num_programs(1) - 1

    def chunk(hb):
        gu = _dot(hb, wgu_ref[...])
        gate, up = gu[:, :FFN_TILE_HIDDEN], gu[:, FFN_TILE_HIDDEN:]
        return _dot((jax.nn.silu(gate) * up).astype(_bf16), wout_ref[...])

    @pl.when(j == 0)
    def _():
        hb = _rms_norm(h_ref[...], gpre_ref[...]).astype(_bf16)
        hb_scr[...] = hb
        o_ref[...] = chunk(hb)

    @pl.when(jnp.logical_and(j > 0, j < last))
    def _():
        o_ref[...] += chunk(hb_scr[...])

    @pl.when(j == last)
    def _():
        f = o_ref[...] + chunk(hb_scr[...])
        o_ref[...] = h_ref[...] + _rms_norm(f, gpost_ref[...])


def _ffn(h1, g_pre, w_in, w_out, g_post):
    tm, th = FFN_TILE_ROWS, FFN_TILE_HIDDEN
    n_hidden = FFN_HIDDEN // th
    return pl.pallas_call(
        _ffn_kernel,
        grid=(ROWS // tm, n_hidden),
        in_specs=[
            pl.BlockSpec((tm, D_MODEL), lambda i, j: (i, 0)),
            pl.BlockSpec((1, D_MODEL), lambda i, j: (0, 0)),
            pl.BlockSpec((D_MODEL, 2 * th), lambda i, j: (0, j)),
            pl.BlockSpec((th, D_MODEL), lambda i, j: (j, 0)),
            pl.BlockSpec((1, D_MODEL), lambda i, j: (0, 0)),
        ],
        out_specs=pl.BlockSpec((tm, D_MODEL), lambda i, j: (i, 0)),
        out_shape=jax.ShapeDtypeStruct((ROWS, D_MODEL), _f32),
        scratch_shapes=[pltpu.VMEM((tm, D_MODEL), _bf16)],
        compiler_params=_params(("arbitrary", "arbitrary")),
        name="ffn",
    )(h1, g_pre, w_in, w_out, g_post)


def kernel(x, meta, g_mix_pre, g_mix_post, g_ffn_pre, g_ffn_post, w_in, gate_b, lam_re, lam_im,
           log_dt, b_re, b_im, c_re, c_im, d_skip, w_glu, b_glu, w_s_up, conv_w, conv_b, w_c_up,
           w_o, w_ffn_in, w_ffn_out):
    l = 0
    w_in_b = _cast_w_in(w_in[l])
    gate_b2 = gate_b[l].reshape(2, 1, D_MODEL)
    g_pre = g_mix_pre[l].reshape(1, D_MODEL)

    powers, bz_re, bz_im = _discretise(lam_re[l], lam_im[l], log_dt[l], b_re[l], b_im[l])
    powers = powers.reshape(S5_CHUNK + 1, 2, N_DIR, S5_GROUPS, S5_STATE)
    maps = [m.reshape(N_DIR, S5_WIDTH, S5_STATE) for m in (bz_re, bz_im, c_re[l], c_im[l])]

    u5, bc, q, gate_s, gate_c, um_tiles, qm_tiles, w_ffn_in_b = _in_proj(
        x, meta.astype(x.dtype), g_pre, w_in_b, gate_b2, IN_TILE_ROWS,
        cast=((w_ffn_in[l], _ffn_in_plan()),))
    um5, q_meta = um_tiles[0], qm_tiles[0]

    u = u5.reshape(S5_SLABS, S5_CHUNK_ROWS * S5_CHUNK, V7X_LANES)
    u_meta = jnp.pad(um5.reshape(S5_SLABS, META_CHUNKS, S5_CHUNK_COLS),
                     ((0, 0), (0, V7X_SUBLANES - META_CHUNKS), (0, 0)))
    y, w_glu_b, w_s_up_b, w_c_up_b, w_o_b = _s5(
        u, u_meta, maps, powers, d_skip[l].reshape(S5_SLABS, 1, V7X_LANES),
        cast=(w_glu[l], w_s_up[l], w_c_up[l], w_o[l]))

    h1, w_ffn_out_b = _mixer(x, y, q, q_meta, bc, gate_s, gate_c, w_glu_b,
                             b_glu[l].reshape(1, S5_WIDTH), w_s_up_b, conv_w[l],
                             conv_b[l].reshape(1, CONV_WIDTH), w_c_up_b, w_o_b,
                             g_mix_post[l].reshape(1, D_MODEL), w_ffn_out[l])

    h2 = _ffn(h1.reshape(ROWS, D_MODEL), g_ffn_pre[l].reshape(1, D_MODEL),
              w_ffn_in_b, w_ffn_out_b, g_ffn_post[l].reshape(1, D_MODEL))
    return h2.reshape(BATCH, SEQ, D_MODEL)
```

```python
import functools
import math

import jax
import jax.numpy as jnp
from jax import lax
from jax.experimental import pallas as pl
from jax.experimental.pallas import tpu as pltpu

D_MODEL = 2048
BATCH = 4
SEQ = 2048
N_META = 16
S5_WIDTH = 1024
S5_GROUP = 16
S5_GROUPS = S5_WIDTH // S5_GROUP
S5_STATE = 64
N_DIR = 2
CONV_WIDTH = 1024
FFN_HIDDEN = ((math.ceil(8 * D_MODEL / 3) + 255) // 256) * 256
IN_COLS = S5_WIDTH + 3 * CONV_WIDTH + 2 * D_MODEL
RMS_EPS = 1e-6
LAM_RE_MAX = -1e-4

ROWS = SEQ * BATCH

V7X_SUBLANES = 8
V7X_LANES = 128
V7X_MXU_DIM = 256
V7X_VMEM_LIMIT_BYTES = 56 * 1024 * 1024

S5_SLABS = S5_WIDTH // V7X_LANES
SLAB_GROUPS = V7X_LANES // S5_GROUP
SLAB_STATES = SLAB_GROUPS * S5_STATE
PAIR_LANES = 2 * S5_GROUP
SLAB_PAIRS = V7X_LANES // PAIR_LANES
PAIR_STATES = 2 * S5_STATE
S5_CHUNK = V7X_SUBLANES
PAIR_COLS = S5_CHUNK * PAIR_LANES
assert PAIR_COLS == V7X_MXU_DIM
S5_CHUNKS = SEQ // S5_CHUNK
S5_CHUNK_ROWS = S5_CHUNKS * BATCH
S5_CHUNK_COLS = S5_CHUNK * V7X_LANES
META_CHUNKS = N_META // S5_CHUNK
STATE_REPEATS = V7X_LANES // S5_STATE

IN_TILE_ROWS = 1024
IN_STEPS = 8
IN_SEG_COLS = S5_WIDTH // IN_STEPS
IN_RING_SLOTS = 4
CAST_TILE_ROWS = 128
MIX_TILE_T = 64
FFN_TILE_ROWS = 1024
FFN_TILE_HIDDEN = 256
HALO_ROWS = 16

_f32 = jnp.float32
_bf16 = jnp.bfloat16


def _rms_norm(xf, g):
    r = lax.rsqrt(jnp.mean(xf * xf, axis=-1, keepdims=True) + RMS_EPS)
    return xf * r * g


def _params(sem):
    return pltpu.CompilerParams(dimension_semantics=sem,
                                vmem_limit_bytes=V7X_VMEM_LIMIT_BYTES)


def _dot(a, b):
    return jnp.dot(a, b, preferred_element_type=_f32)


def _dot_nt(a, b):
    return lax.dot_general(a, b, (((1,), (1,)), ((), ())), preferred_element_type=_f32)


def _discretise_kernel(lam_re_ref, lam_im_ref, log_dt_ref, b_re_ref, b_im_ref,
                       pow_ref, bz_re_ref, bz_im_ref):
    lr = jnp.minimum(lam_re_ref[...], LAM_RE_MAX)
    li = lam_im_ref[...]
    dt = jnp.exp(log_dt_ref[...])
    mag = jnp.exp(lr * dt)
    ar = mag * jnp.cos(li * dt)
    ai = mag * jnp.sin(li * dt)
    den = lr * lr + li * li
    nr = ar - 1.0
    zr = (nr * lr + ai * li) / den
    zi = (ai * lr - nr * li) / den
    pr = jnp.ones_like(ar)
    pi = jnp.zeros_like(ar)
    for m in range(S5_CHUNK + 1):
        pow_ref[m, 0] = pr
        pow_ref[m, 1] = pi
        pr, pi = pr * ar - pi * ai, pr * ai + pi * ar
    b_re = b_re_ref[...]
    b_im = b_im_ref[...]
    bz_re_ref[...] = zr[:, None, :] * b_re - zi[:, None, :] * b_im
    bz_im_ref[...] = zr[:, None, :] * b_im + zi[:, None, :] * b_re


def _discretise(lam_re, lam_im, log_dt, b_re, b_im):
    dg = N_DIR * S5_GROUPS
    b_re_t = jnp.swapaxes(b_re, -1, -2).reshape(dg, S5_GROUP, S5_STATE)
    b_im_t = jnp.swapaxes(b_im, -1, -2).reshape(dg, S5_GROUP, S5_STATE)
    return pl.pallas_call(
        _discretise_kernel,
        out_shape=(jax.ShapeDtypeStruct((S5_CHUNK + 1, 2, dg, S5_STATE), _f32),
                   jax.ShapeDtypeStruct((dg, S5_GROUP, S5_STATE), _f32),
                   jax.ShapeDtypeStruct((dg, S5_GROUP, S5_STATE), _f32)),
        name="s5_discretise",
    )(lam_re.reshape(dg, S5_STATE), lam_im.reshape(dg, S5_STATE),
      log_dt.reshape(dg, 1), b_re_t, b_im_t)


def _w_in_pieces(j):
    seg, gseg = IN_SEG_COLS, 2 * IN_SEG_COLS
    pieces = [(4 * S5_WIDTH + n * D_MODEL + j * gseg, gseg) for n in range(2)]
    pieces += [(n * S5_WIDTH + j * seg, seg) for n in range(4)]
    return pieces


def _cast_w_in_kernel(w_ref, o_ref):
    dst = 0
    for j in range(IN_STEPS):
        for src, width in _w_in_pieces(j):
            o_ref[:, dst:dst + width] = w_ref[:, src:src + width].astype(_bf16)
            dst += width


def _cast_w_in(w):
    spec = pl.BlockSpec((CAST_TILE_ROWS, IN_COLS), lambda i: (i, 0))
    return pl.pallas_call(
        _cast_w_in_kernel,
        grid=(D_MODEL // CAST_TILE_ROWS,),
        in_specs=[spec],
        out_specs=spec,
        out_shape=jax.ShapeDtypeStruct(w.shape, _bf16),
        compiler_params=_params(("arbitrary",)),
        name="cast_w_in",
    )(w)


def _ffn_in_plan():
    th = FFN_TILE_HIDDEN
    plan = []
    for j in range(FFN_HIDDEN // th):
        plan.append((j * th, 2 * j * th, th))
        plan.append((FFN_HIDDEN + j * th, (2 * j + 1) * th, th))
    return plan


def _in_proj_kernel(x_ref, m_ref, g_ref, w_hbm, gbs_ref, gbc_ref, *rest, cast_plans):
    n_cast = len(cast_plans)
    cast_in = rest[:n_cast]
    u_ref, bc_ref, q_ref, gs_ref, gc_ref, um_ref, qm_ref = rest[n_cast:n_cast + 7]
    cast_out = rest[n_cast + 7:2 * n_cast + 7]
    h_scr, w_buf, w_sem = rest[2 * n_cast + 7:]
    seg = IN_SEG_COLS
    tile_cols = IN_COLS // IN_STEPS
    tm = x_ref.shape[0]

    step = ((pl.program_id(0) * pl.num_programs(1) + pl.program_id(1)) * IN_STEPS
            + pl.program_id(2))
    n_steps = pl.num_programs(0) * pl.num_programs(1) * IN_STEPS

    def w_copy(s):
        cols = pl.ds(pl.multiple_of((s % IN_STEPS) * tile_cols, tile_cols), tile_cols)
        slot = s % IN_RING_SLOTS
        return pltpu.make_async_copy(w_hbm.at[:, cols], w_buf.at[slot], w_sem.at[slot])

    @pl.when(step == 0)
    def _():
        for s in range(IN_RING_SLOTS - 1):
            w_copy(s).start()

    @pl.when(step + IN_RING_SLOTS - 1 < n_steps)
    def _():
        w_copy(step + IN_RING_SLOTS - 1).start()

    w_copy(step).wait()
    w_ref = w_buf.at[step % IN_RING_SLOTS]

    def column_step(h):
        both = _dot(h, w_ref[...])
        proj, meta = both[:tm], both[tm:]
        gs_ref[...] = jax.nn.sigmoid(proj[:, 0:2 * seg] + gbs_ref[0]).astype(_bf16)
        gc_ref[...] = jax.nn.sigmoid(proj[:, 2 * seg:4 * seg] + gbc_ref[0]).astype(_bf16)
        for s in range(seg // V7X_LANES):
            lanes = slice(4 * seg + s * V7X_LANES, 4 * seg + (s + 1) * V7X_LANES)
            u_ref[s] = proj[:, lanes].reshape(-1, S5_CHUNK, V7X_LANES)
            um_ref[s] = meta[:, lanes].reshape(-1, S5_CHUNK, V7X_LANES)
        bc_ref[...] = proj[:, 6 * seg:7 * seg].astype(_bf16)
        q_ref[...] = (proj[:, 7 * seg:8 * seg] * proj[:, 5 * seg:6 * seg]).astype(_bf16)
        qm_ref[...] = (meta[:, 7 * seg:8 * seg] * meta[:, 5 * seg:6 * seg]).astype(_bf16)
        for src, dst, plan in zip(cast_in, cast_out, cast_plans):
            for s0, d0, width in plan:
                dst[:, d0:d0 + width] = src[:, s0:s0 + width].astype(_bf16)

    @pl.when(pl.program_id(2) == 0)
    def _():
        h = jnp.concatenate([_rms_norm(x_ref[...], g_ref[...]),
                             _rms_norm(m_ref[...], g_ref[...])], axis=0).astype(_bf16)
        h_scr[...] = h
        column_step(h)

    @pl.when(pl.program_id(2) > 0)
    def _():
        column_step(h_scr[...])


def _in_proj(x3, meta, g, w_in_b, gate_b, tile_rows, cast=()):
    nb, t_len, _ = x3.shape
    n_t = t_len // tile_rows
    n_steps = nb * n_t * IN_STEPS
    seg, gseg = IN_SEG_COLS, 2 * IN_SEG_COLS
    by_step = lambda b, t, j: (j, b, t, 0)
    step = lambda b, t, j: ((b * n_t + t) * IN_STEPS + j, 0)
    cast_w = [w for w, _ in cast]
    cast_plans = tuple(tuple(plan) if plan else ((0, 0, w.shape[1]),) for w, plan in cast)
    cast_specs = [pl.BlockSpec((w.shape[0] // n_steps, w.shape[1]), step) for w in cast_w]
    gb = lambda n: pl.BlockSpec((1, 1, gseg), lambda b, t, j: (n, 0, j))
    return pl.pallas_call(
        functools.partial(_in_proj_kernel, cast_plans=cast_plans),
        grid=(nb, n_t, IN_STEPS),
        in_specs=[
            pl.BlockSpec((None, tile_rows, D_MODEL), lambda b, t, j: (b, t, 0)),
            pl.BlockSpec((N_META, D_MODEL), lambda b, t, j: (0, 0)),
            pl.BlockSpec((1, D_MODEL), lambda b, t, j: (0, 0)),
            pl.BlockSpec(memory_space=pl.ANY),
            gb(0), gb(1),
        ] + cast_specs,
        out_specs=[
            pl.BlockSpec((seg // V7X_LANES, tile_rows // S5_CHUNK, None, S5_CHUNK, V7X_LANES),
                         lambda b, t, j: (j, t, b, 0, 0)),
            pl.BlockSpec((None, None, tile_rows, seg), by_step),
            pl.BlockSpec((None, None, tile_rows, seg), by_step),
            pl.BlockSpec((None, None, tile_rows, gseg), by_step),
            pl.BlockSpec((None, None, tile_rows, gseg), by_step),
            pl.BlockSpec((None, seg // V7X_LANES, META_CHUNKS, S5_CHUNK, V7X_LANES),
                         lambda b, t, j: (b * n_t + t, j, 0, 0, 0)),
            pl.BlockSpec((None, N_META, seg), lambda b, t, j: (b * n_t + t, 0, j)),
        ] + cast_specs,
        out_shape=[jax.ShapeDtypeStruct(
                       (S5_SLABS, t_len // S5_CHUNK, nb, S5_CHUNK, V7X_LANES), _f32),
                   jax.ShapeDtypeStruct((IN_STEPS, nb, t_len, seg), _bf16),
                   jax.ShapeDtypeStruct((IN_STEPS, nb, t_len, seg), _bf16),
                   jax.ShapeDtypeStruct((IN_STEPS, nb, t_len, gseg), _bf16),
                   jax.ShapeDtypeStruct((IN_STEPS, nb, t_len, gseg), _bf16),
                   jax.ShapeDtypeStruct((nb * n_t, S5_SLABS, META_CHUNKS, S5_CHUNK, V7X_LANES),
                                        _f32),
                   jax.ShapeDtypeStruct((nb * n_t, N_META, CONV_WIDTH), _bf16)]
                  + [jax.ShapeDtypeStruct(w.shape, _bf16) for w in cast_w],
        scratch_shapes=[pltpu.VMEM((tile_rows + N_META, D_MODEL), _bf16),
                        pltpu.VMEM((IN_RING_SLOTS, D_MODEL, IN_COLS // IN_STEPS), _bf16),
                        pltpu.SemaphoreType.DMA((IN_RING_SLOTS,))],
        compiler_params=_params(("arbitrary", "arbitrary", "arbitrary")),
        name="in_proj",
    )(x3, meta, g, w_in_b, gate_b, gate_b, *cast_w)


def _pair_map(ref, d, q):
    rows = lax.broadcasted_iota(jnp.int32, (PAIR_LANES, V7X_LANES), 0)
    cols = lax.broadcasted_iota(jnp.int32, (PAIR_LANES, V7X_LANES), 1)
    same_group = (rows // S5_GROUP) == (cols // S5_STATE)
    per_group = ref[d, q * PAIR_LANES:(q + 1) * PAIR_LANES, :]
    return jnp.where(same_group, jnp.concatenate([per_group] * STATE_REPEATS, axis=1), 0.0)


def _transpose_pieces(src):
    assert SLAB_PAIRS == 4 and len(src) == SLAB_PAIRS
    lane = lax.broadcasted_iota(jnp.int32, src[0].shape, 1)

    def exchange(a, b, width):
        keep = (lane // width) % 2 == 0
        return (jnp.where(keep, a, pltpu.roll(b, width, 1)),
                jnp.where(keep, pltpu.roll(a, V7X_LANES - width, 1), b))

    b0, b2 = exchange(src[0], src[2], 2 * PAIR_LANES)
    b1, b3 = exchange(src[1], src[3], 2 * PAIR_LANES)
    c0, c1 = exchange(b0, b1, PAIR_LANES)
    c2, c3 = exchange(b2, b3, PAIR_LANES)
    return [c0, c1, c2, c3]


def _boundary_scan(x_scr, a_re, a_im, init):
    n_tiles = S5_CHUNK_ROWS // V7X_SUBLANES
    low = lax.broadcasted_iota(jnp.int32, (V7X_SUBLANES, V7X_LANES), 0) < BATCH

    def step(ar, ai, sr, si, xr, xi):
        return ar * sr - ai * si + xr, ar * si + ai * sr + xi

    def swap(v):
        return pltpu.roll(v, BATCH, 0)

    def body(i, carry):
        rf = pl.multiple_of(i * V7X_SUBLANES, V7X_SUBLANES)
        rb = pl.multiple_of((n_tiles - 1 - i) * V7X_SUBLANES, V7X_SUBLANES)
        loaded = []
        for q in range(SLAB_PAIRS):
            for part in range(2):
                lanes = slice(q * PAIR_COLS + part * PAIR_STATES,
                              q * PAIR_COLS + (part + 1) * PAIR_STATES)
                loaded.append((lanes, x_scr[0, pl.ds(rf, V7X_SUBLANES), lanes],
                               x_scr[1, pl.ds(rb, V7X_SUBLANES), lanes]))
        new, stores = [], []
        for q in range(SLAB_PAIRS):
            (re, xfr, xbr), (im, xfi, xbi) = loaded[2 * q], loaded[2 * q + 1]
            w1r, w1i = jnp.where(low, xfr, xbr), jnp.where(low, xfi, xbi)
            w2r, w2i = swap(jnp.where(low, xbr, xfr)), swap(jnp.where(low, xbi, xfi))
            cr, ci = carry[2 * q], carry[2 * q + 1]
            s1r, s1i = step(a_re[q], a_im[q], cr, ci, w1r, w1i)
            s2r, s2i = step(a_re[q], a_im[q], s1r, s1i, w2r, w2i)
            t1r, t1i = swap(s1r), swap(s1i)
            stores += [(0, rf, re, jnp.where(low, cr, t1r)), (0, rf, im, jnp.where(low, ci, t1i)),
                       (1, rb, re, jnp.where(low, t1r, cr)), (1, rb, im, jnp.where(low, t1i, ci))]
            new += [s2r, s2i]
        for d, r0, lanes, value in stores:
            x_scr[d, pl.ds(r0, V7X_SUBLANES), lanes] = value
        return tuple(new)

    lax.fori_loop(0, n_tiles, body, init)


def _s5_kernel(u_ref, um_ref, bre_ref, bim_ref, cre_ref, cim_ref, p_ref, d_ref, *rest, n_cast):
    cast_in = rest[:n_cast]
    y_ref = rest[n_cast]
    cast_out = rest[n_cast + 1:2 * n_cast + 1]
    ub_scr, uf_scr, w_scr, x_scr, sp_scr, tap_scr, toep_scr, xm_scr = rest[2 * n_cast + 1:]
    for src, dst in zip(cast_in, cast_out):
        dst[...] = src[...].astype(_bf16)

    half_steps = S5_CHUNK // 2
    blk = PAIR_LANES

    def pair_cols(q):
        return slice(q * PAIR_COLS, (q + 1) * PAIR_COLS)

    @functools.cache
    def power(d, m, part, q):
        g = q * (PAIR_LANES // S5_GROUP)
        return jnp.concatenate([p_ref[m, part, d, g + k:g + k + 1, :]
                                for k in range(PAIR_LANES // S5_GROUP)], axis=1)

    def strided_u(j):
        return u_ref[0, pl.ds(j, S5_CHUNK_ROWS, stride=S5_CHUNK), :]

    um = um_ref[0]
    um_q = [[] for _ in range(SLAB_PAIRS)]
    for h in range(2):
        steps = range(h * half_steps, (h + 1) * half_steps)
        for q, t in enumerate(_transpose_pieces([strided_u(j) for j in steps])):
            uf_scr[q, :, h * V7X_LANES:(h + 1) * V7X_LANES] = t
            ub_scr[q, :, h * V7X_LANES:(h + 1) * V7X_LANES] = t.astype(_bf16)
        meta = [um[:, j * V7X_LANES:(j + 1) * V7X_LANES] for j in steps]
        for q, t in enumerate(_transpose_pieces(meta)):
            um_q[q].append(t)

    for d in range(N_DIR):
        for q in range(SLAB_PAIRS):
            w_re = _pair_map(bre_ref, d, q)
            w_im = _pair_map(bim_ref, d, q)
            for j in range(S5_CHUNK):
                m = (S5_CHUNK - 1 - j) if d == 0 else j
                pr = power(d, m, 0, q)
                pi = power(d, m, 1, q)
                rows = slice(j * blk, (j + 1) * blk)
                w_scr[d, q, rows, 0:PAIR_STATES] = (w_re * pr - w_im * pi).astype(_bf16)
                w_scr[d, q, rows, PAIR_STATES:] = (w_re * pi + w_im * pr).astype(_bf16)
            x_scr[d, :, pair_cols(q)] = _dot(ub_scr[q], w_scr[d, q])
            if d == 0:
                xm_scr[:, pair_cols(q)] = _dot(
                    jnp.concatenate(um_q[q], axis=1).astype(_bf16), w_scr[0, q])

    low = lax.broadcasted_iota(jnp.int32, (V7X_SUBLANES, V7X_LANES), 0) < BATCH
    a_re, a_im, init = [], [], []
    for q in range(SLAB_PAIRS):
        re = slice(q * PAIR_COLS, q * PAIR_COLS + PAIR_STATES)
        im = slice(q * PAIR_COLS + PAIR_STATES, (q + 1) * PAIR_COLS)
        ar = power(0, S5_CHUNK, 0, q)
        ai = power(0, S5_CHUNK, 1, q)
        a_re.append(jnp.where(low, ar, power(1, S5_CHUNK, 0, q)))
        a_im.append(jnp.where(low, ai, power(1, S5_CHUNK, 1, q)))
        sr = jnp.zeros((1, V7X_LANES), _f32)
        si = jnp.zeros((1, V7X_LANES), _f32)
        for c in range(META_CHUNKS):
            xr = xm_scr[c:c + 1, re]
            xi = xm_scr[c:c + 1, im]
            sr, si = ar * sr - ai * si + xr, ar * si + ai * sr + xi
        init.append(jnp.where(low, sr, 0.0))
        init.append(jnp.where(low, si, 0.0))
    _boundary_scan(x_scr, a_re, a_im, tuple(init))
    for d in range(N_DIR):
        sp_scr[d] = x_scr[d].astype(_bf16)

    def out_weights(c_re, c_imn, d, m, q):
        pr = power(d, m, 0, q)
        pi = power(d, m, 1, q)
        return jnp.concatenate([c_re * pr + c_imn * pi, c_imn * pr - c_re * pi],
                               axis=1).astype(_bf16)

    piece = lax.broadcasted_iota(jnp.int32, (1, V7X_LANES), 1) // PAIR_LANES

    for q in range(SLAB_PAIRS):
        wy = []
        for d in range(N_DIR):
            c_re = _pair_map(cre_ref, d, q)
            c_imn = -_pair_map(cim_ref, d, q)
            c_map = jnp.concatenate([c_re, c_imn], axis=1).astype(_bf16)
            tap_scr[d] = _dot_nt(w_scr[d, q], jnp.concatenate([c_map] * S5_CHUNK, axis=0))
            powers = [o + 1 if d == 0 else S5_CHUNK - o for o in range(S5_CHUNK)]
            wy.append(jnp.concatenate([out_weights(c_re, c_imn, d, m, q) for m in powers],
                                      axis=0))
        for o in range(S5_CHUNK):
            lanes = slice(o * blk, (o + 1) * blk)
            if o:
                toep_scr[0:o * blk, lanes] = tap_scr[0, (S5_CHUNK - 1 - o) * blk:
                                                     (S5_CHUNK - 1) * blk, lanes]
            toep_scr[o * blk:(o + 1) * blk, lanes] = (
                tap_scr[0, (S5_CHUNK - 1) * blk:, lanes] + tap_scr[1, 0:blk, lanes])
            if o < S5_CHUNK - 1:
                toep_scr[(o + 1) * blk:, lanes] = tap_scr[1, blk:(S5_CHUNK - o) * blk, lanes]
        y = _dot_nt(sp_scr[0, :, pair_cols(q)], wy[0])
        y += _dot_nt(sp_scr[1, :, pair_cols(q)], wy[1])
        y += _dot(ub_scr[q], toep_scr[...].astype(_bf16))
        d_q = jnp.where(piece == q, d_ref[0], 0.0)
        for shift in range(PAIR_LANES, V7X_LANES, PAIR_LANES):
            d_q = d_q + pltpu.roll(jnp.where(piece == q, d_ref[0], 0.0), shift, 1)
        y += uf_scr[q] * jnp.concatenate([d_q] * (PAIR_COLS // V7X_LANES), axis=1)
        for h in range(2):
            y_ref[0, q, h] = y[:, h * V7X_LANES:(h + 1) * V7X_LANES]


def _s5(u, u_meta, maps, powers, d_skip, cast=()):
    slab_map = pl.BlockSpec((N_DIR, V7X_LANES, S5_STATE), lambda s: (0, s, 0))
    cast_specs = [pl.BlockSpec((w.shape[0] // S5_SLABS, w.shape[1]), lambda s: (s, 0))
                  for w in cast]
    return pl.pallas_call(
        functools.partial(_s5_kernel, n_cast=len(cast)),
        grid=(S5_SLABS,),
        in_specs=[
            pl.BlockSpec((1, S5_CHUNK_ROWS * S5_CHUNK, V7X_LANES), lambda s: (s, 0, 0)),
            pl.BlockSpec((1, V7X_SUBLANES, S5_CHUNK_COLS), lambda s: (s, 0, 0)),
            slab_map, slab_map, slab_map, slab_map,
            pl.BlockSpec((S5_CHUNK + 1, 2, N_DIR, SLAB_GROUPS, S5_STATE),
                         lambda s: (0, 0, 0, s, 0)),
            pl.BlockSpec((1, 1, V7X_LANES), lambda s: (s, 0, 0)),
        ] + cast_specs,
        out_specs=[pl.BlockSpec((1, SLAB_PAIRS, 2, S5_CHUNK_ROWS, V7X_LANES),
                                lambda s: (s, 0, 0, 0, 0))] + cast_specs,
        out_shape=[jax.ShapeDtypeStruct((S5_SLABS, SLAB_PAIRS, 2, S5_CHUNK_ROWS, V7X_LANES),
                                        _f32)]
                  + [jax.ShapeDtypeStruct(w.shape, _bf16) for w in cast],
        scratch_shapes=[
            pltpu.VMEM((SLAB_PAIRS, S5_CHUNK_ROWS, PAIR_COLS), _bf16),
            pltpu.VMEM((SLAB_PAIRS, S5_CHUNK_ROWS, PAIR_COLS), _f32),
            pltpu.VMEM((N_DIR, SLAB_PAIRS, PAIR_COLS, PAIR_COLS), _bf16),
            pltpu.VMEM((N_DIR, S5_CHUNK_ROWS, 2 * SLAB_STATES), _f32),
            pltpu.VMEM((N_DIR, S5_CHUNK_ROWS, 2 * SLAB_STATES), _bf16),
            pltpu.VMEM((N_DIR, PAIR_COLS, PAIR_COLS), _f32),
            pltpu.VMEM((PAIR_COLS, PAIR_COLS), _f32),
            pltpu.VMEM((V7X_SUBLANES, 2 * SLAB_STATES), _f32),
        ],
        compiler_params=_params(("arbitrary",)),
        name="s5",
    )(u, u_meta, *maps, powers, d_skip, *cast)


def _mixer_kernel(h_ref, y_ref, q_ref, qp_ref, qn_ref, qm_ref, bc_ref, gs_ref, gc_ref,
                  wglu_ref, bglu_ref, wsup_ref, cw_ref, cb_ref, wcup_ref, wo_ref,
                  g_ref, wcast_ref, o_ref, wcast_out_ref, ys_scr, q_scr):
    t = pl.program_id(0)
    last = pl.num_programs(0) - 1
    tt = MIX_TILE_T
    tm = BATCH * tt
    n_c = tt // S5_CHUNK

    def columns(ref, b):
        return jnp.concatenate([ref[j, b] for j in range(IN_STEPS)], axis=1)

    def all_rows(ref):
        return jnp.concatenate([ref[j].reshape(tm, -1) for j in range(IN_STEPS)], axis=1)

    vs = []
    for b in range(BATCH):
        prev = jnp.where(t == 0, qm_ref[...], columns(qp_ref, b)).astype(_f32)
        nxt = jnp.where(t == last, 0.0, columns(qn_ref, b).astype(_f32))
        q_scr[b, 0:HALO_ROWS, :] = prev
        q_scr[b, HALO_ROWS:HALO_ROWS + tt, :] = columns(q_ref, b).astype(_f32)
        q_scr[b, HALO_ROWS + tt:, :] = nxt
        vs.append(cw_ref[0:1, :] * q_scr[b, HALO_ROWS - 1:HALO_ROWS - 1 + tt, :]
                  + cw_ref[1:2, :] * q_scr[b, HALO_ROWS:HALO_ROWS + tt, :]
                  + cw_ref[2:3, :] * q_scr[b, HALO_ROWS + 1:HALO_ROWS + 1 + tt, :]
                  + cb_ref[...])
    v = jnp.concatenate(vs, axis=0)
    bc = all_rows(bc_ref).astype(_f32)
    y_c = _dot((bc * v).astype(_bf16), wcup_ref[...])

    half_steps = S5_CHUNK // 2
    for s in range(S5_SLABS):
        for h in range(2):
            for b in range(BATCH):
                tiles = [y_ref[s, q, h, pl.ds(b, n_c, stride=BATCH), :]
                         for q in range(SLAB_PAIRS)]
                for k, tile in enumerate(_transpose_pieces(tiles)):
                    o = h * half_steps + k
                    ys_scr[s, pl.ds(b * tt + o, n_c, stride=S5_CHUNK), :] = tile
    ys = jnp.concatenate([ys_scr[s] for s in range(S5_SLABS)], axis=1)
    ys = jax.nn.gelu(ys)
    glu = _dot(ys.astype(_bf16), wglu_ref[...]) + bglu_ref[...]
    ys = ys * jax.nn.sigmoid(glu)
    y_s = _dot(ys.astype(_bf16), wsup_ref[...])

    merged = all_rows(gs_ref).astype(_f32) * y_s + all_rows(gc_ref).astype(_f32) * y_c
    mixed = _dot(merged.astype(_bf16), wo_ref[...])
    out = h_ref[...].reshape(tm, D_MODEL) + _rms_norm(mixed, g_ref[...])
    o_ref[...] = out.reshape(BATCH, tt, D_MODEL)
    wcast_out_ref[...] = wcast_ref[...].astype(_bf16)


def _mixer(x, y, q, q_meta, bc, gate_s, gate_c, w_glu, b_glu, w_s_up, conv_w, conv_b, w_c_up,
           w_o, g_post, w_cast):
    tt = MIX_TILE_T
    n_t = SEQ // tt
    cast_spec = pl.BlockSpec((w_cast.shape[0] // n_t, w_cast.shape[1]), lambda t: (t, 0))
    halo_per_tile = tt // HALO_ROWS
    n_halo = SEQ // HALO_ROWS
    tile = lambda t: (0, t, 0)
    stepped = lambda rows, cols: pl.BlockSpec((IN_STEPS, BATCH, rows, cols),
                                              lambda t: (0, 0, t, 0))
    resident = functools.partial(pl.BlockSpec, index_map=lambda t: (0, 0),
                                 pipeline_mode=pl.Buffered(1))
    return pl.pallas_call(
        _mixer_kernel,
        grid=(n_t,),
        in_specs=[
            pl.BlockSpec((BATCH, tt, D_MODEL), tile),
            pl.BlockSpec((S5_SLABS, SLAB_PAIRS, 2, tt // S5_CHUNK * BATCH, V7X_LANES),
                         lambda t: (0, 0, 0, t, 0)),
            stepped(tt, IN_SEG_COLS),
            pl.BlockSpec((IN_STEPS, BATCH, HALO_ROWS, IN_SEG_COLS),
                         lambda t: (0, 0, jnp.maximum(t * halo_per_tile - 1, 0), 0)),
            pl.BlockSpec((IN_STEPS, BATCH, HALO_ROWS, IN_SEG_COLS),
                         lambda t: (0, 0, jnp.minimum((t + 1) * halo_per_tile, n_halo - 1), 0)),
            resident((HALO_ROWS, CONV_WIDTH)),
            stepped(tt, IN_SEG_COLS),
            stepped(tt, 2 * IN_SEG_COLS),
            stepped(tt, 2 * IN_SEG_COLS),
            resident((S5_WIDTH, S5_WIDTH)),
            resident((1, S5_WIDTH)),
            resident((S5_WIDTH, D_MODEL)),
            resident((3, CONV_WIDTH)),
            resident((1, CONV_WIDTH)),
            resident((CONV_WIDTH, D_MODEL)),
            resident((D_MODEL, D_MODEL)),
            resident((1, D_MODEL)),
            cast_spec,
        ],
        out_specs=[pl.BlockSpec((BATCH, tt, D_MODEL), tile), cast_spec],
        out_shape=[jax.ShapeDtypeStruct((BATCH, SEQ, D_MODEL), _f32),
                   jax.ShapeDtypeStruct(w_cast.shape, _bf16)],
        scratch_shapes=[pltpu.VMEM((S5_SLABS, BATCH * tt, V7X_LANES), _f32),
                        pltpu.VMEM((BATCH, tt + 2 * HALO_ROWS, CONV_WIDTH), _f32)],
        compiler_params=_params(("arbitrary",)),
        name="mixer_tail",
    )(x, y, q, q, q, q_meta, bc, gate_s, gate_c, w_glu, b_glu, w_s_up, conv_w, conv_b, w_c_up,
      w_o, g_post, w_cast)


def _ffn_kernel(h_ref, gpre_ref, wgu_ref, wout_ref, gpost_ref, o_ref, hb_scr):
    j = pl.program_id(1)
    last = pl.num_programs(1) - 1

    def chunk(hb):
        gu = _dot(hb, wgu_ref[...])
        gate, up = gu[:, :FFN_TILE_HIDDEN], gu[:, FFN_TILE_HIDDEN:]
        return _dot((jax.nn.silu(gate) * up).astype(_bf16), wout_ref[...])

    @pl.when(j == 0)
    def _():
        hb = _rms_norm(h_ref[...], gpre_ref[...]).astype(_bf16)
        hb_scr[...] = hb
        o_ref[...] = chunk(hb)

    @pl.when(jnp.logical_and(j > 0, j < last))
    def _():
        o_ref[...] += chunk(hb_scr[...])

    @pl.when(j == last)
    def _():
        f = o_ref[...] + chunk(hb_scr[...])
        o_ref[...] = h_ref[...] + _rms_norm(f, gpost_ref[...])


def _ffn(h1, g_pre, w_in, w_out, g_post):
    tm, th = FFN_TILE_ROWS, FFN_TILE_HIDDEN
    n_hidden = FFN_HIDDEN // th
    return pl.pallas_call(
        _ffn_kernel,
        grid=(ROWS // tm, n_hidden),
        in_specs=[
            pl.BlockSpec((tm, D_MODEL), lambda i, j: (i, 0)),
            pl.BlockSpec((1, D_MODEL), lambda i, j: (0, 0)),
            pl.BlockSpec((D_MODEL, 2 * th), lambda i, j: (0, j)),
            pl.BlockSpec((th, D_MODEL), lambda i, j: (j, 0)),
            pl.BlockSpec((1, D_MODEL), lambda i, j: (0, 0)),
        ],
        out_specs=pl.BlockSpec((tm, D_MODEL), lambda i, j: (i, 0)),
        out_shape=jax.ShapeDtypeStruct((ROWS, D_MODEL), _f32),
        scratch_shapes=[pltpu.VMEM((tm, D_MODEL), _bf16)],
        compiler_params=_params(("arbitrary", "arbitrary")),
        name="ffn",
    )(h1, g_pre, w_in, w_out, g_post)


def kernel(x, meta, g_mix_pre, g_mix_post, g_ffn_pre, g_ffn_post, w_in, gate_b, lam_re, lam_im,
           log_dt, b_re, b_im, c_re, c_im, d_skip, w_glu, b_glu, w_s_up, conv_w, conv_b, w_c_up,
           w_o, w_ffn_in, w_ffn_out):
    l = 0
    w_in_b = _cast_w_in(w_in[l])
    gate_b2 = gate_b[l].reshape(2, 1, D_MODEL)
    g_pre = g_mix_pre[l].reshape(1, D_MODEL)

    powers, bz_re, bz_im = _discretise(lam_re[l], lam_im[l], log_dt[l], b_re[l], b_im[l])
    powers = powers.reshape(S5_CHUNK + 1, 2, N_DIR, S5_GROUPS, S5_STATE)
    maps = [m.reshape(N_DIR, S5_WIDTH, S5_STATE) for m in (bz_re, bz_im, c_re[l], c_im[l])]

    u5, bc, q, gate_s, gate_c, um_tiles, qm_tiles, w_ffn_in_b = _in_proj(
        x, meta.astype(x.dtype), g_pre, w_in_b, gate_b2, IN_TILE_ROWS,
        cast=((w_ffn_in[l], _ffn_in_plan()),))
    um5, q_meta = um_tiles[0], qm_tiles[0]

    u = u5.reshape(S5_SLABS, S5_CHUNK_ROWS * S5_CHUNK, V7X_LANES)
    u_meta = jnp.pad(um5.reshape(S5_SLABS, META_CHUNKS, S5_CHUNK_COLS),
                     ((0, 0), (0, V7X_SUBLANES - META_CHUNKS), (0, 0)))
    y, w_glu_b, w_s_up_b, w_c_up_b, w_o_b = _s5(
        u, u_meta, maps, powers, d_skip[l].reshape(S5_SLABS, 1, V7X_LANES),
        cast=(w_glu[l], w_s_up[l], w_c_up[l], w_o[l]))

    h1, w_ffn_out_b = _mixer(x, y, q, q_meta, bc, gate_s, gate_c, w_glu_b,
                             b_glu[l].reshape(1, S5_WIDTH), w_s_up_b, conv_w[l],
                             conv_b[l].reshape(1, CONV_WIDTH), w_c_up_b, w_o_b,
                             g_mix_post[l].reshape(1, D_MODEL), w_ffn_out[l])

    h2 = _ffn(h1.reshape(ROWS, D_MODEL), g_ffn_pre[l].reshape(1, D_MODEL),
              w_ffn_in_b, w_ffn_out_b, g_ffn_post[l].reshape(1, D_MODEL))
    return h2.reshape(BATCH, SEQ, D_MODEL)
```

```python
import functools
import math

import jax
import jax.numpy as jnp
from jax import lax
from jax.experimental import pallas as pl
from jax.experimental.pallas import tpu as pltpu

D_MODEL = 2048
BATCH = 4
SEQ = 2048
N_META = 16
S5_WIDTH = 1024
S5_GROUP = 16
S5_GROUPS = S5_WIDTH // S5_GROUP
S5_STATE = 64
N_DIR = 2
CONV_WIDTH = 1024
FFN_HIDDEN = ((math.ceil(8 * D_MODEL / 3) + 255) // 256) * 256
IN_COLS = S5_WIDTH + 3 * CONV_WIDTH + 2 * D_MODEL
RMS_EPS = 1e-6
LAM_RE_MAX = -1e-4

ROWS = SEQ * BATCH

V7X_SUBLANES = 8
V7X_LANES = 128
V7X_MXU_DIM = 256
V7X_VMEM_LIMIT_BYTES = 56 * 1024 * 1024

S5_SLABS = S5_WIDTH // V7X_LANES
SLAB_GROUPS = V7X_LANES // S5_GROUP
SLAB_STATES = SLAB_GROUPS * S5_STATE
PAIR_LANES = 2 * S5_GROUP
SLAB_PAIRS = V7X_LANES // PAIR_LANES
PAIR_STATES = 2 * S5_STATE
S5_CHUNK = V7X_SUBLANES
PAIR_COLS = S5_CHUNK * PAIR_LANES
assert PAIR_COLS == V7X_MXU_DIM
S5_CHUNKS = SEQ // S5_CHUNK
S5_CHUNK_ROWS = S5_CHUNKS * BATCH
S5_CHUNK_COLS = S5_CHUNK * V7X_LANES
META_CHUNKS = N_META // S5_CHUNK
STATE_REPEATS = V7X_LANES // S5_STATE

IN_TILE_ROWS = 1024
IN_STEPS = 8
IN_SEG_COLS = S5_WIDTH // IN_STEPS
IN_RING_SLOTS = 3
CAST_TILE_ROWS = 128
MIX_TILE_T = 64
FFN_TILE_ROWS = 1024
FFN_TILE_HIDDEN = 256
HALO_ROWS = 16

_f32 = jnp.float32
_bf16 = jnp.bfloat16


def _rms_norm(xf, g):
    r = lax.rsqrt(jnp.mean(xf * xf, axis=-1, keepdims=True) + RMS_EPS)
    return xf * r * g


def _params(sem):
    return pltpu.CompilerParams(dimension_semantics=sem,
                                vmem_limit_bytes=V7X_VMEM_LIMIT_BYTES)


def _dot(a, b):
    return jnp.dot(a, b, preferred_element_type=_f32)


def _dot_nt(a, b):
    return lax.dot_general(a, b, (((1,), (1,)), ((), ())), preferred_element_type=_f32)


def _discretise_kernel(lam_re_ref, lam_im_ref, log_dt_ref, b_re_ref, b_im_ref,
                       pow_ref, bz_re_ref, bz_im_ref):
    lr = jnp.minimum(lam_re_ref[...], LAM_RE_MAX)
    li = lam_im_ref[...]
    dt = jnp.exp(log_dt_ref[...])
    mag = jnp.exp(lr * dt)
    ar = mag * jnp.cos(li * dt)
    ai = mag * jnp.sin(li * dt)
    den = lr * lr + li * li
    nr = ar - 1.0
    zr = (nr * lr + ai * li) / den
    zi = (ai * lr - nr * li) / den
    pr = jnp.ones_like(ar)
    pi = jnp.zeros_like(ar)
    for m in range(S5_CHUNK + 1):
        pow_ref[m, 0] = pr
        pow_ref[m, 1] = pi
        pr, pi = pr * ar - pi * ai, pr * ai + pi * ar
    b_re = b_re_ref[...]
    b_im = b_im_ref[...]
    bz_re_ref[...] = zr[:, None, :] * b_re - zi[:, None, :] * b_im
    bz_im_ref[...] = zr[:, None, :] * b_im + zi[:, None, :] * b_re


def _discretise(lam_re, lam_im, log_dt, b_re, b_im):
    dg = N_DIR * S5_GROUPS
    b_re_t = jnp.swapaxes(b_re, -1, -2).reshape(dg, S5_GROUP, S5_STATE)
    b_im_t = jnp.swapaxes(b_im, -1, -2).reshape(dg, S5_GROUP, S5_STATE)
    return pl.pallas_call(
        _discretise_kernel,
        out_shape=(jax.ShapeDtypeStruct((S5_CHUNK + 1, 2, dg, S5_STATE), _f32),
                   jax.ShapeDtypeStruct((dg, S5_GROUP, S5_STATE), _f32),
                   jax.ShapeDtypeStruct((dg, S5_GROUP, S5_STATE), _f32)),
        name="s5_discretise",
    )(lam_re.reshape(dg, S5_STATE), lam_im.reshape(dg, S5_STATE),
      log_dt.reshape(dg, 1), b_re_t, b_im_t)


def _w_in_pieces(j):
    seg, gseg = IN_SEG_COLS, 2 * IN_SEG_COLS
    pieces = [(4 * S5_WIDTH + n * D_MODEL + j * gseg, gseg) for n in range(2)]
    pieces += [(n * S5_WIDTH + j * seg, seg) for n in range(4)]
    return pieces


def _cast_w_in_kernel(w_ref, o_ref):
    dst = 0
    for j in range(IN_STEPS):
        for src, width in _w_in_pieces(j):
            o_ref[:, dst:dst + width] = w_ref[:, src:src + width].astype(_bf16)
            dst += width


def _cast_w_in(w):
    spec = pl.BlockSpec((CAST_TILE_ROWS, IN_COLS), lambda i: (i, 0))
    return pl.pallas_call(
        _cast_w_in_kernel,
        grid=(D_MODEL // CAST_TILE_ROWS,),
        in_specs=[spec],
        out_specs=spec,
        out_shape=jax.ShapeDtypeStruct(w.shape, _bf16),
        compiler_params=_params(("arbitrary",)),
        name="cast_w_in",
    )(w)


def _ffn_in_plan():
    th = FFN_TILE_HIDDEN
    plan = []
    for j in range(FFN_HIDDEN // th):
        plan.append((j * th, 2 * j * th, th))
        plan.append((FFN_HIDDEN + j * th, (2 * j + 1) * th, th))
    return plan


def _in_proj_kernel(x_ref, m_ref, g_ref, w_hbm, gbs_ref, gbc_ref, *rest, cast_plans):
    n_cast = len(cast_plans)
    cast_in = rest[:n_cast]
    u_ref, bc_ref, q_ref, gs_ref, gc_ref, um_ref, qm_ref = rest[n_cast:n_cast + 7]
    cast_out = rest[n_cast + 7:2 * n_cast + 7]
    h_scr, w_buf, w_sem = rest[2 * n_cast + 7:]
    seg = IN_SEG_COLS
    tile_cols = IN_COLS // IN_STEPS
    tm = x_ref.shape[0]

    step = ((pl.program_id(0) * pl.num_programs(1) + pl.program_id(1)) * IN_STEPS
            + pl.program_id(2))
    n_steps = pl.num_programs(0) * pl.num_programs(1) * IN_STEPS

    def w_copy(s):
        cols = pl.ds(pl.multiple_of((s % IN_STEPS) * tile_cols, tile_cols), tile_cols)
        slot = s % IN_RING_SLOTS
        return pltpu.make_async_copy(w_hbm.at[:, cols], w_buf.at[slot], w_sem.at[slot])

    @pl.when(step == 0)
    def _():
        for s in range(IN_RING_SLOTS - 1):
            w_copy(s).start()

    @pl.when(step + IN_RING_SLOTS - 1 < n_steps)
    def _():
        w_copy(step + IN_RING_SLOTS - 1).start()

    w_copy(step).wait()
    w_ref = w_buf.at[step % IN_RING_SLOTS]

    def column_step(h):
        both = _dot(h, w_ref[...])
        proj, meta = both[:tm], both[tm:]
        gs_ref[...] = jax.nn.sigmoid(proj[:, 0:2 * seg] + gbs_ref[0]).astype(_bf16)
        gc_ref[...] = jax.nn.sigmoid(proj[:, 2 * seg:4 * seg] + gbc_ref[0]).astype(_bf16)
        for s in range(seg // V7X_LANES):
            lanes = slice(4 * seg + s * V7X_LANES, 4 * seg + (s + 1) * V7X_LANES)
            u_ref[s] = proj[:, lanes].reshape(-1, S5_CHUNK, V7X_LANES)
            um_ref[s] = meta[:, lanes].reshape(-1, S5_CHUNK, V7X_LANES)
        bc_ref[...] = proj[:, 6 * seg:7 * seg].astype(_bf16)
        q_ref[...] = (proj[:, 7 * seg:8 * seg] * proj[:, 5 * seg:6 * seg]).astype(_bf16)
        qm_ref[...] = (meta[:, 7 * seg:8 * seg] * meta[:, 5 * seg:6 * seg]).astype(_bf16)
        for src, dst, plan in zip(cast_in, cast_out, cast_plans):
            for s0, d0, width in plan:
                dst[:, d0:d0 + width] = src[:, s0:s0 + width].astype(_bf16)

    @pl.when(pl.program_id(2) == 0)
    def _():
        h = jnp.concatenate([_rms_norm(x_ref[...], g_ref[...]),
                             _rms_norm(m_ref[...], g_ref[...])], axis=0).astype(_bf16)
        h_scr[...] = h
        column_step(h)

    @pl.when(pl.program_id(2) > 0)
    def _():
        column_step(h_scr[...])


def _in_proj(x3, meta, g, w_in_b, gate_b, tile_rows, cast=()):
    nb, t_len, _ = x3.shape
    n_t = t_len // tile_rows
    n_steps = nb * n_t * IN_STEPS
    seg, gseg = IN_SEG_COLS, 2 * IN_SEG_COLS
    by_step = lambda b, t, j: (j, b, t, 0)
    step = lambda b, t, j: ((b * n_t + t) * IN_STEPS + j, 0)
    cast_w = [w for w, _ in cast]
    cast_plans = tuple(tuple(plan) if plan else ((0, 0, w.shape[1]),) for w, plan in cast)
    cast_specs = [pl.BlockSpec((w.shape[0] // n_steps, w.shape[1]), step) for w in cast_w]
    gb = lambda n: pl.BlockSpec((1, 1, gseg), lambda b, t, j: (n, 0, j))
    return pl.pallas_call(
        functools.partial(_in_proj_kernel, cast_plans=cast_plans),
        grid=(nb, n_t, IN_STEPS),
        in_specs=[
            pl.BlockSpec((None, tile_rows, D_MODEL), lambda b, t, j: (b, t, 0)),
            pl.BlockSpec((N_META, D_MODEL), lambda b, t, j: (0, 0)),
            pl.BlockSpec((1, D_MODEL), lambda b, t, j: (0, 0)),
            pl.BlockSpec(memory_space=pl.ANY),
            gb(0), gb(1),
        ] + cast_specs,
        out_specs=[
            pl.BlockSpec((seg // V7X_LANES, tile_rows // S5_CHUNK, None, S5_CHUNK, V7X_LANES),
                         lambda b, t, j: (j, t, b, 0, 0)),
            pl.BlockSpec((None, None, tile_rows, seg), by_step),
            pl.BlockSpec((None, None, tile_rows, seg), by_step),
            pl.BlockSpec((None, None, tile_rows, gseg), by_step),
            pl.BlockSpec((None, None, tile_rows, gseg), by_step),
            pl.BlockSpec((None, seg // V7X_LANES, META_CHUNKS, S5_CHUNK, V7X_LANES),
                         lambda b, t, j: (b * n_t + t, j, 0, 0, 0)),
            pl.BlockSpec((None, N_META, seg), lambda b, t, j: (b * n_t + t, 0, j)),
        ] + cast_specs,
        out_shape=[jax.ShapeDtypeStruct(
                       (S5_SLABS, t_len // S5_CHUNK, nb, S5_CHUNK, V7X_LANES), _f32),
                   jax.ShapeDtypeStruct((IN_STEPS, nb, t_len, seg), _bf16),
                   jax.ShapeDtypeStruct((IN_STEPS, nb, t_len, seg), _bf16),
                   jax.ShapeDtypeStruct((IN_STEPS, nb, t_len, gseg), _bf16),
                   jax.ShapeDtypeStruct((IN_STEPS, nb, t_len, gseg), _bf16),
                   jax.ShapeDtypeStruct((nb * n_t, S5_SLABS, META_CHUNKS, S5_CHUNK, V7X_LANES),
                                        _f32),
                   jax.ShapeDtypeStruct((nb * n_t, N_META, CONV_WIDTH), _bf16)]
                  + [jax.ShapeDtypeStruct(w.shape, _bf16) for w in cast_w],
        scratch_shapes=[pltpu.VMEM((tile_rows + N_META, D_MODEL), _bf16),
                        pltpu.VMEM((IN_RING_SLOTS, D_MODEL, IN_COLS // IN_STEPS), _bf16),
                        pltpu.SemaphoreType.DMA((IN_RING_SLOTS,))],
        compiler_params=_params(("arbitrary", "arbitrary", "arbitrary")),
        name="in_proj",
    )(x3, meta, g, w_in_b, gate_b, gate_b, *cast_w)


def _pair_map(ref, d, q):
    rows = lax.broadcasted_iota(jnp.int32, (PAIR_LANES, V7X_LANES), 0)
    cols = lax.broadcasted_iota(jnp.int32, (PAIR_LANES, V7X_LANES), 1)
    same_group = (rows // S5_GROUP) == (cols // S5_STATE)
    per_group = ref[d, q * PAIR_LANES:(q + 1) * PAIR_LANES, :]
    return jnp.where(same_group, jnp.concatenate([per_group] * STATE_REPEATS, axis=1), 0.0)


def _transpose_pieces(src):
    assert SLAB_PAIRS == 4 and len(src) == SLAB_PAIRS
    lane = lax.broadcasted_iota(jnp.int32, src[0].shape, 1)

    def exchange(a, b, width):
        keep = (lane // width) % 2 == 0
        return (jnp.where(keep, a, pltpu.roll(b, width, 1)),
                jnp.where(keep, pltpu.roll(a, V7X_LANES - width, 1), b))

    b0, b2 = exchange(src[0], src[2], 2 * PAIR_LANES)
    b1, b3 = exchange(src[1], src[3], 2 * PAIR_LANES)
    c0, c1 = exchange(b0, b1, PAIR_LANES)
    c2, c3 = exchange(b2, b3, PAIR_LANES)
    return [c0, c1, c2, c3]


def _boundary_scan(x_scr, a_re, a_im, init):
    n_tiles = S5_CHUNK_ROWS // V7X_SUBLANES
    low = lax.broadcasted_iota(jnp.int32, (V7X_SUBLANES, V7X_LANES), 0) < BATCH

    def step(ar, ai, sr, si, xr, xi):
        return ar * sr - ai * si + xr, ar * si + ai * sr + xi

    def swap(v):
        return pltpu.roll(v, BATCH, 0)

    def body(i, carry):
        rf = pl.multiple_of(i * V7X_SUBLANES, V7X_SUBLANES)
        rb = pl.multiple_of((n_tiles - 1 - i) * V7X_SUBLANES, V7X_SUBLANES)
        loaded = []
        for q in range(SLAB_PAIRS):
            for part in range(2):
                lanes = slice(q * PAIR_COLS + part * PAIR_STATES,
                              q * PAIR_COLS + (part + 1) * PAIR_STATES)
                loaded.append((lanes, x_scr[0, pl.ds(rf, V7X_SUBLANES), lanes],
                               x_scr[1, pl.ds(rb, V7X_SUBLANES), lanes]))
        new, stores = [], []
        for q in range(SLAB_PAIRS):
            (re, xfr, xbr), (im, xfi, xbi) = loaded[2 * q], loaded[2 * q + 1]
            w1r, w1i = jnp.where(low, xfr, xbr), jnp.where(low, xfi, xbi)
            w2r, w2i = swap(jnp.where(low, xbr, xfr)), swap(jnp.where(low, xbi, xfi))
            cr, ci = carry[2 * q], carry[2 * q + 1]
            s1r, s1i = step(a_re[q], a_im[q], cr, ci, w1r, w1i)
            s2r, s2i = step(a_re[q], a_im[q], s1r, s1i, w2r, w2i)
            t1r, t1i = swap(s1r), swap(s1i)
            stores += [(0, rf, re, jnp.where(low, cr, t1r)), (0, rf, im, jnp.where(low, ci, t1i)),
                       (1, rb, re, jnp.where(low, t1r, cr)), (1, rb, im, jnp.where(low, t1i, ci))]
            new += [s2r, s2i]
        for d, r0, lanes, value in stores:
            x_scr[d, pl.ds(r0, V7X_SUBLANES), lanes] = value
        return tuple(new)

    lax.fori_loop(0, n_tiles, body, init)


def _s5_kernel(u_ref, um_ref, bre_ref, bim_ref, cre_ref, cim_ref, p_ref, d_ref, *rest, n_cast):
    cast_in = rest[:n_cast]
    y_ref = rest[n_cast]
    cast_out = rest[n_cast + 1:2 * n_cast + 1]
    ub_scr, w_scr, x_scr, sp_scr, tap_scr, toep_scr, xm_scr = rest[2 * n_cast + 1:]
    for src, dst in zip(cast_in, cast_out):
        dst[...] = src[...].astype(_bf16)

    half_steps = S5_CHUNK // 2
    blk = PAIR_LANES

    def pair_cols(q):
        return slice(q * PAIR_COLS, (q + 1) * PAIR_COLS)

    @functools.cache
    def power(d, m, part, q):
        g = q * (PAIR_LANES // S5_GROUP)
        return jnp.concatenate([p_ref[m, part, d, g + k:g + k + 1, :]
                                for k in range(PAIR_LANES // S5_GROUP)], axis=1)

    def strided_u(j):
        return u_ref[0, pl.ds(j, S5_CHUNK_ROWS, stride=S5_CHUNK), :]

    um = um_ref[0]
    um_q = [[] for _ in range(SLAB_PAIRS)]
    for h in range(2):
        steps = range(h * half_steps, (h + 1) * half_steps)
        for q, t in enumerate(_transpose_pieces([strided_u(j) for j in steps])):
            ub_scr[q, :, h * V7X_LANES:(h + 1) * V7X_LANES] = t.astype(_bf16)
        meta = [um[:, j * V7X_LANES:(j + 1) * V7X_LANES] for j in steps]
        for q, t in enumerate(_transpose_pieces(meta)):
            um_q[q].append(t)

    for d in range(N_DIR):
        for q in range(SLAB_PAIRS):
            w_re = _pair_map(bre_ref, d, q)
            w_im = _pair_map(bim_ref, d, q)
            for j in range(S5_CHUNK):
                m = (S5_CHUNK - 1 - j) if d == 0 else j
                pr = power(d, m, 0, q)
                pi = power(d, m, 1, q)
                rows = slice(j * blk, (j + 1) * blk)
                w_scr[d, q, rows, 0:PAIR_STATES] = (w_re * pr - w_im * pi).astype(_bf16)
                w_scr[d, q, rows, PAIR_STATES:] = (w_re * pi + w_im * pr).astype(_bf16)
            x_scr[d, :, pair_cols(q)] = _dot(ub_scr[q], w_scr[d, q])
            if d == 0:
                xm_scr[:, pair_cols(q)] = _dot(
                    jnp.concatenate(um_q[q], axis=1).astype(_bf16), w_scr[0, q])

    low = lax.broadcasted_iota(jnp.int32, (V7X_SUBLANES, V7X_LANES), 0) < BATCH
    a_re, a_im, init = [], [], []
    for q in range(SLAB_PAIRS):
        re = slice(q * PAIR_COLS, q * PAIR_COLS + PAIR_STATES)
        im = slice(q * PAIR_COLS + PAIR_STATES, (q + 1) * PAIR_COLS)
        ar = power(0, S5_CHUNK, 0, q)
        ai = power(0, S5_CHUNK, 1, q)
        a_re.append(jnp.where(low, ar, power(1, S5_CHUNK, 0, q)))
        a_im.append(jnp.where(low, ai, power(1, S5_CHUNK, 1, q)))
        sr = jnp.zeros((1, V7X_LANES), _f32)
        si = jnp.zeros((1, V7X_LANES), _f32)
        for c in range(META_CHUNKS):
            xr = xm_scr[c:c + 1, re]
            xi = xm_scr[c:c + 1, im]
            sr, si = ar * sr - ai * si + xr, ar * si + ai * sr + xi
        init.append(jnp.where(low, sr, 0.0))
        init.append(jnp.where(low, si, 0.0))
    _boundary_scan(x_scr, a_re, a_im, tuple(init))
    for d in range(N_DIR):
        sp_scr[d] = x_scr[d].astype(_bf16)

    def out_weights(c_re, c_imn, d, m, q):
        pr = power(d, m, 0, q)
        pi = power(d, m, 1, q)
        return jnp.concatenate([c_re * pr + c_imn * pi, c_imn * pr - c_re * pi],
                               axis=1).astype(_bf16)

    piece = lax.broadcasted_iota(jnp.int32, (1, V7X_LANES), 1) // PAIR_LANES

    for q in range(SLAB_PAIRS):
        wy = []
        for d in range(N_DIR):
            c_re = _pair_map(cre_ref, d, q)
            c_imn = -_pair_map(cim_ref, d, q)
            c_map = jnp.concatenate([c_re, c_imn], axis=1).astype(_bf16)
            tap_scr[d] = _dot_nt(w_scr[d, q], jnp.concatenate([c_map] * S5_CHUNK, axis=0))
            powers = [o + 1 if d == 0 else S5_CHUNK - o for o in range(S5_CHUNK)]
            wy.append(jnp.concatenate([out_weights(c_re, c_imn, d, m, q) for m in powers],
                                      axis=0))
        for o in range(S5_CHUNK):
            lanes = slice(o * blk, (o + 1) * blk)
            if o:
                toep_scr[0:o * blk, lanes] = tap_scr[0, (S5_CHUNK - 1 - o) * blk:
                                                     (S5_CHUNK - 1) * blk, lanes]
            toep_scr[o * blk:(o + 1) * blk, lanes] = (
                tap_scr[0, (S5_CHUNK - 1) * blk:, lanes] + tap_scr[1, 0:blk, lanes])
            if o < S5_CHUNK - 1:
                toep_scr[(o + 1) * blk:, lanes] = tap_scr[1, blk:(S5_CHUNK - o) * blk, lanes]
        y = _dot_nt(sp_scr[0, :, pair_cols(q)], wy[0])
        y += _dot_nt(sp_scr[1, :, pair_cols(q)], wy[1])
        d_q = jnp.where(piece == q, d_ref[0], 0.0)
        for shift in range(PAIR_LANES, V7X_LANES, PAIR_LANES):
            d_q = d_q + pltpu.roll(jnp.where(piece == q, d_ref[0], 0.0), shift, 1)
        d_row = jnp.concatenate([d_q] * (PAIR_COLS // V7X_LANES), axis=1)
        on_diagonal = (lax.broadcasted_iota(jnp.int32, (PAIR_COLS, PAIR_COLS), 0)
                       == lax.broadcasted_iota(jnp.int32, (PAIR_COLS, PAIR_COLS), 1))
        toeplitz = toep_scr[...] + jnp.where(on_diagonal, d_row, 0.0)
        y += _dot(ub_scr[q], toeplitz.astype(_bf16))
        for h in range(2):
            y_ref[0, q, h] = y[:, h * V7X_LANES:(h + 1) * V7X_LANES]


def _s5(u, u_meta, maps, powers, d_skip, cast=()):
    slab_map = pl.BlockSpec((N_DIR, V7X_LANES, S5_STATE), lambda s: (0, s, 0))
    cast_specs = [pl.BlockSpec((w.shape[0] // S5_SLABS, w.shape[1]), lambda s: (s, 0))
                  for w in cast]
    return pl.pallas_call(
        functools.partial(_s5_kernel, n_cast=len(cast)),
        grid=(S5_SLABS,),
        in_specs=[
            pl.BlockSpec((1, S5_CHUNK_ROWS * S5_CHUNK, V7X_LANES), lambda s: (s, 0, 0)),
            pl.BlockSpec((1, V7X_SUBLANES, S5_CHUNK_COLS), lambda s: (s, 0, 0)),
            slab_map, slab_map, slab_map, slab_map,
            pl.BlockSpec((S5_CHUNK + 1, 2, N_DIR, SLAB_GROUPS, S5_STATE),
                         lambda s: (0, 0, 0, s, 0)),
            pl.BlockSpec((1, 1, V7X_LANES), lambda s: (s, 0, 0)),
        ] + cast_specs,
        out_specs=[pl.BlockSpec((1, SLAB_PAIRS, 2, S5_CHUNK_ROWS, V7X_LANES),
                                lambda s: (s, 0, 0, 0, 0))] + cast_specs,
        out_shape=[jax.ShapeDtypeStruct((S5_SLABS, SLAB_PAIRS, 2, S5_CHUNK_ROWS, V7X_LANES),
                                        _f32)]
                  + [jax.ShapeDtypeStruct(w.shape, _bf16) for w in cast],
        scratch_shapes=[
            pltpu.VMEM((SLAB_PAIRS, S5_CHUNK_ROWS, PAIR_COLS), _bf16),
            pltpu.VMEM((N_DIR, SLAB_PAIRS, PAIR_COLS, PAIR_COLS), _bf16),
            pltpu.VMEM((N_DIR, S5_CHUNK_ROWS, 2 * SLAB_STATES), _f32),
            pltpu.VMEM((N_DIR, S5_CHUNK_ROWS, 2 * SLAB_STATES), _bf16),
            pltpu.VMEM((N_DIR, PAIR_COLS, PAIR_COLS), _f32),
            pltpu.VMEM((PAIR_COLS, PAIR_COLS), _f32),
            pltpu.VMEM((V7X_SUBLANES, 2 * SLAB_STATES), _f32),
        ],
        compiler_params=_params(("arbitrary",)),
        name="s5",
    )(u, u_meta, *maps, powers, d_skip, *cast)


def _mixer_kernel(h_ref, y_ref, q_ref, qp_ref, qn_ref, qm_ref, bc_ref, gs_ref, gc_ref,
                  wglu_ref, bglu_ref, wsup_ref, cw_ref, cb_ref, wcup_ref, wo_ref,
                  g_ref, wcast_ref, o_ref, wcast_out_ref, ys_scr, q_scr):
    t = pl.program_id(0)
    last = pl.num_programs(0) - 1
    tt = MIX_TILE_T
    tm = BATCH * tt
    n_c = tt // S5_CHUNK

    def columns(ref, b):
        return jnp.concatenate([ref[j, b] for j in range(IN_STEPS)], axis=1)

    def all_rows(ref):
        return jnp.concatenate([ref[j].reshape(tm, -1) for j in range(IN_STEPS)], axis=1)

    vs = []
    for b in range(BATCH):
        prev = jnp.where(t == 0, qm_ref[...], columns(qp_ref, b)).astype(_f32)
        nxt = jnp.where(t == last, 0.0, columns(qn_ref, b).astype(_f32))
        q_scr[b, 0:HALO_ROWS, :] = prev
        q_scr[b, HALO_ROWS:HALO_ROWS + tt, :] = columns(q_ref, b).astype(_f32)
        q_scr[b, HALO_ROWS + tt:, :] = nxt
        vs.append(cw_ref[0:1, :] * q_scr[b, HALO_ROWS - 1:HALO_ROWS - 1 + tt, :]
                  + cw_ref[1:2, :] * q_scr[b, HALO_ROWS:HALO_ROWS + tt, :]
                  + cw_ref[2:3, :] * q_scr[b, HALO_ROWS + 1:HALO_ROWS + 1 + tt, :]
                  + cb_ref[...])
    v = jnp.concatenate(vs, axis=0)
    bc = all_rows(bc_ref).astype(_f32)
    y_c = _dot((bc * v).astype(_bf16), wcup_ref[...])

    half_steps = S5_CHUNK // 2
    for s in range(S5_SLABS):
        for h in range(2):
            for b in range(BATCH):
                tiles = [y_ref[s, q, h, pl.ds(b, n_c, stride=BATCH), :]
                         for q in range(SLAB_PAIRS)]
                for k, tile in enumerate(_transpose_pieces(tiles)):
                    o = h * half_steps + k
                    ys_scr[s, pl.ds(b * tt + o, n_c, stride=S5_CHUNK), :] = tile
    ys = jnp.concatenate([ys_scr[s] for s in range(S5_SLABS)], axis=1)
    ys = jax.nn.gelu(ys)
    glu = _dot(ys.astype(_bf16), wglu_ref[...]) + bglu_ref[...]
    ys = ys * jax.nn.sigmoid(glu)
    y_s = _dot(ys.astype(_bf16), wsup_ref[...])

    merged = all_rows(gs_ref).astype(_f32) * y_s + all_rows(gc_ref).astype(_f32) * y_c
    mixed = _dot(merged.astype(_bf16), wo_ref[...])
    out = h_ref[...].reshape(tm, D_MODEL) + _rms_norm(mixed, g_ref[...])
    o_ref[...] = out.reshape(BATCH, tt, D_MODEL)
    wcast_out_ref[...] = wcast_ref[...].astype(_bf16)


def _mixer(x, y, q, q_meta, bc, gate_s, gate_c, w_glu, b_glu, w_s_up, conv_w, conv_b, w_c_up,
           w_o, g_post, w_cast):
    tt = MIX_TILE_T
    n_t = SEQ // tt
    cast_spec = pl.BlockSpec((w_cast.shape[0] // n_t, w_cast.shape[1]), lambda t: (t, 0))
    halo_per_tile = tt // HALO_ROWS
    n_halo = SEQ // HALO_ROWS
    tile = lambda t: (0, t, 0)
    stepped = lambda rows, cols: pl.BlockSpec((IN_STEPS, BATCH, rows, cols),
                                              lambda t: (0, 0, t, 0))
    resident = functools.partial(pl.BlockSpec, index_map=lambda t: (0, 0),
                                 pipeline_mode=pl.Buffered(1))
    return pl.pallas_call(
        _mixer_kernel,
        grid=(n_t,),
        in_specs=[
            pl.BlockSpec((BATCH, tt, D_MODEL), tile),
            pl.BlockSpec((S5_SLABS, SLAB_PAIRS, 2, tt // S5_CHUNK * BATCH, V7X_LANES),
                         lambda t: (0, 0, 0, t, 0)),
            stepped(tt, IN_SEG_COLS),
            pl.BlockSpec((IN_STEPS, BATCH, HALO_ROWS, IN_SEG_COLS),
                         lambda t: (0, 0, jnp.maximum(t * halo_per_tile - 1, 0), 0)),
            pl.BlockSpec((IN_STEPS, BATCH, HALO_ROWS, IN_SEG_COLS),
                         lambda t: (0, 0, jnp.minimum((t + 1) * halo_per_tile, n_halo - 1), 0)),
            resident((HALO_ROWS, CONV_WIDTH)),
            stepped(tt, IN_SEG_COLS),
            stepped(tt, 2 * IN_SEG_COLS),
            stepped(tt, 2 * IN_SEG_COLS),
            resident((S5_WIDTH, S5_WIDTH)),
            resident((1, S5_WIDTH)),
            resident((S5_WIDTH, D_MODEL)),
            resident((3, CONV_WIDTH)),
            resident((1, CONV_WIDTH)),
            resident((CONV_WIDTH, D_MODEL)),
            resident((D_MODEL, D_MODEL)),
            resident((1, D_MODEL)),
            cast_spec,
        ],
        out_specs=[pl.BlockSpec((BATCH, tt, D_MODEL), tile), cast_spec],
        out_shape=[jax.ShapeDtypeStruct((BATCH, SEQ, D_MODEL), _f32),
                   jax.ShapeDtypeStruct(w_cast.shape, _bf16)],
        scratch_shapes=[pltpu.VMEM((S5_SLABS, BATCH * tt, V7X_LANES), _f32),
                        pltpu.VMEM((BATCH, tt + 2 * HALO_ROWS, CONV_WIDTH), _f32)],
        compiler_params=_params(("arbitrary",)),
        name="mixer_tail",
    )(x, y, q, q, q, q_meta, bc, gate_s, gate_c, w_glu, b_glu, w_s_up, conv_w, conv_b, w_c_up,
      w_o, g_post, w_cast)


def _ffn_kernel(h_ref, gpre_ref, wgu_ref, wout_ref, gpost_ref, o_ref, hb_scr):
    j = pl.program_id(1)
    last = pl.num_programs(1) - 1

    def chunk(hb):
        gu = _dot(hb, wgu_ref[...])
        gate, up = gu[:, :FFN_TILE_HIDDEN], gu[:, FFN_TILE_HIDDEN:]
        return _dot((jax.nn.silu(gate) * up).astype(_bf16), wout_ref[...])

    @pl.when(j == 0)
    def _():
        hb = _rms_norm(h_ref[...], gpre_ref[...]).astype(_bf16)
        hb_scr[...] = hb
        o_ref[...] = chunk(hb)

    @pl.when(jnp.logical_and(j > 0, j < last))
    def _():
        o_ref[...] += chunk(hb_scr[...])

    @pl.when(j == last)
    def _():
        f = o_ref[...] + chunk(hb_scr[...])
        o_ref[...] = h_ref[...] + _rms_norm(f, gpost_ref[...])


def _ffn(h1, g_pre, w_in, w_out, g_post):
    tm, th = FFN_TILE_ROWS, FFN_TILE_HIDDEN
    n_hidden = FFN_HIDDEN // th
    return pl.pallas_call(
        _ffn_kernel,
        grid=(ROWS // tm, n_hidden),
        in_specs=[
            pl.BlockSpec((tm, D_MODEL), lambda i, j: (i, 0)),
            pl.BlockSpec((1, D_MODEL), lambda i, j: (0, 0)),
            pl.BlockSpec((D_MODEL, 2 * th), lambda i, j: (0, j)),
            pl.BlockSpec((th, D_MODEL), lambda i, j: (j, 0)),
            pl.BlockSpec((1, D_MODEL), lambda i, j: (0, 0)),
        ],
        out_specs=pl.BlockSpec((tm, D_MODEL), lambda i, j: (i, 0)),
        out_shape=jax.ShapeDtypeStruct((ROWS, D_MODEL), _f32),
        scratch_shapes=[pltpu.VMEM((tm, D_MODEL), _bf16)],
        compiler_params=_params(("arbitrary", "arbitrary")),
        name="ffn",
    )(h1, g_pre, w_in, w_out, g_post)


def kernel(x, meta, g_mix_pre, g_mix_post, g_ffn_pre, g_ffn_post, w_in, gate_b, lam_re, lam_im,
           log_dt, b_re, b_im, c_re, c_im, d_skip, w_glu, b_glu, w_s_up, conv_w, conv_b, w_c_up,
           w_o, w_ffn_in, w_ffn_out):
    l = 0
    w_in_b = _cast_w_in(w_in[l])
    gate_b2 = gate_b[l].reshape(2, 1, D_MODEL)
    g_pre = g_mix_pre[l].reshape(1, D_MODEL)

    powers, bz_re, bz_im = _discretise(lam_re[l], lam_im[l], log_dt[l], b_re[l], b_im[l])
    powers = powers.reshape(S5_CHUNK + 1, 2, N_DIR, S5_GROUPS, S5_STATE)
    maps = [m.reshape(N_DIR, S5_WIDTH, S5_STATE) for m in (bz_re, bz_im, c_re[l], c_im[l])]

    u5, bc, q, gate_s, gate_c, um_tiles, qm_tiles, w_ffn_in_b = _in_proj(
        x, meta.astype(x.dtype), g_pre, w_in_b, gate_b2, IN_TILE_ROWS,
        cast=((w_ffn_in[l], _ffn_in_plan()),))
    um5, q_meta = um_tiles[0], qm_tiles[0]

    u = u5.reshape(S5_SLABS, S5_CHUNK_ROWS * S5_CHUNK, V7X_LANES)
    u_meta = jnp.pad(um5.reshape(S5_SLABS, META_CHUNKS, S5_CHUNK_COLS),
                     ((0, 0), (0, V7X_SUBLANES - META_CHUNKS), (0, 0)))
    y, w_glu_b, w_s_up_b, w_c_up_b, w_o_b = _s5(
        u, u_meta, maps, powers, d_skip[l].reshape(S5_SLABS, 1, V7X_LANES),
        cast=(w_glu[l], w_s_up[l], w_c_up[l], w_o[l]))

    h1, w_ffn_out_b = _mixer(x, y, q, q_meta, bc, gate_s, gate_c, w_glu_b,
                             b_glu[l].reshape(1, S5_WIDTH), w_s_up_b, conv_w[l],
                             conv_b[l].reshape(1, CONV_WIDTH), w_c_up_b, w_o_b,
                             g_mix_post[l].reshape(1, D_MODEL), w_ffn_out[l])

    h2 = _ffn(h1.reshape(ROWS, D_MODEL), g_ffn_pre[l].reshape(1, D_MODEL),
              w_ffn_in_b, w_ffn_out_b, g_ffn_post[l].reshape(1, D_MODEL))
    return h2.reshape(BATCH, SEQ, D_MODEL)
```
